```python
import math
import jax, jax.numpy as jnp
from jax import lax
import numpy as np

D_MODEL = 1024
BATCH = 8
SEQ = 4096
DEPTH = 1
DEC_BATCH = 8
DEC_SEQ = 64
PAST_LEN = 1024

CHUNK = 64
D_MIX = D_MODEL
D_SSD = D_MIX // 2
P_SSD = 64
H_SSD = D_SSD // P_SSD
G_SSD = 2
HG_SSD = H_SSD // G_SSD
N_STATE = 64
CONV_W = 4
D_CONV = D_SSD + 2 * G_SSD * N_STATE
D_ATT = D_MIX - D_SSD
HD_ATT = 64
H_ATT = D_ATT // HD_ATT
Q_BLOCK = 128
D_IN = D_SSD + D_CONV + H_SSD + 3 * D_ATT + H_ATT
N_EGROUPS = 4
EXPERTS_PER_GROUP = 4
N_EXPERTS = N_EGROUPS * EXPERTS_PER_GROUP
TOP_K = 2
D_EXPERT = D_MODEL // 2
MOE_BLOCK = 256
EPS = 1e-6
NEG_BIG = -1e30

kernel_name = 'hymba_ssd_fox_hmoe_stream_step'


def rmsnorm(x, w):
    xf = x.astype(jnp.float32)
    xf = xf * lax.rsqrt(jnp.mean(xf * xf, axis=-1, keepdims=True) + EPS)
    return (xf * w.astype(jnp.float32)).astype(x.dtype)


def causal_conv(xbc, prev, w, b):
    L = xbc.shape[1]
    xpad = jnp.concatenate([prev.astype(xbc.dtype), xbc], axis=1)
    out = b + xpad[:, 0:L] * w[0]
    for i in range(1, CONV_W):
        out = out + xpad[:, i:i + L] * w[i]
    return out, xpad[:, xpad.shape[1] - (CONV_W - 1):]


def ssd_scan(x, dA, Bm, Cm, h0):
    Bn, L = x.shape[0], x.shape[1]
    pad = (-L) % CHUNK
    if pad:
        padw = lambda a: jnp.pad(a, [(0, 0), (0, pad)] + [(0, 0)] * (a.ndim - 2))
        x, dA, Bm, Cm = padw(x), padw(dA), padw(Bm), padw(Cm)
    nc = (L + pad) // CHUNK
    x = x.reshape(Bn, nc, CHUNK, G_SSD, HG_SSD, P_SSD)
    dA = dA.reshape(Bn, nc, CHUNK, G_SSD, HG_SSD)
    Bm = Bm.reshape(Bn, nc, CHUNK, G_SSD, N_STATE)
    Cm = Cm.reshape(Bn, nc, CHUNK, G_SSD, N_STATE)
    a_cum = jnp.cumsum(dA, axis=2)
    causal = jnp.tril(jnp.ones((CHUNK, CHUNK), bool))[:, :, None, None]
    seg = a_cum[:, :, :, None] - a_cum[:, :, None, :]
    decay = jnp.exp(jnp.where(causal, seg, -jnp.inf))
    cb = jnp.einsum('bctgn,bcsgn->bctsg', Cm, Bm)
    y_diag = jnp.einsum('bctsg,bctsgh,bcsghp->bctghp', cb, decay, x)
    decay_to_end = jnp.exp(a_cum[:, :, -1:] - a_cum)
    chunk_states = jnp.einsum('bcsgn,bcsgh,bcsghp->bcghpn', Bm, decay_to_end, x)
    chunk_decay = jnp.exp(a_cum[:, :, -1])

    def step(h, inp):
        cs, cd = inp
        return h * cd[..., None, None] + cs, h

    h_final, h_in = lax.scan(step, h0, (jnp.moveaxis(chunk_states, 1, 0), jnp.moveaxis(chunk_decay, 1, 0)))
    h_in = jnp.moveaxis(h_in, 0, 1)
    y_off = jnp.einsum('bctgn,bcghpn,bctgh->bctghp', Cm, h_in, jnp.exp(a_cum))
    y = (y_diag + y_off).reshape(Bn, nc * CHUNK, G_SSD, HG_SSD, P_SSD)[:, :L]
    return y, h_final


def fox_attention(q, k, v, Fq, Fk, q_pos, k_pos):
    Bn, Lq, H, Dh = q.shape
    blk = min(Q_BLOCK, Lq)
    nb = Lq // blk
    scale = Dh ** -0.5
    kf = k.astype(jnp.float32)
    vf = v.astype(jnp.float32)
    Fk_t = jnp.swapaxes(Fk, 1, 2)
    qb = jnp.moveaxis(q.reshape(Bn, nb, blk, H, Dh), 1, 0)
    Fqb = jnp.moveaxis(Fq.reshape(Bn, nb, blk, H), 1, 0)
    pb = q_pos.reshape(nb, blk)

    def one_block(args):
        qi, Fi, pi = args
        s = jnp.einsum('bqhd,bkhd->bhqk', qi.astype(jnp.float32), kf) * scale
        s = s + (jnp.swapaxes(Fi, 1, 2)[..., :, None] - Fk_t[..., None, :])
        s = jnp.where(k_pos[None, None, None, :] <= pi[None, None, :, None], s, NEG_BIG)
        p = jax.nn.softmax(s, axis=-1)
        return jnp.einsum('bhqk,bkhd->bqhd', p, vf)

    o = lax.map(one_block, (qb, Fqb, pb))
    return jnp.moveaxis(o, 0, 1).reshape(Bn, Lq, H, Dh)


def mixer(h, k_prev, v_prev, logf_prev, conv_prev, ssm_prev, w_in, conv_w, conv_b, dt_bias, a_log,
          d_skip, ssd_norm_w, f_bias, attn_norm_w, w_out):
    Bn, L, _ = h.shape
    f32 = jnp.float32
    proj = h @ w_in
    i0 = D_SSD
    i1 = i0 + D_CONV
    i2 = i1 + H_SSD
    i3 = i2 + D_ATT
    i4 = i3 + D_ATT
    i5 = i4 + D_ATT
    z, xbc, dt_raw = proj[..., :i0], proj[..., i0:i1], proj[..., i1:i2]
    q, k, v, f_raw = proj[..., i2:i3], proj[..., i3:i4], proj[..., i4:i5], proj[..., i5:]
    xbc_c, new_conv = causal_conv(xbc, conv_prev, conv_w, conv_b)
    xbc_c = jax.nn.silu(xbc_c.astype(f32))
    xs = xbc_c[..., :D_SSD].reshape(Bn, L, G_SSD, HG_SSD, P_SSD)
    Bm = xbc_c[..., D_SSD:D_SSD + G_SSD * N_STATE].reshape(Bn, L, G_SSD, N_STATE)
    Cm = xbc_c[..., D_SSD + G_SSD * N_STATE:].reshape(Bn, L, G_SSD, N_STATE)
    dt = jax.nn.softplus(dt_raw.astype(f32) + dt_bias.astype(f32)).reshape(Bn, L, G_SSD, HG_SSD)
    A = -jnp.exp(a_log.astype(f32)).reshape(G_SSD, HG_SSD)
    h0 = ssm_prev.astype(f32).reshape(Bn, G_SSD, HG_SSD, P_SSD, N_STATE)
    y, h_fin = ssd_scan(xs * dt[..., None], dt * A, Bm, Cm, h0)
    y = y + d_skip.astype(f32).reshape(G_SSD, HG_SSD)[:, :, None] * xs
    y = y.reshape(Bn, L, D_SSD) * jax.nn.silu(z.astype(f32))
    y_ssd = rmsnorm(y, ssd_norm_w)
    new_ssm = h_fin.reshape(Bn, H_SSD, P_SSD, N_STATE)
    q = q.reshape(Bn, L, H_ATT, HD_ATT)
    k = k.reshape(Bn, L, H_ATT, HD_ATT)
    v = v.reshape(Bn, L, H_ATT, HD_ATT)
    logf = jax.nn.log_sigmoid(f_raw.astype(f32) + f_bias.astype(f32))
    p0 = k_prev.shape[1]
    k_all = jnp.concatenate([k_prev.astype(k.dtype), k], axis=1)
    v_all = jnp.concatenate([v_prev.astype(v.dtype), v], axis=1)
    F_all = jnp.cumsum(jnp.concatenate([logf_prev.astype(f32), logf], axis=1), axis=1)
    o = fox_attention(q, k_all, v_all, F_all[:, p0:], F_all, p0 + jnp.arange(L), jnp.arange(p0 + L))
    y_att = rmsnorm(o.reshape(Bn, L, D_ATT), attn_norm_w)
    out = jnp.concatenate([y_ssd, y_att], axis=-1).astype(h.dtype) @ w_out
    return out, (k, v, logf, new_conv, new_ssm)


def hier_moe(h, w_rg, b_rg, w_re, b_re, w_gate, w_up, w_down):
    T, D = h.shape
    f32 = jnp.float32
    hf = h.astype(f32)
    p_group = jax.nn.softmax(hf @ w_rg.astype(f32) + b_rg.astype(f32), axis=-1)
    p_sel, g_sel = lax.top_k(p_group, 1)
    p_sel, g_sel = p_sel[:, 0], g_sel[:, 0]
    logits_in = jnp.einsum('td,gde->tge', hf, w_re.astype(f32)) + b_re.astype(f32)
    l_sel = jnp.take_along_axis(logits_in, g_sel[:, None, None], axis=1)[:, 0]
    top_l, top_i = lax.top_k(l_sel, TOP_K)
    w_top = jax.nn.softmax(top_l, axis=-1) * p_sel[:, None]
    expert = g_sel[:, None] * EXPERTS_PER_GROUP + top_i
    A = T * TOP_K
    e_flat = expert.reshape(A)
    w_flat = w_top.reshape(A)
    tok_flat = jnp.repeat(jnp.arange(T, dtype=jnp.int32), TOP_K)
    order = jnp.argsort(e_flat)
    e_sorted = e_flat[order]
    counts = jnp.zeros((N_EXPERTS,), jnp.int32).at[e_flat].add(1)
    padded = (counts + MOE_BLOCK - 1) // MOE_BLOCK * MOE_BLOCK
    start = jnp.cumsum(counts) - counts
    pend = jnp.cumsum(padded)
    pstart = pend - padded
    dest = pstart[e_sorted] + (jnp.arange(A, dtype=jnp.int32) - start[e_sorted])
    nb = -(-(A + N_EXPERTS * (MOE_BLOCK - 1)) // MOE_BLOCK)
    P = nb * MOE_BLOCK
    tok_sorted = tok_flat[order]
    rows_x = jnp.zeros((P, D), h.dtype).at[dest].set(h[tok_sorted])
    rows_tok = jnp.full((P,), T, jnp.int32).at[dest].set(tok_sorted)
    rows_w = jnp.zeros((P,), f32).at[dest].set(w_flat[order])
    blk_start = jnp.arange(nb, dtype=jnp.int32) * MOE_BLOCK
    blk_expert = jnp.minimum(jnp.searchsorted(pend, blk_start, side='right'), N_EXPERTS - 1)

    def run_block(args):
        xb, e = args
        return (jax.nn.silu(xb @ w_gate[e]) * (xb @ w_up[e])) @ w_down[e]

    out = lax.map(run_block, (rows_x.reshape(nb, MOE_BLOCK, D), blk_expert)).reshape(P, D)
    y = jax.ops.segment_sum(out.astype(f32) * rows_w[:, None], rows_tok, num_segments=T + 1)[:T]
    return y.astype(h.dtype)


def block(x, c, k_prev, v_prev, logf_prev, conv_prev, ssm_prev, lw):
    (norm1_w, w_ada, b_ada, w_in, conv_w, conv_b, dt_bias, a_log, d_skip, ssd_norm_w, f_bias,
     attn_norm_w, w_out, norm2_w, w_rg, b_rg, w_re, b_re, w_gate, w_up, w_down) = lw
    Bn, L, D = x.shape
    mod = (jax.nn.silu(c) @ w_ada + b_ada)[:, None, :]
    sh1, sc1, g1, sh2, sc2, g2 = jnp.split(mod, 6, axis=-1)
    h = rmsnorm(x, norm1_w) * (1 + sc1) + sh1
    m, states = mixer(h, k_prev, v_prev, logf_prev, conv_prev, ssm_prev, w_in, conv_w, conv_b, dt_bias,
                      a_log, d_skip, ssd_norm_w, f_bias, attn_norm_w, w_out)
    x = x + g1 * m
    h = rmsnorm(x, norm2_w) * (1 + sc2) + sh2
    x = x + g2 * hier_moe(h.reshape(Bn * L, D), w_rg, b_rg, w_re, b_re, w_gate, w_up, w_down).reshape(Bn, L, D)
    return x, states


def setup_inputs(seed: int = 0) -> dict:
    key = jax.random.key(seed)
    ks = iter(jax.random.split(key, 48))
    f32 = jnp.float32
    nrm = lambda shape, s: jax.random.normal(next(ks), shape, f32) * s
    x_prompt = nrm((BATCH, SEQ, D_MODEL), 1.0)
    x_sample = nrm((DEC_BATCH, DEC_SEQ, D_MODEL), 1.0)
    c_prompt = nrm((BATCH, D_MODEL), 1.0)
    c_sample = nrm((DEC_BATCH, D_MODEL), 1.0)
    cache_k = nrm((DEPTH, DEC_BATCH, PAST_LEN, H_ATT, HD_ATT), 1.0)
    cache_v = nrm((DEPTH, DEC_BATCH, PAST_LEN, H_ATT, HD_ATT), 1.0)
    cache_logf = jax.nn.log_sigmoid(3.0 + nrm((DEPTH, DEC_BATCH, PAST_LEN, H_ATT), 1.0))
    state_conv = nrm((DEPTH, DEC_BATCH, CONV_W - 1, D_CONV), 1.0)
    state_ssm = nrm((DEPTH, DEC_BATCH, H_SSD, P_SSD, N_STATE), 0.1)
    norm1_w = 1.0 + nrm((DEPTH, D_MODEL), 0.02)
    w_ada = nrm((DEPTH, D_MODEL, 6 * D_MODEL), 0.5 * D_MODEL ** -0.5)
    b_ada = nrm((DEPTH, 6 * D_MODEL), 0.01)
    w_in = nrm((DEPTH, D_MODEL, D_IN), D_MODEL ** -0.5)
    conv_w = nrm((DEPTH, CONV_W, D_CONV), CONV_W ** -0.5)
    conv_b = nrm((DEPTH, D_CONV), 0.01)
    dt0 = jnp.exp(jax.random.uniform(next(ks), (DEPTH, H_SSD), f32, math.log(1e-3), math.log(1e-1)))
    dt_bias = dt0 + jnp.log(-jnp.expm1(-dt0))
    a_log = jnp.log(jax.random.uniform(next(ks), (DEPTH, H_SSD), f32, 1.0, 16.0))
    d_skip = 1.0 + nrm((DEPTH, H_SSD), 0.02)
    ssd_norm_w = 1.0 + nrm((DEPTH, D_SSD), 0.02)
    f_bias = jax.random.uniform(next(ks), (DEPTH, H_ATT), f32, 1.0, 5.0)
    attn_norm_w = 1.0 + nrm((DEPTH, D_ATT), 0.02)
    w_out = nrm((DEPTH, D_MIX, D_MODEL), D_MIX ** -0.5)
    norm2_w = 1.0 + nrm((DEPTH, D_MODEL), 0.02)
    w_rg = nrm((DEPTH, D_MODEL, N_EGROUPS), D_MODEL ** -0.5)
    b_rg = nrm((DEPTH, N_EGROUPS), 0.01)
    w_re = nrm((DEPTH, N_EGROUPS, D_MODEL, EXPERTS_PER_GROUP), D_MODEL ** -0.5)
    b_re = nrm((DEPTH, N_EGROUPS, EXPERTS_PER_GROUP), 0.01)
    w_gate = nrm((DEPTH, N_EXPERTS, D_MODEL, D_EXPERT), D_MODEL ** -0.5)
    w_up = nrm((DEPTH, N_EXPERTS, D_MODEL, D_EXPERT), D_MODEL ** -0.5)
    w_down = nrm((DEPTH, N_EXPERTS, D_EXPERT, D_MODEL), D_EXPERT ** -0.5)
    final_norm_w = 1.0 + nrm((D_MODEL,), 0.02)
    w_ada_f = nrm((D_MODEL, 2 * D_MODEL), 0.5 * D_MODEL ** -0.5)
    b_ada_f = nrm((2 * D_MODEL,), 0.01)
    return {'x_prompt': x_prompt, 'x_sample': x_sample, 'c_prompt': c_prompt, 'c_sample': c_sample,
            'cache_k': cache_k, 'cache_v': cache_v, 'cache_logf': cache_logf,
            'state_conv': state_conv, 'state_ssm': state_ssm,
            'norm1_w': norm1_w, 'w_ada': w_ada, 'b_ada': b_ada, 'w_in': w_in, 'conv_w': conv_w,
            'conv_b': conv_b, 'dt_bias': dt_bias, 'a_log': a_log, 'd_skip': d_skip,
            'ssd_norm_w': ssd_norm_w, 'f_bias': f_bias, 'attn_norm_w': attn_norm_w, 'w_out': w_out,
            'norm2_w': norm2_w, 'w_rg': w_rg, 'b_rg': b_rg, 'w_re': w_re, 'b_re': b_re,
            'w_gate': w_gate, 'w_up': w_up, 'w_down': w_down,
            'final_norm_w': final_norm_w, 'w_ada_f': w_ada_f, 'b_ada_f': b_ada_f}


def reference(x_prompt, x_sample, c_prompt, c_sample, cache_k, cache_v, cache_logf, state_conv, state_ssm,
              norm1_w, w_ada, b_ada, w_in, conv_w, conv_b, dt_bias, a_log, d_skip, ssd_norm_w, f_bias,
              attn_norm_w, w_out, norm2_w, w_rg, b_rg, w_re, b_re, w_gate, w_up, w_down,
              final_norm_w, w_ada_f, b_ada_f):
    dtype = x_prompt.dtype
    bp = x_prompt.shape[0]
    k0 = jnp.zeros((bp, 0, H_ATT, HD_ATT), dtype)
    lf0 = jnp.zeros((bp, 0, H_ATT), jnp.float32)
    conv0 = jnp.zeros((bp, CONV_W - 1, D_CONV), dtype)
    ssm0 = jnp.zeros((bp, H_SSD, P_SSD, N_STATE), jnp.float32)
    yp, ys = x_prompt, x_sample
    kp, vp, lfp, cvp, smp = [], [], [], [], []
    ks_, vs_, lfs, cvs, sms = [], [], [], [], []
    for l in range(DEPTH):
        lw = (norm1_w[l], w_ada[l], b_ada[l], w_in[l], conv_w[l], conv_b[l], dt_bias[l], a_log[l], d_skip[l],
              ssd_norm_w[l], f_bias[l], attn_norm_w[l], w_out[l], norm2_w[l], w_rg[l], b_rg[l], w_re[l],
              b_re[l], w_gate[l], w_up[l], w_down[l])
        yp, (k_n, v_n, lf_n, cv_n, sm_n) = block(yp, c_prompt, k0, k0, lf0, conv0, ssm0, lw)
        kp.append(k_n); vp.append(v_n); lfp.append(lf_n); cvp.append(cv_n); smp.append(sm_n)
        ys, (k_n, v_n, lf_n, cv_n, sm_n) = block(ys, c_sample, cache_k[l], cache_v[l], cache_logf[l],
                                                 state_conv[l], state_ssm[l], lw)
        ks_.append(k_n); vs_.append(v_n); lfs.append(lf_n); cvs.append(cv_n); sms.append(sm_n)
    mod_p = (jax.nn.silu(c_prompt) @ w_ada_f + b_ada_f)[:, None, :]
    mod_s = (jax.nn.silu(c_sample) @ w_ada_f + b_ada_f)[:, None, :]
    sh_p, sc_p = jnp.split(mod_p, 2, axis=-1)
    sh_s, sc_s = jnp.split(mod_s, 2, axis=-1)
    y_prompt = rmsnorm(yp, final_norm_w) * (1 + sc_p) + sh_p
    y_sample = rmsnorm(ys, final_norm_w) * (1 + sc_s) + sh_s
    return (y_prompt, y_sample,
            jnp.stack(kp), jnp.stack(vp), jnp.stack(lfp), jnp.stack(cvp), jnp.stack(smp),
            jnp.stack(ks_), jnp.stack(vs_), jnp.stack(lfs), jnp.stack(cvs), jnp.stack(sms))
```

```python
import functools

import jax
import jax.numpy as jnp
from jax import lax
from jax.experimental import pallas as pl
from jax.experimental.pallas import tpu as pltpu

F32 = jnp.float32
BF16 = jnp.bfloat16

P_SSD = 64
N_STATE = 64
G_SSD = 2
CONV_W = 4
HD_ATT = 64
N_EGROUPS = 4
EXPERTS_PER_GROUP = 4
N_EXPERTS = N_EGROUPS * EXPERTS_PER_GROUP
EPS = 1e-6
NEG_BIG = -1e30

LANES = 128
SMALL_W = 16
VMEM_LIMIT = 56 * 1024 * 1024


def _params(*sem):
    return pltpu.CompilerParams(dimension_semantics=sem, vmem_limit_bytes=VMEM_LIMIT)


def _split2(x):
    hi = x.astype(BF16)
    lo = (x - hi.astype(F32)).astype(BF16)
    return hi, lo


def _split3(x):
    hi = x.astype(BF16)
    r = x - hi.astype(F32)
    mid = r.astype(BF16)
    lo = (r - mid.astype(F32)).astype(BF16)
    return hi, mid, lo


def _dot(a, b):
    return jnp.dot(a, b, preferred_element_type=F32)


def _dot_nt(a, b):
    return lax.dot_general(a, b, (((1,), (1,)), ((), ())), preferred_element_type=F32)


def _dot_tn(a, b):
    return lax.dot_general(a, b, (((0,), (0,)), ((), ())), preferred_element_type=F32)


def _silu(x):
    return x / (1.0 + jnp.exp(-x))


def _rms(x, w):
    return x * lax.rsqrt(jnp.mean(x * x, axis=-1, keepdims=True) + EPS) * w


def _pick_tile(n, candidates):
    for c in candidates:
        if n % c == 0:
            return c
    return n


def _mod_kernel(c_ref, w_ref, b_ref, o_ref):
    a = _silu(c_ref[...])
    a_hi, a_lo = _split2(a)
    w_hi, w_lo = _split2(w_ref[...])
    o_ref[...] = _dot(a_hi, w_hi) + _dot(a_lo, w_hi) + _dot(a_hi, w_lo) + b_ref[...]


def _modulation(c, w, b):
    m, d = c.shape
    n = w.shape[1]
    tn = _pick_tile(n, (1024, 512, 256, 128))
    return pl.pallas_call(
        _mod_kernel,
        grid=(n // tn,),
        in_specs=[pl.BlockSpec((m, d), lambda j: (0, 0)),
                  pl.BlockSpec((d, tn), lambda j: (0, j)),
                  pl.BlockSpec((1, tn), lambda j: (0, j))],
        out_specs=pl.BlockSpec((m, tn), lambda j: (0, j)),
        out_shape=jax.ShapeDtypeStruct((m, n), F32),
        compiler_params=_params("parallel"),
        name="adaln_mod",
    )(c, w, b.reshape(1, n))


def _inproj_kernel(x_ref, sh_ref, sc_ref, nw_ref, wz_ref, wx_ref, wq_ref, wk_ref, wv_ref, ws_ref, bs_ref,
                   z_ref, xbc_ref, q_ref, k_ref, v_ref, kb_ref, vb_ref, sm_ref, smt_ref, tail_ref):
    l = pl.program_id(1)
    x = x_ref[0]
    h = (_rms(x, nw_ref[...]) * (1.0 + sc_ref[0]) + sh_ref[0]).astype(BF16)
    z_ref[0] = _dot(h, wz_ref[...]).astype(BF16)
    xbc = _dot(h, wx_ref[...])
    xbc_ref[0] = xbc.astype(BF16)
    q_ref[0] = _dot(h, wq_ref[...]).astype(BF16)
    k = _dot(h, wk_ref[...])
    k_ref[0] = k
    kb_ref[0] = k.astype(BF16)
    v = _dot(h, wv_ref[...])
    v_ref[0] = v
    vb_ref[0] = v.astype(BF16)
    s = _dot(h, ws_ref[...]) + bs_ref[...]
    t = jnp.log(1.0 + jnp.exp(-jnp.abs(s)))
    lane = lax.broadcasted_iota(jnp.int32, s.shape, 1)
    s = jnp.where(lane < SMALL_W // 2, jnp.maximum(s, 0.0) + t, jnp.minimum(s, 0.0) - t)
    sm_ref[0] = s[:, :SMALL_W]
    smt_ref[0] = s.T[:SMALL_W, :]

    @pl.when(l == pl.num_programs(1) - 1)
    def _():
        tl = xbc.shape[0]
        tail_ref[0] = xbc[tl - (CONV_W - 1):, :]


def _inproj(x, mod3, boff, norm_w, wz, wx, wq, wk, wv, ws, bs):
    b, l, d = x.shape
    tl = _pick_tile(l, (512, 256, 128, 64))
    nl = l // tl
    d_ssd, d_conv, d_att = wz.shape[1], wx.shape[1], wq.shape[1]
    row = lambda col: pl.BlockSpec((1, 1, d), lambda i, j, col=col: (i + boff, 0, col))
    full = lambda a: pl.BlockSpec(a.shape, lambda i, j: (0,) * a.ndim)
    tok = lambda w: pl.BlockSpec((1, tl, w), lambda i, j: (i, j, 0))
    out_shape = [
        jax.ShapeDtypeStruct((b, l, d_ssd), BF16),
        jax.ShapeDtypeStruct((b, l, d_conv), BF16),
        jax.ShapeDtypeStruct((b, l, d_att), BF16),
        jax.ShapeDtypeStruct((b, l, d_att), F32),
        jax.ShapeDtypeStruct((b, l, d_att), F32),
        jax.ShapeDtypeStruct((b, l, d_att), BF16),
        jax.ShapeDtypeStruct((b, l, d_att), BF16),
        jax.ShapeDtypeStruct((b, l, SMALL_W), F32),
        jax.ShapeDtypeStruct((b, SMALL_W, l), F32),
        jax.ShapeDtypeStruct((b, CONV_W - 1, d_conv), F32),
    ]
    out_specs = [tok(d_ssd), tok(d_conv), tok(d_att), tok(d_att), tok(d_att), tok(d_att), tok(d_att),
                 tok(SMALL_W), pl.BlockSpec((1, SMALL_W, tl), lambda i, j: (i, 0, j)),
                 pl.BlockSpec((1, CONV_W - 1, d_conv), lambda i, j: (i, 0, 0))]
    return pl.pallas_call(
        _inproj_kernel,
        grid=(b, nl),
        in_specs=[tok(d), row(0), row(1), full(norm_w), full(wz), full(wx), full(wq), full(wk), full(wv),
                  full(ws), full(bs)],
        out_specs=out_specs,
        out_shape=out_shape,
        compiler_params=_params("parallel", "arbitrary"),
        name="inproj",
    )(x, mod3, mod3, norm_w, wz, wx, wq, wk, wv, ws, bs)


def _cumsum_kernel(x_ref, ft_ref, f_ref, carry):
    @pl.when(pl.program_id(1) == 0)
    def _():
        carry[...] = jnp.zeros_like(carry)

    x = x_ref[0]
    tc = x.shape[1]
    r = lax.broadcasted_iota(jnp.int32, (tc, tc), 0)
    c = lax.broadcasted_iota(jnp.int32, (tc, tc), 1)
    upper = jnp.where(r <= c, 1.0, 0.0).astype(BF16)
    hi, mid, lo = _split3(x)
    cs = _dot(hi, upper) + _dot(mid, upper) + _dot(lo, upper) + carry[...]
    carry[...] = cs[:, tc - 1:]
    ft_ref[0] = cs
    pad = jnp.concatenate([cs, jnp.zeros((LANES - cs.shape[0], tc), F32)], axis=0)
    f_ref[0] = pad.T[:, :cs.shape[0]]


def _forget_cumsum(logf_t):
    b, h, lk = logf_t.shape
    tc = _pick_tile(lk, (256, 128))
    return pl.pallas_call(
        _cumsum_kernel,
        grid=(b, lk // tc),
        in_specs=[pl.BlockSpec((1, h, tc), lambda i, j: (i, 0, j))],
        out_specs=[pl.BlockSpec((1, h, tc), lambda i, j: (i, 0, j)),
                   pl.BlockSpec((1, tc, h), lambda i, j: (i, j, 0))],
        out_shape=[jax.ShapeDtypeStruct((b, h, lk), F32), jax.ShapeDtypeStruct((b, lk, h), F32)],
        scratch_shapes=[pltpu.VMEM((h, 1), F32)],
        compiler_params=_params("parallel", "arbitrary"),
        name="forget_cumsum",
    )(logf_t)


def _ssd_kernel(xbc_ref, z_ref, sm_ref, smt_ref, cprev_ref, sprev_ref, cw_ref, cb_ref, arow_ref, acol_ref,
                dskip_ref, nw_ref, y_ref, snew_ref, xpad, state, ybuf):
    l = pl.program_id(1)
    q = xbc_ref.shape[1]
    n_heads = state.shape[0]
    hg = n_heads // G_SSD
    d_ssd = n_heads * P_SSD
    top = 8

    @pl.when(l == 0)
    def _():
        xpad[0:top, :] = jnp.zeros((top, xpad.shape[1]), F32)
        xpad[top - (CONV_W - 1):top, :] = cprev_ref[0]
        state[...] = sprev_ref[0]

    xpad[top:top + q, :] = xbc_ref[0].astype(F32)
    cw = cw_ref[...]
    conv = cb_ref[...] + xpad[top - 3:top - 3 + q, :] * cw[0:1, :]
    for i in range(1, CONV_W):
        conv = conv + xpad[top - 3 + i:top - 3 + i + q, :] * cw[i:i + 1, :]
    tail = xpad[top + q - (CONV_W - 1):top + q, :]
    xpad[top - (CONV_W - 1):top, :] = tail
    act = _silu(conv)

    dt = sm_ref[0][:, :n_heads]
    dt_t = smt_ref[0][:n_heads, :]
    a_row = -jnp.exp(arow_ref[...])
    a_col = -jnp.exp(acol_ref[...])
    r = lax.broadcasted_iota(jnp.int32, (q, q), 0)
    c = lax.broadcasted_iota(jnp.int32, (q, q), 1)
    causal = r >= c
    lower = jnp.where(causal, 1.0, 0.0).astype(BF16)
    upper = jnp.where(r <= c, 1.0, 0.0).astype(BF16)
    h0, h1, h2 = _split3(dt * a_row)
    acum = _dot(lower, h0) + _dot(lower, h1) + _dot(lower, h2)
    t0, t1, t2 = _split3(dt_t * a_col)
    acum_t = _dot(t0, upper) + _dot(t1, upper) + _dot(t2, upper)
    a_last = acum[q - 1:q, :]
    e_cum = jnp.exp(acum)
    e_end = jnp.exp(a_last - acum)
    e_all = jnp.exp(a_last)
    dskip = dskip_ref[...]

    for g in range(G_SSD):
        bg = act[:, d_ssd + g * N_STATE:d_ssd + (g + 1) * N_STATE]
        cg = act[:, d_ssd + (G_SSD + g) * N_STATE:d_ssd + (G_SSD + g + 1) * N_STATE]
        cg_b = cg.astype(BF16)
        cb = _dot_nt(cg_b, bg.astype(BF16))
        for hh in range(hg):
            h = g * hg + hh
            xs = act[:, h * P_SSD:(h + 1) * P_SSD]
            xdt = xs * dt[:, h:h + 1]
            xdt_b = xdt.astype(BF16)
            seg = acum[:, h:h + 1] - acum_t[h:h + 1, :]
            decay = jnp.exp(jnp.where(causal, seg, NEG_BIG))
            y = _dot((cb * decay).astype(BF16), xdt_b)
            h_in = state[h]
            y = y + _dot_nt(cg_b, h_in.astype(BF16)) * e_cum[:, h:h + 1]
            bd = (bg * e_end[:, h:h + 1]).astype(BF16)
            state[h] = h_in * e_all[:, h:h + 1] + _dot_tn(xdt_b, bd)
            ybuf[:, h * P_SSD:(h + 1) * P_SSD] = y + dskip[:, h:h + 1] * xs

    yg = ybuf[...] * _silu(z_ref[0].astype(F32))
    y_ref[0] = _rms(yg, nw_ref[...]).astype(BF16)
    snew_ref[0] = state[...]


def _ssd(xbc, z, sm, smt, conv_prev, ssm_prev, conv_w, conv_b, a_log, d_skip, norm_w):
    b, l, d_conv = xbc.shape
    d_ssd = z.shape[2]
    n_heads = d_ssd // P_SSD
    q = _pick_tile(l, (256, 128, 64))
    tok = lambda w: pl.BlockSpec((1, q, w), lambda i, j: (i, j, 0))
    full = lambda a: pl.BlockSpec(a.shape, lambda i, j: (0,) * a.ndim)
    a_row = a_log.reshape(1, n_heads)
    a_col = a_log.reshape(n_heads, 1)
    conv_b = conv_b.reshape(1, d_conv)
    d_skip = d_skip.reshape(1, n_heads)
    norm_w = norm_w.reshape(1, d_ssd)
    state_spec = pl.BlockSpec((1, n_heads, P_SSD, N_STATE), lambda i, j: (i, 0, 0, 0))
    return pl.pallas_call(
        _ssd_kernel,
        grid=(b, l // q),
        in_specs=[tok(d_conv), tok(d_ssd), tok(SMALL_W), pl.BlockSpec((1, SMALL_W, q), lambda i, j: (i, 0, j)),
                  pl.BlockSpec((1, CONV_W - 1, d_conv), lambda i, j: (i, 0, 0)), state_spec,
                  full(conv_w), full(conv_b), full(a_row), full(a_col), full(d_skip), full(norm_w)],
        out_specs=[tok(d_ssd), state_spec],
        out_shape=[jax.ShapeDtypeStruct((b, l, d_ssd), BF16),
                   jax.ShapeDtypeStruct((b, n_heads, P_SSD, N_STATE), F32)],
        scratch_shapes=[pltpu.VMEM((q + 8, d_conv), F32),
                        pltpu.VMEM((n_heads, P_SSD, N_STATE), F32),
                        pltpu.VMEM((q, d_ssd), F32)],
        compiler_params=_params("parallel", "arbitrary"),
        name="ssd",
    )(xbc, z, sm, smt, conv_prev, ssm_prev, conv_w, conv_b, a_row, a_col, d_skip, norm_w)


def _attn_kernel(q_ref, k_ref, v_ref, fq_ref, fk_ref, o_ref, m_s, l_s, acc_s, fq_s, *, p0, tq, tk):
    hp = pl.program_id(1)
    i = pl.program_id(2)
    j = pl.program_id(3)
    nj = pl.num_programs(3)
    lane = lax.broadcasted_iota(jnp.int32, (1, 2 * HD_ATT), 1)

    @pl.when(j == 0)
    def _():
        m_s[...] = jnp.full(m_s.shape, NEG_BIG, F32)
        l_s[...] = jnp.zeros(l_s.shape, F32)
        acc_s[...] = jnp.zeros(acc_s.shape, F32)
        fq = fq_ref[0]
        col = lax.broadcasted_iota(jnp.int32, fq.shape, 1)
        for hh in range(2):
            fq_s[hh] = jnp.sum(jnp.where(col == 2 * hp + hh, fq, 0.0), axis=1, keepdims=True)

    @pl.when(j * tk <= p0 + i * tq + tq - 1)
    def _():
        q = q_ref[0]
        k = k_ref[0]
        v = v_ref[0]
        q_pos = p0 + i * tq + lax.broadcasted_iota(jnp.int32, (tq, tk), 0)
        k_pos = j * tk + lax.broadcasted_iota(jnp.int32, (tq, tk), 1)
        visible = k_pos <= q_pos
        for hh in range(2):
            in_head = (lane >= hh * HD_ATT) & (lane < (hh + 1) * HD_ATT)
            qm = jnp.where(in_head, q, jnp.zeros_like(q))
            s = _dot_nt(qm, k) + (fq_s[hh] - fk_ref[0, 0, hh:hh + 1, :])
            s = jnp.where(visible, s, NEG_BIG)
            m_prev = m_s[hh]
            m_new = jnp.maximum(m_prev, jnp.max(s, axis=1, keepdims=True))
            alpha = jnp.exp(m_prev - m_new)
            p = jnp.exp(s - m_new)
            l_s[hh] = alpha * l_s[hh] + jnp.sum(p, axis=1, keepdims=True)
            acc_s[hh] = alpha * acc_s[hh] + _dot(p.astype(BF16), v)
            m_s[hh] = m_new

    @pl.when(j == nj - 1)
    def _():
        o0 = acc_s[0] / l_s[0]
        o1 = acc_s[1] / l_s[1]
        o_ref[0] = jnp.where(lane < HD_ATT, o0, o1)


def _attention(q, k, v, f_rows, f_t, p0):
    b, lq, d_att = q.shape
    lk = k.shape[1]
    n_heads = d_att // HD_ATT
    tq = _pick_tile(lq, (512, 256, 128, 64))
    tk = _pick_tile(lk, (512, 384, 256, 128))
    assert p0 % tq == 0
    last = lambda i: (p0 + (i + 1) * tq - 1) // tk
    f_t4 = f_t.reshape(b, n_heads // 2, 2, lk)
    kv_spec = pl.BlockSpec((1, tk, 2 * HD_ATT), lambda bi, hp, i, j: (bi, jnp.minimum(j, last(i)), hp))
    return pl.pallas_call(
        functools.partial(_attn_kernel, p0=p0, tq=tq, tk=tk),
        grid=(b, n_heads // 2, lq // tq, lk // tk),
        in_specs=[pl.BlockSpec((1, tq, 2 * HD_ATT), lambda bi, hp, i, j: (bi, i, hp)),
                  kv_spec, kv_spec,
                  pl.BlockSpec((1, tq, n_heads), lambda bi, hp, i, j: (bi, i + p0 // tq, 0)),
                  pl.BlockSpec((1, 1, 2, tk), lambda bi, hp, i, j: (bi, hp, 0, jnp.minimum(j, last(i))))],
        out_specs=pl.BlockSpec((1, tq, 2 * HD_ATT), lambda bi, hp, i, j: (bi, i, hp)),
        out_shape=jax.ShapeDtypeStruct((b, lq, d_att), F32),
        scratch_shapes=[pltpu.VMEM((2, tq, 1), F32), pltpu.VMEM((2, tq, 1), F32),
                        pltpu.VMEM((2, tq, 2 * HD_ATT), F32), pltpu.VMEM((2, tq, 1), F32)],
        compiler_params=_params("parallel", "parallel", "parallel", "arbitrary"),
        name="fox_attention",
    )(q, k, v, f_rows, f_t4)


ROUTER_ROWS = 32
EXPERT_ROW0 = 8


def _outproj_kernel(ys_ref, oa_ref, x_ref, g1_ref, sh_ref, sc_ref, anw_ref, wos_ref, woa_ref, n2w_ref, wr_ref,
                    br_ref, x1_ref, h2_ref, wrow_ref, wt_s):
    ya = _rms(oa_ref[0], anw_ref[...]).astype(BF16)
    m = _dot(ys_ref[0], wos_ref[...]) + _dot(ya, woa_ref[...])
    x1 = x_ref[0] + g1_ref[0] * m
    x1_ref[0] = x1
    h2 = _rms(x1, n2w_ref[...]) * (1.0 + sc_ref[0]) + sh_ref[0]
    h_hi, h_lo = _split2(h2)
    h2_ref[0] = h_hi
    wr = wr_ref[...]
    p1 = _dot_nt(wr, h_hi)
    p2 = _dot_nt(wr[:ROUTER_ROWS], h_lo)
    logit = p1[:ROUTER_ROWS] + p1[ROUTER_ROWS:] + p2 + br_ref[...]

    lg = [logit[g:g + 1, :] for g in range(N_EGROUPS)]
    gmax = jnp.maximum(jnp.maximum(lg[0], lg[1]), jnp.maximum(lg[2], lg[3]))
    denom = sum(jnp.exp(x - gmax) for x in lg)
    p_sel = 1.0 / denom
    is_g = []
    taken = jnp.zeros_like(gmax) > 1.0
    for g in range(N_EGROUPS):
        hit = (lg[g] == gmax) & jnp.logical_not(taken)
        is_g.append(hit)
        taken = taken | hit
    le = []
    for e in range(EXPERTS_PER_GROUP):
        v = logit[EXPERT_ROW0 + 3 * EXPERTS_PER_GROUP + e:EXPERT_ROW0 + 3 * EXPERTS_PER_GROUP + e + 1, :]
        for g in range(N_EGROUPS - 2, -1, -1):
            r0 = EXPERT_ROW0 + g * EXPERTS_PER_GROUP + e
            v = jnp.where(is_g[g], logit[r0:r0 + 1, :], v)
        le.append(v)
    m1 = jnp.maximum(jnp.maximum(le[0], le[1]), jnp.maximum(le[2], le[3]))
    first = []
    taken = jnp.zeros_like(m1) > 1.0
    for e in range(EXPERTS_PER_GROUP):
        hit = (le[e] == m1) & jnp.logical_not(taken)
        first.append(hit)
        taken = taken | hit
    rest = [jnp.where(first[e], -jnp.inf, le[e]) for e in range(EXPERTS_PER_GROUP)]
    m2 = jnp.maximum(jnp.maximum(rest[0], rest[1]), jnp.maximum(rest[2], rest[3]))
    second = []
    taken = jnp.zeros_like(m1) > 1.0
    for e in range(EXPERTS_PER_GROUP):
        hit = (rest[e] == m2) & jnp.logical_not(taken)
        second.append(hit)
        taken = taken | hit
    e2 = jnp.exp(m2 - m1)
    w_a = p_sel / (1.0 + e2)
    w_b = w_a * e2
    wt_s[...] = jnp.zeros(wt_s.shape, F32)
    for g in range(N_EGROUPS):
        for e in range(EXPERTS_PER_GROUP):
            w = jnp.where(first[e], w_a, jnp.where(second[e], w_b, 0.0))
            r0 = g * EXPERTS_PER_GROUP + e
            wt_s[r0:r0 + 1, :] = jnp.where(is_g[g], w, 0.0)
    wrow_ref[...] = wt_s[...].T[:, :N_EXPERTS]


def _outproj(y_ssd, o_att, x, mod3, boff, attn_norm_w, wo_ssd, wo_att, norm2_w, wr, br):
    b, l, d = x.shape
    tl = _pick_tile(l, (512, 256, 128, 64))
    nl = l // tl
    d_ssd, d_att = y_ssd.shape[2], o_att.shape[2]
    row = lambda col: pl.BlockSpec((1, 1, d), lambda i, j, col=col: (i + boff, 0, col))
    full = lambda a: pl.BlockSpec(a.shape, lambda i, j: (0,) * a.ndim)
    tok = lambda w: pl.BlockSpec((1, tl, w), lambda i, j: (i, j, 0))
    return pl.pallas_call(
        _outproj_kernel,
        grid=(b, nl),
        in_specs=[tok(d_ssd), tok(d_att), tok(d), row(2), row(3), row(4), full(attn_norm_w), full(wo_ssd),
                  full(wo_att), full(norm2_w), full(wr), full(br)],
        out_specs=[tok(d), tok(d), pl.BlockSpec((tl, N_EXPERTS), lambda i, j: (i * nl + j, 0))],
        out_shape=[jax.ShapeDtypeStruct((b, l, d), F32), jax.ShapeDtypeStruct((b, l, d), BF16),
                   jax.ShapeDtypeStruct((b * l, N_EXPERTS), F32)],
        scratch_shapes=[pltpu.VMEM((LANES, tl), F32)],
        compiler_params=_params("parallel", "parallel"),
        name="outproj_router",
    )(y_ssd, o_att, x, mod3, mod3, mod3, attn_norm_w, wo_ssd, wo_att, norm2_w, wr, br)


def _moe_kernel(h_ref, w_ref, wg_ref, wu_ref, wd_ref, x1_ref, g2_ref, shf_ref, scf_ref, fnw_ref, y_ref, acc, *,
                final_norm):
    e = pl.program_id(1)

    @pl.when(e == 0)
    def _():
        acc[...] = jnp.zeros_like(acc)

    x = h_ref[...]
    w = w_ref[...]
    col = lax.broadcasted_iota(jnp.int32, w.shape, 1)
    w_e = jnp.sum(jnp.where(col == e, w, 0.0), axis=1, keepdims=True)
    hid = _silu(_dot(x, wg_ref[0])) * _dot(x, wu_ref[0]) * w_e
    acc[...] += _dot(hid.astype(BF16), wd_ref[0])

    @pl.when(e == pl.num_programs(1) - 1)
    def _():
        xo = x1_ref[...] + g2_ref[0] * acc[...]
        if final_norm:
            xo = _rms(xo, fnw_ref[...]) * (1.0 + scf_ref[0]) + shf_ref[0]
        y_ref[...] = xo


def _moe_final(h2, w_rows, wg, wu, wd, x1, mod3, modf3, boff, rows_per_batch, final_norm_w, final_norm):
    t, d = h2.shape
    d_e = wg.shape[2]
    tm = _pick_tile(rows_per_batch, (1024, 512, 256, 128, 64))
    per = rows_per_batch // tm
    row = lambda col: pl.BlockSpec((1, 1, d), lambda i, e, col=col: (i // per + boff, 0, col))
    return pl.pallas_call(
        functools.partial(_moe_kernel, final_norm=final_norm),
        grid=(t // tm, N_EXPERTS),
        in_specs=[pl.BlockSpec((tm, d), lambda i, e: (i, 0)),
                  pl.BlockSpec((tm, N_EXPERTS), lambda i, e: (i, 0)),
                  pl.BlockSpec((1, d, d_e), lambda i, e: (e, 0, 0)),
                  pl.BlockSpec((1, d, d_e), lambda i, e: (e, 0, 0)),
                  pl.BlockSpec((1, d_e, d), lambda i, e: (e, 0, 0)),
                  pl.BlockSpec((tm, d), lambda i, e: (i, 0)),
                  row(5), row(0), row(1),
                  pl.BlockSpec((1, d), lambda i, e: (0, 0))],
        out_specs=pl.BlockSpec((tm, d), lambda i, e: (i, 0)),
        out_shape=jax.ShapeDtypeStruct((t, d), F32),
        scratch_shapes=[pltpu.VMEM((tm, d), F32)],
        compiler_params=_params("parallel", "arbitrary"),
        name="moe_final",
    )(h2, w_rows, wg, wu, wd, x1, mod3, modf3, modf3, final_norm_w)


def _layer(x, mod3, modf3, boff, k_prev, v_prev, logf_prev, conv_prev, ssm_prev, p, final_norm_w, final_norm):
    b, l, d = x.shape
    z, xbc, q, k, v, k_b, v_b, sm, smt, conv_new = _inproj(
        x, mod3, boff, p["norm1_w"], p["wz"], p["wx"], p["wq"], p["wk"], p["wv"], p["ws"], p["bs"])
    n_heads_ssd = p["wz"].shape[1] // P_SSD
    y_ssd, ssm_new = _ssd(xbc, z, sm, smt, conv_prev, ssm_prev, p["conv_w"], p["conv_b"], p["a_log"],
                          p["d_skip"], p["ssd_norm_w"])
    n_heads = p["wq"].shape[1] // HD_ATT
    logf = sm[:, :, n_heads_ssd:n_heads_ssd + n_heads]
    logf_t = smt[:, n_heads_ssd:n_heads_ssd + n_heads, :]
    if k_prev is None:
        p0, k_all, v_all, lf_all = 0, k_b, v_b, logf_t
    else:
        p0 = k_prev.shape[1]
        lk = p0 + l
        pad = (-lk) % LANES
        zeros = lambda *s: jnp.zeros(s, BF16)
        k_all = jnp.concatenate([k_prev.astype(BF16), k_b, zeros(b, pad, k_b.shape[2])], axis=1)
        v_all = jnp.concatenate([v_prev.astype(BF16), v_b, zeros(b, pad, v_b.shape[2])], axis=1)
        lf_all = jnp.concatenate([jnp.swapaxes(logf_prev, 1, 2), logf_t, jnp.zeros((b, n_heads, pad), F32)],
                                 axis=2)
    f_t, f_rows = _forget_cumsum(lf_all)
    o_att = _attention(q, k_all, v_all, f_rows, f_t, p0)
    x1, h2, w_rows = _outproj(y_ssd, o_att, x, mod3, boff, p["attn_norm_w"], p["wo_ssd"], p["wo_att"],
                              p["norm2_w"], p["wr"], p["br"])
    y = _moe_final(h2.reshape(b * l, d), w_rows, p["wg"], p["wu"], p["wd"], x1.reshape(b * l, d), mod3, modf3,
                   boff, l, final_norm_w, final_norm)
    return y.reshape(b, l, d), (k, v, logf, conv_new, ssm_new)


def kernel(x_prompt, x_sample, c_prompt, c_sample, cache_k, cache_v, cache_logf, state_conv, state_ssm, norm1_w, w_ada, b_ada, w_in, conv_w, conv_b, dt_bias, a_log, d_skip, ssd_norm_w, f_bias, attn_norm_w, w_out, norm2_w, w_rg, b_rg, w_re, b_re, w_gate, w_up, w_down, final_norm_w, w_ada_f, b_ada_f):
    depth = w_in.shape[0]
    bp, lp, d = x_prompt.shape
    bs = x_sample.shape[0]
    d_conv = conv_w.shape[2]
    d_ssd = ssd_norm_w.shape[1]
    d_att = attn_norm_w.shape[1]
    h_ssd = dt_bias.shape[1]
    h_att = f_bias.shape[1]
    assert h_ssd + h_att == SMALL_W and d_att // HD_ATT == h_att and d_ssd // P_SSD == h_ssd

    c_all = jnp.concatenate([c_prompt, c_sample], axis=0)
    modf3 = _modulation(c_all, w_ada_f, b_ada_f).reshape(bp + bs, 1, 2 * d)
    final_w = final_norm_w.reshape(1, d)

    i0 = d_ssd
    i1 = i0 + d_conv
    i2 = i1 + h_ssd
    i3 = i2 + d_att
    i4 = i3 + d_att
    i5 = i4 + d_att
    yp, ys = x_prompt, x_sample
    outs_p, outs_s = [], []
    for layer in range(depth):
        mod3 = _modulation(c_all, w_ada[layer], b_ada[layer]).reshape(bp + bs, 1, 6 * d)
        wi = w_in[layer]
        w_small = jnp.concatenate([wi[:, i1:i2], wi[:, i5:], jnp.zeros((d, LANES - SMALL_W), F32)], axis=1)
        b_small = jnp.concatenate([dt_bias[layer], f_bias[layer], jnp.zeros((LANES - SMALL_W,), F32)])
        wr = jnp.zeros((ROUTER_ROWS, d), F32)
        wr = wr.at[:N_EGROUPS].set(w_rg[layer].T)
        wr = wr.at[EXPERT_ROW0:EXPERT_ROW0 + N_EXPERTS].set(
            jnp.transpose(w_re[layer], (0, 2, 1)).reshape(N_EXPERTS, d))
        wr_hi, wr_lo = _split2(wr)
        br = jnp.zeros((ROUTER_ROWS,), F32)
        br = br.at[:N_EGROUPS].set(b_rg[layer])
        br = br.at[EXPERT_ROW0:EXPERT_ROW0 + N_EXPERTS].set(b_re[layer].reshape(N_EXPERTS))
        p = dict(
            norm1_w=norm1_w[layer].reshape(1, d),
            wz=wi[:, :i0].astype(BF16), wx=wi[:, i0:i1].astype(BF16),
            wq=(wi[:, i2:i3] * (HD_ATT ** -0.5)).astype(BF16),
            wk=wi[:, i3:i4].astype(BF16), wv=wi[:, i4:i5].astype(BF16),
            ws=w_small.astype(BF16), bs=b_small.reshape(1, LANES),
            conv_w=conv_w[layer], conv_b=conv_b[layer], a_log=a_log[layer], d_skip=d_skip[layer],
            ssd_norm_w=ssd_norm_w[layer], attn_norm_w=attn_norm_w[layer].reshape(1, d_att),
            wo_ssd=w_out[layer][:d_ssd].astype(BF16), wo_att=w_out[layer][d_ssd:].astype(BF16),
            norm2_w=norm2_w[layer].reshape(1, d),
            wr=jnp.concatenate([wr_hi, wr_lo], axis=0), br=br.reshape(ROUTER_ROWS, 1),
            wg=w_gate[layer].astype(BF16), wu=w_up[layer].astype(BF16), wd=w_down[layer].astype(BF16),
        )
        conv0 = jnp.zeros((bp, CONV_W - 1, d_conv), F32)
        ssm0 = jnp.zeros((bp, h_ssd, P_SSD, N_STATE), F32)
        last = layer == depth - 1
        yp, st_p = _layer(yp, mod3, modf3, 0, None, None, None, conv0, ssm0, p, final_w, last)
        ck = cache_k[layer].reshape(bs, -1, d_att)
        cv = cache_v[layer].reshape(bs, -1, d_att)
        ys, st_s = _layer(ys, mod3, modf3, bp, ck, cv, cache_logf[layer], state_conv[layer], state_ssm[layer], p,
                          final_w, last)
        outs_p.append(st_p)
        outs_s.append(st_s)

    def stack(outs, b, l):
        k = jnp.stack([o[0].reshape(b, l, h_att, HD_ATT) for o in outs])
        v = jnp.stack([o[1].reshape(b, l, h_att, HD_ATT) for o in outs])
        return (k, v, jnp.stack([o[2] for o in outs]), jnp.stack([o[3] for o in outs]),
                jnp.stack([o[4] for o in outs]))

    return (yp, ys) + stack(outs_p, bp, lp) + stack(outs_s, bs, x_sample.shape[1])
```

```python
import functools

import jax
import jax.numpy as jnp
import numpy as np
from jax import lax
from jax.experimental import pallas as pl
from jax.experimental.pallas import tpu as pltpu

F32 = jnp.float32
BF16 = jnp.bfloat16

P_SSD = 64
N_STATE = 64
G_SSD = 2
CONV_W = 4
HD_ATT = 64
N_EGROUPS = 4
EXPERTS_PER_GROUP = 4
N_EXPERTS = N_EGROUPS * EXPERTS_PER_GROUP
EPS = 1e-6
NEG_BIG = -1e30

LANES = 128
SMALL_W = 16
VMEM_LIMIT = 56 * 1024 * 1024


def _params(*sem):
    return pltpu.CompilerParams(dimension_semantics=sem, vmem_limit_bytes=VMEM_LIMIT)


def _split2(x):
    hi = x.astype(BF16)
    lo = (x - hi.astype(F32)).astype(BF16)
    return hi, lo


def _split3(x):
    hi = x.astype(BF16)
    r = x - hi.astype(F32)
    mid = r.astype(BF16)
    lo = (r - mid.astype(F32)).astype(BF16)
    return hi, mid, lo


def _dot(a, b):
    return jnp.dot(a, b, preferred_element_type=F32)


def _dot_nt(a, b):
    return lax.dot_general(a, b, (((1,), (1,)), ((), ())), preferred_element_type=F32)


def _dot_tn(a, b):
    return lax.dot_general(a, b, (((0,), (0,)), ((), ())), preferred_element_type=F32)


def _silu(x):
    return x / (1.0 + jnp.exp(-x))


def _rms(x, w):
    return x * lax.rsqrt(jnp.mean(x * x, axis=-1, keepdims=True) + EPS) * w


def _pick_tile(n, candidates):
    for c in candidates:
        if n % c == 0:
            return c
    return n


def _mod_kernel(c_ref, w_ref, b_ref, o_ref):
    a = _silu(c_ref[...])
    a_hi, a_lo = _split2(a)
    w_hi, w_lo = _split2(w_ref[...])
    o_ref[...] = _dot(a_hi, w_hi) + _dot(a_lo, w_hi) + _dot(a_hi, w_lo) + b_ref[...]


def _modulation(c, w, b):
    m, d = c.shape
    n = w.shape[1]
    tn = _pick_tile(n, (1024, 512, 256, 128))
    return pl.pallas_call(
        _mod_kernel,
        grid=(n // tn,),
        in_specs=[pl.BlockSpec((m, d), lambda j: (0, 0)),
                  pl.BlockSpec((d, tn), lambda j: (0, j)),
                  pl.BlockSpec((1, tn), lambda j: (0, j))],
        out_specs=pl.BlockSpec((m, tn), lambda j: (0, j)),
        out_shape=jax.ShapeDtypeStruct((m, n), F32),
        compiler_params=_params("parallel"),
        name="adaln_mod",
    )(c, w, b.reshape(1, n))


def _inproj_kernel(x_ref, sh_ref, sc_ref, nw_ref, wz_ref, wx_ref, wq_ref, wk_ref, wv_ref, ws_ref, bs_ref,
                   z_ref, xbc_ref, q_ref, k_ref, v_ref, kb_ref, vb_ref, sm_ref, smt_ref, tail_ref):
    l = pl.program_id(1)
    x = x_ref[0]
    h = (_rms(x, nw_ref[...]) * (1.0 + sc_ref[0]) + sh_ref[0]).astype(BF16)
    z_ref[0] = _dot(h, wz_ref[...]).astype(BF16)
    xbc = _dot(h, wx_ref[...])
    xbc_ref[0] = xbc.astype(BF16)
    q_ref[0] = _dot(h, wq_ref[...]).astype(BF16)
    k = _dot(h, wk_ref[...])
    k_ref[0] = k
    kb_ref[0] = k.astype(BF16)
    v = _dot(h, wv_ref[...])
    v_ref[0] = v
    vb_ref[0] = v.astype(BF16)
    s = _dot(h, ws_ref[...]) + bs_ref[...]
    t = jnp.log(1.0 + jnp.exp(-jnp.abs(s)))
    lane = lax.broadcasted_iota(jnp.int32, s.shape, 1)
    s = jnp.where(lane < SMALL_W // 2, jnp.maximum(s, 0.0) + t, jnp.minimum(s, 0.0) - t)
    sm_ref[0] = s[:, :SMALL_W]
    smt_ref[0] = s.T[:SMALL_W, :]

    @pl.when(l == pl.num_programs(1) - 1)
    def _():
        tl = xbc.shape[0]
        tail_ref[0] = xbc[tl - (CONV_W - 1):, :]


def _inproj(x, mod3, boff, norm_w, wz, wx, wq, wk, wv, ws, bs):
    b, l, d = x.shape
    tl = _pick_tile(l, (512, 256, 128, 64))
    nl = l // tl
    d_ssd, d_conv, d_att = wz.shape[1], wx.shape[1], wq.shape[1]
    row = lambda col: pl.BlockSpec((1, 1, d), lambda i, j, col=col: (i + boff, 0, col))
    full = lambda a: pl.BlockSpec(a.shape, lambda i, j: (0,) * a.ndim)
    tok = lambda w: pl.BlockSpec((1, tl, w), lambda i, j: (i, j, 0))
    out_shape = [
        jax.ShapeDtypeStruct((b, l, d_ssd), BF16),
        jax.ShapeDtypeStruct((b, l, d_conv), BF16),
        jax.ShapeDtypeStruct((b, l, d_att), BF16),
        jax.ShapeDtypeStruct((b, l, d_att), F32),
        jax.ShapeDtypeStruct((b, l, d_att), F32),
        jax.ShapeDtypeStruct((b, l, d_att), BF16),
        jax.ShapeDtypeStruct((b, l, d_att), BF16),
        jax.ShapeDtypeStruct((b, l, SMALL_W), F32),
        jax.ShapeDtypeStruct((b, SMALL_W, l), F32),
        jax.ShapeDtypeStruct((b, CONV_W - 1, d_conv), F32),
    ]
    out_specs = [tok(d_ssd), tok(d_conv), tok(d_att), tok(d_att), tok(d_att), tok(d_att), tok(d_att),
                 tok(SMALL_W), pl.BlockSpec((1, SMALL_W, tl), lambda i, j: (i, 0, j)),
                 pl.BlockSpec((1, CONV_W - 1, d_conv), lambda i, j: (i, 0, 0))]
    return pl.pallas_call(
        _inproj_kernel,
        grid=(b, nl),
        in_specs=[tok(d), row(0), row(1), full(norm_w), full(wz), full(wx), full(wq), full(wk), full(wv),
                  full(ws), full(bs)],
        out_specs=out_specs,
        out_shape=out_shape,
        compiler_params=_params("parallel", "arbitrary"),
        name="inproj",
    )(x, mod3, mod3, norm_w, wz, wx, wq, wk, wv, ws, bs)


LOG2E = 1.4426950408889634
AUG = 3


def _aug_select(n_heads):
    sq = np.zeros((AUG, n_heads, n_heads * HD_ATT), np.float32)
    sk = np.zeros((AUG, n_heads, n_heads * HD_ATT), np.float32)
    bq = np.zeros((1, n_heads * HD_ATT), np.float32)
    bk = np.zeros((1, n_heads * HD_ATT), np.float32)
    for h in range(n_heads):
        slot = (h ^ 1) * HD_ATT
        for c in range(AUG):
            sq[c, h, slot + c] = 1.0
            bq[0, slot + AUG + c] = 1.0
            bk[0, slot + c] = 1.0
            sk[c, h, slot + AUG + c] = -1.0
    return sq, sk, bq, bk


def _cumsum_kernel(x_ref, sq_ref, sk_ref, bq_ref, bk_ref, qa_ref, ka_ref, carry):
    @pl.when(pl.program_id(1) == 0)
    def _():
        carry[...] = jnp.zeros_like(carry)

    x = x_ref[0]
    tc = x.shape[0]
    r = lax.broadcasted_iota(jnp.int32, (tc, tc), 0)
    c = lax.broadcasted_iota(jnp.int32, (tc, tc), 1)
    lower = jnp.where(r >= c, 1.0, 0.0).astype(BF16)
    hi, mid, lo = _split3(x)
    cs = _dot(lower, hi) + _dot(lower, mid) + _dot(lower, lo) + carry[...]
    carry[...] = cs[tc - 1:, :]
    pieces = _split3(cs * LOG2E)
    qa = bq_ref[...]
    ka = bk_ref[...]
    for i in range(AUG):
        qa = qa + _dot(pieces[i], sq_ref[i])
        ka = ka + _dot(pieces[i], sk_ref[i])
    qa_ref[0] = qa.astype(BF16)
    ka_ref[0] = ka.astype(BF16)


def _forget_cumsum(logf):
    b, lk, h = logf.shape
    tc = _pick_tile(lk, (512, 384, 256, 128))
    sq, sk, bq, bk = _aug_select(h)
    d_att = h * HD_ATT
    full = lambda a: pl.BlockSpec(a.shape, lambda i, j: (0,) * a.ndim)
    out = jax.ShapeDtypeStruct((b, lk, d_att), BF16)
    return pl.pallas_call(
        _cumsum_kernel,
        grid=(b, lk // tc),
        in_specs=[pl.BlockSpec((1, tc, h), lambda i, j: (i, j, 0)), full(sq), full(sk), full(bq), full(bk)],
        out_specs=[pl.BlockSpec((1, tc, d_att), lambda i, j: (i, j, 0))] * 2,
        out_shape=[out, out],
        scratch_shapes=[pltpu.VMEM((1, h), F32)],
        compiler_params=_params("parallel", "arbitrary"),
        name="forget_cumsum",
    )(logf, jnp.asarray(sq, BF16), jnp.asarray(sk, BF16), jnp.asarray(bq), jnp.asarray(bk))


def _ssd_kernel(xbc_ref, z_ref, sm_ref, smt_ref, cprev_ref, sprev_ref, cw_ref, cb_ref, arow_ref, acol_ref,
                dskip_ref, nw_ref, y_ref, snew_ref, xpad, state, ybuf):
    l = pl.program_id(1)
    q = xbc_ref.shape[1]
    n_heads = state.shape[0]
    hg = n_heads // G_SSD
    d_ssd = n_heads * P_SSD
    top = 8

    @pl.when(l == 0)
    def _():
        xpad[0:top, :] = jnp.zeros((top, xpad.shape[1]), F32)
        xpad[top - (CONV_W - 1):top, :] = cprev_ref[0]
        state[...] = sprev_ref[0]

    xpad[top:top + q, :] = xbc_ref[0].astype(F32)
    cw = cw_ref[...]
    conv = cb_ref[...] + xpad[top - 3:top - 3 + q, :] * cw[0:1, :]
    for i in range(1, CONV_W):
        conv = conv + xpad[top - 3 + i:top - 3 + i + q, :] * cw[i:i + 1, :]
    tail = xpad[top + q - (CONV_W - 1):top + q, :]
    xpad[top - (CONV_W - 1):top, :] = tail
    act = _silu(conv)

    dt = sm_ref[0][:, :n_heads]
    dt_t = smt_ref[0][:n_heads, :]
    a_row = -jnp.exp(arow_ref[...])
    a_col = -jnp.exp(acol_ref[...])
    r = lax.broadcasted_iota(jnp.int32, (q, q), 0)
    c = lax.broadcasted_iota(jnp.int32, (q, q), 1)
    causal = r >= c
    lower = jnp.where(causal, 1.0, 0.0).astype(BF16)
    upper = jnp.where(r <= c, 1.0, 0.0).astype(BF16)
    h0, h1, h2 = _split3(dt * a_row)
    acum = _dot(lower, h0) + _dot(lower, h1) + _dot(lower, h2)
    t0, t1, t2 = _split3(dt_t * a_col)
    acum_t = _dot(t0, upper) + _dot(t1, upper) + _dot(t2, upper)
    a_last = acum[q - 1:q, :]
    e_cum = jnp.exp(acum)
    e_end = jnp.exp(a_last - acum)
    e_all = jnp.exp(a_last)
    dskip = dskip_ref[...]

    for g in range(G_SSD):
        bg = act[:, d_ssd + g * N_STATE:d_ssd + (g + 1) * N_STATE]
        cg = act[:, d_ssd + (G_SSD + g) * N_STATE:d_ssd + (G_SSD + g + 1) * N_STATE]
        cg_b = cg.astype(BF16)
        cb = _dot_nt(cg_b, bg.astype(BF16))
        for hh in range(hg):
            h = g * hg + hh
            xs = act[:, h * P_SSD:(h + 1) * P_SSD]
            xdt = xs * dt[:, h:h + 1]
            xdt_b = xdt.astype(BF16)
            seg = acum[:, h:h + 1] - acum_t[h:h + 1, :]
            decay = jnp.exp(jnp.where(causal, seg, NEG_BIG))
            y = _dot((cb * decay).astype(BF16), xdt_b)
            h_in = state[h]
            y = y + _dot_nt(cg_b, h_in.astype(BF16)) * e_cum[:, h:h + 1]
            bd = (bg * e_end[:, h:h + 1]).astype(BF16)
            state[h] = h_in * e_all[:, h:h + 1] + _dot_tn(xdt_b, bd)
            ybuf[:, h * P_SSD:(h + 1) * P_SSD] = y + dskip[:, h:h + 1] * xs

    yg = ybuf[...] * _silu(z_ref[0].astype(F32))
    y_ref[0] = _rms(yg, nw_ref[...]).astype(BF16)
    snew_ref[0] = state[...]


def _ssd(xbc, z, sm, smt, conv_prev, ssm_prev, conv_w, conv_b, a_log, d_skip, norm_w):
    b, l, d_conv = xbc.shape
    d_ssd = z.shape[2]
    n_heads = d_ssd // P_SSD
    q = _pick_tile(l, (256, 128, 64))
    tok = lambda w: pl.BlockSpec((1, q, w), lambda i, j: (i, j, 0))
    full = lambda a: pl.BlockSpec(a.shape, lambda i, j: (0,) * a.ndim)
    a_row = a_log.reshape(1, n_heads)
    a_col = a_log.reshape(n_heads, 1)
    conv_b = conv_b.reshape(1, d_conv)
    d_skip = d_skip.reshape(1, n_heads)
    norm_w = norm_w.reshape(1, d_ssd)
    state_spec = pl.BlockSpec((1, n_heads, P_SSD, N_STATE), lambda i, j: (i, 0, 0, 0))
    return pl.pallas_call(
        _ssd_kernel,
        grid=(b, l // q),
        in_specs=[tok(d_conv), tok(d_ssd), tok(SMALL_W), pl.BlockSpec((1, SMALL_W, q), lambda i, j: (i, 0, j)),
                  pl.BlockSpec((1, CONV_W - 1, d_conv), lambda i, j: (i, 0, 0)), state_spec,
                  full(conv_w), full(conv_b), full(a_row), full(a_col), full(d_skip), full(norm_w)],
        out_specs=[tok(d_ssd), state_spec],
        out_shape=[jax.ShapeDtypeStruct((b, l, d_ssd), BF16),
                   jax.ShapeDtypeStruct((b, n_heads, P_SSD, N_STATE), F32)],
        scratch_shapes=[pltpu.VMEM((q + 8, d_conv), F32),
                        pltpu.VMEM((n_heads, P_SSD, N_STATE), F32),
                        pltpu.VMEM((q, d_ssd), F32)],
        compiler_params=_params("parallel", "arbitrary"),
        name="ssd",
    )(xbc, z, sm, smt, conv_prev, ssm_prev, conv_w, conv_b, a_row, a_col, d_skip, norm_w)


def _attn_kernel(q_ref, qa_ref, k_ref, ka_ref, v_ref, o_ref, kk_s, m_s, l_s, acc_s, *, p0, tq, tk):
    i = pl.program_id(2)
    lane = lax.broadcasted_iota(jnp.int32, (1, 2 * HD_ATT), 1)
    first = lane < HD_ATT

    @pl.when(i == 0)
    def _():
        k = k_ref[0]
        ka = ka_ref[0]
        kk_s[0] = jnp.where(first, k, ka)
        kk_s[1] = jnp.where(first, ka, k)

    q = q_ref[0]
    qa = qa_ref[0]
    qq = (jnp.where(first, q, qa), jnp.where(first, qa, q))
    m_s[...] = jnp.full(m_s.shape, NEG_BIG, F32)
    l_s[...] = jnp.zeros(l_s.shape, F32)
    acc_s[...] = jnp.zeros(acc_s.shape, F32)
    nc = tk // LANES

    def step(j, masked):
        off = pl.multiple_of(j * tk, tk)
        v = v_ref[0, pl.ds(off, tk), :]
        if masked:
            q_pos = p0 + i * tq + lax.broadcasted_iota(jnp.int32, (tq, tk), 0)
            k_pos = j * tk + lax.broadcasted_iota(jnp.int32, (tq, tk), 1)
            visible = k_pos <= q_pos
        ps, alphas = [], []
        for hh in range(2):
            s = _dot_nt(qq[hh], kk_s[hh, pl.ds(off, tk), :])
            if masked:
                s = jnp.where(visible, s, NEG_BIG)
            cols = [s[:, c * LANES:(c + 1) * LANES] for c in range(nc)]
            m_cur = functools.reduce(jnp.maximum, cols)
            m_prev = m_s[hh]
            m_new = jnp.maximum(m_prev, jnp.max(m_cur, axis=1, keepdims=True))
            alpha = jnp.exp2(m_prev - m_new)
            p = [jnp.exp2(col - m_new) for col in cols]
            l_s[hh] = alpha * l_s[hh] + functools.reduce(jnp.add, p)
            m_s[hh] = m_new
            ps.extend(x.astype(BF16) for x in p)
            alphas.append(alpha)
        zero = jnp.zeros_like(v)
        vv = jnp.concatenate([jnp.where(first, v, zero), jnp.where(first, zero, v)], axis=0)
        pv = _dot(jnp.concatenate(ps, axis=1), vv)
        acc_s[...] = jnp.where(first, alphas[0], alphas[1]) * acc_s[...] + pv

    n_full = (p0 + i * tq + 1) // tk
    n_vis = (p0 + i * tq + tq - 1) // tk + 1

    def full_body(j, carry):
        step(j, False)
        return carry

    def masked_body(j, carry):
        step(j, True)
        return carry

    lax.fori_loop(0, n_full, full_body, 0)
    lax.fori_loop(n_full, n_vis, masked_body, 0)
    l0 = jnp.sum(l_s[0], axis=1, keepdims=True)
    l1 = jnp.sum(l_s[1], axis=1, keepdims=True)
    o_ref[0] = (acc_s[...] / jnp.where(first, l0, l1)).astype(o_ref.dtype)


def _attention(q, q_aug, k, k_aug, v, p0):
    b, lq, d_att = q.shape
    lk = k.shape[1]
    n_heads = d_att // HD_ATT
    tq = _pick_tile(lq, (512, 256, 128, 64))
    tk = _pick_tile(lk, (512, 256, 128))
    assert p0 % tq == 0 and lk >= p0 + lq
    pair = 2 * HD_ATT
    q_spec = pl.BlockSpec((1, tq, pair), lambda bi, hp, i: (bi, i, hp))
    qa_spec = pl.BlockSpec((1, tq, pair), lambda bi, hp, i: (bi, i + p0 // tq, hp))
    kv_spec = pl.BlockSpec((1, lk, pair), lambda bi, hp, i: (bi, 0, hp))
    return pl.pallas_call(
        functools.partial(_attn_kernel, p0=p0, tq=tq, tk=tk),
        grid=(b, n_heads // 2, lq // tq),
        in_specs=[q_spec, qa_spec, kv_spec, kv_spec, kv_spec],
        out_specs=q_spec,
        out_shape=jax.ShapeDtypeStruct((b, lq, d_att), BF16),
        scratch_shapes=[pltpu.VMEM((2, lk, pair), BF16), pltpu.VMEM((2, tq, LANES), F32),
                        pltpu.VMEM((2, tq, LANES), F32), pltpu.VMEM((tq, pair), F32)],
        compiler_params=_params("parallel", "parallel", "arbitrary"),
        name="fox_attention",
    )(q, q_aug, k, k_aug, v)


ROUTER_ROWS = 32
EXPERT_ROW0 = 8


def _outproj_kernel(ys_ref, oa_ref, x_ref, g1_ref, sh_ref, sc_ref, anw_ref, wos_ref, woa_ref, n2w_ref, wr_ref,
                    br_ref, x1_ref, h2_ref, wrow_ref, wt_s):
    ya = _rms(oa_ref[0].astype(F32), anw_ref[...]).astype(BF16)
    m = _dot(ys_ref[0], wos_ref[...]) + _dot(ya, woa_ref[...])
    x1 = x_ref[0] + g1_ref[0] * m
    x1_ref[0] = x1
    h2 = _rms(x1, n2w_ref[...]) * (1.0 + sc_ref[0]) + sh_ref[0]
    h_hi, h_lo = _split2(h2)
    h2_ref[0] = h_hi
    wr = wr_ref[...]
    p1 = _dot_nt(wr, h_hi)
    p2 = _dot_nt(wr[:ROUTER_ROWS], h_lo)
    logit = p1[:ROUTER_ROWS] + p1[ROUTER_ROWS:] + p2 + br_ref[...]

    lg = [logit[g:g + 1, :] for g in range(N_EGROUPS)]
    gmax = jnp.maximum(jnp.maximum(lg[0], lg[1]), jnp.maximum(lg[2], lg[3]))
    denom = sum(jnp.exp(x - gmax) for x in lg)
    p_sel = 1.0 / denom
    is_g = []
    taken = jnp.zeros_like(gmax) > 1.0
    for g in range(N_EGROUPS):
        hit = (lg[g] == gmax) & jnp.logical_not(taken)
        is_g.append(hit)
        taken = taken | hit
    le = []
    for e in range(EXPERTS_PER_GROUP):
        v = logit[EXPERT_ROW0 + 3 * EXPERTS_PER_GROUP + e:EXPERT_ROW0 + 3 * EXPERTS_PER_GROUP + e + 1, :]
        for g in range(N_EGROUPS - 2, -1, -1):
            r0 = EXPERT_ROW0 + g * EXPERTS_PER_GROUP + e
            v = jnp.where(is_g[g], logit[r0:r0 + 1, :], v)
        le.append(v)
    m1 = jnp.maximum(jnp.maximum(le[0], le[1]), jnp.maximum(le[2], le[3]))
    first = []
    taken = jnp.zeros_like(m1) > 1.0
    for e in range(EXPERTS_PER_GROUP):
        hit = (le[e] == m1) & jnp.logical_not(taken)
        first.append(hit)
        taken = taken | hit
    rest = [jnp.where(first[e], -jnp.inf, le[e]) for e in range(EXPERTS_PER_GROUP)]
    m2 = jnp.maximum(jnp.maximum(rest[0], rest[1]), jnp.maximum(rest[2], rest[3]))
    second = []
    taken = jnp.zeros_like(m1) > 1.0
    for e in range(EXPERTS_PER_GROUP):
        hit = (rest[e] == m2) & jnp.logical_not(taken)
        second.append(hit)
        taken = taken | hit
    e2 = jnp.exp(m2 - m1)
    w_a = p_sel / (1.0 + e2)
    w_b = w_a * e2
    wt_s[...] = jnp.zeros(wt_s.shape, F32)
    for g in range(N_EGROUPS):
        for e in range(EXPERTS_PER_GROUP):
            w = jnp.where(first[e], w_a, jnp.where(second[e], w_b, 0.0))
            r0 = g * EXPERTS_PER_GROUP + e
            wt_s[r0:r0 + 1, :] = jnp.where(is_g[g], w, 0.0)
    wrow_ref[...] = wt_s[...].T[:, :N_EXPERTS]


def _outproj(y_ssd, o_att, x, mod3, boff, attn_norm_w, wo_ssd, wo_att, norm2_w, wr, br):
    b, l, d = x.shape
    tl = _pick_tile(l, (512, 256, 128, 64))
    nl = l // tl
    d_ssd, d_att = y_ssd.shape[2], o_att.shape[2]
    row = lambda col: pl.BlockSpec((1, 1, d), lambda i, j, col=col: (i + boff, 0, col))
    full = lambda a: pl.BlockSpec(a.shape, lambda i, j: (0,) * a.ndim)
    tok = lambda w: pl.BlockSpec((1, tl, w), lambda i, j: (i, j, 0))
    return pl.pallas_call(
        _outproj_kernel,
        grid=(b, nl),
        in_specs=[tok(d_ssd), tok(d_att), tok(d), row(2), row(3), row(4), full(attn_norm_w), full(wo_ssd),
                  full(wo_att), full(norm2_w), full(wr), full(br)],
        out_specs=[tok(d), tok(d), pl.BlockSpec((tl, N_EXPERTS), lambda i, j: (i * nl + j, 0))],
        out_shape=[jax.ShapeDtypeStruct((b, l, d), F32), jax.ShapeDtypeStruct((b, l, d), BF16),
                   jax.ShapeDtypeStruct((b * l, N_EXPERTS), F32)],
        scratch_shapes=[pltpu.VMEM((LANES, tl), F32)],
        compiler_params=_params("parallel", "parallel"),
        name="outproj_router",
    )(y_ssd, o_att, x, mod3, mod3, mod3, attn_norm_w, wo_ssd, wo_att, norm2_w, wr, br)


def _moe_kernel(h_ref, w_ref, wg_ref, wu_ref, wd_ref, x1_ref, g2_ref, shf_ref, scf_ref, fnw_ref, y_ref, acc, *,
                final_norm):
    e = pl.program_id(1)

    @pl.when(e == 0)
    def _():
        acc[...] = jnp.zeros_like(acc)

    x = h_ref[...]
    w = w_ref[...]
    col = lax.broadcasted_iota(jnp.int32, w.shape, 1)
    w_e = jnp.sum(jnp.where(col == e, w, 0.0), axis=1, keepdims=True)
    hid = _silu(_dot(x, wg_ref[0])) * _dot(x, wu_ref[0]) * w_e
    acc[...] += _dot(hid.astype(BF16), wd_ref[0])

    @pl.when(e == pl.num_programs(1) - 1)
    def _():
        xo = x1_ref[...] + g2_ref[0] * acc[...]
        if final_norm:
            xo = _rms(xo, fnw_ref[...]) * (1.0 + scf_ref[0]) + shf_ref[0]
        y_ref[...] = xo


def _moe_final(h2, w_rows, wg, wu, wd, x1, mod3, modf3, boff, rows_per_batch, final_norm_w, final_norm):
    t, d = h2.shape
    d_e = wg.shape[2]
    tm = _pick_tile(rows_per_batch, (1024, 512, 256, 128, 64))
    per = rows_per_batch // tm
    row = lambda col: pl.BlockSpec((1, 1, d), lambda i, e, col=col: (i // per + boff, 0, col))
    return pl.pallas_call(
        functools.partial(_moe_kernel, final_norm=final_norm),
        grid=(t // tm, N_EXPERTS),
        in_specs=[pl.BlockSpec((tm, d), lambda i, e: (i, 0)),
                  pl.BlockSpec((tm, N_EXPERTS), lambda i, e: (i, 0)),
                  pl.BlockSpec((1, d, d_e), lambda i, e: (e, 0, 0)),
                  pl.BlockSpec((1, d, d_e), lambda i, e: (e, 0, 0)),
                  pl.BlockSpec((1, d_e, d), lambda i, e: (e, 0, 0)),
                  pl.BlockSpec((tm, d), lambda i, e: (i, 0)),
                  row(5), row(0), row(1),
                  pl.BlockSpec((1, d), lambda i, e: (0, 0))],
        out_specs=pl.BlockSpec((tm, d), lambda i, e: (i, 0)),
        out_shape=jax.ShapeDtypeStruct((t, d), F32),
        scratch_shapes=[pltpu.VMEM((tm, d), F32)],
        compiler_params=_params("parallel", "arbitrary"),
        name="moe_final",
    )(h2, w_rows, wg, wu, wd, x1, mod3, modf3, modf3, final_norm_w)


def _layer(x, mod3, modf3, boff, k_prev, v_prev, logf_prev, conv_prev, ssm_prev, p, final_norm_w, final_norm):
    b, l, d = x.shape
    z, xbc, q, k, v, k_b, v_b, sm, smt, conv_new = _inproj(
        x, mod3, boff, p["norm1_w"], p["wz"], p["wx"], p["wq"], p["wk"], p["wv"], p["ws"], p["bs"])
    n_heads_ssd = p["wz"].shape[1] // P_SSD
    y_ssd, ssm_new = _ssd(xbc, z, sm, smt, conv_prev, ssm_prev, p["conv_w"], p["conv_b"], p["a_log"],
                          p["d_skip"], p["ssd_norm_w"])
    n_heads = p["wq"].shape[1] // HD_ATT
    logf = sm[:, :, n_heads_ssd:n_heads_ssd + n_heads]
    if k_prev is None:
        p0, k_all, v_all, lf_all = 0, k_b, v_b, logf
    else:
        p0 = k_prev.shape[1]
        pad = (-(p0 + l)) % LANES
        zeros = lambda w, dt: jnp.zeros((b, pad, w), dt)
        k_all = jnp.concatenate([k_prev.astype(BF16), k_b, zeros(k_b.shape[2], BF16)], axis=1)
        v_all = jnp.concatenate([v_prev.astype(BF16), v_b, zeros(v_b.shape[2], BF16)], axis=1)
        lf_all = jnp.concatenate([logf_prev, logf, zeros(n_heads, F32)], axis=1)
    q_aug, k_aug = _forget_cumsum(lf_all)
    o_att = _attention(q, q_aug, k_all, k_aug, v_all, p0)
    x1, h2, w_rows = _outproj(y_ssd, o_att, x, mod3, boff, p["attn_norm_w"], p["wo_ssd"], p["wo_att"],
                              p["norm2_w"], p["wr"], p["br"])
    y = _moe_final(h2.reshape(b * l, d), w_rows, p["wg"], p["wu"], p["wd"], x1.reshape(b * l, d), mod3, modf3,
                   boff, l, final_norm_w, final_norm)
    return y.reshape(b, l, d), (k, v, logf, conv_new, ssm_new)


def kernel(x_prompt, x_sample, c_prompt, c_sample, cache_k, cache_v, cache_logf, state_conv, state_ssm, norm1_w, w_ada, b_ada, w_in, conv_w, conv_b, dt_bias, a_log, d_skip, ssd_norm_w, f_bias, attn_norm_w, w_out, norm2_w, w_rg, b_rg, w_re, b_re, w_gate, w_up, w_down, final_norm_w, w_ada_f, b_ada_f):
    depth = w_in.shape[0]
    bp, lp, d = x_prompt.shape
    bs = x_sample.shape[0]
    d_conv = conv_w.shape[2]
    d_ssd = ssd_norm_w.shape[1]
    d_att = attn_norm_w.shape[1]
    h_ssd = dt_bias.shape[1]
    h_att = f_bias.shape[1]
    assert h_ssd + h_att == SMALL_W and d_att // HD_ATT == h_att and d_ssd // P_SSD == h_ssd

    c_all = jnp.concatenate([c_prompt, c_sample], axis=0)
    modf3 = _modulation(c_all, w_ada_f, b_ada_f).reshape(bp + bs, 1, 2 * d)
    final_w = final_norm_w.reshape(1, d)

    i0 = d_ssd
    i1 = i0 + d_conv
    i2 = i1 + h_ssd
    i3 = i2 + d_att
    i4 = i3 + d_att
    i5 = i4 + d_att
    yp, ys = x_prompt, x_sample
    outs_p, outs_s = [], []
    for layer in range(depth):
        mod3 = _modulation(c_all, w_ada[layer], b_ada[layer]).reshape(bp + bs, 1, 6 * d)
        wi = w_in[layer]
        w_small = jnp.concatenate([wi[:, i1:i2], wi[:, i5:], jnp.zeros((d, LANES - SMALL_W), F32)], axis=1)
        b_small = jnp.concatenate([dt_bias[layer], f_bias[layer], jnp.zeros((LANES - SMALL_W,), F32)])
        wr = jnp.zeros((ROUTER_ROWS, d), F32)
        wr = wr.at[:N_EGROUPS].set(w_rg[layer].T)
        wr = wr.at[EXPERT_ROW0:EXPERT_ROW0 + N_EXPERTS].set(
            jnp.transpose(w_re[layer], (0, 2, 1)).reshape(N_EXPERTS, d))
        wr_hi, wr_lo = _split2(wr)
        br = jnp.zeros((ROUTER_ROWS,), F32)
        br = br.at[:N_EGROUPS].set(b_rg[layer])
        br = br.at[EXPERT_ROW0:EXPERT_ROW0 + N_EXPERTS].set(b_re[layer].reshape(N_EXPERTS))
        p = dict(
            norm1_w=norm1_w[layer].reshape(1, d),
            wz=wi[:, :i0].astype(BF16), wx=wi[:, i0:i1].astype(BF16),
            wq=(wi[:, i2:i3] * (LOG2E * HD_ATT ** -0.5)).astype(BF16),
            wk=wi[:, i3:i4].astype(BF16), wv=wi[:, i4:i5].astype(BF16),
            ws=w_small.astype(BF16), bs=b_small.reshape(1, LANES),
            conv_w=conv_w[layer], conv_b=conv_b[layer], a_log=a_log[layer], d_skip=d_skip[layer],
            ssd_norm_w=ssd_norm_w[layer], attn_norm_w=attn_norm_w[layer].reshape(1, d_att),
            wo_ssd=w_out[layer][:d_ssd].astype(BF16), wo_att=w_out[layer][d_ssd:].astype(BF16),
            norm2_w=norm2_w[layer].reshape(1, d),
            wr=jnp.concatenate([wr_hi, wr_lo], axis=0), br=br.reshape(ROUTER_ROWS, 1),
            wg=w_gate[layer].astype(BF16), wu=w_up[layer].astype(BF16), wd=w_down[layer].astype(BF16),
        )
        conv0 = jnp.zeros((bp, CONV_W - 1, d_conv), F32)
        ssm0 = jnp.zeros((bp, h_ssd, P_SSD, N_STATE), F32)
        last = layer == depth - 1
        yp, st_p = _layer(yp, mod3, modf3, 0, None, None, None, conv0, ssm0, p, final_w, last)
        ck = cache_k[layer].reshape(bs, -1, d_att)
        cv = cache_v[layer].reshape(bs, -1, d_att)
        ys, st_s = _layer(ys, mod3, modf3, bp, ck, cv, cache_logf[layer], state_conv[layer], state_ssm[layer], p,
                          final_w, last)
        outs_p.append(st_p)
        outs_s.append(st_s)

    def stack(outs, b, l):
        k = jnp.stack([o[0].reshape(b, l, h_att, HD_ATT) for o in outs])
        v = jnp.stack([o[1].reshape(b, l, h_att, HD_ATT) for o in outs])
        return (k, v, jnp.stack([o[2] for o in outs]), jnp.stack([o[3] for o in outs]),
                jnp.stack([o[4] for o in outs]))

    return (yp, ys) + stack(outs_p, bp, lp) + stack(outs_s, bs, x_sample.shape[1])
```

```python
import functools

import jax
import jax.numpy as jnp
import numpy as np
from jax import lax
from jax.experimental import pallas as pl
from jax.experimental.pallas import tpu as pltpu

F32 = jnp.float32
BF16 = jnp.bfloat16

P_SSD = 64
N_STATE = 64
G_SSD = 2
CONV_W = 4
HD_ATT = 64
N_EGROUPS = 4
EXPERTS_PER_GROUP = 4
N_EXPERTS = N_EGROUPS * EXPERTS_PER_GROUP
EPS = 1e-6
NEG_BIG = -1e30

LANES = 128
SEG_ALIGN = 16
SORT_TILE = 256
MOE_BLOCK = 256
SMALL_W = 16
VMEM_LIMIT = 56 * 1024 * 1024


def _params(*sem):
    return pltpu.CompilerParams(dimension_semantics=sem, vmem_limit_bytes=VMEM_LIMIT)


def _split2(x):
    hi = x.astype(BF16)
    lo = (x - hi.astype(F32)).astype(BF16)
    return hi, lo


def _split3(x):
    hi = x.astype(BF16)
    r = x - hi.astype(F32)
    mid = r.astype(BF16)
    lo = (r - mid.astype(F32)).astype(BF16)
    return hi, mid, lo


def _dot(a, b):
    return jnp.dot(a, b, preferred_element_type=F32)


def _dot_nt(a, b):
    return lax.dot_general(a, b, (((1,), (1,)), ((), ())), preferred_element_type=F32)


def _dot_tn(a, b):
    return lax.dot_general(a, b, (((0,), (0,)), ((), ())), preferred_element_type=F32)


def _silu(x):
    return x / (1.0 + jnp.exp(-x))


def _rms(x, w):
    return x * lax.rsqrt(jnp.mean(x * x, axis=-1, keepdims=True) + EPS) * w


def _pick_tile(n, candidates):
    for c in candidates:
        if n % c == 0:
            return c
    return n


def _mod_kernel(c_ref, w_ref, b_ref, o_ref):
    a = _silu(c_ref[...])
    a_hi, a_lo = _split2(a)
    w_hi, w_lo = _split2(w_ref[...])
    o_ref[...] = _dot(a_hi, w_hi) + _dot(a_lo, w_hi) + _dot(a_hi, w_lo) + b_ref[...]


def _modulation(c, w, b):
    m, d = c.shape
    n = w.shape[1]
    tn = _pick_tile(n, (1024, 512, 256, 128))
    return pl.pallas_call(
        _mod_kernel,
        grid=(n // tn,),
        in_specs=[pl.BlockSpec((m, d), lambda j: (0, 0)),
                  pl.BlockSpec((d, tn), lambda j: (0, j)),
                  pl.BlockSpec((1, tn), lambda j: (0, j))],
        out_specs=pl.BlockSpec((m, tn), lambda j: (0, j)),
        out_shape=jax.ShapeDtypeStruct((m, n), F32),
        compiler_params=_params("parallel"),
        name="adaln_mod",
    )(c, w, b.reshape(1, n))


def _inproj_kernel(x_ref, sh_ref, sc_ref, nw_ref, wz_ref, wx_ref, wq_ref, wk_ref, wv_ref, ws_ref, bs_ref,
                   z_ref, xbc_ref, q_ref, k_ref, v_ref, kb_ref, vb_ref, sm_ref, smt_ref, tail_ref):
    l = pl.program_id(1)
    x = x_ref[0]
    h = (_rms(x, nw_ref[...]) * (1.0 + sc_ref[0]) + sh_ref[0]).astype(BF16)
    z_ref[0] = _dot(h, wz_ref[...]).astype(BF16)
    xbc = _dot(h, wx_ref[...])
    xbc_ref[0] = xbc.astype(BF16)
    q_ref[0] = _dot(h, wq_ref[...]).astype(BF16)
    k = _dot(h, wk_ref[...])
    k_ref[0] = k
    kb_ref[0] = k.astype(BF16)
    v = _dot(h, wv_ref[...])
    v_ref[0] = v
    vb_ref[0] = v.astype(BF16)
    s = _dot(h, ws_ref[...]) + bs_ref[...]
    t = jnp.log(1.0 + jnp.exp(-jnp.abs(s)))
    lane = lax.broadcasted_iota(jnp.int32, s.shape, 1)
    s = jnp.where(lane < SMALL_W // 2, jnp.maximum(s, 0.0) + t, jnp.minimum(s, 0.0) - t)
    sm_ref[0] = s[:, :SMALL_W]
    smt_ref[0] = s.T[:SMALL_W, :]

    @pl.when(l == pl.num_programs(1) - 1)
    def _():
        tl = xbc.shape[0]
        tail_ref[0] = xbc[tl - (CONV_W - 1):, :]


def _inproj(x, mod3, boff, norm_w, wz, wx, wq, wk, wv, ws, bs):
    b, l, d = x.shape
    tl = _pick_tile(l, (512, 256, 128, 64))
    nl = l // tl
    d_ssd, d_conv, d_att = wz.shape[1], wx.shape[1], wq.shape[1]
    row = lambda col: pl.BlockSpec((1, 1, d), lambda i, j, col=col: (i + boff, 0, col))
    full = lambda a: pl.BlockSpec(a.shape, lambda i, j: (0,) * a.ndim)
    tok = lambda w: pl.BlockSpec((1, tl, w), lambda i, j: (i, j, 0))
    out_shape = [
        jax.ShapeDtypeStruct((b, l, d_ssd), BF16),
        jax.ShapeDtypeStruct((b, l, d_conv), BF16),
        jax.ShapeDtypeStruct((b, l, d_att), BF16),
        jax.ShapeDtypeStruct((b, l, d_att), F32),
        jax.ShapeDtypeStruct((b, l, d_att), F32),
        jax.ShapeDtypeStruct((b, l, d_att), BF16),
        jax.ShapeDtypeStruct((b, l, d_att), BF16),
        jax.ShapeDtypeStruct((b, l, SMALL_W), F32),
        jax.ShapeDtypeStruct((b, SMALL_W, l), F32),
        jax.ShapeDtypeStruct((b, CONV_W - 1, d_conv), F32),
    ]
    out_specs = [tok(d_ssd), tok(d_conv), tok(d_att), tok(d_att), tok(d_att), tok(d_att), tok(d_att),
                 tok(SMALL_W), pl.BlockSpec((1, SMALL_W, tl), lambda i, j: (i, 0, j)),
                 pl.BlockSpec((1, CONV_W - 1, d_conv), lambda i, j: (i, 0, 0))]
    return pl.pallas_call(
        _inproj_kernel,
        grid=(b, nl),
        in_specs=[tok(d), row(0), row(1), full(norm_w), full(wz), full(wx), full(wq), full(wk), full(wv),
                  full(ws), full(bs)],
        out_specs=out_specs,
        out_shape=out_shape,
        compiler_params=_params("parallel", "arbitrary"),
        name="inproj",
    )(x, mod3, mod3, norm_w, wz, wx, wq, wk, wv, ws, bs)


LOG2E = 1.4426950408889634
AUG = 3


def _aug_select(n_heads):
    sq = np.zeros((AUG, n_heads, n_heads * HD_ATT), np.float32)
    sk = np.zeros((AUG, n_heads, n_heads * HD_ATT), np.float32)
    bq = np.zeros((1, n_heads * HD_ATT), np.float32)
    bk = np.zeros((1, n_heads * HD_ATT), np.float32)
    for h in range(n_heads):
        slot = (h ^ 1) * HD_ATT
        for c in range(AUG):
            sq[c, h, slot + c] = 1.0
            bq[0, slot + AUG + c] = 1.0
            bk[0, slot + c] = 1.0
            sk[c, h, slot + AUG + c] = -1.0
    return sq, sk, bq, bk


def _cumsum_kernel(x_ref, sq_ref, sk_ref, bq_ref, bk_ref, qa_ref, ka_ref, carry):
    @pl.when(pl.program_id(1) == 0)
    def _():
        carry[...] = jnp.zeros_like(carry)

    x = x_ref[0]
    tc = x.shape[0]
    r = lax.broadcasted_iota(jnp.int32, (tc, tc), 0)
    c = lax.broadcasted_iota(jnp.int32, (tc, tc), 1)
    lower = jnp.where(r >= c, 1.0, 0.0).astype(BF16)
    hi, mid, lo = _split3(x)
    cs = _dot(lower, hi) + _dot(lower, mid) + _dot(lower, lo) + carry[...]
    carry[...] = cs[tc - 1:, :]
    pieces = _split3(cs * LOG2E)
    qa = bq_ref[...]
    ka = bk_ref[...]
    for i in range(AUG):
        qa = qa + _dot(pieces[i], sq_ref[i])
        ka = ka + _dot(pieces[i], sk_ref[i])
    qa_ref[0] = qa.astype(BF16)
    ka_ref[0] = ka.astype(BF16)


def _forget_cumsum(logf):
    b, lk, h = logf.shape
    tc = _pick_tile(lk, (512, 384, 256, 128))
    sq, sk, bq, bk = _aug_select(h)
    d_att = h * HD_ATT
    full = lambda a: pl.BlockSpec(a.shape, lambda i, j: (0,) * a.ndim)
    out = jax.ShapeDtypeStruct((b, lk, d_att), BF16)
    return pl.pallas_call(
        _cumsum_kernel,
        grid=(b, lk // tc),
        in_specs=[pl.BlockSpec((1, tc, h), lambda i, j: (i, j, 0)), full(sq), full(sk), full(bq), full(bk)],
        out_specs=[pl.BlockSpec((1, tc, d_att), lambda i, j: (i, j, 0))] * 2,
        out_shape=[out, out],
        scratch_shapes=[pltpu.VMEM((1, h), F32)],
        compiler_params=_params("parallel", "arbitrary"),
        name="forget_cumsum",
    )(logf, jnp.asarray(sq, BF16), jnp.asarray(sk, BF16), jnp.asarray(bq), jnp.asarray(bk))


def _ssd_kernel(xbc_ref, z_ref, sm_ref, smt_ref, cprev_ref, sprev_ref, cw_ref, cb_ref, arow_ref, acol_ref,
                dskip_ref, nw_ref, y_ref, snew_ref, xpad, state, ybuf):
    l = pl.program_id(1)
    q = xbc_ref.shape[1]
    n_heads = state.shape[0]
    hg = n_heads // G_SSD
    d_ssd = n_heads * P_SSD
    top = 8

    @pl.when(l == 0)
    def _():
        xpad[0:top, :] = jnp.zeros((top, xpad.shape[1]), F32)
        xpad[top - (CONV_W - 1):top, :] = cprev_ref[0]
        state[...] = sprev_ref[0]

    xpad[top:top + q, :] = xbc_ref[0].astype(F32)
    cw = cw_ref[...]
    conv = cb_ref[...] + xpad[top - 3:top - 3 + q, :] * cw[0:1, :]
    for i in range(1, CONV_W):
        conv = conv + xpad[top - 3 + i:top - 3 + i + q, :] * cw[i:i + 1, :]
    tail = xpad[top + q - (CONV_W - 1):top + q, :]
    xpad[top - (CONV_W - 1):top, :] = tail
    act = _silu(conv)

    dt = sm_ref[0][:, :n_heads]
    dt_t = smt_ref[0][:n_heads, :]
    a_row = -jnp.exp(arow_ref[...])
    a_col = -jnp.exp(acol_ref[...])
    r = lax.broadcasted_iota(jnp.int32, (q, q), 0)
    c = lax.broadcasted_iota(jnp.int32, (q, q), 1)
    causal = r >= c
    lower = jnp.where(causal, 1.0, 0.0).astype(BF16)
    upper = jnp.where(r <= c, 1.0, 0.0).astype(BF16)
    h0, h1, h2 = _split3(dt * a_row)
    acum = _dot(lower, h0) + _dot(lower, h1) + _dot(lower, h2)
    t0, t1, t2 = _split3(dt_t * a_col)
    acum_t = _dot(t0, upper) + _dot(t1, upper) + _dot(t2, upper)
    a_last = acum[q - 1:q, :]
    e_cum = jnp.exp(acum)
    e_end = jnp.exp(a_last - acum)
    e_all = jnp.exp(a_last)
    dskip = dskip_ref[...]

    for g in range(G_SSD):
        bg = act[:, d_ssd + g * N_STATE:d_ssd + (g + 1) * N_STATE]
        cg = act[:, d_ssd + (G_SSD + g) * N_STATE:d_ssd + (G_SSD + g + 1) * N_STATE]
        cg_b = cg.astype(BF16)
        cb = _dot_nt(cg_b, bg.astype(BF16))
        for hh in range(hg):
            h = g * hg + hh
            xs = act[:, h * P_SSD:(h + 1) * P_SSD]
            xdt = xs * dt[:, h:h + 1]
            xdt_b = xdt.astype(BF16)
            seg = acum[:, h:h + 1] - acum_t[h:h + 1, :]
            decay = jnp.exp(jnp.where(causal, seg, NEG_BIG))
            y = _dot((cb * decay).astype(BF16), xdt_b)
            h_in = state[h]
            y = y + _dot_nt(cg_b, h_in.astype(BF16)) * e_cum[:, h:h + 1]
            bd = (bg * e_end[:, h:h + 1]).astype(BF16)
            state[h] = h_in * e_all[:, h:h + 1] + _dot_tn(xdt_b, bd)
            ybuf[:, h * P_SSD:(h + 1) * P_SSD] = y + dskip[:, h:h + 1] * xs

    yg = ybuf[...] * _silu(z_ref[0].astype(F32))
    y_ref[0] = _rms(yg, nw_ref[...]).astype(BF16)
    snew_ref[0] = state[...]


def _ssd(xbc, z, sm, smt, conv_prev, ssm_prev, conv_w, conv_b, a_log, d_skip, norm_w):
    b, l, d_conv = xbc.shape
    d_ssd = z.shape[2]
    n_heads = d_ssd // P_SSD
    q = _pick_tile(l, (256, 128, 64))
    tok = lambda w: pl.BlockSpec((1, q, w), lambda i, j: (i, j, 0))
    full = lambda a: pl.BlockSpec(a.shape, lambda i, j: (0,) * a.ndim)
    a_row = a_log.reshape(1, n_heads)
    a_col = a_log.reshape(n_heads, 1)
    conv_b = conv_b.reshape(1, d_conv)
    d_skip = d_skip.reshape(1, n_heads)
    norm_w = norm_w.reshape(1, d_ssd)
    state_spec = pl.BlockSpec((1, n_heads, P_SSD, N_STATE), lambda i, j: (i, 0, 0, 0))
    return pl.pallas_call(
        _ssd_kernel,
        grid=(b, l // q),
        in_specs=[tok(d_conv), tok(d_ssd), tok(SMALL_W), pl.BlockSpec((1, SMALL_W, q), lambda i, j: (i, 0, j)),
                  pl.BlockSpec((1, CONV_W - 1, d_conv), lambda i, j: (i, 0, 0)), state_spec,
                  full(conv_w), full(conv_b), full(a_row), full(a_col), full(d_skip), full(norm_w)],
        out_specs=[tok(d_ssd), state_spec],
        out_shape=[jax.ShapeDtypeStruct((b, l, d_ssd), BF16),
                   jax.ShapeDtypeStruct((b, n_heads, P_SSD, N_STATE), F32)],
        scratch_shapes=[pltpu.VMEM((q + 8, d_conv), F32),
                        pltpu.VMEM((n_heads, P_SSD, N_STATE), F32),
                        pltpu.VMEM((q, d_ssd), F32)],
        compiler_params=_params("parallel", "arbitrary"),
        name="ssd",
    )(xbc, z, sm, smt, conv_prev, ssm_prev, conv_w, conv_b, a_row, a_col, d_skip, norm_w)


def _attn_kernel(q_ref, qa_ref, k_ref, ka_ref, v_ref, o_ref, kk_s, m_s, l_s, acc_s, *, p0, tq, tk):
    i = pl.program_id(2)
    lane = lax.broadcasted_iota(jnp.int32, (1, 2 * HD_ATT), 1)
    first = lane < HD_ATT

    @pl.when(i == 0)
    def _():
        k = k_ref[0]
        ka = ka_ref[0]
        kk_s[0] = jnp.where(first, k, ka)
        kk_s[1] = jnp.where(first, ka, k)

    q = q_ref[0]
    qa = qa_ref[0]
    qq = (jnp.where(first, q, qa), jnp.where(first, qa, q))
    m_s[...] = jnp.full(m_s.shape, NEG_BIG, F32)
    l_s[...] = jnp.zeros(l_s.shape, F32)
    acc_s[...] = jnp.zeros(acc_s.shape, F32)
    nc = tk // LANES

    def step(j, masked):
        off = pl.multiple_of(j * tk, tk)
        v = v_ref[0, pl.ds(off, tk), :]
        if masked:
            q_pos = p0 + i * tq + lax.broadcasted_iota(jnp.int32, (tq, tk), 0)
            k_pos = j * tk + lax.broadcasted_iota(jnp.int32, (tq, tk), 1)
            visible = k_pos <= q_pos
        ps, alphas = [], []
        for hh in range(2):
            s = _dot_nt(qq[hh], kk_s[hh, pl.ds(off, tk), :])
            if masked:
                s = jnp.where(visible, s, NEG_BIG)
            cols = [s[:, c * LANES:(c + 1) * LANES] for c in range(nc)]
            m_cur = functools.reduce(jnp.maximum, cols)
            m_prev = m_s[hh]
            m_new = jnp.maximum(m_prev, jnp.max(m_cur, axis=1, keepdims=True))
            alpha = jnp.exp2(m_prev - m_new)
            p = [jnp.exp2(col - m_new) for col in cols]
            l_s[hh] = alpha * l_s[hh] + functools.reduce(jnp.add, p)
            m_s[hh] = m_new
            ps.extend(x.astype(BF16) for x in p)
            alphas.append(alpha)
        zero = jnp.zeros_like(v)
        vv = jnp.concatenate([jnp.where(first, v, zero), jnp.where(first, zero, v)], axis=0)
        pv = _dot(jnp.concatenate(ps, axis=1), vv)
        acc_s[...] = jnp.where(first, alphas[0], alphas[1]) * acc_s[...] + pv

    n_full = (p0 + i * tq + 1) // tk
    n_vis = (p0 + i * tq + tq - 1) // tk + 1

    def full_body(j, carry):
        step(j, False)
        return carry

    def masked_body(j, carry):
        step(j, True)
        return carry

    lax.fori_loop(0, n_full, full_body, 0)
    lax.fori_loop(n_full, n_vis, masked_body, 0)
    l0 = jnp.sum(l_s[0], axis=1, keepdims=True)
    l1 = jnp.sum(l_s[1], axis=1, keepdims=True)
    o_ref[0] = (acc_s[...] / jnp.where(first, l0, l1)).astype(o_ref.dtype)


def _attention(q, q_aug, k, k_aug, v, p0):
    b, lq, d_att = q.shape
    lk = k.shape[1]
    n_heads = d_att // HD_ATT
    tq = _pick_tile(lq, (512, 256, 128, 64))
    tk = _pick_tile(lk, (512, 256, 128))
    assert p0 % tq == 0 and lk >= p0 + lq
    pair = 2 * HD_ATT
    q_spec = pl.BlockSpec((1, tq, pair), lambda bi, hp, i: (bi, i, hp))
    qa_spec = pl.BlockSpec((1, tq, pair), lambda bi, hp, i: (bi, i + p0 // tq, hp))
    kv_spec = pl.BlockSpec((1, lk, pair), lambda bi, hp, i: (bi, 0, hp))
    return pl.pallas_call(
        functools.partial(_attn_kernel, p0=p0, tq=tq, tk=tk),
        grid=(b, n_heads // 2, lq // tq),
        in_specs=[q_spec, qa_spec, kv_spec, kv_spec, kv_spec],
        out_specs=q_spec,
        out_shape=jax.ShapeDtypeStruct((b, lq, d_att), BF16),
        scratch_shapes=[pltpu.VMEM((2, lk, pair), BF16), pltpu.VMEM((2, tq, LANES), F32),
                        pltpu.VMEM((2, tq, LANES), F32), pltpu.VMEM((tq, pair), F32)],
        compiler_params=_params("parallel", "parallel", "arbitrary"),
        name="fox_attention",
    )(q, q_aug, k, k_aug, v)


ROUTER_ROWS = 32
EXPERT_ROW0 = 8


def _outproj_kernel(ys_ref, oa_ref, x_ref, g1_ref, sh_ref, sc_ref, anw_ref, wos_ref, woa_ref, n2w_ref, wr_ref,
                    br_ref, x1_ref, h2_ref, ld_ref, cnt_ref, wt_s, g_s):
    ya = _rms(oa_ref[0].astype(F32), anw_ref[...]).astype(BF16)
    m = _dot(ys_ref[0], wos_ref[...]) + _dot(ya, woa_ref[...])
    x1 = x_ref[0] + g1_ref[0] * m
    x1_ref[0] = x1
    h2 = _rms(x1, n2w_ref[...]) * (1.0 + sc_ref[0]) + sh_ref[0]
    h_hi, h_lo = _split2(h2)
    wr = wr_ref[...]
    p1 = _dot_nt(wr, h_hi)
    p2 = _dot_nt(wr[:ROUTER_ROWS], h_lo)
    logit = p1[:ROUTER_ROWS] + p1[ROUTER_ROWS:] + p2 + br_ref[...]

    lg = [logit[g:g + 1, :] for g in range(N_EGROUPS)]
    gmax = jnp.maximum(jnp.maximum(lg[0], lg[1]), jnp.maximum(lg[2], lg[3]))
    denom = sum(jnp.exp(x - gmax) for x in lg)
    p_sel = 1.0 / denom
    is_g = []
    taken = jnp.zeros_like(gmax) > 1.0
    for g in range(N_EGROUPS):
        hit = (lg[g] == gmax) & jnp.logical_not(taken)
        is_g.append(hit)
        taken = taken | hit
    le = []
    for e in range(EXPERTS_PER_GROUP):
        v = logit[EXPERT_ROW0 + 3 * EXPERTS_PER_GROUP + e:EXPERT_ROW0 + 3 * EXPERTS_PER_GROUP + e + 1, :]
        for g in range(N_EGROUPS - 2, -1, -1):
            r0 = EXPERT_ROW0 + g * EXPERTS_PER_GROUP + e
            v = jnp.where(is_g[g], logit[r0:r0 + 1, :], v)
        le.append(v)
    m1 = jnp.maximum(jnp.maximum(le[0], le[1]), jnp.maximum(le[2], le[3]))
    first = []
    taken = jnp.zeros_like(m1) > 1.0
    for e in range(EXPERTS_PER_GROUP):
        hit = (le[e] == m1) & jnp.logical_not(taken)
        first.append(hit)
        taken = taken | hit
    rest = [jnp.where(first[e], -jnp.inf, le[e]) for e in range(EXPERTS_PER_GROUP)]
    m2 = jnp.maximum(jnp.maximum(rest[0], rest[1]), jnp.maximum(rest[2], rest[3]))
    second = []
    taken = jnp.zeros_like(m1) > 1.0
    for e in range(EXPERTS_PER_GROUP):
        hit = (rest[e] == m2) & jnp.logical_not(taken)
        second.append(hit)
        taken = taken | hit
    e2 = jnp.exp(m2 - m1)
    w_a = p_sel / (1.0 + e2)
    w_b = w_a * e2
    d = x1.shape[1]
    wt_s[...] = jnp.zeros(wt_s.shape, F32)
    for e in range(EXPERTS_PER_GROUP):
        w = jnp.where(first[e], w_a, jnp.where(second[e], w_b, 0.0))
        w_hi = w.astype(BF16).astype(F32)
        wt_s[e:e + 1, :] = w_hi
        wt_s[EXPERTS_PER_GROUP + e:EXPERTS_PER_GROUP + e + 1, :] = w - w_hi
    h2_ref[:, :d] = h_hi
    h2_ref[:, d:] = wt_s[...].T.astype(BF16)

    tl = logit.shape[1]
    ts = min(SORT_TILE, tl)
    g_s[...] = jnp.zeros(g_s.shape, F32)
    for g in range(N_EGROUPS):
        g_s[g:g + 1, :] = jnp.where(is_g[g], 1.0, 0.0)
    r = lax.broadcasted_iota(jnp.int32, (tl, tl), 0)
    c = lax.broadcasted_iota(jnp.int32, (tl, tl), 1)
    same_tile = (r // ts) == (c // ts)
    upper = jnp.where((r <= c) & same_tile, 1.0, 0.0).astype(BF16)
    cum = _dot(g_s[...].astype(BF16), upper)
    lane = lax.broadcasted_iota(jnp.int32, (1, tl), 1)
    crow = lax.broadcasted_iota(jnp.int32, (8, LANES), 0)
    clane = lax.broadcasted_iota(jnp.int32, (8, LANES), 1)
    ldest = -1.0
    cnt = jnp.zeros((8, LANES), F32)
    for g in range(N_EGROUPS):
        ldest = ldest + jnp.where(is_g[g], cum[g:g + 1, :], 0.0)
    lo = [0.0] * (tl // ts)
    for g in range(N_EGROUPS):
        lo_row = jnp.zeros((1, tl), F32)
        for sub in range(tl // ts):
            n = cum[g:g + 1, (sub + 1) * ts - 1:(sub + 1) * ts]
            n_pad = jnp.ceil(n / SEG_ALIGN) * SEG_ALIGN
            lo_row = jnp.where(lane // ts == sub, lo[sub], lo_row)
            cnt = jnp.where((crow == sub) & (clane == g), n_pad, cnt)
            lo[sub] = lo[sub] + n_pad
        ldest = ldest + jnp.where(is_g[g], lo_row, 0.0)
    ld_ref[0] = ldest.astype(jnp.int32)
    cnt_ref[0] = cnt.astype(jnp.int32)


def _outproj(y_ssd, o_att, x, mod3, boff, attn_norm_w, wo_ssd, wo_att, norm2_w, wr, br):
    b, l, d = x.shape
    tl = _pick_tile(l, (512, 256, 128, 64))
    nl = l // tl
    d_ssd, d_att = y_ssd.shape[2], o_att.shape[2]
    row = lambda col: pl.BlockSpec((1, 1, d), lambda i, j, col=col: (i + boff, 0, col))
    full = lambda a: pl.BlockSpec(a.shape, lambda i, j: (0,) * a.ndim)
    tok = lambda w: pl.BlockSpec((1, tl, w), lambda i, j: (i, j, 0))
    return pl.pallas_call(
        _outproj_kernel,
        grid=(b, nl),
        in_specs=[tok(d_ssd), tok(d_att), tok(d), row(2), row(3), row(4), full(attn_norm_w), full(wo_ssd),
                  full(wo_att), full(norm2_w), full(wr), full(br)],
        out_specs=[tok(d), pl.BlockSpec((tl, d + LANES), lambda i, j: (i * nl + j, 0)),
                   pl.BlockSpec((1, 1, tl), lambda i, j: (i * nl + j, 0, 0)),
                   pl.BlockSpec((1, 8, LANES), lambda i, j: (i * nl + j, 0, 0))],
        out_shape=[jax.ShapeDtypeStruct((b, l, d), F32),
                   jax.ShapeDtypeStruct((b * l, d + LANES), BF16),
                   jax.ShapeDtypeStruct((b * nl, 1, tl), jnp.int32),
                   jax.ShapeDtypeStruct((b * nl, 8, LANES), jnp.int32)],
        scratch_shapes=[pltpu.VMEM((LANES, tl), F32), pltpu.VMEM((8, tl), F32)],
        compiler_params=_params("parallel", "parallel"),
        name="outproj_router",
    )(y_ssd, o_att, x, mod3, mod3, mod3, attn_norm_w, wo_ssd, wo_att, norm2_w, wr, br)


def _sort_plan(cnt, n_tiles, ts, chunk, n_blocks):
    ns = n_tiles // cnt.shape[0]
    n_pad = cnt[:, :ns, :N_EGROUPS].reshape(n_tiles, N_EGROUPS)
    lo = jnp.cumsum(n_pad, axis=1) - n_pad
    region = (jnp.sum(n_pad, axis=0) + chunk + MOE_BLOCK - 1) // MOE_BLOCK * MOE_BLOCK
    end = jnp.cumsum(region)
    off = (end - region)[None, :] + jnp.cumsum(n_pad, axis=0) - n_pad
    blk = jnp.arange(n_blocks, dtype=jnp.int32) * MOE_BLOCK
    blk_group = jnp.minimum(jnp.searchsorted(end, blk, side="right"), N_EGROUPS - 1).astype(jnp.int32)
    n_used = (end[-1] // MOE_BLOCK).astype(jnp.int32).reshape(1)
    flat = lambda a: a.astype(jnp.int32).reshape(-1)
    return flat(lo), flat(off), flat(n_pad), blk_group, n_used


def _sort_kernel(lo_ref, off_ref, np_ref, x_ref, ld_ref, init_ref, out_ref, cbuf, sems, *, ts, chunk):
    del init_ref
    i = pl.program_id(0)
    slot = i % 2
    rows = ts + N_EGROUPS * SEG_ALIGN

    @pl.when(i == 0)
    def _():
        cbuf[...] = jnp.zeros(cbuf.shape, BF16)

    r = lax.broadcasted_iota(jnp.int32, (rows, ts), 0)
    perm = jnp.where(r == ld_ref[0], 1.0, 0.0).astype(BF16)
    cbuf[slot, 0:rows, :] = _dot(perm, x_ref[...]).astype(BF16)

    def copies(step, sl):
        out = []
        for g in range(N_EGROUPS):
            lo = pl.multiple_of(lo_ref[step * N_EGROUPS + g], SEG_ALIGN)
            off = pl.multiple_of(off_ref[step * N_EGROUPS + g], SEG_ALIGN)
            n_pad = np_ref[step * N_EGROUPS + g]
            for c in range(ts // chunk):
                cp = pltpu.make_async_copy(cbuf.at[sl, pl.ds(lo + c * chunk, chunk)],
                                           out_ref.at[pl.ds(off + c * chunk, chunk)], sems.at[g, c])
                out.append((None if c == 0 else n_pad > c * chunk, cp))
        return out

    def for_each(step, sl, act):
        for pred, cp in copies(step, sl):
            if pred is None:
                act(cp)
            else:
                pl.when(pred)(functools.partial(act, cp))

    @pl.when(i > 0)
    def _():
        for_each(i - 1, 1 - slot, lambda cp: cp.wait())

    for_each(i, slot, lambda cp: cp.start())

    @pl.when(i == pl.num_programs(0) - 1)
    def _():
        for_each(i, slot, lambda cp: cp.wait())


def _experts_kernel(grp_ref, nb_ref, x_ref, wg_ref, wu_ref, wd_ref, o_ref):
    del grp_ref
    b = pl.program_id(0)
    d = o_ref.shape[1]

    @pl.when(b < nb_ref[0])
    def _():
        blk = x_ref[...]
        x = blk[:, :d]
        wp = blk[:, d:].astype(F32)
        acc = jnp.zeros(o_ref.shape, F32)
        for e in range(EXPERTS_PER_GROUP):
            w_e = wp[:, e:e + 1] + wp[:, EXPERTS_PER_GROUP + e:EXPERTS_PER_GROUP + e + 1]
            hid = _silu(_dot(x, wg_ref[e])) * _dot(x, wu_ref[e]) * w_e
            acc = acc + _dot(hid.astype(BF16), wd_ref[e])
        o_ref[...] = acc.astype(BF16)

    @pl.when(b >= nb_ref[0])
    def _():
        o_ref[...] = jnp.zeros(o_ref.shape, BF16)


def _combine_kernel(lo_ref, off_ref, np_ref, ld_ref, x1_ref, g2_ref, shf_ref, scf_ref, fnw_ref, src_ref, y_ref,
                    seg, sems, *, ts, final_norm):
    i = pl.program_id(0)
    slot = i % 2

    def fetch(step, sl):
        return [pltpu.make_async_copy(
            src_ref.at[pl.ds(pl.multiple_of(off_ref[step * N_EGROUPS + g], SEG_ALIGN), ts)],
            seg.at[sl, g], sems.at[sl, g]) for g in range(N_EGROUPS)]

    @pl.when(i == 0)
    def _():
        for cp in fetch(0, 0):
            cp.start()

    @pl.when(i + 1 < pl.num_programs(0))
    def _():
        for cp in fetch(i + 1, 1 - slot):
            cp.start()

    for cp in fetch(i, slot):
        cp.wait()

    ld = ld_ref[0]
    r = lax.broadcasted_iota(jnp.int32, (ts, ts), 0)
    moe = jnp.zeros(y_ref.shape, F32)
    for g in range(N_EGROUPS):
        lo = lo_ref[i * N_EGROUPS + g]
        n_pad = np_ref[i * N_EGROUPS + g]
        perm = jnp.where((r + lo == ld) & (r < n_pad), 1.0, 0.0).astype(BF16)
        moe = moe + _dot_tn(perm, seg[slot, g])
    xo = x1_ref[...] + g2_ref[0] * moe
    if final_norm:
        xo = _rms(xo, fnw_ref[...]) * (1.0 + scf_ref[0]) + shf_ref[0]
    y_ref[...] = xo


def _moe_final(h2ext, ldest, cnt, wg, wu, wd, x1, mod3, modf3, boff, rows_per_batch, final_norm_w, final_norm):
    t, d = x1.shape
    payload = h2ext.shape[1]
    ts = min(SORT_TILE, rows_per_batch)
    chunk = min(LANES, ts)
    n_tiles = t // ts
    ldest = ldest.reshape(n_tiles, 1, ts)
    n_blocks = -(-(t + n_tiles * N_EGROUPS * (SEG_ALIGN - 1) + N_EGROUPS * (chunk + MOE_BLOCK - 1)) // MOE_BLOCK) + 1
    lo, off, n_pad, blk_group, n_used = _sort_plan(cnt, n_tiles, ts, chunk, n_blocks)

    sorted_rows = pl.pallas_call(
        functools.partial(_sort_kernel, ts=ts, chunk=chunk),
        grid_spec=pltpu.PrefetchScalarGridSpec(
            num_scalar_prefetch=3,
            grid=(n_tiles,),
            in_specs=[pl.BlockSpec((ts, payload), lambda i, *_: (i, 0)),
                      pl.BlockSpec((1, 1, ts), lambda i, *_: (i, 0, 0)),
                      pl.BlockSpec(memory_space=pl.ANY)],
            out_specs=pl.BlockSpec(memory_space=pl.ANY),
            scratch_shapes=[pltpu.VMEM((2, ts + N_EGROUPS * SEG_ALIGN + ts, payload), BF16),
                            pltpu.SemaphoreType.DMA((N_EGROUPS, ts // chunk))]),
        out_shape=jax.ShapeDtypeStruct((n_blocks * MOE_BLOCK, payload), BF16),
        input_output_aliases={5: 0},
        compiler_params=_params("arbitrary"),
        name="moe_sort",
    )(lo, off, n_pad, h2ext, ldest, jnp.zeros((n_blocks * MOE_BLOCK, payload), BF16))

    d_e = wg.shape[2]
    live = lambda b, nb: jnp.minimum(b, nb[0] - 1)
    out_sorted = pl.pallas_call(
        _experts_kernel,
        grid_spec=pltpu.PrefetchScalarGridSpec(
            num_scalar_prefetch=2,
            grid=(n_blocks,),
            in_specs=[pl.BlockSpec((MOE_BLOCK, payload), lambda b, grp, nb: (live(b, nb), 0)),
                      pl.BlockSpec((EXPERTS_PER_GROUP, d, d_e), lambda b, grp, nb: (grp[b], 0, 0)),
                      pl.BlockSpec((EXPERTS_PER_GROUP, d, d_e), lambda b, grp, nb: (grp[b], 0, 0)),
                      pl.BlockSpec((EXPERTS_PER_GROUP, d_e, d), lambda b, grp, nb: (grp[b], 0, 0))],
            out_specs=pl.BlockSpec((MOE_BLOCK, d), lambda b, grp, nb: (b, 0))),
        out_shape=jax.ShapeDtypeStruct((n_blocks * MOE_BLOCK, d), BF16),
        compiler_params=_params("arbitrary"),
        name="moe_experts",
    )(blk_group, n_used, sorted_rows, wg, wu, wd)

    per = rows_per_batch // ts
    row = lambda col: pl.BlockSpec((1, 1, d), lambda i, *_, col=col: (i // per + boff, 0, col))
    return pl.pallas_call(
        functools.partial(_combine_kernel, ts=ts, final_norm=final_norm),
        grid_spec=pltpu.PrefetchScalarGridSpec(
            num_scalar_prefetch=3,
            grid=(n_tiles,),
            in_specs=[pl.BlockSpec((1, 1, ts), lambda i, *_: (i, 0, 0)),
                      pl.BlockSpec((ts, d), lambda i, *_: (i, 0)),
                      row(5), row(0), row(1),
                      pl.BlockSpec((1, d), lambda i, *_: (0, 0)),
                      pl.BlockSpec(memory_space=pl.ANY)],
            out_specs=pl.BlockSpec((ts, d), lambda i, *_: (i, 0)),
            scratch_shapes=[pltpu.VMEM((2, N_EGROUPS, ts, d), BF16),
                            pltpu.SemaphoreType.DMA((2, N_EGROUPS))]),
        out_shape=jax.ShapeDtypeStruct((t, d), F32),
        compiler_params=_params("arbitrary"),
        name="moe_combine",
    )(lo, off, n_pad, ldest, x1, mod3, modf3, modf3, final_norm_w, out_sorted)


def _layer(x, mod3, modf3, boff, k_prev, v_prev, logf_prev, conv_prev, ssm_prev, p, final_norm_w, final_norm):
    b, l, d = x.shape
    z, xbc, q, k, v, k_b, v_b, sm, smt, conv_new = _inproj(
        x, mod3, boff, p["norm1_w"], p["wz"], p["wx"], p["wq"], p["wk"], p["wv"], p["ws"], p["bs"])
    n_heads_ssd = p["wz"].shape[1] // P_SSD
    y_ssd, ssm_new = _ssd(xbc, z, sm, smt, conv_prev, ssm_prev, p["conv_w"], p["conv_b"], p["a_log"],
                          p["d_skip"], p["ssd_norm_w"])
    n_heads = p["wq"].shape[1] // HD_ATT
    logf = sm[:, :, n_heads_ssd:n_heads_ssd + n_heads]
    if k_prev is None:
        p0, k_all, v_all, lf_all = 0, k_b, v_b, logf
    else:
        p0 = k_prev.shape[1]
        pad = (-(p0 + l)) % LANES
        zeros = lambda w, dt: jnp.zeros((b, pad, w), dt)
        k_all = jnp.concatenate([k_prev.astype(BF16), k_b, zeros(k_b.shape[2], BF16)], axis=1)
        v_all = jnp.concatenate([v_prev.astype(BF16), v_b, zeros(v_b.shape[2], BF16)], axis=1)
        lf_all = jnp.concatenate([logf_prev, logf, zeros(n_heads, F32)], axis=1)
    q_aug, k_aug = _forget_cumsum(lf_all)
    o_att = _attention(q, q_aug, k_all, k_aug, v_all, p0)
    x1, h2ext, ldest, cnt = _outproj(y_ssd, o_att, x, mod3, boff, p["attn_norm_w"], p["wo_ssd"], p["wo_att"],
                                     p["norm2_w"], p["wr"], p["br"])
    y = _moe_final(h2ext, ldest, cnt, p["wg"], p["wu"], p["wd"], x1.reshape(b * l, d), mod3, modf3, boff, l,
                   final_norm_w, final_norm)
    return y.reshape(b, l, d), (k, v, logf, conv_new, ssm_new)


def kernel(x_prompt, x_sample, c_prompt, c_sample, cache_k, cache_v, cache_logf, state_conv, state_ssm, norm1_w, w_ada, b_ada, w_in, conv_w, conv_b, dt_bias, a_log, d_skip, ssd_norm_w, f_bias, attn_norm_w, w_out, norm2_w, w_rg, b_rg, w_re, b_re, w_gate, w_up, w_down, final_norm_w, w_ada_f, b_ada_f):
    depth = w_in.shape[0]
    bp, lp, d = x_prompt.shape
    bs = x_sample.shape[0]
    d_conv = conv_w.shape[2]
    d_ssd = ssd_norm_w.shape[1]
    d_att = attn_norm_w.shape[1]
    h_ssd = dt_bias.shape[1]
    h_att = f_bias.shape[1]
    assert h_ssd + h_att == SMALL_W and d_att // HD_ATT == h_att and d_ssd // P_SSD == h_ssd

    c_all = jnp.concatenate([c_prompt, c_sample], axis=0)
    modf3 = _modulation(c_all, w_ada_f, b_ada_f).reshape(bp + bs, 1, 2 * d)
    final_w = final_norm_w.reshape(1, d)

    i0 = d_ssd
    i1 = i0 + d_conv
    i2 = i1 + h_ssd
    i3 = i2 + d_att
    i4 = i3 + d_att
    i5 = i4 + d_att
    yp, ys = x_prompt, x_sample
    outs_p, outs_s = [], []
    for layer in range(depth):
        mod3 = _modulation(c_all, w_ada[layer], b_ada[layer]).reshape(bp + bs, 1, 6 * d)
        wi = w_in[layer]
        w_small = jnp.concatenate([wi[:, i1:i2], wi[:, i5:], jnp.zeros((d, LANES - SMALL_W), F32)], axis=1)
        b_small = jnp.concatenate([dt_bias[layer], f_bias[layer], jnp.zeros((LANES - SMALL_W,), F32)])
        wr = jnp.zeros((ROUTER_ROWS, d), F32)
        wr = wr.at[:N_EGROUPS].set(w_rg[layer].T)
        wr = wr.at[EXPERT_ROW0:EXPERT_ROW0 + N_EXPERTS].set(
            jnp.transpose(w_re[layer], (0, 2, 1)).reshape(N_EXPERTS, d))
        wr_hi, wr_lo = _split2(wr)
        br = jnp.zeros((ROUTER_ROWS,), F32)
        br = br.at[:N_EGROUPS].set(b_rg[layer])
        br = br.at[EXPERT_ROW0:EXPERT_ROW0 + N_EXPERTS].set(b_re[layer].reshape(N_EXPERTS))
        p = dict(
            norm1_w=norm1_w[layer].reshape(1, d),
            wz=wi[:, :i0].astype(BF16), wx=wi[:, i0:i1].astype(BF16),
            wq=(wi[:, i2:i3] * (LOG2E * HD_ATT ** -0.5)).astype(BF16),
            wk=wi[:, i3:i4].astype(BF16), wv=wi[:, i4:i5].astype(BF16),
            ws=w_small.astype(BF16), bs=b_small.reshape(1, LANES),
            conv_w=conv_w[layer], conv_b=conv_b[layer], a_log=a_log[layer], d_skip=d_skip[layer],
            ssd_norm_w=ssd_norm_w[layer], attn_norm_w=attn_norm_w[layer].reshape(1, d_att),
            wo_ssd=w_out[layer][:d_ssd].astype(BF16), wo_att=w_out[layer][d_ssd:].astype(BF16),
            norm2_w=norm2_w[layer].reshape(1, d),
            wr=jnp.concatenate([wr_hi, wr_lo], axis=0), br=br.reshape(ROUTER_ROWS, 1),
            wg=w_gate[layer].astype(BF16), wu=w_up[layer].astype(BF16), wd=w_down[layer].astype(BF16),
        )
        conv0 = jnp.zeros((bp, CONV_W - 1, d_conv), F32)
        ssm0 = jnp.zeros((bp, h_ssd, P_SSD, N_STATE), F32)
        last = layer == depth - 1
        yp, st_p = _layer(yp, mod3, modf3, 0, None, None, None, conv0, ssm0, p, final_w, last)
        ck = cache_k[layer].reshape(bs, -1, d_att)
        cv = cache_v[layer].reshape(bs, -1, d_att)
        ys, st_s = _layer(ys, mod3, modf3, bp, ck, cv, cache_logf[layer], state_conv[layer], state_ssm[layer], p,
                          final_w, last)
        outs_p.append(st_p)
        outs_s.append(st_s)

    def stack(outs, b, l):
        k = jnp.stack([o[0].reshape(b, l, h_att, HD_ATT) for o in outs])
        v = jnp.stack([o[1].reshape(b, l, h_att, HD_ATT) for o in outs])
        return (k, v, jnp.stack([o[2] for o in outs]), jnp.stack([o[3] for o in outs]),
                jnp.stack([o[4] for o in outs]))

    return (yp, ys) + stack(outs_p, bp, lp) + stack(outs_s, bs, x_sample.shape[1])
```

```python
import functools

import jax
import jax.numpy as jnp
import numpy as np
from jax import lax
from jax.experimental import pallas as pl
from jax.experimental.pallas import tpu as pltpu

F32 = jnp.float32
BF16 = jnp.bfloat16

P_SSD = 64
N_STATE = 64
G_SSD = 2
CONV_W = 4
HD_ATT = 64
N_EGROUPS = 4
EXPERTS_PER_GROUP = 4
N_EXPERTS = N_EGROUPS * EXPERTS_PER_GROUP
EPS = 1e-6
NEG_BIG = -1e30

LANES = 128
SEG_ALIGN = 16
SORT_TILE = 256
MOE_BLOCK = 256
MOE_CHUNK = 64
SMALL_W = 16
VMEM_LIMIT = 56 * 1024 * 1024


def _params(*sem):
    return pltpu.CompilerParams(dimension_semantics=sem, vmem_limit_bytes=VMEM_LIMIT)


def _split2(x):
    hi = x.astype(BF16)
    lo = (x - hi.astype(F32)).astype(BF16)
    return hi, lo


def _split3(x):
    hi = x.astype(BF16)
    r = x - hi.astype(F32)
    mid = r.astype(BF16)
    lo = (r - mid.astype(F32)).astype(BF16)
    return hi, mid, lo


def _dot(a, b):
    return jnp.dot(a, b, preferred_element_type=F32)


def _dot_nt(a, b):
    return lax.dot_general(a, b, (((1,), (1,)), ((), ())), preferred_element_type=F32)


def _dot_tn(a, b):
    return lax.dot_general(a, b, (((0,), (0,)), ((), ())), preferred_element_type=F32)


def _silu(x):
    return x / (1.0 + jnp.exp(-x))


def _rms(x, w):
    return x * lax.rsqrt(jnp.mean(x * x, axis=-1, keepdims=True) + EPS) * w


def _pick_tile(n, candidates):
    for c in candidates:
        if n % c == 0:
            return c
    return n


def _mod_kernel(c_ref, w_ref, b_ref, o_ref):
    a = _silu(c_ref[...])
    a_hi, a_lo = _split2(a)
    w_hi, w_lo = _split2(w_ref[...])
    o_ref[...] = _dot(a_hi, w_hi) + _dot(a_lo, w_hi) + _dot(a_hi, w_lo) + b_ref[...]


def _modulation(c, w, b):
    m, d = c.shape
    n = w.shape[1]
    tn = _pick_tile(n, (1024, 512, 256, 128))
    return pl.pallas_call(
        _mod_kernel,
        grid=(n // tn,),
        in_specs=[pl.BlockSpec((m, d), lambda j: (0, 0)),
                  pl.BlockSpec((d, tn), lambda j: (0, j)),
                  pl.BlockSpec((1, tn), lambda j: (0, j))],
        out_specs=pl.BlockSpec((m, tn), lambda j: (0, j)),
        out_shape=jax.ShapeDtypeStruct((m, n), F32),
        compiler_params=_params("parallel"),
        name="adaln_mod",
    )(c, w, b.reshape(1, n))


def _inproj_kernel(x_ref, sh_ref, sc_ref, nw_ref, wz_ref, wx_ref, wq_ref, wk_ref, wv_ref, ws_ref, bs_ref,
                   z_ref, xbc_ref, q_ref, k_ref, v_ref, kb_ref, vb_ref, sm_ref, smt_ref, tail_ref):
    l = pl.program_id(1)
    x = x_ref[0]
    h = (_rms(x, nw_ref[...]) * (1.0 + sc_ref[0]) + sh_ref[0]).astype(BF16)
    z_ref[0] = _dot(h, wz_ref[...]).astype(BF16)
    xbc = _dot(h, wx_ref[...])
    xbc_ref[0] = xbc.astype(BF16)
    q_ref[0] = _dot(h, wq_ref[...]).astype(BF16)
    k = _dot(h, wk_ref[...])
    k_ref[0] = k
    kb_ref[0] = k.astype(BF16)
    v = _dot(h, wv_ref[...])
    v_ref[0] = v
    vb_ref[0] = v.astype(BF16)
    s = _dot(h, ws_ref[...]) + bs_ref[...]
    t = jnp.log(1.0 + jnp.exp(-jnp.abs(s)))
    lane = lax.broadcasted_iota(jnp.int32, s.shape, 1)
    s = jnp.where(lane < SMALL_W // 2, jnp.maximum(s, 0.0) + t, jnp.minimum(s, 0.0) - t)
    sm_ref[0] = s[:, :SMALL_W]
    smt_ref[0] = s.T[:SMALL_W, :]

    @pl.when(l == pl.num_programs(1) - 1)
    def _():
        tl = xbc.shape[0]
        tail_ref[0] = xbc[tl - (CONV_W - 1):, :]


def _inproj(x, mod3, boff, norm_w, wz, wx, wq, wk, wv, ws, bs):
    b, l, d = x.shape
    tl = _pick_tile(l, (512, 256, 128, 64))
    nl = l // tl
    d_ssd, d_conv, d_att = wz.shape[1], wx.shape[1], wq.shape[1]
    row = lambda col: pl.BlockSpec((1, 1, d), lambda i, j, col=col: (i + boff, 0, col))
    full = lambda a: pl.BlockSpec(a.shape, lambda i, j: (0,) * a.ndim)
    tok = lambda w: pl.BlockSpec((1, tl, w), lambda i, j: (i, j, 0))
    out_shape = [
        jax.ShapeDtypeStruct((b, l, d_ssd), BF16),
        jax.ShapeDtypeStruct((b, l, d_conv), BF16),
        jax.ShapeDtypeStruct((b, l, d_att), BF16),
        jax.ShapeDtypeStruct((b, l, d_att), F32),
        jax.ShapeDtypeStruct((b, l, d_att), F32),
        jax.ShapeDtypeStruct((b, l, d_att), BF16),
        jax.ShapeDtypeStruct((b, l, d_att), BF16),
        jax.ShapeDtypeStruct((b, l, SMALL_W), F32),
        jax.ShapeDtypeStruct((b, SMALL_W, l), F32),
        jax.ShapeDtypeStruct((b, CONV_W - 1, d_conv), F32),
    ]
    out_specs = [tok(d_ssd), tok(d_conv), tok(d_att), tok(d_att), tok(d_att), tok(d_att), tok(d_att),
                 tok(SMALL_W), pl.BlockSpec((1, SMALL_W, tl), lambda i, j: (i, 0, j)),
                 pl.BlockSpec((1, CONV_W - 1, d_conv), lambda i, j: (i, 0, 0))]
    return pl.pallas_call(
        _inproj_kernel,
        grid=(b, nl),
        in_specs=[tok(d), row(0), row(1), full(norm_w), full(wz), full(wx), full(wq), full(wk), full(wv),
                  full(ws), full(bs)],
        out_specs=out_specs,
        out_shape=out_shape,
        compiler_params=_params("parallel", "arbitrary"),
        name="inproj",
    )(x, mod3, mod3, norm_w, wz, wx, wq, wk, wv, ws, bs)


LOG2E = 1.4426950408889634
AUG = 3


def _aug_select(n_heads):
    sq = np.zeros((AUG, n_heads, n_heads * HD_ATT), np.float32)
    sk = np.zeros((AUG, n_heads, n_heads * HD_ATT), np.float32)
    bq = np.zeros((1, n_heads * HD_ATT), np.float32)
    bk = np.zeros((1, n_heads * HD_ATT), np.float32)
    for h in range(n_heads):
        slot = (h ^ 1) * HD_ATT
        for c in range(AUG):
            sq[c, h, slot + c] = 1.0
            bq[0, slot + AUG + c] = 1.0
            bk[0, slot + c] = 1.0
            sk[c, h, slot + AUG + c] = -1.0
    return sq, sk, bq, bk


def _cumsum_kernel(x_ref, sel_ref, base_ref, qa_ref, ka_ref, carry):
    @pl.when(pl.program_id(1) == 0)
    def _():
        carry[...] = jnp.zeros_like(carry)

    x = x_ref[0]
    tc = x.shape[0]
    r = lax.broadcasted_iota(jnp.int32, (tc, tc), 0)
    c = lax.broadcasted_iota(jnp.int32, (tc, tc), 1)
    lower = jnp.where(r >= c, 1.0, 0.0).astype(BF16)
    h = x.shape[1]
    stack = lambda ps: jnp.concatenate([p.astype(F32) for p in ps], axis=1).astype(BF16)
    parts = _dot(lower, stack(_split3(x)))
    cs = parts[:, :h] + parts[:, h:2 * h] + parts[:, 2 * h:] + carry[...]
    carry[...] = cs[tc - 1:, :]
    aug = _dot(stack(_split3(cs * LOG2E)), sel_ref[...]) + base_ref[...]
    d_att = qa_ref.shape[2]
    qa_ref[0] = aug[:, :d_att].astype(BF16)
    ka_ref[0] = aug[:, d_att:].astype(BF16)


def _forget_cumsum(logf):
    b, lk, h = logf.shape
    tc = _pick_tile(lk, (512, 384, 256, 128))
    sq, sk, bq, bk = _aug_select(h)
    d_att = h * HD_ATT
    sel = jnp.asarray(np.concatenate([sq.reshape(AUG * h, d_att), sk.reshape(AUG * h, d_att)], axis=1), BF16)
    base = jnp.asarray(np.concatenate([bq, bk], axis=1))
    full = lambda a: pl.BlockSpec(a.shape, lambda i, j: (0,) * a.ndim)
    out = jax.ShapeDtypeStruct((b, lk, d_att), BF16)
    return pl.pallas_call(
        _cumsum_kernel,
        grid=(b, lk // tc),
        in_specs=[pl.BlockSpec((1, tc, h), lambda i, j: (i, j, 0)), full(sel), full(base)],
        out_specs=[pl.BlockSpec((1, tc, d_att), lambda i, j: (i, j, 0))] * 2,
        out_shape=[out, out],
        scratch_shapes=[pltpu.VMEM((1, h), F32)],
        compiler_params=_params("parallel", "arbitrary"),
        name="forget_cumsum",
    )(logf, sel, base)


def _ssd_spread(n_heads):
    e = np.zeros((2 * AUG * n_heads, 2 * n_heads * P_SSD), np.float32)
    for v in range(2):
        for c in range(AUG):
            for h in range(n_heads):
                col = v * n_heads * P_SSD + h * P_SSD
                e[(v * AUG + c) * n_heads + h, col:col + P_SSD] = 1.0
    return e


def _ssd_kernel(xbc_ref, z_ref, sm_ref, smt_ref, cprev_ref, sprev_ref, cw_ref, cb_ref, arow_ref, acol_ref,
                dskip_ref, nw_ref, spread_ref, y_ref, snew_ref, hist, state):
    l = pl.program_id(1)
    q = xbc_ref.shape[1]
    d_ssd = z_ref.shape[2]
    n_heads = d_ssd // P_SSD
    pair = 2 * P_SSD
    hist_rows = hist.shape[0]

    @pl.when(l == 0)
    def _():
        hist[...] = jnp.zeros(hist.shape, F32)
        hist[hist_rows - (CONV_W - 1):, :] = cprev_ref[0]
        state[...] = sprev_ref[0]

    xb = xbc_ref[0]
    r3 = lax.broadcasted_iota(jnp.int32, ((CONV_W - 1) * q, q), 0)
    c3 = lax.broadcasted_iota(jnp.int32, ((CONV_W - 1) * q, q), 1)
    shift = jnp.where(r3 % q - c3 == r3 // q + 1, 1.0, 0.0).astype(BF16)
    shifted = _dot(shift, xb)
    cw = cw_ref[...]
    conv = cb_ref[...] + xb.astype(F32) * cw[CONV_W - 1:CONV_W, :]
    for k in range(CONV_W - 1):
        conv = conv + shifted[k * q:(k + 1) * q, :] * cw[CONV_W - 2 - k:CONV_W - 1 - k, :]
    hrow = lambda j: hist[hist_rows - j:hist_rows - j + 1, :]
    frow = lax.broadcasted_iota(jnp.int32, (hist_rows, 1), 0)
    fix = jnp.zeros((hist_rows, conv.shape[1]), F32)
    for t in range(CONV_W - 1):
        acc = 0.0
        for j in range(1, CONV_W - t):
            acc = acc + hrow(j) * cw[CONV_W - 1 - t - j:CONV_W - t - j, :]
        fix = jnp.where(frow == t, acc, fix)
    conv = jnp.concatenate([conv[:hist_rows] + fix, conv[hist_rows:]], axis=0)
    hist[...] = xb[q - hist_rows:, :].astype(F32)
    act = _silu(conv)

    dt = sm_ref[0][:, :n_heads]
    dt_t = smt_ref[0][:n_heads, :]
    a_row = -jnp.exp(arow_ref[...]) * LOG2E
    a_col = -jnp.exp(acol_ref[...]) * LOG2E
    r = lax.broadcasted_iota(jnp.int32, (q, q), 0)
    c = lax.broadcasted_iota(jnp.int32, (q, q), 1)
    causal = r >= c
    lower = jnp.where(causal, 1.0, 0.0).astype(BF16)
    upper = jnp.where(r <= c, 1.0, 0.0).astype(BF16)
    h0, h1, h2 = _split3(dt * a_row)
    acum = _dot(lower, h0) + _dot(lower, h1) + _dot(lower, h2)
    t0, t1, t2 = _split3(dt_t * a_col)
    acum_t = _dot(t0, upper) + _dot(t1, upper) + _dot(t2, upper)
    pieces = [p.astype(F32) for p in _split3(acum) + _split3(dt)]
    wide = _dot(jnp.concatenate(pieces, axis=1).astype(BF16), spread_ref[...])
    acum_x = wide[:, :d_ssd]
    dt_x = wide[:, d_ssd:]
    a_last = acum[q - 1:q, :]
    e_all = jnp.exp2(a_last)
    e_cum_x = jnp.exp2(acum_x)
    e_end_x = jnp.exp2(acum_x[q - 1:q, :] - acum_x)

    xs = act[:, :d_ssd]
    bm = act[:, d_ssd:d_ssd + G_SSD * N_STATE]
    cm = act[:, d_ssd + G_SSD * N_STATE:]
    bm_b = bm.astype(BF16)
    cm_b = cm.astype(BF16)
    xdt = xs * dt_x
    xdt_b = xdt.astype(BF16)
    xe_b = (xdt * e_end_x).astype(BF16)
    lane = lax.broadcasted_iota(jnp.int32, (1, pair), 1)
    first = lane < P_SSD
    srow = lax.broadcasted_iota(jnp.int32, (pair, 1), 0)
    ys = []
    for p in range(n_heads // 2):
        g = (2 * p * G_SSD) // n_heads
        in_group = first if g == 0 else jnp.logical_not(first)
        sl = slice(p * pair, (p + 1) * pair)
        if (2 * p) % (n_heads // G_SSD) == 0:
            cb = _dot_nt(jnp.where(in_group, cm_b, jnp.zeros_like(cm_b)), bm_b)
        ms = []
        for hh in range(2):
            h = 2 * p + hh
            seg = acum[:, h:h + 1] - acum_t[h:h + 1, :]
            ms.append((cb * jnp.exp2(jnp.where(causal, seg, NEG_BIG))).astype(BF16))
        xp = xdt_b[:, sl]
        zero = jnp.zeros_like(xp)
        y = _dot(jnp.concatenate(ms, axis=1),
                 jnp.concatenate([jnp.where(first, xp, zero), jnp.where(first, zero, xp)], axis=0))
        s_in = state[sl, :]
        y = y + _dot_nt(cm_b, s_in.astype(BF16)) * e_cum_x[:, sl]
        upd = _dot_tn(xe_b[:, sl], bm_b)
        keep = jnp.where(srow < P_SSD, e_all[:, 2 * p:2 * p + 1], e_all[:, 2 * p + 1:2 * p + 2])
        state[sl, :] = s_in * keep + jnp.where(in_group, upd, 0.0)
        ys.append(y)
    y_all = jnp.concatenate(ys, axis=1) + dskip_ref[...] * xs
    yg = y_all * _silu(z_ref[0].astype(F32))
    y_ref[0] = _rms(yg, nw_ref[...]).astype(BF16)
    snew_ref[0] = state[...]


def _ssd(xbc, z, sm, smt, conv_prev, ssm_prev, conv_w, conv_b, a_log, d_skip, norm_w):
    b, l, d_conv = xbc.shape
    d_ssd = z.shape[2]
    n_heads = d_ssd // P_SSD
    hg = n_heads // G_SSD
    q = _pick_tile(l, (256, 128, 64))
    tok = lambda w: pl.BlockSpec((1, q, w), lambda i, j: (i, j, 0))
    full = lambda a: pl.BlockSpec(a.shape, lambda i, j: (0,) * a.ndim)
    a_row = a_log.reshape(1, n_heads)
    a_col = a_log.reshape(n_heads, 1)
    conv_b = conv_b.reshape(1, d_conv)
    d_skip = jnp.repeat(d_skip, P_SSD).reshape(1, d_ssd)
    norm_w = norm_w.reshape(1, d_ssd)
    spread = jnp.asarray(_ssd_spread(n_heads), BF16)
    s4 = ssm_prev.reshape(b, G_SSD, hg * P_SSD, N_STATE)
    s_in = jnp.concatenate([jnp.pad(s4[:, g], ((0, 0), (0, 0), (g * N_STATE, (G_SSD - 1 - g) * N_STATE)))
                            for g in range(G_SSD)], axis=1)
    state_spec = pl.BlockSpec((1, n_heads * P_SSD, G_SSD * N_STATE), lambda i, j: (i, 0, 0))
    y, s_out = pl.pallas_call(
        _ssd_kernel,
        grid=(b, l // q),
        in_specs=[tok(d_conv), tok(d_ssd), tok(SMALL_W), pl.BlockSpec((1, SMALL_W, q), lambda i, j: (i, 0, j)),
                  pl.BlockSpec((1, CONV_W - 1, d_conv), lambda i, j: (i, 0, 0)), state_spec,
                  full(conv_w), full(conv_b), full(a_row), full(a_col), full(d_skip), full(norm_w), full(spread)],
        out_specs=[tok(d_ssd), state_spec],
        out_shape=[jax.ShapeDtypeStruct((b, l, d_ssd), BF16),
                   jax.ShapeDtypeStruct((b, n_heads * P_SSD, G_SSD * N_STATE), F32)],
        scratch_shapes=[pltpu.VMEM((8, d_conv), F32),
                        pltpu.VMEM((n_heads * P_SSD, G_SSD * N_STATE), F32)],
        compiler_params=_params("parallel", "arbitrary"),
        name="ssd",
    )(xbc, z, sm, smt, conv_prev, s_in, conv_w, conv_b, a_row, a_col, d_skip, norm_w, spread)
    s_out = s_out.reshape(b, G_SSD, hg * P_SSD, G_SSD * N_STATE)
    s_new = jnp.concatenate([s_out[:, g, :, g * N_STATE:(g + 1) * N_STATE] for g in range(G_SSD)], axis=1)
    return y, s_new.reshape(b, n_heads, P_SSD, N_STATE)


def _attn_kernel(q_ref, qa_ref, k_ref, ka_ref, v_ref, o_ref, kk_s, m_s, l_s, acc_s, *, p0, tq, tk):
    i = pl.program_id(2)
    lane = lax.broadcasted_iota(jnp.int32, (1, 2 * HD_ATT), 1)
    first = lane < HD_ATT

    @pl.when(i == 0)
    def _():
        k = k_ref[0]
        ka = ka_ref[0]
        kk_s[0] = jnp.where(first, k, ka)
        kk_s[1] = jnp.where(first, ka, k)

    q = q_ref[0]
    qa = qa_ref[0]
    qq = (jnp.where(first, q, qa), jnp.where(first, qa, q))
    m_s[...] = jnp.full(m_s.shape, NEG_BIG, F32)
    l_s[...] = jnp.zeros(l_s.shape, F32)
    acc_s[...] = jnp.zeros(acc_s.shape, F32)
    nc = tk // LANES

    def step(j, masked):
        off = pl.multiple_of(j * tk, tk)
        v = v_ref[0, pl.ds(off, tk), :]
        if masked:
            q_pos = p0 + i * tq + lax.broadcasted_iota(jnp.int32, (tq, tk), 0)
            k_pos = j * tk + lax.broadcasted_iota(jnp.int32, (tq, tk), 1)
            visible = k_pos <= q_pos
        ps, alphas = [], []
        for hh in range(2):
            s = _dot_nt(qq[hh], kk_s[hh, pl.ds(off, tk), :])
            if masked:
                s = jnp.where(visible, s, NEG_BIG)
            cols = [s[:, c * LANES:(c + 1) * LANES] for c in range(nc)]
            m_cur = functools.reduce(jnp.maximum, cols)
            m_prev = m_s[hh]
            m_new = jnp.maximum(m_prev, jnp.max(m_cur, axis=1, keepdims=True))
            alpha = jnp.exp2(m_prev - m_new)
            p = [jnp.exp2(col - m_new) for col in cols]
            l_s[hh] = alpha * l_s[hh] + functools.reduce(jnp.add, p)
            m_s[hh] = m_new
            ps.extend(x.astype(BF16) for x in p)
            alphas.append(alpha)
        zero = jnp.zeros_like(v)
        vv = jnp.concatenate([jnp.where(first, v, zero), jnp.where(first, zero, v)], axis=0)
        pv = _dot(jnp.concatenate(ps, axis=1), vv)
        acc_s[...] = jnp.where(first, alphas[0], alphas[1]) * acc_s[...] + pv

    n_full = (p0 + i * tq + 1) // tk
    n_vis = (p0 + i * tq + tq - 1) // tk + 1

    def full_body(j, carry):
        step(j, False)
        return carry

    def masked_body(j, carry):
        step(j, True)
        return carry

    lax.fori_loop(0, n_full, full_body, 0)
    lax.fori_loop(n_full, n_vis, masked_body, 0)
    l0 = jnp.sum(l_s[0], axis=1, keepdims=True)
    l1 = jnp.sum(l_s[1], axis=1, keepdims=True)
    o_ref[0] = (acc_s[...] / jnp.where(first, l0, l1)).astype(o_ref.dtype)


def _attention(q, q_aug, k, k_aug, v, p0):
    b, lq, d_att = q.shape
    lk = k.shape[1]
    n_heads = d_att // HD_ATT
    tq = _pick_tile(lq, (512, 256, 128, 64))
    tk = lk if lk <= 1536 else _pick_tile(lk, (512, 256, 128))
    assert p0 % tq == 0 and lk >= p0 + lq
    pair = 2 * HD_ATT
    q_spec = pl.BlockSpec((1, tq, pair), lambda bi, hp, i: (bi, i, hp))
    qa_spec = pl.BlockSpec((1, tq, pair), lambda bi, hp, i: (bi, i + p0 // tq, hp))
    kv_spec = pl.BlockSpec((1, lk, pair), lambda bi, hp, i: (bi, 0, hp))
    return pl.pallas_call(
        functools.partial(_attn_kernel, p0=p0, tq=tq, tk=tk),
        grid=(b, n_heads // 2, lq // tq),
        in_specs=[q_spec, qa_spec, kv_spec, kv_spec, kv_spec],
        out_specs=q_spec,
        out_shape=jax.ShapeDtypeStruct((b, lq, d_att), BF16),
        scratch_shapes=[pltpu.VMEM((2, lk, pair), BF16), pltpu.VMEM((2, tq, LANES), F32),
                        pltpu.VMEM((2, tq, LANES), F32), pltpu.VMEM((tq, pair), F32)],
        compiler_params=_params("parallel", "parallel", "arbitrary"),
        name="fox_attention",
    )(q, q_aug, k, k_aug, v)


ROUTER_ROWS = 32
EXPERT_ROW0 = 8


def _outproj_kernel(ys_ref, oa_ref, x_ref, g1_ref, sh_ref, sc_ref, anw_ref, wos_ref, woa_ref, n2w_ref, wr_ref,
                    br_ref, x1_ref, h2_ref, ld_ref, cnt_ref, wt_s, g_s):
    ya = _rms(oa_ref[0].astype(F32), anw_ref[...]).astype(BF16)
    m = _dot(ys_ref[0], wos_ref[...]) + _dot(ya, woa_ref[...])
    x1 = x_ref[0] + g1_ref[0] * m
    x1_ref[0] = x1
    h2 = _rms(x1, n2w_ref[...]) * (1.0 + sc_ref[0]) + sh_ref[0]
    h_hi, h_lo = _split2(h2)
    wr = wr_ref[...]
    p1 = _dot_nt(wr, h_hi)
    p2 = _dot_nt(wr[:ROUTER_ROWS], h_lo)
    logit = p1[:ROUTER_ROWS] + p1[ROUTER_ROWS:] + p2 + br_ref[...]

    lg = [logit[g:g + 1, :] for g in range(N_EGROUPS)]
    gmax = jnp.maximum(jnp.maximum(lg[0], lg[1]), jnp.maximum(lg[2], lg[3]))
    denom = sum(jnp.exp(x - gmax) for x in lg)
    p_sel = 1.0 / denom
    is_g = []
    taken = jnp.zeros_like(gmax) > 1.0
    for g in range(N_EGROUPS):
        hit = (lg[g] == gmax) & jnp.logical_not(taken)
        is_g.append(hit)
        taken = taken | hit
    le = []
    for e in range(EXPERTS_PER_GROUP):
        v = logit[EXPERT_ROW0 + 3 * EXPERTS_PER_GROUP + e:EXPERT_ROW0 + 3 * EXPERTS_PER_GROUP + e + 1, :]
        for g in range(N_EGROUPS - 2, -1, -1):
            r0 = EXPERT_ROW0 + g * EXPERTS_PER_GROUP + e
            v = jnp.where(is_g[g], logit[r0:r0 + 1, :], v)
        le.append(v)
    m1 = jnp.maximum(jnp.maximum(le[0], le[1]), jnp.maximum(le[2], le[3]))
    first = []
    taken = jnp.zeros_like(m1) > 1.0
    for e in range(EXPERTS_PER_GROUP):
        hit = (le[e] == m1) & jnp.logical_not(taken)
        first.append(hit)
        taken = taken | hit
    rest = [jnp.where(first[e], -jnp.inf, le[e]) for e in range(EXPERTS_PER_GROUP)]
    m2 = jnp.maximum(jnp.maximum(rest[0], rest[1]), jnp.maximum(rest[2], rest[3]))
    second = []
    taken = jnp.zeros_like(m1) > 1.0
    for e in range(EXPERTS_PER_GROUP):
        hit = (rest[e] == m2) & jnp.logical_not(taken)
        second.append(hit)
        taken = taken | hit
    e2 = jnp.exp(m2 - m1)
    w_a = p_sel / (1.0 + e2)
    w_b = w_a * e2
    d = x1.shape[1]
    wt_s[...] = jnp.zeros(wt_s.shape, F32)
    for e in range(EXPERTS_PER_GROUP):
        w = jnp.where(first[e], w_a, jnp.where(second[e], w_b, 0.0))
        w_hi = w.astype(BF16).astype(F32)
        wt_s[e:e + 1, :] = w_hi
        wt_s[EXPERTS_PER_GROUP + e:EXPERTS_PER_GROUP + e + 1, :] = w - w_hi
    h2_ref[:, :d] = h_hi
    h2_ref[:, d:] = wt_s[...].T.astype(BF16)

    tl = logit.shape[1]
    ts = min(SORT_TILE, tl)
    g_s[...] = jnp.zeros(g_s.shape, F32)
    for g in range(N_EGROUPS):
        g_s[g:g + 1, :] = jnp.where(is_g[g], 1.0, 0.0)
    r = lax.broadcasted_iota(jnp.int32, (tl, tl), 0)
    c = lax.broadcasted_iota(jnp.int32, (tl, tl), 1)
    same_tile = (r // ts) == (c // ts)
    upper = jnp.where((r <= c) & same_tile, 1.0, 0.0).astype(BF16)
    cum = _dot(g_s[...].astype(BF16), upper)
    lane = lax.broadcasted_iota(jnp.int32, (1, tl), 1)
    crow = lax.broadcasted_iota(jnp.int32, (8, LANES), 0)
    clane = lax.broadcasted_iota(jnp.int32, (8, LANES), 1)
    ldest = -1.0
    cnt = jnp.zeros((8, LANES), F32)
    for g in range(N_EGROUPS):
        ldest = ldest + jnp.where(is_g[g], cum[g:g + 1, :], 0.0)
    lo = [0.0] * (tl // ts)
    for g in range(N_EGROUPS):
        lo_row = jnp.zeros((1, tl), F32)
        for sub in range(tl // ts):
            n = cum[g:g + 1, (sub + 1) * ts - 1:(sub + 1) * ts]
            n_pad = jnp.ceil(n / SEG_ALIGN) * SEG_ALIGN
            lo_row = jnp.where(lane // ts == sub, lo[sub], lo_row)
            cnt = jnp.where((crow == sub) & (clane == g), n_pad, cnt)
            lo[sub] = lo[sub] + n_pad
        ldest = ldest + jnp.where(is_g[g], lo_row, 0.0)
    ld_ref[0] = ldest.astype(jnp.int32)
    cnt_ref[0] = cnt.astype(jnp.int32)


def _outproj(y_ssd, o_att, x, mod3, boff, attn_norm_w, wo_ssd, wo_att, norm2_w, wr, br):
    b, l, d = x.shape
    tl = _pick_tile(l, (512, 256, 128, 64))
    nl = l // tl
    d_ssd, d_att = y_ssd.shape[2], o_att.shape[2]
    row = lambda col: pl.BlockSpec((1, 1, d), lambda i, j, col=col: (i + boff, 0, col))
    full = lambda a: pl.BlockSpec(a.shape, lambda i, j: (0,) * a.ndim)
    tok = lambda w: pl.BlockSpec((1, tl, w), lambda i, j: (i, j, 0))
    return pl.pallas_call(
        _outproj_kernel,
        grid=(b, nl),
        in_specs=[tok(d_ssd), tok(d_att), tok(d), row(2), row(3), row(4), full(attn_norm_w), full(wo_ssd),
                  full(wo_att), full(norm2_w), full(wr), full(br)],
        out_specs=[tok(d), pl.BlockSpec((tl, d + LANES), lambda i, j: (i * nl + j, 0)),
                   pl.BlockSpec((1, 1, tl), lambda i, j: (i * nl + j, 0, 0)),
                   pl.BlockSpec((1, 8, LANES), lambda i, j: (i * nl + j, 0, 0))],
        out_shape=[jax.ShapeDtypeStruct((b, l, d), F32),
                   jax.ShapeDtypeStruct((b * l, d + LANES), BF16),
                   jax.ShapeDtypeStruct((b * nl, 1, tl), jnp.int32),
                   jax.ShapeDtypeStruct((b * nl, 8, LANES), jnp.int32)],
        scratch_shapes=[pltpu.VMEM((LANES, tl), F32), pltpu.VMEM((8, tl), F32)],
        compiler_params=_params("parallel", "parallel"),
        name="outproj_router",
    )(y_ssd, o_att, x, mod3, mod3, mod3, attn_norm_w, wo_ssd, wo_att, norm2_w, wr, br)


def _sort_plan(cnt, n_tiles, ts, chunk, n_blocks):
    ns = n_tiles // cnt.shape[0]
    n_pad = cnt[:, :ns, :N_EGROUPS].reshape(n_tiles, N_EGROUPS)
    lo = jnp.cumsum(n_pad, axis=1) - n_pad
    region = (jnp.sum(n_pad, axis=0) + chunk + MOE_BLOCK - 1) // MOE_BLOCK * MOE_BLOCK
    end = jnp.cumsum(region)
    off = (end - region)[None, :] + jnp.cumsum(n_pad, axis=0) - n_pad
    blk = jnp.arange(n_blocks, dtype=jnp.int32) * MOE_BLOCK
    blk_group = jnp.minimum(jnp.searchsorted(end, blk, side="right"), N_EGROUPS - 1).astype(jnp.int32)
    n_used = (end[-1] // MOE_BLOCK).astype(jnp.int32).reshape(1)
    flat = lambda a: a.astype(jnp.int32).reshape(-1)
    return flat(lo), flat(off), flat(n_pad), blk_group, n_used


def _sort_kernel(lo_ref, off_ref, np_ref, x_ref, ld_ref, init_ref, out_ref, cbuf, sems, *, ts, chunk):
    del init_ref
    i = pl.program_id(0)
    slot = i % 2
    rows = ts + N_EGROUPS * SEG_ALIGN

    @pl.when(i == 0)
    def _():
        cbuf[...] = jnp.zeros(cbuf.shape, BF16)

    r = lax.broadcasted_iota(jnp.int32, (rows, ts), 0)
    perm = jnp.where(r == ld_ref[0], 1.0, 0.0).astype(BF16)
    cbuf[slot, 0:rows, :] = _dot(perm, x_ref[...]).astype(BF16)

    def copies(step, sl):
        out = []
        for g in range(N_EGROUPS):
            lo = pl.multiple_of(lo_ref[step * N_EGROUPS + g], SEG_ALIGN)
            off = pl.multiple_of(off_ref[step * N_EGROUPS + g], SEG_ALIGN)
            n_pad = np_ref[step * N_EGROUPS + g]
            for c in range(ts // chunk):
                cp = pltpu.make_async_copy(cbuf.at[sl, pl.ds(lo + c * chunk, chunk)],
                                           out_ref.at[pl.ds(off + c * chunk, chunk)], sems.at[g, c])
                out.append((None if c == 0 else n_pad > c * chunk, cp))
        return out

    def for_each(step, sl, act):
        for pred, cp in copies(step, sl):
            if pred is None:
                act(cp)
            else:
                pl.when(pred)(functools.partial(act, cp))

    @pl.when(i > 0)
    def _():
        for_each(i - 1, 1 - slot, lambda cp: cp.wait())

    for_each(i, slot, lambda cp: cp.start())

    @pl.when(i == pl.num_programs(0) - 1)
    def _():
        for_each(i, slot, lambda cp: cp.wait())


def _experts_kernel(grp_ref, nb_ref, x_ref, wg_ref, wu_ref, wd_ref, o_ref):
    del grp_ref
    b = pl.program_id(0)
    d = o_ref.shape[1]

    @pl.when(b < nb_ref[0])
    def _():
        blk = x_ref[...]
        x = blk[:, :d]
        wp = blk[:, d:].astype(F32)
        acc = jnp.zeros(o_ref.shape, F32)
        for e in range(EXPERTS_PER_GROUP):
            w_e = wp[:, e:e + 1] + wp[:, EXPERTS_PER_GROUP + e:EXPERTS_PER_GROUP + e + 1]
            hid = _silu(_dot(x, wg_ref[e])) * _dot(x, wu_ref[e]) * w_e
            acc = acc + _dot(hid.astype(BF16), wd_ref[e])
        o_ref[...] = acc.astype(BF16)

    @pl.when(b >= nb_ref[0])
    def _():
        o_ref[...] = jnp.zeros(o_ref.shape, BF16)


def _combine_kernel(lo_ref, off_ref, np_ref, ld_ref, x1_ref, g2_ref, shf_ref, scf_ref, fnw_ref, src_ref, y_ref,
                    seg, sems, *, ts, chunk, final_norm):
    i = pl.program_id(0)
    slot = i % 2

    def fetch(step, sl, act):
        for g in range(N_EGROUPS):
            off = pl.multiple_of(off_ref[step * N_EGROUPS + g], SEG_ALIGN)
            n_pad = np_ref[step * N_EGROUPS + g]
            for c in range(ts // chunk):
                cp = pltpu.make_async_copy(src_ref.at[pl.ds(off + c * chunk, chunk)],
                                           seg.at[sl, g, pl.ds(c * chunk, chunk)], sems.at[sl, g, c])
                if c == 0:
                    act(cp)
                else:
                    pl.when(n_pad > c * chunk)(functools.partial(act, cp))

    @pl.when(i == 0)
    def _():
        seg[...] = jnp.zeros(seg.shape, BF16)
        fetch(0, 0, lambda cp: cp.start())

    @pl.when(i + 1 < pl.num_programs(0))
    def _():
        fetch(i + 1, 1 - slot, lambda cp: cp.start())

    fetch(i, slot, lambda cp: cp.wait())

    ld = ld_ref[0]
    r = lax.broadcasted_iota(jnp.int32, (ts, ts), 0)
    moe = jnp.zeros(y_ref.shape, F32)
    for g in range(N_EGROUPS):
        lo = lo_ref[i * N_EGROUPS + g]
        n_pad = np_ref[i * N_EGROUPS + g]
        perm = jnp.where((r + lo == ld) & (r < n_pad), 1.0, 0.0).astype(BF16)
        moe = moe + _dot_tn(perm, seg[slot, g])
    xo = x1_ref[...] + g2_ref[0] * moe
    if final_norm:
        xo = _rms(xo, fnw_ref[...]) * (1.0 + scf_ref[0]) + shf_ref[0]
    y_ref[...] = xo


def _moe_final(h2ext, ldest, cnt, wg, wu, wd, x1, mod3, modf3, boff, rows_per_batch, final_norm_w, final_norm):
    t, d = x1.shape
    payload = h2ext.shape[1]
    ts = min(SORT_TILE, rows_per_batch)
    chunk = min(MOE_CHUNK, ts)
    n_tiles = t // ts
    ldest = ldest.reshape(n_tiles, 1, ts)
    n_blocks = -(-(t + n_tiles * N_EGROUPS * (SEG_ALIGN - 1) + N_EGROUPS * (chunk + MOE_BLOCK - 1)) // MOE_BLOCK) + 1
    lo, off, n_pad, blk_group, n_used = _sort_plan(cnt, n_tiles, ts, chunk, n_blocks)

    sorted_rows = pl.pallas_call(
        functools.partial(_sort_kernel, ts=ts, chunk=chunk),
        grid_spec=pltpu.PrefetchScalarGridSpec(
            num_scalar_prefetch=3,
            grid=(n_tiles,),
            in_specs=[pl.BlockSpec((ts, payload), lambda i, *_: (i, 0)),
                      pl.BlockSpec((1, 1, ts), lambda i, *_: (i, 0, 0)),
                      pl.BlockSpec(memory_space=pl.ANY)],
            out_specs=pl.BlockSpec(memory_space=pl.ANY),
            scratch_shapes=[pltpu.VMEM((2, ts + N_EGROUPS * SEG_ALIGN + ts, payload), BF16),
                            pltpu.SemaphoreType.DMA((N_EGROUPS, ts // chunk))]),
        out_shape=jax.ShapeDtypeStruct((n_blocks * MOE_BLOCK, payload), BF16),
        input_output_aliases={5: 0},
        compiler_params=_params("arbitrary"),
        name="moe_sort",
    )(lo, off, n_pad, h2ext, ldest, jnp.zeros((n_blocks * MOE_BLOCK, payload), BF16))

    d_e = wg.shape[2]
    live = lambda b, nb: jnp.minimum(b, nb[0] - 1)
    out_sorted = pl.pallas_call(
        _experts_kernel,
        grid_spec=pltpu.PrefetchScalarGridSpec(
            num_scalar_prefetch=2,
            grid=(n_blocks,),
            in_specs=[pl.BlockSpec((MOE_BLOCK, payload), lambda b, grp, nb: (live(b, nb), 0)),
                      pl.BlockSpec((EXPERTS_PER_GROUP, d, d_e), lambda b, grp, nb: (grp[b], 0, 0)),
                      pl.BlockSpec((EXPERTS_PER_GROUP, d, d_e), lambda b, grp, nb: (grp[b], 0, 0)),
                      pl.BlockSpec((EXPERTS_PER_GROUP, d_e, d), lambda b, grp, nb: (grp[b], 0, 0))],
            out_specs=pl.BlockSpec((MOE_BLOCK, d), lambda b, grp, nb: (b, 0))),
        out_shape=jax.ShapeDtypeStruct((n_blocks * MOE_BLOCK, d), BF16),
        compiler_params=_params("arbitrary"),
        name="moe_experts",
    )(blk_group, n_used, sorted_rows, wg, wu, wd)

    per = rows_per_batch // ts
    row = lambda col: pl.BlockSpec((1, 1, d), lambda i, *_, col=col: (i // per + boff, 0, col))
    return pl.pallas_call(
        functools.partial(_combine_kernel, ts=ts, chunk=chunk, final_norm=final_norm),
        grid_spec=pltpu.PrefetchScalarGridSpec(
            num_scalar_prefetch=3,
            grid=(n_tiles,),
            in_specs=[pl.BlockSpec((1, 1, ts), lambda i, *_: (i, 0, 0)),
                      pl.BlockSpec((ts, d), lambda i, *_: (i, 0)),
                      row(5), row(0), row(1),
                      pl.BlockSpec((1, d), lambda i, *_: (0, 0)),
                      pl.BlockSpec(memory_space=pl.ANY)],
            out_specs=pl.BlockSpec((ts, d), lambda i, *_: (i, 0)),
            scratch_shapes=[pltpu.VMEM((2, N_EGROUPS, ts, d), BF16),
                            pltpu.SemaphoreType.DMA((2, N_EGROUPS, ts // chunk))]),
        out_shape=jax.ShapeDtypeStruct((t, d), F32),
        compiler_params=_params("arbitrary"),
        name="moe_combine",
    )(lo, off, n_pad, ldest, x1, mod3, modf3, modf3, final_norm_w, out_sorted)


def _layer(x, mod3, modf3, boff, k_prev, v_prev, logf_prev, conv_prev, ssm_prev, p, final_norm_w, final_norm):
    b, l, d = x.shape
    z, xbc, q, k, v, k_b, v_b, sm, smt, conv_new = _inproj(
        x, mod3, boff, p["norm1_w"], p["wz"], p["wx"], p["wq"], p["wk"], p["wv"], p["ws"], p["bs"])
    n_heads_ssd = p["wz"].shape[1] // P_SSD
    y_ssd, ssm_new = _ssd(xbc, z, sm, smt, conv_prev, ssm_prev, p["conv_w"], p["conv_b"], p["a_log"],
                          p["d_skip"], p["ssd_norm_w"])
    n_heads = p["wq"].shape[1] // HD_ATT
    logf = sm[:, :, n_heads_ssd:n_heads_ssd + n_heads]
    if k_prev is None:
        p0, k_all, v_all, lf_all = 0, k_b, v_b, logf
    else:
        p0 = k_prev.shape[1]
        pad = (-(p0 + l)) % LANES
        zeros = lambda w, dt: jnp.zeros((b, pad, w), dt)
        k_all = jnp.concatenate([k_prev.astype(BF16), k_b, zeros(k_b.shape[2], BF16)], axis=1)
        v_all = jnp.concatenate([v_prev.astype(BF16), v_b, zeros(v_b.shape[2], BF16)], axis=1)
        lf_all = jnp.concatenate([logf_prev, logf, zeros(n_heads, F32)], axis=1)
    q_aug, k_aug = _forget_cumsum(lf_all)
    o_att = _attention(q, q_aug, k_all, k_aug, v_all, p0)
    x1, h2ext, ldest, cnt = _outproj(y_ssd, o_att, x, mod3, boff, p["attn_norm_w"], p["wo_ssd"], p["wo_att"],
                                     p["norm2_w"], p["wr"], p["br"])
    y = _moe_final(h2ext, ldest, cnt, p["wg"], p["wu"], p["wd"], x1.reshape(b * l, d), mod3, modf3, boff, l,
                   final_norm_w, final_norm)
    return y.reshape(b, l, d), (k, v, logf, conv_new, ssm_new)


def kernel(x_prompt, x_sample, c_prompt, c_sample, cache_k, cache_v, cache_logf, state_conv, state_ssm, norm1_w, w_ada, b_ada, w_in, conv_w, conv_b, dt_bias, a_log, d_skip, ssd_norm_w, f_bias, attn_norm_w, w_out, norm2_w, w_rg, b_rg, w_re, b_re, w_gate, w_up, w_down, final_norm_w, w_ada_f, b_ada_f):
    depth = w_in.shape[0]
    bp, lp, d = x_prompt.shape
    bs = x_sample.shape[0]
    d_conv = conv_w.shape[2]
    d_ssd = ssd_norm_w.shape[1]
    d_att = attn_norm_w.shape[1]
    h_ssd = dt_bias.shape[1]
    h_att = f_bias.shape[1]
    assert h_ssd + h_att == SMALL_W and d_att // HD_ATT == h_att and d_ssd // P_SSD == h_ssd

    c_all = jnp.concatenate([c_prompt, c_sample], axis=0)
    modf3 = _modulation(c_all, w_ada_f, b_ada_f).reshape(bp + bs, 1, 2 * d)
    final_w = final_norm_w.reshape(1, d)

    i0 = d_ssd
    i1 = i0 + d_conv
    i2 = i1 + h_ssd
    i3 = i2 + d_att
    i4 = i3 + d_att
    i5 = i4 + d_att
    yp, ys = x_prompt, x_sample
    outs_p, outs_s = [], []
    for layer in range(depth):
        mod3 = _modulation(c_all, w_ada[layer], b_ada[layer]).reshape(bp + bs, 1, 6 * d)
        wi = w_in[layer]
        w_small = jnp.concatenate([wi[:, i1:i2], wi[:, i5:], jnp.zeros((d, LANES - SMALL_W), F32)], axis=1)
        b_small = jnp.concatenate([dt_bias[layer], f_bias[layer], jnp.zeros((LANES - SMALL_W,), F32)])
        wr = jnp.zeros((ROUTER_ROWS, d), F32)
        wr = wr.at[:N_EGROUPS].set(w_rg[layer].T)
        wr = wr.at[EXPERT_ROW0:EXPERT_ROW0 + N_EXPERTS].set(
            jnp.transpose(w_re[layer], (0, 2, 1)).reshape(N_EXPERTS, d))
        wr_hi, wr_lo = _split2(wr)
        br = jnp.zeros((ROUTER_ROWS,), F32)
        br = br.at[:N_EGROUPS].set(b_rg[layer])
        br = br.at[EXPERT_ROW0:EXPERT_ROW0 + N_EXPERTS].set(b_re[layer].reshape(N_EXPERTS))
        p = dict(
            norm1_w=norm1_w[layer].reshape(1, d),
            wz=wi[:, :i0].astype(BF16), wx=wi[:, i0:i1].astype(BF16),
            wq=(wi[:, i2:i3] * (LOG2E * HD_ATT ** -0.5)).astype(BF16),
            wk=wi[:, i3:i4].astype(BF16), wv=wi[:, i4:i5].astype(BF16),
            ws=w_small.astype(BF16), bs=b_small.reshape(1, LANES),
            conv_w=conv_w[layer], conv_b=conv_b[layer], a_log=a_log[layer], d_skip=d_skip[layer],
            ssd_norm_w=ssd_norm_w[layer], attn_norm_w=attn_norm_w[layer].reshape(1, d_att),
            wo_ssd=w_out[layer][:d_ssd].astype(BF16), wo_att=w_out[layer][d_ssd:].astype(BF16),
            norm2_w=norm2_w[layer].reshape(1, d),
            wr=jnp.concatenate([wr_hi, wr_lo], axis=0), br=br.reshape(ROUTER_ROWS, 1),
            wg=w_gate[layer].astype(BF16), wu=w_up[layer].astype(BF16), wd=w_down[layer].astype(BF16),
        )
        conv0 = jnp.zeros((bp, CONV_W - 1, d_conv), F32)
        ssm0 = jnp.zeros((bp, h_ssd, P_SSD, N_STATE), F32)
        last = layer == depth - 1
        yp, st_p = _layer(yp, mod3, modf3, 0, None, None, None, conv0, ssm0, p, final_w, last)
        ck = cache_k[layer].reshape(bs, -1, d_att)
        cv = cache_v[layer].reshape(bs, -1, d_att)
        ys, st_s = _layer(ys, mod3, modf3, bp, ck, cv, cache_logf[layer], state_conv[layer], state_ssm[layer], p,
                          final_w, last)
        outs_p.append(st_p)
        outs_s.append(st_s)

    def stack(outs, b, l):
        k = jnp.stack([o[0].reshape(b, l, h_att, HD_ATT) for o in outs])
        v = jnp.stack([o[1].reshape(b, l, h_att, HD_ATT) for o in outs])
        return (k, v, jnp.stack([o[2] for o in outs]), jnp.stack([o[3] for o in outs]),
                jnp.stack([o[4] for o in outs]))

    return (yp, ys) + stack(outs_p, bp, lp) + stack(outs_s, bs, x_sample.shape[1])
```

```python
import functools

import jax
import jax.numpy as jnp
import numpy as np
from jax import lax
from jax.experimental import pallas as pl
from jax.experimental.pallas import tpu as pltpu

F32 = jnp.float32
BF16 = jnp.bfloat16

P_SSD = 64
N_STATE = 64
G_SSD = 2
CONV_W = 4
HD_ATT = 64
N_EGROUPS = 4
EXPERTS_PER_GROUP = 4
N_EXPERTS = N_EGROUPS * EXPERTS_PER_GROUP
EPS = 1e-6
NEG_BIG = -1e30

LANES = 128
SEG_ALIGN = 16
SORT_TILE = 256
MOE_BLOCK = 256
MOE_CHUNK = 64
SMALL_W = 16
VMEM_LIMIT = 56 * 1024 * 1024


def _params(*sem):
    return pltpu.CompilerParams(dimension_semantics=sem, vmem_limit_bytes=VMEM_LIMIT)


def _split2(x):
    hi = x.astype(BF16)
    lo = (x - hi.astype(F32)).astype(BF16)
    return hi, lo


def _split3(x):
    hi = x.astype(BF16)
    r = x - hi.astype(F32)
    mid = r.astype(BF16)
    lo = (r - mid.astype(F32)).astype(BF16)
    return hi, mid, lo


def _dot(a, b):
    return jnp.dot(a, b, preferred_element_type=F32)


def _dot_nt(a, b):
    return lax.dot_general(a, b, (((1,), (1,)), ((), ())), preferred_element_type=F32)


def _dot_tn(a, b):
    return lax.dot_general(a, b, (((0,), (0,)), ((), ())), preferred_element_type=F32)


def _silu(x):
    return x / (1.0 + jnp.exp(-x))


def _rms(x, w):
    return x * lax.rsqrt(jnp.mean(x * x, axis=-1, keepdims=True) + EPS) * w


def _pick_tile(n, candidates):
    for c in candidates:
        if n % c == 0:
            return c
    return n


def _mod_kernel(c_ref, w_ref, b_ref, o_ref):
    a = _silu(c_ref[...])
    a_hi, a_lo = _split2(a)
    w_hi, w_lo = _split2(w_ref[...])
    o_ref[...] = _dot(a_hi, w_hi) + _dot(a_lo, w_hi) + _dot(a_hi, w_lo) + b_ref[...]


def _modulation(c, w, b):
    m, d = c.shape
    n = w.shape[1]
    tn = _pick_tile(n, (1024, 512, 256, 128))
    return pl.pallas_call(
        _mod_kernel,
        grid=(n // tn,),
        in_specs=[pl.BlockSpec((m, d), lambda j: (0, 0)),
                  pl.BlockSpec((d, tn), lambda j: (0, j)),
                  pl.BlockSpec((1, tn), lambda j: (0, j))],
        out_specs=pl.BlockSpec((m, tn), lambda j: (0, j)),
        out_shape=jax.ShapeDtypeStruct((m, n), F32),
        compiler_params=_params("parallel"),
        name="adaln_mod",
    )(c, w, b.reshape(1, n))


def _inproj_kernel(x_ref, sh_ref, sc_ref, nw_ref, wz_ref, wx_ref, wq_ref, wk_ref, wv_ref, ws_ref, bs_ref,
                   z_ref, xbc_ref, q_ref, k_ref, v_ref, kb_ref, vb_ref, sm_ref, smt_ref, tail_ref):
    l = pl.program_id(1)
    x = x_ref[0]
    h = (_rms(x, nw_ref[...]) * (1.0 + sc_ref[0]) + sh_ref[0]).astype(BF16)
    z_ref[0] = _dot(h, wz_ref[...]).astype(BF16)
    xbc = _dot(h, wx_ref[...])
    xbc_ref[0] = xbc.astype(BF16)
    q_ref[0] = _dot(h, wq_ref[...]).astype(BF16)
    k = _dot(h, wk_ref[...])
    k_ref[0] = k
    kb_ref[0] = k.astype(BF16)
    v = _dot(h, wv_ref[...])
    v_ref[0] = v
    vb_ref[0] = v.astype(BF16)
    s = _dot(h, ws_ref[...]) + bs_ref[...]
    t = jnp.log(1.0 + jnp.exp(-jnp.abs(s)))
    lane = lax.broadcasted_iota(jnp.int32, s.shape, 1)
    s = jnp.where(lane < SMALL_W // 2, jnp.maximum(s, 0.0) + t, jnp.minimum(s, 0.0) - t)
    sm_ref[0] = s[:, :SMALL_W]
    smt_ref[0] = s.T[:SMALL_W, :]

    @pl.when(l == pl.num_programs(1) - 1)
    def _():
        tl = xbc.shape[0]
        tail_ref[0] = xbc[tl - (CONV_W - 1):, :]


def _inproj(x, mod3, boff, norm_w, wz, wx, wq, wk, wv, ws, bs):
    b, l, d = x.shape
    tl = _pick_tile(l, (512, 256, 128, 64))
    nl = l // tl
    d_ssd, d_conv, d_att = wz.shape[1], wx.shape[1], wq.shape[1]
    row = lambda col: pl.BlockSpec((1, 1, d), lambda i, j, col=col: (i + boff, 0, col))
    full = lambda a: pl.BlockSpec(a.shape, lambda i, j: (0,) * a.ndim)
    tok = lambda w: pl.BlockSpec((1, tl, w), lambda i, j: (i, j, 0))
    out_shape = [
        jax.ShapeDtypeStruct((b, l, d_ssd), BF16),
        jax.ShapeDtypeStruct((b, l, d_conv), BF16),
        jax.ShapeDtypeStruct((b, l, d_att), BF16),
        jax.ShapeDtypeStruct((b, l, d_att), F32),
        jax.ShapeDtypeStruct((b, l, d_att), F32),
        jax.ShapeDtypeStruct((b, l, d_att), BF16),
        jax.ShapeDtypeStruct((b, l, d_att), BF16),
        jax.ShapeDtypeStruct((b, l, SMALL_W), F32),
        jax.ShapeDtypeStruct((b, SMALL_W, l), F32),
        jax.ShapeDtypeStruct((b, CONV_W - 1, d_conv), F32),
    ]
    out_specs = [tok(d_ssd), tok(d_conv), tok(d_att), tok(d_att), tok(d_att), tok(d_att), tok(d_att),
                 tok(SMALL_W), pl.BlockSpec((1, SMALL_W, tl), lambda i, j: (i, 0, j)),
                 pl.BlockSpec((1, CONV_W - 1, d_conv), lambda i, j: (i, 0, 0))]
    return pl.pallas_call(
        _inproj_kernel,
        grid=(b, nl),
        in_specs=[tok(d), row(0), row(1), full(norm_w), full(wz), full(wx), full(wq), full(wk), full(wv),
                  full(ws), full(bs)],
        out_specs=out_specs,
        out_shape=out_shape,
        compiler_params=_params("parallel", "arbitrary"),
        name="inproj",
    )(x, mod3, mod3, norm_w, wz, wx, wq, wk, wv, ws, bs)


LOG2E = 1.4426950408889634
AUG = 3


def _aug_select(n_heads):
    sq = np.zeros((AUG, n_heads, n_heads * HD_ATT), np.float32)
    sk = np.zeros((AUG, n_heads, n_heads * HD_ATT), np.float32)
    bq = np.zeros((1, n_heads * HD_ATT), np.float32)
    bk = np.zeros((1, n_heads * HD_ATT), np.float32)
    for h in range(n_heads):
        slot = (h ^ 1) * HD_ATT
        for c in range(AUG):
            sq[c, h, slot + c] = 1.0
            bq[0, slot + AUG + c] = 1.0
            bk[0, slot + c] = 1.0
            sk[c, h, slot + AUG + c] = -1.0
    return sq, sk, bq, bk


def _cumsum_kernel(x_ref, sel_ref, base_ref, qa_ref, ka_ref, carry):
    @pl.when(pl.program_id(1) == 0)
    def _():
        carry[...] = jnp.zeros_like(carry)

    x = x_ref[0]
    h, tc = x.shape
    r = lax.broadcasted_iota(jnp.int32, (tc, tc), 0)
    c = lax.broadcasted_iota(jnp.int32, (tc, tc), 1)
    upper = jnp.where(r <= c, 1.0, 0.0).astype(BF16)
    stack = lambda ps: jnp.concatenate([p.astype(F32) for p in ps], axis=0).astype(BF16)
    parts = _dot(stack(_split3(x)), upper)
    cs = parts[:h] + parts[h:2 * h] + parts[2 * h:] + carry[...]
    carry[...] = cs[:, tc - 1:]
    aug = _dot_tn(stack(_split3(cs * LOG2E)), sel_ref[...]) + base_ref[...]
    d_att = qa_ref.shape[2]
    qa_ref[0] = aug[:, :d_att].astype(BF16)
    ka_ref[0] = aug[:, d_att:].astype(BF16)


def _forget_cumsum(logf_t):
    b, h, lk = logf_t.shape
    tc = _pick_tile(lk, (512, 384, 256, 128))
    sq, sk, bq, bk = _aug_select(h)
    d_att = h * HD_ATT
    sel = jnp.asarray(np.concatenate([sq.reshape(AUG * h, d_att), sk.reshape(AUG * h, d_att)], axis=1), BF16)
    base = jnp.asarray(np.concatenate([bq, bk], axis=1))
    full = lambda a: pl.BlockSpec(a.shape, lambda i, j: (0,) * a.ndim)
    out = jax.ShapeDtypeStruct((b, lk, d_att), BF16)
    return pl.pallas_call(
        _cumsum_kernel,
        grid=(b, lk // tc),
        in_specs=[pl.BlockSpec((1, h, tc), lambda i, j: (i, 0, j)), full(sel), full(base)],
        out_specs=[pl.BlockSpec((1, tc, d_att), lambda i, j: (i, j, 0))] * 2,
        out_shape=[out, out],
        scratch_shapes=[pltpu.VMEM((h, 1), F32)],
        compiler_params=_params("parallel", "arbitrary"),
        name="forget_cumsum",
    )(logf_t, sel, base)


def _ssd_spread(n_heads):
    e = np.zeros((2 * AUG * n_heads, 2 * n_heads * P_SSD), np.float32)
    for v in range(2):
        for c in range(AUG):
            for h in range(n_heads):
                col = v * n_heads * P_SSD + h * P_SSD
                e[(v * AUG + c) * n_heads + h, col:col + P_SSD] = 1.0
    return e


def _ssd_kernel(xbc_ref, z_ref, sm_ref, smt_ref, cprev_ref, sprev_ref, cw_ref, cb_ref, arow_ref, acol_ref,
                dskip_ref, nw_ref, spread_ref, y_ref, snew_ref, hist, state):
    l = pl.program_id(1)
    q = xbc_ref.shape[1]
    d_ssd = z_ref.shape[2]
    n_heads = d_ssd // P_SSD
    pair = 2 * P_SSD
    hist_rows = hist.shape[0]

    @pl.when(l == 0)
    def _():
        hist[...] = jnp.zeros(hist.shape, F32)
        hist[hist_rows - (CONV_W - 1):, :] = cprev_ref[0]
        state[...] = sprev_ref[0]

    xb = xbc_ref[0]
    r3 = lax.broadcasted_iota(jnp.int32, ((CONV_W - 1) * q, q), 0)
    c3 = lax.broadcasted_iota(jnp.int32, ((CONV_W - 1) * q, q), 1)
    shift = jnp.where(r3 % q - c3 == r3 // q + 1, 1.0, 0.0).astype(BF16)
    shifted = _dot(shift, xb)
    cw = cw_ref[...]
    conv = cb_ref[...] + xb.astype(F32) * cw[CONV_W - 1:CONV_W, :]
    for k in range(CONV_W - 1):
        conv = conv + shifted[k * q:(k + 1) * q, :] * cw[CONV_W - 2 - k:CONV_W - 1 - k, :]
    hrow = lambda j: hist[hist_rows - j:hist_rows - j + 1, :]
    frow = lax.broadcasted_iota(jnp.int32, (hist_rows, 1), 0)
    fix = jnp.zeros((hist_rows, conv.shape[1]), F32)
    for t in range(CONV_W - 1):
        acc = 0.0
        for j in range(1, CONV_W - t):
            acc = acc + hrow(j) * cw[CONV_W - 1 - t - j:CONV_W - t - j, :]
        fix = jnp.where(frow == t, acc, fix)
    conv = jnp.concatenate([conv[:hist_rows] + fix, conv[hist_rows:]], axis=0)
    hist[...] = xb[q - hist_rows:, :].astype(F32)
    act = _silu(conv)

    dt = sm_ref[0][:, :n_heads]
    dt_t = smt_ref[0][:n_heads, :]
    a_row = -jnp.exp(arow_ref[...]) * LOG2E
    a_col = -jnp.exp(acol_ref[...]) * LOG2E
    r = lax.broadcasted_iota(jnp.int32, (q, q), 0)
    c = lax.broadcasted_iota(jnp.int32, (q, q), 1)
    causal = r >= c
    lower = jnp.where(causal, 1.0, 0.0).astype(BF16)
    upper = jnp.where(r <= c, 1.0, 0.0).astype(BF16)
    h0, h1, h2 = _split3(dt * a_row)
    acum = _dot(lower, h0) + _dot(lower, h1) + _dot(lower, h2)
    t0, t1, t2 = _split3(dt_t * a_col)
    acum_t = _dot(t0, upper) + _dot(t1, upper) + _dot(t2, upper)
    pieces = [p.astype(F32) for p in _split3(acum) + _split3(dt)]
    wide = _dot(jnp.concatenate(pieces, axis=1).astype(BF16), spread_ref[...])
    acum_x = wide[:, :d_ssd]
    dt_x = wide[:, d_ssd:]
    a_last = acum[q - 1:q, :]
    e_all = jnp.exp2(a_last)
    e_cum_x = jnp.exp2(acum_x)
    e_end_x = jnp.exp2(acum_x[q - 1:q, :] - acum_x)

    xs = act[:, :d_ssd]
    bm = act[:, d_ssd:d_ssd + G_SSD * N_STATE]
    cm = act[:, d_ssd + G_SSD * N_STATE:]
    bm_b = bm.astype(BF16)
    cm_b = cm.astype(BF16)
    xdt = xs * dt_x
    xdt_b = xdt.astype(BF16)
    xe_b = (xdt * e_end_x).astype(BF16)
    lane = lax.broadcasted_iota(jnp.int32, (1, pair), 1)
    first = lane < P_SSD
    srow = lax.broadcasted_iota(jnp.int32, (pair, 1), 0)
    ys = []
    for p in range(n_heads // 2):
        g = (2 * p * G_SSD) // n_heads
        in_group = first if g == 0 else jnp.logical_not(first)
        sl = slice(p * pair, (p + 1) * pair)
        if (2 * p) % (n_heads // G_SSD) == 0:
            cb = _dot_nt(jnp.where(in_group, cm_b, jnp.zeros_like(cm_b)), bm_b)
        ms = []
        for hh in range(2):
            h = 2 * p + hh
            seg = acum[:, h:h + 1] - acum_t[h:h + 1, :]
            ms.append((cb * jnp.exp2(jnp.where(causal, seg, NEG_BIG))).astype(BF16))
        xp = xdt_b[:, sl]
        zero = jnp.zeros_like(xp)
        y = _dot(jnp.concatenate(ms, axis=1),
                 jnp.concatenate([jnp.where(first, xp, zero), jnp.where(first, zero, xp)], axis=0))
        s_in = state[sl, :]
        y = y + _dot_nt(cm_b, s_in.astype(BF16)) * e_cum_x[:, sl]
        upd = _dot_tn(xe_b[:, sl], bm_b)
        keep = jnp.where(srow < P_SSD, e_all[:, 2 * p:2 * p + 1], e_all[:, 2 * p + 1:2 * p + 2])
        state[sl, :] = s_in * keep + jnp.where(in_group, upd, 0.0)
        ys.append(y)
    y_all = jnp.concatenate(ys, axis=1) + dskip_ref[...] * xs
    yg = y_all * _silu(z_ref[0].astype(F32))
    y_ref[0] = _rms(yg, nw_ref[...]).astype(BF16)
    snew_ref[0] = state[...]


def _ssd(xbc, z, sm, smt, conv_prev, ssm_prev, conv_w, conv_b, a_log, d_skip, norm_w):
    b, l, d_conv = xbc.shape
    d_ssd = z.shape[2]
    n_heads = d_ssd // P_SSD
    hg = n_heads // G_SSD
    q = _pick_tile(l, (256, 128, 64))
    tok = lambda w: pl.BlockSpec((1, q, w), lambda i, j: (i, j, 0))
    full = lambda a: pl.BlockSpec(a.shape, lambda i, j: (0,) * a.ndim)
    a_row = a_log.reshape(1, n_heads)
    a_col = a_log.reshape(n_heads, 1)
    conv_b = conv_b.reshape(1, d_conv)
    d_skip = jnp.repeat(d_skip, P_SSD).reshape(1, d_ssd)
    norm_w = norm_w.reshape(1, d_ssd)
    spread = jnp.asarray(_ssd_spread(n_heads), BF16)
    s4 = ssm_prev.reshape(b, G_SSD, hg * P_SSD, N_STATE)
    s_in = jnp.concatenate([jnp.pad(s4[:, g], ((0, 0), (0, 0), (g * N_STATE, (G_SSD - 1 - g) * N_STATE)))
                            for g in range(G_SSD)], axis=1)
    state_spec = pl.BlockSpec((1, n_heads * P_SSD, G_SSD * N_STATE), lambda i, j: (i, 0, 0))
    y, s_out = pl.pallas_call(
        _ssd_kernel,
        grid=(b, l // q),
        in_specs=[tok(d_conv), tok(d_ssd), tok(SMALL_W), pl.BlockSpec((1, SMALL_W, q), lambda i, j: (i, 0, j)),
                  pl.BlockSpec((1, CONV_W - 1, d_conv), lambda i, j: (i, 0, 0)), state_spec,
                  full(conv_w), full(conv_b), full(a_row), full(a_col), full(d_skip), full(norm_w), full(spread)],
        out_specs=[tok(d_ssd), state_spec],
        out_shape=[jax.ShapeDtypeStruct((b, l, d_ssd), BF16),
                   jax.ShapeDtypeStruct((b, n_heads * P_SSD, G_SSD * N_STATE), F32)],
        scratch_shapes=[pltpu.VMEM((8, d_conv), F32),
                        pltpu.VMEM((n_heads * P_SSD, G_SSD * N_STATE), F32)],
        compiler_params=_params("parallel", "arbitrary"),
        name="ssd",
    )(xbc, z, sm, smt, conv_prev, s_in, conv_w, conv_b, a_row, a_col, d_skip, norm_w, spread)
    s_out = s_out.reshape(b, G_SSD, hg * P_SSD, G_SSD * N_STATE)
    s_new = jnp.concatenate([s_out[:, g, :, g * N_STATE:(g + 1) * N_STATE] for g in range(G_SSD)], axis=1)
    return y, s_new.reshape(b, n_heads, P_SSD, N_STATE)


def _attn_kernel(q_ref, qa_ref, k_ref, ka_ref, v_ref, o_ref, kk_s, m_s, acc_s, *, p0, tq, tk):
    i = pl.program_id(2)
    lane = lax.broadcasted_iota(jnp.int32, (1, 2 * HD_ATT), 1)
    first = lane < HD_ATT
    own = (first, jnp.logical_not(first))
    sum_lane = (HD_ATT, 0)

    @pl.when(i == 0)
    def _():
        k = k_ref[0]
        ka = ka_ref[0]
        kk_s[0] = jnp.where(first, k, ka)
        kk_s[1] = jnp.where(first, ka, k)

    q = q_ref[0]
    qa = qa_ref[0]
    qq = (jnp.where(first, q, qa), jnp.where(first, qa, q))
    m_s[...] = jnp.full(m_s.shape, NEG_BIG, F32)
    acc_s[...] = jnp.zeros(acc_s.shape, F32)
    nc = tk // LANES

    def step(j, masked):
        off = pl.multiple_of(j * tk, tk)
        v = v_ref[0, pl.ds(off, tk), :]
        if masked:
            q_pos = p0 + i * tq + lax.broadcasted_iota(jnp.int32, (tq, tk), 0)
            k_pos = j * tk + lax.broadcasted_iota(jnp.int32, (tq, tk), 1)
            visible = k_pos <= q_pos
        for hh in range(2):
            s = _dot_nt(qq[hh], kk_s[hh, pl.ds(off, tk), :])
            if masked:
                s = jnp.where(visible, s, NEG_BIG)
            cols = [s[:, c * LANES:(c + 1) * LANES] for c in range(nc)]
            m_cur = functools.reduce(jnp.maximum, cols)
            m_prev = m_s[hh]
            m_new = jnp.maximum(m_prev, jnp.max(m_cur, axis=1, keepdims=True))
            alpha = jnp.exp2(m_prev - m_new)
            p = jnp.concatenate([jnp.exp2((col - m_new).astype(BF16)) for col in cols], axis=1)
            m_s[hh] = m_new
            ones = jnp.where(lane == sum_lane[hh], 1.0, 0.0).astype(BF16)
            acc_s[hh] = alpha * acc_s[hh] + _dot(p, jnp.where(own[hh], v, ones))

    n_full = (p0 + i * tq + 1) // tk
    n_vis = (p0 + i * tq + tq - 1) // tk + 1

    def full_body(j, carry):
        step(j, False)
        return carry

    def masked_body(j, carry):
        step(j, True)
        return carry

    lax.fori_loop(0, n_full, full_body, 0)
    lax.fori_loop(n_full, n_vis, masked_body, 0)
    a0 = acc_s[0]
    a1 = acc_s[1]
    o = jnp.where(first, a0 / a0[:, sum_lane[0]:sum_lane[0] + 1], a1 / a1[:, sum_lane[1]:sum_lane[1] + 1])
    o_ref[0] = o.astype(o_ref.dtype)


def _attention(q, q_aug, k, k_aug, v, p0):
    b, lq, d_att = q.shape
    lk = k.shape[1]
    n_heads = d_att // HD_ATT
    tq = _pick_tile(lq, (512, 256, 128, 64))
    tk = lk if lk <= 1536 else _pick_tile(lk, (512, 256, 128))
    assert p0 % tq == 0 and lk >= p0 + lq
    pair = 2 * HD_ATT
    q_spec = pl.BlockSpec((1, tq, pair), lambda bi, hp, i: (bi, i, hp))
    qa_spec = pl.BlockSpec((1, tq, pair), lambda bi, hp, i: (bi, i + p0 // tq, hp))
    kv_spec = pl.BlockSpec((1, lk, pair), lambda bi, hp, i: (bi, 0, hp))
    return pl.pallas_call(
        functools.partial(_attn_kernel, p0=p0, tq=tq, tk=tk),
        grid=(b, n_heads // 2, lq // tq),
        in_specs=[q_spec, qa_spec, kv_spec, kv_spec, kv_spec],
        out_specs=q_spec,
        out_shape=jax.ShapeDtypeStruct((b, lq, d_att), BF16),
        scratch_shapes=[pltpu.VMEM((2, lk, pair), BF16), pltpu.VMEM((2, tq, LANES), F32),
                        pltpu.VMEM((2, tq, pair), F32)],
        compiler_params=_params("parallel", "parallel", "arbitrary"),
        name="fox_attention",
    )(q, q_aug, k, k_aug, v)


ROUTER_ROWS = 32
EXPERT_ROW0 = 8


def _outproj_kernel(ys_ref, oa_ref, x_ref, g1_ref, sh_ref, sc_ref, anw_ref, wos_ref, woa_ref, n2w_ref, wr_ref,
                    br_ref, x1_ref, h2_ref, ld_ref, cnt_ref, wt_s, g_s):
    ya = _rms(oa_ref[0].astype(F32), anw_ref[...]).astype(BF16)
    m = _dot(ys_ref[0], wos_ref[...]) + _dot(ya, woa_ref[...])
    x1 = x_ref[0] + g1_ref[0] * m
    x1_ref[0] = x1
    h2 = _rms(x1, n2w_ref[...]) * (1.0 + sc_ref[0]) + sh_ref[0]
    h_hi, h_lo = _split2(h2)
    wr = wr_ref[...]
    p1 = _dot_nt(wr, h_hi)
    p2 = _dot_nt(wr[:ROUTER_ROWS], h_lo)
    logit = p1[:ROUTER_ROWS] + p1[ROUTER_ROWS:] + p2 + br_ref[...]

    lg = [logit[g:g + 1, :] for g in range(N_EGROUPS)]
    gmax = jnp.maximum(jnp.maximum(lg[0], lg[1]), jnp.maximum(lg[2], lg[3]))
    denom = sum(jnp.exp(x - gmax) for x in lg)
    p_sel = 1.0 / denom
    is_g = []
    taken = jnp.zeros_like(gmax) > 1.0
    for g in range(N_EGROUPS):
        hit = (lg[g] == gmax) & jnp.logical_not(taken)
        is_g.append(hit)
        taken = taken | hit
    le = []
    for e in range(EXPERTS_PER_GROUP):
        v = logit[EXPERT_ROW0 + 3 * EXPERTS_PER_GROUP + e:EXPERT_ROW0 + 3 * EXPERTS_PER_GROUP + e + 1, :]
        for g in range(N_EGROUPS - 2, -1, -1):
            r0 = EXPERT_ROW0 + g * EXPERTS_PER_GROUP + e
            v = jnp.where(is_g[g], logit[r0:r0 + 1, :], v)
        le.append(v)
    m1 = jnp.maximum(jnp.maximum(le[0], le[1]), jnp.maximum(le[2], le[3]))
    first = []
    taken = jnp.zeros_like(m1) > 1.0
    for e in range(EXPERTS_PER_GROUP):
        hit = (le[e] == m1) & jnp.logical_not(taken)
        first.append(hit)
        taken = taken | hit
    rest = [jnp.where(first[e], -jnp.inf, le[e]) for e in range(EXPERTS_PER_GROUP)]
    m2 = jnp.maximum(jnp.maximum(rest[0], rest[1]), jnp.maximum(rest[2], rest[3]))
    second = []
    taken = jnp.zeros_like(m1) > 1.0
    for e in range(EXPERTS_PER_GROUP):
        hit = (rest[e] == m2) & jnp.logical_not(taken)
        second.append(hit)
        taken = taken | hit
    e2 = jnp.exp(m2 - m1)
    w_a = p_sel / (1.0 + e2)
    w_b = w_a * e2
    d = x1.shape[1]
    wt_s[...] = jnp.zeros(wt_s.shape, F32)
    for e in range(EXPERTS_PER_GROUP):
        w = jnp.where(first[e], w_a, jnp.where(second[e], w_b, 0.0))
        w_hi = w.astype(BF16).astype(F32)
        wt_s[e:e + 1, :] = w_hi
        wt_s[EXPERTS_PER_GROUP + e:EXPERTS_PER_GROUP + e + 1, :] = w - w_hi
    h2_ref[:, :d] = h_hi
    h2_ref[:, d:] = wt_s[...].T.astype(BF16)

    tl = logit.shape[1]
    ts = min(SORT_TILE, tl)
    g_s[...] = jnp.zeros(g_s.shape, F32)
    for g in range(N_EGROUPS):
        g_s[g:g + 1, :] = jnp.where(is_g[g], 1.0, 0.0)
    r = lax.broadcasted_iota(jnp.int32, (tl, tl), 0)
    c = lax.broadcasted_iota(jnp.int32, (tl, tl), 1)
    same_tile = (r // ts) == (c // ts)
    upper = jnp.where((r <= c) & same_tile, 1.0, 0.0).astype(BF16)
    cum = _dot(g_s[...].astype(BF16), upper)
    lane = lax.broadcasted_iota(jnp.int32, (1, tl), 1)
    crow = lax.broadcasted_iota(jnp.int32, (8, LANES), 0)
    clane = lax.broadcasted_iota(jnp.int32, (8, LANES), 1)
    ldest = -1.0
    cnt = jnp.zeros((8, LANES), F32)
    for g in range(N_EGROUPS):
        ldest = ldest + jnp.where(is_g[g], cum[g:g + 1, :], 0.0)
    lo = [0.0] * (tl // ts)
    for g in range(N_EGROUPS):
        lo_row = jnp.zeros((1, tl), F32)
        for sub in range(tl // ts):
            n = cum[g:g + 1, (sub + 1) * ts - 1:(sub + 1) * ts]
            n_pad = jnp.ceil(n / SEG_ALIGN) * SEG_ALIGN
            lo_row = jnp.where(lane // ts == sub, lo[sub], lo_row)
            cnt = jnp.where((crow == sub) & (clane == g), n_pad, cnt)
            lo[sub] = lo[sub] + n_pad
        ldest = ldest + jnp.where(is_g[g], lo_row, 0.0)
    ld_ref[0] = ldest.astype(jnp.int32)
    cnt_ref[0] = cnt.astype(jnp.int32)


def _outproj(y_ssd, o_att, x, mod3, boff, attn_norm_w, wo_ssd, wo_att, norm2_w, wr, br):
    b, l, d = x.shape
    tl = _pick_tile(l, (512, 256, 128, 64))
    nl = l // tl
    d_ssd, d_att = y_ssd.shape[2], o_att.shape[2]
    row = lambda col: pl.BlockSpec((1, 1, d), lambda i, j, col=col: (i + boff, 0, col))
    full = lambda a: pl.BlockSpec(a.shape, lambda i, j: (0,) * a.ndim)
    tok = lambda w: pl.BlockSpec((1, tl, w), lambda i, j: (i, j, 0))
    return pl.pallas_call(
        _outproj_kernel,
        grid=(b, nl),
        in_specs=[tok(d_ssd), tok(d_att), tok(d), row(2), row(3), row(4), full(attn_norm_w), full(wo_ssd),
                  full(wo_att), full(norm2_w), full(wr), full(br)],
        out_specs=[tok(d), pl.BlockSpec((tl, d + LANES), lambda i, j: (i * nl + j, 0)),
                   pl.BlockSpec((1, 1, tl), lambda i, j: (i * nl + j, 0, 0)),
                   pl.BlockSpec((1, 8, LANES), lambda i, j: (i * nl + j, 0, 0))],
        out_shape=[jax.ShapeDtypeStruct((b, l, d), F32),
                   jax.ShapeDtypeStruct((b * l, d + LANES), BF16),
                   jax.ShapeDtypeStruct((b * nl, 1, tl), jnp.int32),
                   jax.ShapeDtypeStruct((b * nl, 8, LANES), jnp.int32)],
        scratch_shapes=[pltpu.VMEM((LANES, tl), F32), pltpu.VMEM((8, tl), F32)],
        compiler_params=_params("parallel", "parallel"),
        name="outproj_router",
    )(y_ssd, o_att, x, mod3, mod3, mod3, attn_norm_w, wo_ssd, wo_att, norm2_w, wr, br)


def _sort_plan(cnt, n_tiles, ts, chunk, n_blocks):
    ns = n_tiles // cnt.shape[0]
    n_pad = cnt[:, :ns, :N_EGROUPS].reshape(n_tiles, N_EGROUPS)
    lo = jnp.cumsum(n_pad, axis=1) - n_pad
    region = (jnp.sum(n_pad, axis=0) + chunk + MOE_BLOCK - 1) // MOE_BLOCK * MOE_BLOCK
    end = jnp.cumsum(region)
    off = (end - region)[None, :] + jnp.cumsum(n_pad, axis=0) - n_pad
    blk = jnp.arange(n_blocks, dtype=jnp.int32) * MOE_BLOCK
    blk_group = jnp.minimum(jnp.sum(blk[:, None] >= end[None, :], axis=1), N_EGROUPS - 1).astype(jnp.int32)
    n_used = (end[-1] // MOE_BLOCK).astype(jnp.int32).reshape(1)
    flat = lambda a: a.astype(jnp.int32).reshape(-1)
    return flat(lo), flat(off), flat(n_pad), blk_group, n_used


def _sort_kernel(lo_ref, off_ref, np_ref, x_ref, ld_ref, init_ref, out_ref, cbuf, sems, *, ts, chunk):
    del init_ref
    i = pl.program_id(0)
    slot = i % 2
    rows = ts + N_EGROUPS * SEG_ALIGN

    @pl.when(i == 0)
    def _():
        cbuf[...] = jnp.zeros(cbuf.shape, BF16)

    r = lax.broadcasted_iota(jnp.int32, (rows, ts), 0)
    perm = jnp.where(r == ld_ref[0], 1.0, 0.0).astype(BF16)
    cbuf[slot, 0:rows, :] = _dot(perm, x_ref[...]).astype(BF16)

    def copies(step, sl):
        out = []
        for g in range(N_EGROUPS):
            lo = pl.multiple_of(lo_ref[step * N_EGROUPS + g], SEG_ALIGN)
            off = pl.multiple_of(off_ref[step * N_EGROUPS + g], SEG_ALIGN)
            n_pad = np_ref[step * N_EGROUPS + g]
            for c in range(ts // chunk):
                cp = pltpu.make_async_copy(cbuf.at[sl, pl.ds(lo + c * chunk, chunk)],
                                           out_ref.at[pl.ds(off + c * chunk, chunk)], sems.at[g, c])
                out.append((None if c == 0 else n_pad > c * chunk, cp))
        return out

    def for_each(step, sl, act):
        for pred, cp in copies(step, sl):
            if pred is None:
                act(cp)
            else:
                pl.when(pred)(functools.partial(act, cp))

    @pl.when(i > 0)
    def _():
        for_each(i - 1, 1 - slot, lambda cp: cp.wait())

    for_each(i, slot, lambda cp: cp.start())

    @pl.when(i == pl.num_programs(0) - 1)
    def _():
        for_each(i, slot, lambda cp: cp.wait())


def _experts_kernel(grp_ref, nb_ref, x_ref, wg_ref, wu_ref, wd_ref, o_ref):
    del grp_ref
    b = pl.program_id(0)
    d = o_ref.shape[1]

    @pl.when(b < nb_ref[0])
    def _():
        blk = x_ref[...]
        x = blk[:, :d]
        wp = blk[:, d:].astype(F32)
        acc = jnp.zeros(o_ref.shape, F32)
        for e in range(EXPERTS_PER_GROUP):
            w_e = wp[:, e:e + 1] + wp[:, EXPERTS_PER_GROUP + e:EXPERTS_PER_GROUP + e + 1]
            hid = _silu(_dot(x, wg_ref[e])) * _dot(x, wu_ref[e]) * w_e
            acc = acc + _dot(hid.astype(BF16), wd_ref[e])
        o_ref[...] = acc.astype(BF16)

    @pl.when(b >= nb_ref[0])
    def _():
        o_ref[...] = jnp.zeros(o_ref.shape, BF16)


def _combine_kernel(lo_ref, off_ref, np_ref, ld_ref, x1_ref, g2_ref, shf_ref, scf_ref, fnw_ref, src_ref, y_ref,
                    seg, sems, *, ts, chunk, final_norm):
    i = pl.program_id(0)
    slot = i % 2

    def fetch(step, sl, act):
        for g in range(N_EGROUPS):
            off = pl.multiple_of(off_ref[step * N_EGROUPS + g], SEG_ALIGN)
            n_pad = np_ref[step * N_EGROUPS + g]
            for c in range(ts // chunk):
                cp = pltpu.make_async_copy(src_ref.at[pl.ds(off + c * chunk, chunk)],
                                           seg.at[sl, g, pl.ds(c * chunk, chunk)], sems.at[sl, g, c])
                if c == 0:
                    act(cp)
                else:
                    pl.when(n_pad > c * chunk)(functools.partial(act, cp))

    @pl.when(i == 0)
    def _():
        seg[...] = jnp.zeros(seg.shape, BF16)
        fetch(0, 0, lambda cp: cp.start())

    @pl.when(i + 1 < pl.num_programs(0))
    def _():
        fetch(i + 1, 1 - slot, lambda cp: cp.start())

    fetch(i, slot, lambda cp: cp.wait())

    ld = ld_ref[0]
    r = lax.broadcasted_iota(jnp.int32, (ts, ts), 0)
    moe = jnp.zeros(y_ref.shape, F32)
    for g in range(N_EGROUPS):
        lo = lo_ref[i * N_EGROUPS + g]
        n_pad = np_ref[i * N_EGROUPS + g]
        perm = jnp.where((r + lo == ld) & (r < n_pad), 1.0, 0.0).astype(BF16)
        moe = moe + _dot_tn(perm, seg[slot, g])
    xo = x1_ref[...] + g2_ref[0] * moe
    if final_norm:
        xo = _rms(xo, fnw_ref[...]) * (1.0 + scf_ref[0]) + shf_ref[0]
    y_ref[...] = xo


def _moe_final(h2ext, ldest, cnt, wg, wu, wd, x1, mod3, modf3, boff, rows_per_batch, final_norm_w, final_norm):
    t, d = x1.shape
    payload = h2ext.shape[1]
    ts = min(SORT_TILE, rows_per_batch)
    chunk = min(MOE_CHUNK, ts)
    n_tiles = t // ts
    ldest = ldest.reshape(n_tiles, 1, ts)
    n_blocks = -(-(t + n_tiles * N_EGROUPS * (SEG_ALIGN - 1) + N_EGROUPS * (chunk + MOE_BLOCK - 1)) // MOE_BLOCK) + 1
    lo, off, n_pad, blk_group, n_used = _sort_plan(cnt, n_tiles, ts, chunk, n_blocks)

    sorted_rows = pl.pallas_call(
        functools.partial(_sort_kernel, ts=ts, chunk=chunk),
        grid_spec=pltpu.PrefetchScalarGridSpec(
            num_scalar_prefetch=3,
            grid=(n_tiles,),
            in_specs=[pl.BlockSpec((ts, payload), lambda i, *_: (i, 0)),
                      pl.BlockSpec((1, 1, ts), lambda i, *_: (i, 0, 0)),
                      pl.BlockSpec(memory_space=pl.ANY)],
            out_specs=pl.BlockSpec(memory_space=pl.ANY),
            scratch_shapes=[pltpu.VMEM((2, ts + N_EGROUPS * SEG_ALIGN + ts, payload), BF16),
                            pltpu.SemaphoreType.DMA((N_EGROUPS, ts // chunk))]),
        out_shape=jax.ShapeDtypeStruct((n_blocks * MOE_BLOCK, payload), BF16),
        input_output_aliases={5: 0},
        compiler_params=_params("arbitrary"),
        name="moe_sort",
    )(lo, off, n_pad, h2ext, ldest, jnp.zeros((n_blocks * MOE_BLOCK, payload), BF16))

    d_e = wg.shape[2]
    live = lambda b, nb: jnp.minimum(b, nb[0] - 1)
    out_sorted = pl.pallas_call(
        _experts_kernel,
        grid_spec=pltpu.PrefetchScalarGridSpec(
            num_scalar_prefetch=2,
            grid=(n_blocks,),
            in_specs=[pl.BlockSpec((MOE_BLOCK, payload), lambda b, grp, nb: (live(b, nb), 0)),
                      pl.BlockSpec((EXPERTS_PER_GROUP, d, d_e), lambda b, grp, nb: (grp[b], 0, 0)),
                      pl.BlockSpec((EXPERTS_PER_GROUP, d, d_e), lambda b, grp, nb: (grp[b], 0, 0)),
                      pl.BlockSpec((EXPERTS_PER_GROUP, d_e, d), lambda b, grp, nb: (grp[b], 0, 0))],
            out_specs=pl.BlockSpec((MOE_BLOCK, d), lambda b, grp, nb: (b, 0))),
        out_shape=jax.ShapeDtypeStruct((n_blocks * MOE_BLOCK, d), BF16),
        compiler_params=_params("arbitrary"),
        name="moe_experts",
    )(blk_group, n_used, sorted_rows, wg, wu, wd)

    per = rows_per_batch // ts
    row = lambda col: pl.BlockSpec((1, 1, d), lambda i, *_, col=col: (i // per + boff, 0, col))
    return pl.pallas_call(
        functools.partial(_combine_kernel, ts=ts, chunk=chunk, final_norm=final_norm),
        grid_spec=pltpu.PrefetchScalarGridSpec(
            num_scalar_prefetch=3,
            grid=(n_tiles,),
            in_specs=[pl.BlockSpec((1, 1, ts), lambda i, *_: (i, 0, 0)),
                      pl.BlockSpec((ts, d), lambda i, *_: (i, 0)),
                      row(5), row(0), row(1),
                      pl.BlockSpec((1, d), lambda i, *_: (0, 0)),
                      pl.BlockSpec(memory_space=pl.ANY)],
            out_specs=pl.BlockSpec((ts, d), lambda i, *_: (i, 0)),
            scratch_shapes=[pltpu.VMEM((2, N_EGROUPS, ts, d), BF16),
                            pltpu.SemaphoreType.DMA((2, N_EGROUPS, ts // chunk))]),
        out_shape=jax.ShapeDtypeStruct((t, d), F32),
        compiler_params=_params("arbitrary"),
        name="moe_combine",
    )(lo, off, n_pad, ldest, x1, mod3, modf3, modf3, final_norm_w, out_sorted)


def _layer(x, mod3, modf3, boff, k_prev, v_prev, logf_prev, conv_prev, ssm_prev, p, final_norm_w, final_norm):
    b, l, d = x.shape
    z, xbc, q, k, v, k_b, v_b, sm, smt, conv_new = _inproj(
        x, mod3, boff, p["norm1_w"], p["wz"], p["wx"], p["wq"], p["wk"], p["wv"], p["ws"], p["bs"])
    n_heads_ssd = p["wz"].shape[1] // P_SSD
    y_ssd, ssm_new = _ssd(xbc, z, sm, smt, conv_prev, ssm_prev, p["conv_w"], p["conv_b"], p["a_log"],
                          p["d_skip"], p["ssd_norm_w"])
    n_heads = p["wq"].shape[1] // HD_ATT
    logf_t = smt[:, n_heads_ssd:n_heads_ssd + n_heads, :]
    if k_prev is None:
        p0, k_all, v_all, lf_all = 0, k_b, v_b, logf_t
    else:
        p0 = k_prev.shape[1]
        pad = (-(p0 + l)) % LANES
        zeros = lambda w: jnp.zeros((b, pad, w), BF16)
        k_all = jnp.concatenate([k_prev.astype(BF16), k_b, zeros(k_b.shape[2])], axis=1)
        v_all = jnp.concatenate([v_prev.astype(BF16), v_b, zeros(v_b.shape[2])], axis=1)
        lf_all = jnp.concatenate([jnp.swapaxes(logf_prev, 1, 2), logf_t, jnp.zeros((b, n_heads, pad), F32)], axis=2)
    q_aug, k_aug = _forget_cumsum(lf_all)
    o_att = _attention(q, q_aug, k_all, k_aug, v_all, p0)
    x1, h2ext, ldest, cnt = _outproj(y_ssd, o_att, x, mod3, boff, p["attn_norm_w"], p["wo_ssd"], p["wo_att"],
                                     p["norm2_w"], p["wr"], p["br"])
    y = _moe_final(h2ext, ldest, cnt, p["wg"], p["wu"], p["wd"], x1.reshape(b * l, d), mod3, modf3, boff, l,
                   final_norm_w, final_norm)
    return y.reshape(b, l, d), (k, v, jnp.swapaxes(logf_t, 1, 2), conv_new, ssm_new)


def kernel(x_prompt, x_sample, c_prompt, c_sample, cache_k, cache_v, cache_logf, state_conv, state_ssm, norm1_w, w_ada, b_ada, w_in, conv_w, conv_b, dt_bias, a_log, d_skip, ssd_norm_w, f_bias, attn_norm_w, w_out, norm2_w, w_rg, b_rg, w_re, b_re, w_gate, w_up, w_down, final_norm_w, w_ada_f, b_ada_f):
    depth = w_in.shape[0]
    bp, lp, d = x_prompt.shape
    bs = x_sample.shape[0]
    d_conv = conv_w.shape[2]
    d_ssd = ssd_norm_w.shape[1]
    d_att = attn_norm_w.shape[1]
    h_ssd = dt_bias.shape[1]
    h_att = f_bias.shape[1]
    assert h_ssd + h_att == SMALL_W and d_att // HD_ATT == h_att and d_ssd // P_SSD == h_ssd

    c_all = jnp.concatenate([c_prompt, c_sample], axis=0)
    modf3 = _modulation(c_all, w_ada_f, b_ada_f).reshape(bp + bs, 1, 2 * d)
    final_w = final_norm_w.reshape(1, d)

    i0 = d_ssd
    i1 = i0 + d_conv
    i2 = i1 + h_ssd
    i3 = i2 + d_att
    i4 = i3 + d_att
    i5 = i4 + d_att
    yp, ys = x_prompt, x_sample
    outs_p, outs_s = [], []
    for layer in range(depth):
        mod3 = _modulation(c_all, w_ada[layer], b_ada[layer]).reshape(bp + bs, 1, 6 * d)
        wi = w_in[layer]
        w_small = jnp.concatenate([wi[:, i1:i2], wi[:, i5:], jnp.zeros((d, LANES - SMALL_W), F32)], axis=1)
        b_small = jnp.concatenate([dt_bias[layer], f_bias[layer], jnp.zeros((LANES - SMALL_W,), F32)])
        wr = jnp.zeros((ROUTER_ROWS, d), F32)
        wr = wr.at[:N_EGROUPS].set(w_rg[layer].T)
        wr = wr.at[EXPERT_ROW0:EXPERT_ROW0 + N_EXPERTS].set(
            jnp.transpose(w_re[layer], (0, 2, 1)).reshape(N_EXPERTS, d))
        wr_hi, wr_lo = _split2(wr)
        br = jnp.zeros((ROUTER_ROWS,), F32)
        br = br.at[:N_EGROUPS].set(b_rg[layer])
        br = br.at[EXPERT_ROW0:EXPERT_ROW0 + N_EXPERTS].set(b_re[layer].reshape(N_EXPERTS))
        p = dict(
            norm1_w=norm1_w[layer].reshape(1, d),
            wz=wi[:, :i0].astype(BF16), wx=wi[:, i0:i1].astype(BF16),
            wq=(wi[:, i2:i3] * (LOG2E * HD_ATT ** -0.5)).astype(BF16),
            wk=wi[:, i3:i4].astype(BF16), wv=wi[:, i4:i5].astype(BF16),
            ws=w_small.astype(BF16), bs=b_small.reshape(1, LANES),
            conv_w=conv_w[layer], conv_b=conv_b[layer], a_log=a_log[layer], d_skip=d_skip[layer],
            ssd_norm_w=ssd_norm_w[layer], attn_norm_w=attn_norm_w[layer].reshape(1, d_att),
            wo_ssd=w_out[layer][:d_ssd].astype(BF16), wo_att=w_out[layer][d_ssd:].astype(BF16),
            norm2_w=norm2_w[layer].reshape(1, d),
            wr=jnp.concatenate([wr_hi, wr_lo], axis=0), br=br.reshape(ROUTER_ROWS, 1),
            wg=w_gate[layer].astype(BF16), wu=w_up[layer].astype(BF16), wd=w_down[layer].astype(BF16),
        )
        conv0 = jnp.zeros((bp, CONV_W - 1, d_conv), F32)
        ssm0 = jnp.zeros((bp, h_ssd, P_SSD, N_STATE), F32)
        last = layer == depth - 1
        yp, st_p = _layer(yp, mod3, modf3, 0, None, None, None, conv0, ssm0, p, final_w, last)
        ck = cache_k[layer].reshape(bs, -1, d_att)
        cv = cache_v[layer].reshape(bs, -1, d_att)
        ys, st_s = _layer(ys, mod3, modf3, bp, ck, cv, cache_logf[layer], state_conv[layer], state_ssm[layer], p,
                          final_w, last)
        outs_p.append(st_p)
        outs_s.append(st_s)

    def stack(outs, b, l):
        k = jnp.stack([o[0].reshape(b, l, h_att, HD_ATT) for o in outs])
        v = jnp.stack([o[1].reshape(b, l, h_att, HD_ATT) for o in outs])
        return (k, v, jnp.stack([o[2] for o in outs]), jnp.stack([o[3] for o in outs]),
                jnp.stack([o[4] for o in outs]))

    return (yp, ys) + stack(outs_p, bp, lp) + stack(outs_s, bs, x_sample.shape[1])
```

```python
import functools

import jax
import jax.numpy as jnp
import numpy as np
from jax import lax
from jax.experimental import pallas as pl
from jax.experimental.pallas import tpu as pltpu

F32 = jnp.float32
BF16 = jnp.bfloat16

P_SSD = 64
N_STATE = 64
G_SSD = 2
CONV_W = 4
HD_ATT = 64
N_EGROUPS = 4
EXPERTS_PER_GROUP = 4
N_EXPERTS = N_EGROUPS * EXPERTS_PER_GROUP
EPS = 1e-6
NEG_BIG = -1e30

LANES = 128
SEG_ALIGN = 16
SORT_TILE = 256
MOE_BLOCK = 256
MOE_CHUNK = 64
SMALL_W = 16
VMEM_LIMIT = 56 * 1024 * 1024


def _params(*sem):
    return pltpu.CompilerParams(dimension_semantics=sem, vmem_limit_bytes=VMEM_LIMIT)


def _split2(x):
    hi = x.astype(BF16)
    lo = (x - hi.astype(F32)).astype(BF16)
    return hi, lo


def _split3(x):
    hi = x.astype(BF16)
    r = x - hi.astype(F32)
    mid = r.astype(BF16)
    lo = (r - mid.astype(F32)).astype(BF16)
    return hi, mid, lo


def _dot(a, b):
    return jnp.dot(a, b, preferred_element_type=F32)


def _dot_nt(a, b):
    return lax.dot_general(a, b, (((1,), (1,)), ((), ())), preferred_element_type=F32)


def _dot_tn(a, b):
    return lax.dot_general(a, b, (((0,), (0,)), ((), ())), preferred_element_type=F32)


def _silu(x):
    return x / (1.0 + jnp.exp(-x))


def _rms(x, w):
    return x * lax.rsqrt(jnp.mean(x * x, axis=-1, keepdims=True) + EPS) * w


def _pick_tile(n, candidates):
    for c in candidates:
        if n % c == 0:
            return c
    return n


def _mod_kernel(c_ref, w_ref, b_ref, o_ref):
    a = _silu(c_ref[...])
    a_hi, a_lo = _split2(a)
    w_hi, w_lo = _split2(w_ref[...])
    o_ref[...] = _dot(a_hi, w_hi) + _dot(a_lo, w_hi) + _dot(a_hi, w_lo) + b_ref[...]


def _modulation(c, w, b):
    m, d = c.shape
    n = w.shape[1]
    tn = _pick_tile(n, (1024, 512, 256, 128))
    return pl.pallas_call(
        _mod_kernel,
        grid=(n // tn,),
        in_specs=[pl.BlockSpec((m, d), lambda j: (0, 0)),
                  pl.BlockSpec((d, tn), lambda j: (0, j)),
                  pl.BlockSpec((1, tn), lambda j: (0, j))],
        out_specs=pl.BlockSpec((m, tn), lambda j: (0, j)),
        out_shape=jax.ShapeDtypeStruct((m, n), F32),
        compiler_params=_params("parallel"),
        name="adaln_mod",
    )(c, w, b.reshape(1, n))


def _inproj_kernel(x_ref, sh_ref, sc_ref, nw_ref, wz_ref, wx_ref, wq_ref, wk_ref, wv_ref, ws_ref, bs_ref,
                   z_ref, xbc_ref, q_ref, k_ref, v_ref, kb_ref, vb_ref, sm_ref, smt_ref, tail_ref):
    l = pl.program_id(1)
    x = x_ref[0]
    h = (_rms(x, nw_ref[...]) * (1.0 + sc_ref[0]) + sh_ref[0]).astype(BF16)
    z_ref[0] = _dot(h, wz_ref[...]).astype(BF16)
    xbc = _dot(h, wx_ref[...])
    xbc_ref[0] = xbc.astype(BF16)
    q_ref[0] = _dot(h, wq_ref[...]).astype(BF16)
    k = _dot(h, wk_ref[...])
    k_ref[0] = k
    kb_ref[0] = k.astype(BF16)
    v = _dot(h, wv_ref[...])
    v_ref[0] = v
    vb_ref[0] = v.astype(BF16)
    s = _dot(h, ws_ref[...]) + bs_ref[...]
    t = jnp.log(1.0 + jnp.exp(-jnp.abs(s)))
    lane = lax.broadcasted_iota(jnp.int32, s.shape, 1)
    s = jnp.where(lane < SMALL_W // 2, jnp.maximum(s, 0.0) + t, jnp.minimum(s, 0.0) - t)
    sm_ref[0] = s[:, :SMALL_W]
    smt_ref[0] = s.T[:SMALL_W, :]

    @pl.when(l == pl.num_programs(1) - 1)
    def _():
        tl = xbc.shape[0]
        tail_ref[0] = xbc[tl - (CONV_W - 1):, :]


def _inproj(x, mod3, boff, norm_w, wz, wx, wq, wk, wv, ws, bs):
    b, l, d = x.shape
    tl = _pick_tile(l, (512, 256, 128, 64))
    nl = l // tl
    d_ssd, d_conv, d_att = wz.shape[1], wx.shape[1], wq.shape[1]
    row = lambda col: pl.BlockSpec((1, 1, d), lambda i, j, col=col: (i + boff, 0, col))
    full = lambda a: pl.BlockSpec(a.shape, lambda i, j: (0,) * a.ndim)
    tok = lambda w: pl.BlockSpec((1, tl, w), lambda i, j: (i, j, 0))
    out_shape = [
        jax.ShapeDtypeStruct((b, l, d_ssd), BF16),
        jax.ShapeDtypeStruct((b, l, d_conv), BF16),
        jax.ShapeDtypeStruct((b, l, d_att), BF16),
        jax.ShapeDtypeStruct((b, l, d_att), F32),
        jax.ShapeDtypeStruct((b, l, d_att), F32),
        jax.ShapeDtypeStruct((b, l, d_att), BF16),
        jax.ShapeDtypeStruct((b, l, d_att), BF16),
        jax.ShapeDtypeStruct((b, l, SMALL_W), F32),
        jax.ShapeDtypeStruct((b, SMALL_W, l), F32),
        jax.ShapeDtypeStruct((b, CONV_W - 1, d_conv), F32),
    ]
    out_specs = [tok(d_ssd), tok(d_conv), tok(d_att), tok(d_att), tok(d_att), tok(d_att), tok(d_att),
                 tok(SMALL_W), pl.BlockSpec((1, SMALL_W, tl), lambda i, j: (i, 0, j)),
                 pl.BlockSpec((1, CONV_W - 1, d_conv), lambda i, j: (i, 0, 0))]
    return pl.pallas_call(
        _inproj_kernel,
        grid=(b, nl),
        in_specs=[tok(d), row(0), row(1), full(norm_w), full(wz), full(wx), full(wq), full(wk), full(wv),
                  full(ws), full(bs)],
        out_specs=out_specs,
        out_shape=out_shape,
        compiler_params=_params("parallel", "arbitrary"),
        name="inproj",
    )(x, mod3, mod3, norm_w, wz, wx, wq, wk, wv, ws, bs)


LOG2E = 1.4426950408889634
AUG = 3
SPLIT_ROWS = 128


def _aug_select(n_heads):
    sq = np.zeros((AUG, n_heads, n_heads * HD_ATT), np.float32)
    sk = np.zeros((AUG, n_heads, n_heads * HD_ATT), np.float32)
    bq = np.zeros((1, n_heads * HD_ATT), np.float32)
    bk = np.zeros((1, n_heads * HD_ATT), np.float32)
    for h in range(n_heads):
        slot = (h ^ 1) * HD_ATT
        for c in range(AUG):
            sq[c, h, slot + c] = 1.0
            bq[0, slot + AUG + c] = 1.0
            bk[0, slot + c] = 1.0
            sk[c, h, slot + AUG + c] = -1.0
    return sq, sk, bq, bk


def _cumsum_kernel(x_ref, sel_ref, base_ref, qa_ref, ka_ref, carry):
    @pl.when(pl.program_id(1) == 0)
    def _():
        carry[...] = jnp.zeros_like(carry)

    x = x_ref[0]
    h, tc = x.shape
    r = lax.broadcasted_iota(jnp.int32, (tc, tc), 0)
    c = lax.broadcasted_iota(jnp.int32, (tc, tc), 1)
    upper = jnp.where(r <= c, 1.0, 0.0).astype(BF16)
    stack = lambda ps: jnp.concatenate([p.astype(F32) for p in ps], axis=0).astype(BF16)
    parts = _dot(stack(_split3(x)), upper)
    cs = parts[:h] + parts[h:2 * h] + parts[2 * h:] + carry[...]
    carry[...] = cs[:, tc - 1:]
    aug = _dot_tn(stack(_split3(cs * LOG2E)), sel_ref[...]) + base_ref[...]
    d_att = qa_ref.shape[2]
    qa_ref[0] = aug[:, :d_att].astype(BF16)
    ka_ref[0] = aug[:, d_att:].astype(BF16)


def _forget_cumsum(logf_t):
    b, h, lk = logf_t.shape
    tc = _pick_tile(lk, (512, 384, 256, 128))
    sq, sk, bq, bk = _aug_select(h)
    d_att = h * HD_ATT
    sel = jnp.asarray(np.concatenate([sq.reshape(AUG * h, d_att), sk.reshape(AUG * h, d_att)], axis=1), BF16)
    base = jnp.asarray(np.concatenate([bq, bk], axis=1))
    full = lambda a: pl.BlockSpec(a.shape, lambda i, j: (0,) * a.ndim)
    out = jax.ShapeDtypeStruct((b, lk, d_att), BF16)
    return pl.pallas_call(
        _cumsum_kernel,
        grid=(b, lk // tc),
        in_specs=[pl.BlockSpec((1, h, tc), lambda i, j: (i, 0, j)), full(sel), full(base)],
        out_specs=[pl.BlockSpec((1, tc, d_att), lambda i, j: (i, j, 0))] * 2,
        out_shape=[out, out],
        scratch_shapes=[pltpu.VMEM((h, 1), F32)],
        compiler_params=_params("parallel", "arbitrary"),
        name="forget_cumsum",
    )(logf_t, sel, base)


def _ssd_spread(n_heads):
    e = np.zeros((2 * AUG * n_heads, 2 * n_heads * P_SSD), np.float32)
    for v in range(2):
        for c in range(AUG):
            for h in range(n_heads):
                col = v * n_heads * P_SSD + h * P_SSD
                e[(v * AUG + c) * n_heads + h, col:col + P_SSD] = 1.0
    return e


def _ssd_kernel(xbc_ref, z_ref, sm_ref, smt_ref, cprev_ref, sprev_ref, cw_ref, cb_ref, arow_ref, acol_ref,
                dskip_ref, nw_ref, spread_ref, y_ref, snew_ref, hist, state):
    l = pl.program_id(1)
    q = xbc_ref.shape[1]
    d_ssd = z_ref.shape[2]
    n_heads = d_ssd // P_SSD
    pair = 2 * P_SSD
    hist_rows = hist.shape[0]

    @pl.when(l == 0)
    def _():
        hist[...] = jnp.zeros(hist.shape, F32)
        hist[hist_rows - (CONV_W - 1):, :] = cprev_ref[0]
        state[...] = sprev_ref[0]

    xb = xbc_ref[0]
    r3 = lax.broadcasted_iota(jnp.int32, ((CONV_W - 1) * q, q), 0)
    c3 = lax.broadcasted_iota(jnp.int32, ((CONV_W - 1) * q, q), 1)
    shift = jnp.where(r3 % q - c3 == r3 // q + 1, 1.0, 0.0).astype(BF16)
    shifted = _dot(shift, xb)
    cw = cw_ref[...]
    conv = cb_ref[...] + xb.astype(F32) * cw[CONV_W - 1:CONV_W, :]
    for k in range(CONV_W - 1):
        conv = conv + shifted[k * q:(k + 1) * q, :] * cw[CONV_W - 2 - k:CONV_W - 1 - k, :]
    hrow = lambda j: hist[hist_rows - j:hist_rows - j + 1, :]
    frow = lax.broadcasted_iota(jnp.int32, (hist_rows, 1), 0)
    fix = jnp.zeros((hist_rows, conv.shape[1]), F32)
    for t in range(CONV_W - 1):
        acc = 0.0
        for j in range(1, CONV_W - t):
            acc = acc + hrow(j) * cw[CONV_W - 1 - t - j:CONV_W - t - j, :]
        fix = jnp.where(frow == t, acc, fix)
    conv = jnp.concatenate([conv[:hist_rows] + fix, conv[hist_rows:]], axis=0)
    hist[...] = xb[q - hist_rows:, :].astype(F32)
    act = _silu(conv)

    dt = sm_ref[0][:, :n_heads]
    dt_t = smt_ref[0][:n_heads, :]
    a_row = -jnp.exp(arow_ref[...]) * LOG2E
    a_col = -jnp.exp(acol_ref[...]) * LOG2E
    r = lax.broadcasted_iota(jnp.int32, (q, q), 0)
    c = lax.broadcasted_iota(jnp.int32, (q, q), 1)
    causal = r >= c
    lower = jnp.where(causal, 1.0, 0.0).astype(BF16)
    upper = jnp.where(r <= c, 1.0, 0.0).astype(BF16)
    h0, h1, h2 = _split3(dt * a_row)
    acum = _dot(lower, h0) + _dot(lower, h1) + _dot(lower, h2)
    t0, t1, t2 = _split3(dt_t * a_col)
    acum_t = _dot(t0, upper) + _dot(t1, upper) + _dot(t2, upper)
    pieces = [p.astype(F32) for p in _split3(acum) + _split3(dt)]
    wide = _dot(jnp.concatenate(pieces, axis=1).astype(BF16), spread_ref[...])
    acum_x = wide[:, :d_ssd]
    dt_x = wide[:, d_ssd:]
    a_last = acum[q - 1:q, :]
    e_all = jnp.exp2(a_last)
    e_cum_x = jnp.exp2(acum_x)
    e_end_x = jnp.exp2(acum_x[q - 1:q, :] - acum_x)

    xs = act[:, :d_ssd]
    bm = act[:, d_ssd:d_ssd + G_SSD * N_STATE]
    cm = act[:, d_ssd + G_SSD * N_STATE:]
    bm_b = bm.astype(BF16)
    cm_b = cm.astype(BF16)
    xdt = xs * dt_x
    xdt_b = xdt.astype(BF16)
    xe_b = (xdt * e_end_x).astype(BF16)
    lane = lax.broadcasted_iota(jnp.int32, (1, pair), 1)
    first = lane < P_SSD
    srow = lax.broadcasted_iota(jnp.int32, (pair, 1), 0)
    ys = []
    for p in range(n_heads // 2):
        g = (2 * p * G_SSD) // n_heads
        in_group = first if g == 0 else jnp.logical_not(first)
        sl = slice(p * pair, (p + 1) * pair)
        if (2 * p) % (n_heads // G_SSD) == 0:
            cb = _dot_nt(jnp.where(in_group, cm_b, jnp.zeros_like(cm_b)), bm_b)
        ms = []
        for hh in range(2):
            h = 2 * p + hh
            seg = acum[:, h:h + 1] - acum_t[h:h + 1, :]
            ms.append((cb * jnp.exp2(jnp.where(causal, seg, NEG_BIG))).astype(BF16))
        xp = xdt_b[:, sl]
        zero = jnp.zeros_like(xp)
        y = _dot(jnp.concatenate(ms, axis=1),
                 jnp.concatenate([jnp.where(first, xp, zero), jnp.where(first, zero, xp)], axis=0))
        s_in = state[sl, :]
        y = y + _dot_nt(cm_b, s_in.astype(BF16)) * e_cum_x[:, sl]
        upd = _dot_tn(xe_b[:, sl], bm_b)
        keep = jnp.where(srow < P_SSD, e_all[:, 2 * p:2 * p + 1], e_all[:, 2 * p + 1:2 * p + 2])
        state[sl, :] = s_in * keep + jnp.where(in_group, upd, 0.0)
        ys.append(y)
    y_all = jnp.concatenate(ys, axis=1) + dskip_ref[...] * xs
    yg = y_all * _silu(z_ref[0].astype(F32))
    y_ref[0] = _rms(yg, nw_ref[...]).astype(BF16)
    snew_ref[0] = state[...]


def _ssd(xbc, z, sm, smt, conv_prev, ssm_prev, conv_w, conv_b, a_log, d_skip, norm_w):
    b, l, d_conv = xbc.shape
    d_ssd = z.shape[2]
    n_heads = d_ssd // P_SSD
    hg = n_heads // G_SSD
    q = _pick_tile(l, (256, 128, 64))
    tok = lambda w: pl.BlockSpec((1, q, w), lambda i, j: (i, j, 0))
    full = lambda a: pl.BlockSpec(a.shape, lambda i, j: (0,) * a.ndim)
    a_row = a_log.reshape(1, n_heads)
    a_col = a_log.reshape(n_heads, 1)
    conv_b = conv_b.reshape(1, d_conv)
    d_skip = jnp.repeat(d_skip, P_SSD).reshape(1, d_ssd)
    norm_w = norm_w.reshape(1, d_ssd)
    spread = jnp.asarray(_ssd_spread(n_heads), BF16)
    s4 = ssm_prev.reshape(b, G_SSD, hg * P_SSD, N_STATE)
    s_in = jnp.concatenate([jnp.pad(s4[:, g], ((0, 0), (0, 0), (g * N_STATE, (G_SSD - 1 - g) * N_STATE)))
                            for g in range(G_SSD)], axis=1)
    state_spec = pl.BlockSpec((1, n_heads * P_SSD, G_SSD * N_STATE), lambda i, j: (i, 0, 0))
    y, s_out = pl.pallas_call(
        _ssd_kernel,
        grid=(b, l // q),
        in_specs=[tok(d_conv), tok(d_ssd), tok(SMALL_W), pl.BlockSpec((1, SMALL_W, q), lambda i, j: (i, 0, j)),
                  pl.BlockSpec((1, CONV_W - 1, d_conv), lambda i, j: (i, 0, 0)), state_spec,
                  full(conv_w), full(conv_b), full(a_row), full(a_col), full(d_skip), full(norm_w), full(spread)],
        out_specs=[tok(d_ssd), state_spec],
        out_shape=[jax.ShapeDtypeStruct((b, l, d_ssd), BF16),
                   jax.ShapeDtypeStruct((b, n_heads * P_SSD, G_SSD * N_STATE), F32)],
        scratch_shapes=[pltpu.VMEM((8, d_conv), F32),
                        pltpu.VMEM((n_heads * P_SSD, G_SSD * N_STATE), F32)],
        compiler_params=_params("parallel", "arbitrary"),
        name="ssd",
    )(xbc, z, sm, smt, conv_prev, s_in, conv_w, conv_b, a_row, a_col, d_skip, norm_w, spread)
    s_out = s_out.reshape(b, G_SSD, hg * P_SSD, G_SSD * N_STATE)
    s_new = jnp.concatenate([s_out[:, g, :, g * N_STATE:(g + 1) * N_STATE] for g in range(G_SSD)], axis=1)
    return y, s_new.reshape(b, n_heads, P_SSD, N_STATE)


def _attn_kernel(q_ref, qa_ref, k_ref, ka_ref, v_ref, o_ref, kk_s, m_s, acc_s, *, p0, tq, tk):
    i = pl.program_id(2)
    lane = lax.broadcasted_iota(jnp.int32, (1, 2 * HD_ATT), 1)
    first = lane < HD_ATT
    own = (first, jnp.logical_not(first))
    sum_lane = (HD_ATT, 0)

    @pl.when(i == 0)
    def _():
        k = k_ref[0]
        ka = ka_ref[0]
        kk_s[0] = jnp.where(first, k, ka)
        kk_s[1] = jnp.where(first, ka, k)

    q = q_ref[0]
    qa = qa_ref[0]
    qq = (jnp.where(first, q, qa), jnp.where(first, qa, q))
    m_s[...] = jnp.full(m_s.shape, NEG_BIG, F32)
    acc_s[...] = jnp.zeros(acc_s.shape, F32)
    nc = tk // LANES
    n_split = 2 if tq % (2 * SPLIT_ROWS) == 0 else 1
    rows = tq // n_split

    def step(j, masked):
        off = pl.multiple_of(j * tk, tk)
        v = v_ref[0, pl.ds(off, tk), :]
        vv = [jnp.where(own[hh], v, jnp.where(lane == sum_lane[hh], 1.0, 0.0).astype(BF16)) for hh in range(2)]
        logit = [[_dot_nt(qq[hh][r * rows:(r + 1) * rows], kk_s[hh, pl.ds(off, tk), :]) for hh in range(2)]
                 for r in range(n_split)]
        for r in range(n_split):
            rs = slice(r * rows, (r + 1) * rows)
            if masked:
                q_pos = p0 + i * tq + r * rows + lax.broadcasted_iota(jnp.int32, (rows, tk), 0)
                k_pos = j * tk + lax.broadcasted_iota(jnp.int32, (rows, tk), 1)
                visible = k_pos <= q_pos
            for hh in range(2):
                s = jnp.where(visible, logit[r][hh], NEG_BIG) if masked else logit[r][hh]
                cols = [s[:, c * LANES:(c + 1) * LANES] for c in range(nc)]
                m_cur = functools.reduce(jnp.maximum, cols)
                m_prev = m_s[hh, rs, :]
                m_new = jnp.maximum(m_prev, jnp.max(m_cur, axis=1, keepdims=True))
                alpha = jnp.exp2(m_prev - m_new)
                p = jnp.concatenate([jnp.exp2((col - m_new).astype(BF16)) for col in cols], axis=1)
                m_s[hh, rs, :] = m_new
                acc_s[hh, rs, :] = alpha * acc_s[hh, rs, :] + _dot(p, vv[hh])

    n_full = (p0 + i * tq + 1) // tk
    n_vis = (p0 + i * tq + tq - 1) // tk + 1

    def full_pair(jj, carry):
        step(2 * jj, False)
        step(2 * jj + 1, False)
        return carry

    def masked_body(j, carry):
        step(j, True)
        return carry

    lax.fori_loop(0, n_full // 2, full_pair, 0)
    pl.when(n_full % 2 == 1)(lambda: step(n_full - 1, False))
    lax.fori_loop(n_full, n_vis, masked_body, 0)
    a0 = acc_s[0]
    a1 = acc_s[1]
    o = jnp.where(first, a0 / a0[:, sum_lane[0]:sum_lane[0] + 1], a1 / a1[:, sum_lane[1]:sum_lane[1] + 1])
    o_ref[0] = o.astype(o_ref.dtype)


def _attention(q, q_aug, k, k_aug, v, p0):
    b, lq, d_att = q.shape
    lk = k.shape[1]
    n_heads = d_att // HD_ATT
    tq = _pick_tile(lq, (512, 256, 128, 64))
    tk = lk if lk <= 1536 else _pick_tile(lk, (512, 256, 128))
    assert p0 % tq == 0 and lk >= p0 + lq
    pair = 2 * HD_ATT
    q_spec = pl.BlockSpec((1, tq, pair), lambda bi, hp, i: (bi, i, hp))
    qa_spec = pl.BlockSpec((1, tq, pair), lambda bi, hp, i: (bi, i + p0 // tq, hp))
    kv_spec = pl.BlockSpec((1, lk, pair), lambda bi, hp, i: (bi, 0, hp))
    return pl.pallas_call(
        functools.partial(_attn_kernel, p0=p0, tq=tq, tk=tk),
        grid=(b, n_heads // 2, lq // tq),
        in_specs=[q_spec, qa_spec, kv_spec, kv_spec, kv_spec],
        out_specs=q_spec,
        out_shape=jax.ShapeDtypeStruct((b, lq, d_att), BF16),
        scratch_shapes=[pltpu.VMEM((2, lk, pair), BF16), pltpu.VMEM((2, tq, LANES), F32),
                        pltpu.VMEM((2, tq, pair), F32)],
        compiler_params=_params("parallel", "parallel", "arbitrary"),
        name="fox_attention",
    )(q, q_aug, k, k_aug, v)


ROUTER_ROWS = 32
EXPERT_ROW0 = 8


def _outproj_kernel(ys_ref, oa_ref, x_ref, g1_ref, sh_ref, sc_ref, anw_ref, wos_ref, woa_ref, n2w_ref, wr_ref,
                    br_ref, x1_ref, h2_ref, ld_ref, cnt_ref, wt_s, g_s):
    ya = _rms(oa_ref[0].astype(F32), anw_ref[...]).astype(BF16)
    m = _dot(ys_ref[0], wos_ref[...]) + _dot(ya, woa_ref[...])
    x1 = x_ref[0] + g1_ref[0] * m
    x1_ref[0] = x1
    h2 = _rms(x1, n2w_ref[...]) * (1.0 + sc_ref[0]) + sh_ref[0]
    h_hi, h_lo = _split2(h2)
    wr = wr_ref[...]
    p1 = _dot_nt(wr, h_hi)
    p2 = _dot_nt(wr[:ROUTER_ROWS], h_lo)
    logit = p1[:ROUTER_ROWS] + p1[ROUTER_ROWS:] + p2 + br_ref[...]

    lg = [logit[g:g + 1, :] for g in range(N_EGROUPS)]
    gmax = jnp.maximum(jnp.maximum(lg[0], lg[1]), jnp.maximum(lg[2], lg[3]))
    denom = sum(jnp.exp(x - gmax) for x in lg)
    p_sel = 1.0 / denom
    is_g = []
    taken = jnp.zeros_like(gmax) > 1.0
    for g in range(N_EGROUPS):
        hit = (lg[g] == gmax) & jnp.logical_not(taken)
        is_g.append(hit)
        taken = taken | hit
    le = []
    for e in range(EXPERTS_PER_GROUP):
        v = logit[EXPERT_ROW0 + 3 * EXPERTS_PER_GROUP + e:EXPERT_ROW0 + 3 * EXPERTS_PER_GROUP + e + 1, :]
        for g in range(N_EGROUPS - 2, -1, -1):
            r0 = EXPERT_ROW0 + g * EXPERTS_PER_GROUP + e
            v = jnp.where(is_g[g], logit[r0:r0 + 1, :], v)
        le.append(v)
    m1 = jnp.maximum(jnp.maximum(le[0], le[1]), jnp.maximum(le[2], le[3]))
    first = []
    taken = jnp.zeros_like(m1) > 1.0
    for e in range(EXPERTS_PER_GROUP):
        hit = (le[e] == m1) & jnp.logical_not(taken)
        first.append(hit)
        taken = taken | hit
    rest = [jnp.where(first[e], -jnp.inf, le[e]) for e in range(EXPERTS_PER_GROUP)]
    m2 = jnp.maximum(jnp.maximum(rest[0], rest[1]), jnp.maximum(rest[2], rest[3]))
    second = []
    taken = jnp.zeros_like(m1) > 1.0
    for e in range(EXPERTS_PER_GROUP):
        hit = (rest[e] == m2) & jnp.logical_not(taken)
        second.append(hit)
        taken = taken | hit
    e2 = jnp.exp(m2 - m1)
    w_a = p_sel / (1.0 + e2)
    w_b = w_a * e2
    d = x1.shape[1]
    wt_s[...] = jnp.zeros(wt_s.shape, F32)
    for e in range(EXPERTS_PER_GROUP):
        w = jnp.where(first[e], w_a, jnp.where(second[e], w_b, 0.0))
        w_hi = w.astype(BF16).astype(F32)
        wt_s[e:e + 1, :] = w_hi
        wt_s[EXPERTS_PER_GROUP + e:EXPERTS_PER_GROUP + e + 1, :] = w - w_hi
    h2_ref[:, :d] = h_hi
    h2_ref[:, d:] = wt_s[...].T.astype(BF16)

    tl = logit.shape[1]
    ts = min(SORT_TILE, tl)
    g_s[...] = jnp.zeros(g_s.shape, F32)
    for g in range(N_EGROUPS):
        g_s[g:g + 1, :] = jnp.where(is_g[g], 1.0, 0.0)
    r = lax.broadcasted_iota(jnp.int32, (tl, tl), 0)
    c = lax.broadcasted_iota(jnp.int32, (tl, tl), 1)
    same_tile = (r // ts) == (c // ts)
    upper = jnp.where((r <= c) & same_tile, 1.0, 0.0).astype(BF16)
    cum = _dot(g_s[...].astype(BF16), upper)
    lane = lax.broadcasted_iota(jnp.int32, (1, tl), 1)
    crow = lax.broadcasted_iota(jnp.int32, (8, LANES), 0)
    clane = lax.broadcasted_iota(jnp.int32, (8, LANES), 1)
    ldest = -1.0
    cnt = jnp.zeros((8, LANES), F32)
    for g in range(N_EGROUPS):
        ldest = ldest + jnp.where(is_g[g], cum[g:g + 1, :], 0.0)
    lo = [0.0] * (tl // ts)
    for g in range(N_EGROUPS):
        lo_row = jnp.zeros((1, tl), F32)
        for sub in range(tl // ts):
            n = cum[g:g + 1, (sub + 1) * ts - 1:(sub + 1) * ts]
            n_pad = jnp.ceil(n / SEG_ALIGN) * SEG_ALIGN
            lo_row = jnp.where(lane // ts == sub, lo[sub], lo_row)
            cnt = jnp.where((crow == sub) & (clane == g), n_pad, cnt)
            lo[sub] = lo[sub] + n_pad
        ldest = ldest + jnp.where(is_g[g], lo_row, 0.0)
    ld_ref[0] = ldest.astype(jnp.int32)
    cnt_ref[0] = cnt.astype(jnp.int32)


def _outproj(y_ssd, o_att, x, mod3, boff, attn_norm_w, wo_ssd, wo_att, norm2_w, wr, br):
    b, l, d = x.shape
    tl = _pick_tile(l, (512, 256, 128, 64))
    nl = l // tl
    d_ssd, d_att = y_ssd.shape[2], o_att.shape[2]
    row = lambda col: pl.BlockSpec((1, 1, d), lambda i, j, col=col: (i + boff, 0, col))
    full = lambda a: pl.BlockSpec(a.shape, lambda i, j: (0,) * a.ndim)
    tok = lambda w: pl.BlockSpec((1, tl, w), lambda i, j: (i, j, 0))
    return pl.pallas_call(
        _outproj_kernel,
        grid=(b, nl),
        in_specs=[tok(d_ssd), tok(d_att), tok(d), row(2), row(3), row(4), full(attn_norm_w), full(wo_ssd),
                  full(wo_att), full(norm2_w), full(wr), full(br)],
        out_specs=[tok(d), pl.BlockSpec((tl, d + LANES), lambda i, j: (i * nl + j, 0)),
                   pl.BlockSpec((1, 1, tl), lambda i, j: (i * nl + j, 0, 0)),
                   pl.BlockSpec((1, 8, LANES), lambda i, j: (i * nl + j, 0, 0))],
        out_shape=[jax.ShapeDtypeStruct((b, l, d), F32),
                   jax.ShapeDtypeStruct((b * l, d + LANES), BF16),
                   jax.ShapeDtypeStruct((b * nl, 1, tl), jnp.int32),
                   jax.ShapeDtypeStruct((b * nl, 8, LANES), jnp.int32)],
        scratch_shapes=[pltpu.VMEM((LANES, tl), F32), pltpu.VMEM((8, tl), F32)],
        compiler_params=_params("parallel", "parallel"),
        name="outproj_router",
    )(y_ssd, o_att, x, mod3, mod3, mod3, attn_norm_w, wo_ssd, wo_att, norm2_w, wr, br)


def _sort_plan(cnt, n_tiles, ts, chunk, n_blocks):
    ns = n_tiles // cnt.shape[0]
    n_pad = cnt[:, :ns, :N_EGROUPS].reshape(n_tiles, N_EGROUPS)
    lo = jnp.cumsum(n_pad, axis=1) - n_pad
    region = (jnp.sum(n_pad, axis=0) + chunk + MOE_BLOCK - 1) // MOE_BLOCK * MOE_BLOCK
    end = jnp.cumsum(region)
    off = (end - region)[None, :] + jnp.cumsum(n_pad, axis=0) - n_pad
    blk = jnp.arange(n_blocks, dtype=jnp.int32) * MOE_BLOCK
    blk_group = jnp.minimum(jnp.sum(blk[:, None] >= end[None, :], axis=1), N_EGROUPS - 1).astype(jnp.int32)
    n_used = (end[-1] // MOE_BLOCK).astype(jnp.int32).reshape(1)
    flat = lambda a: a.astype(jnp.int32).reshape(-1)
    return flat(lo), flat(off), flat(n_pad), blk_group, n_used


def _sort_kernel(lo_ref, off_ref, np_ref, x_ref, ld_ref, init_ref, out_ref, cbuf, sems, *, ts, chunk):
    del init_ref
    i = pl.program_id(0)
    slot = i % 2
    rows = ts + N_EGROUPS * SEG_ALIGN

    @pl.when(i == 0)
    def _():
        cbuf[...] = jnp.zeros(cbuf.shape, BF16)

    r = lax.broadcasted_iota(jnp.int32, (rows, ts), 0)
    perm = jnp.where(r == ld_ref[0], 1.0, 0.0).astype(BF16)
    cbuf[slot, 0:rows, :] = _dot(perm, x_ref[...]).astype(BF16)

    def copies(step, sl):
        out = []
        for g in range(N_EGROUPS):
            lo = pl.multiple_of(lo_ref[step * N_EGROUPS + g], SEG_ALIGN)
            off = pl.multiple_of(off_ref[step * N_EGROUPS + g], SEG_ALIGN)
            n_pad = np_ref[step * N_EGROUPS + g]
            for c in range(ts // chunk):
                cp = pltpu.make_async_copy(cbuf.at[sl, pl.ds(lo + c * chunk, chunk)],
                                           out_ref.at[pl.ds(off + c * chunk, chunk)], sems.at[g, c])
                out.append((None if c == 0 else n_pad > c * chunk, cp))
        return out

    def for_each(step, sl, act):
        for pred, cp in copies(step, sl):
            if pred is None:
                act(cp)
            else:
                pl.when(pred)(functools.partial(act, cp))

    @pl.when(i > 0)
    def _():
        for_each(i - 1, 1 - slot, lambda cp: cp.wait())

    for_each(i, slot, lambda cp: cp.start())

    @pl.when(i == pl.num_programs(0) - 1)
    def _():
        for_each(i, slot, lambda cp: cp.wait())


def _experts_kernel(grp_ref, nb_ref, x_ref, wg_ref, wu_ref, wd_ref, o_ref):
    del grp_ref
    b = pl.program_id(0)
    d = o_ref.shape[1]

    @pl.when(b < nb_ref[0])
    def _():
        blk = x_ref[...]
        x = blk[:, :d]
        wp = blk[:, d:].astype(F32)
        acc = jnp.zeros(o_ref.shape, F32)
        for e in range(EXPERTS_PER_GROUP):
            w_e = wp[:, e:e + 1] + wp[:, EXPERTS_PER_GROUP + e:EXPERTS_PER_GROUP + e + 1]
            hid = _silu(_dot(x, wg_ref[e])) * _dot(x, wu_ref[e]) * w_e
            acc = acc + _dot(hid.astype(BF16), wd_ref[e])
        o_ref[...] = acc.astype(BF16)

    @pl.when(b >= nb_ref[0])
    def _():
        o_ref[...] = jnp.zeros(o_ref.shape, BF16)


def _combine_kernel(lo_ref, off_ref, np_ref, ld_ref, x1_ref, g2_ref, shf_ref, scf_ref, fnw_ref, src_ref, y_ref,
                    seg, sems, *, ts, chunk, final_norm):
    i = pl.program_id(0)
    slot = i % 2

    def fetch(step, sl, act):
        for g in range(N_EGROUPS):
            off = pl.multiple_of(off_ref[step * N_EGROUPS + g], SEG_ALIGN)
            n_pad = np_ref[step * N_EGROUPS + g]
            for c in range(ts // chunk):
                cp = pltpu.make_async_copy(src_ref.at[pl.ds(off + c * chunk, chunk)],
                                           seg.at[sl, g, pl.ds(c * chunk, chunk)], sems.at[sl, g, c])
                if c == 0:
                    act(cp)
                else:
                    pl.when(n_pad > c * chunk)(functools.partial(act, cp))

    @pl.when(i == 0)
    def _():
        seg[...] = jnp.zeros(seg.shape, BF16)
        fetch(0, 0, lambda cp: cp.start())

    @pl.when(i + 1 < pl.num_programs(0))
    def _():
        fetch(i + 1, 1 - slot, lambda cp: cp.start())

    fetch(i, slot, lambda cp: cp.wait())

    ld = ld_ref[0]
    r = lax.broadcasted_iota(jnp.int32, (ts, ts), 0)
    moe = jnp.zeros(y_ref.shape, F32)
    for g in range(N_EGROUPS):
        lo = lo_ref[i * N_EGROUPS + g]
        n_pad = np_ref[i * N_EGROUPS + g]
        perm = jnp.where((r + lo == ld) & (r < n_pad), 1.0, 0.0).astype(BF16)
        moe = moe + _dot_tn(perm, seg[slot, g])
    xo = x1_ref[...] + g2_ref[0] * moe
    if final_norm:
        xo = _rms(xo, fnw_ref[...]) * (1.0 + scf_ref[0]) + shf_ref[0]
    y_ref[...] = xo


def _moe_final(h2ext, ldest, cnt, wg, wu, wd, x1, mod3, modf3, boff, rows_per_batch, final_norm_w, final_norm):
    t, d = x1.shape
    payload = h2ext.shape[1]
    ts = min(SORT_TILE, rows_per_batch)
    chunk = min(MOE_CHUNK, ts)
    n_tiles = t // ts
    ldest = ldest.reshape(n_tiles, 1, ts)
    n_blocks = -(-(t + n_tiles * N_EGROUPS * (SEG_ALIGN - 1) + N_EGROUPS * (chunk + MOE_BLOCK - 1)) // MOE_BLOCK) + 1
    lo, off, n_pad, blk_group, n_used = _sort_plan(cnt, n_tiles, ts, chunk, n_blocks)

    sorted_rows = pl.pallas_call(
        functools.partial(_sort_kernel, ts=ts, chunk=chunk),
        grid_spec=pltpu.PrefetchScalarGridSpec(
            num_scalar_prefetch=3,
            grid=(n_tiles,),
            in_specs=[pl.BlockSpec((ts, payload), lambda i, *_: (i, 0)),
                      pl.BlockSpec((1, 1, ts), lambda i, *_: (i, 0, 0)),
                      pl.BlockSpec(memory_space=pl.ANY)],
            out_specs=pl.BlockSpec(memory_space=pl.ANY),
            scratch_shapes=[pltpu.VMEM((2, ts + N_EGROUPS * SEG_ALIGN + ts, payload), BF16),
                            pltpu.SemaphoreType.DMA((N_EGROUPS, ts // chunk))]),
        out_shape=jax.ShapeDtypeStruct((n_blocks * MOE_BLOCK, payload), BF16),
        input_output_aliases={5: 0},
        compiler_params=_params("arbitrary"),
        name="moe_sort",
    )(lo, off, n_pad, h2ext, ldest, jnp.zeros((n_blocks * MOE_BLOCK, payload), BF16))

    d_e = wg.shape[2]
    live = lambda b, nb: jnp.minimum(b, nb[0] - 1)
    out_sorted = pl.pallas_call(
        _experts_kernel,
        grid_spec=pltpu.PrefetchScalarGridSpec(
            num_scalar_prefetch=2,
            grid=(n_blocks,),
            in_specs=[pl.BlockSpec((MOE_BLOCK, payload), lambda b, grp, nb: (live(b, nb), 0)),
                      pl.BlockSpec((EXPERTS_PER_GROUP, d, d_e), lambda b, grp, nb: (grp[b], 0, 0)),
                      pl.BlockSpec((EXPERTS_PER_GROUP, d, d_e), lambda b, grp, nb: (grp[b], 0, 0)),
                      pl.BlockSpec((EXPERTS_PER_GROUP, d_e, d), lambda b, grp, nb: (grp[b], 0, 0))],
            out_specs=pl.BlockSpec((MOE_BLOCK, d), lambda b, grp, nb: (b, 0))),
        out_shape=jax.ShapeDtypeStruct((n_blocks * MOE_BLOCK, d), BF16),
        compiler_params=_params("arbitrary"),
        name="moe_experts",
    )(blk_group, n_used, sorted_rows, wg, wu, wd)

    per = rows_per_batch // ts
    row = lambda col: pl.BlockSpec((1, 1, d), lambda i, *_, col=col: (i // per + boff, 0, col))
    return pl.pallas_call(
        functools.partial(_combine_kernel, ts=ts, chunk=chunk, final_norm=final_norm),
        grid_spec=pltpu.PrefetchScalarGridSpec(
            num_scalar_prefetch=3,
            grid=(n_tiles,),
            in_specs=[pl.BlockSpec((1, 1, ts), lambda i, *_: (i, 0, 0)),
                      pl.BlockSpec((ts, d), lambda i, *_: (i, 0)),
                      row(5), row(0), row(1),
                      pl.BlockSpec((1, d), lambda i, *_: (0, 0)),
                      pl.BlockSpec(memory_space=pl.ANY)],
            out_specs=pl.BlockSpec((ts, d), lambda i, *_: (i, 0)),
            scratch_shapes=[pltpu.VMEM((2, N_EGROUPS, ts, d), BF16),
                            pltpu.SemaphoreType.DMA((2, N_EGROUPS, ts // chunk))]),
        out_shape=jax.ShapeDtypeStruct((t, d), F32),
        compiler_params=_params("arbitrary"),
        name="moe_combine",
    )(lo, off, n_pad, ldest, x1, mod3, modf3, modf3, final_norm_w, out_sorted)


def _layer(x, mod3, modf3, boff, k_prev, v_prev, logf_prev, conv_prev, ssm_prev, p, final_norm_w, final_norm):
    b, l, d = x.shape
    z, xbc, q, k, v, k_b, v_b, sm, smt, conv_new = _inproj(
        x, mod3, boff, p["norm1_w"], p["wz"], p["wx"], p["wq"], p["wk"], p["wv"], p["ws"], p["bs"])
    n_heads_ssd = p["wz"].shape[1] // P_SSD
    y_ssd, ssm_new = _ssd(xbc, z, sm, smt, conv_prev, ssm_prev, p["conv_w"], p["conv_b"], p["a_log"],
                          p["d_skip"], p["ssd_norm_w"])
    n_heads = p["wq"].shape[1] // HD_ATT
    logf_t = smt[:, n_heads_ssd:n_heads_ssd + n_heads, :]
    if k_prev is None:
        p0, k_all, v_all, lf_all = 0, k_b, v_b, logf_t
    else:
        p0 = k_prev.shape[1]
        pad = (-(p0 + l)) % LANES
        zeros = lambda w: jnp.zeros((b, pad, w), BF16)
        k_all = jnp.concatenate([k_prev.astype(BF16), k_b, zeros(k_b.shape[2])], axis=1)
        v_all = jnp.concatenate([v_prev.astype(BF16), v_b, zeros(v_b.shape[2])], axis=1)
        lf_all = jnp.concatenate([jnp.swapaxes(logf_prev, 1, 2), logf_t, jnp.zeros((b, n_heads, pad), F32)], axis=2)
    q_aug, k_aug = _forget_cumsum(lf_all)
    o_att = _attention(q, q_aug, k_all, k_aug, v_all, p0)
    x1, h2ext, ldest, cnt = _outproj(y_ssd, o_att, x, mod3, boff, p["attn_norm_w"], p["wo_ssd"], p["wo_att"],
                                     p["norm2_w"], p["wr"], p["br"])
    y = _moe_final(h2ext, ldest, cnt, p["wg"], p["wu"], p["wd"], x1.reshape(b * l, d), mod3, modf3, boff, l,
                   final_norm_w, final_norm)
    return y.reshape(b, l, d), (k, v, jnp.swapaxes(logf_t, 1, 2), conv_new, ssm_new)


def kernel(x_prompt, x_sample, c_prompt, c_sample, cache_k, cache_v, cache_logf, state_conv, state_ssm, norm1_w, w_ada, b_ada, w_in, conv_w, conv_b, dt_bias, a_log, d_skip, ssd_norm_w, f_bias, attn_norm_w, w_out, norm2_w, w_rg, b_rg, w_re, b_re, w_gate, w_up, w_down, final_norm_w, w_ada_f, b_ada_f):
    depth = w_in.shape[0]
    bp, lp, d = x_prompt.shape
    bs = x_sample.shape[0]
    d_conv = conv_w.shape[2]
    d_ssd = ssd_norm_w.shape[1]
    d_att = attn_norm_w.shape[1]
    h_ssd = dt_bias.shape[1]
    h_att = f_bias.shape[1]
    assert h_ssd + h_att == SMALL_W and d_att // HD_ATT == h_att and d_ssd // P_SSD == h_ssd

    c_all = jnp.concatenate([c_prompt, c_sample], axis=0)
    modf3 = _modulation(c_all, w_ada_f, b_ada_f).reshape(bp + bs, 1, 2 * d)
    final_w = final_norm_w.reshape(1, d)

    i0 = d_ssd
    i1 = i0 + d_conv
    i2 = i1 + h_ssd
    i3 = i2 + d_att
    i4 = i3 + d_att
    i5 = i4 + d_att
    yp, ys = x_prompt, x_sample
    outs_p, outs_s = [], []
    for layer in range(depth):
        mod3 = _modulation(c_all, w_ada[layer], b_ada[layer]).reshape(bp + bs, 1, 6 * d)
        wi = w_in[layer]
        w_small = jnp.concatenate([wi[:, i1:i2], wi[:, i5:], jnp.zeros((d, LANES - SMALL_W), F32)], axis=1)
        b_small = jnp.concatenate([dt_bias[layer], f_bias[layer], jnp.zeros((LANES - SMALL_W,), F32)])
        wr = jnp.zeros((ROUTER_ROWS, d), F32)
        wr = wr.at[:N_EGROUPS].set(w_rg[layer].T)
        wr = wr.at[EXPERT_ROW0:EXPERT_ROW0 + N_EXPERTS].set(
            jnp.transpose(w_re[layer], (0, 2, 1)).reshape(N_EXPERTS, d))
        wr_hi, wr_lo = _split2(wr)
        br = jnp.zeros((ROUTER_ROWS,), F32)
        br = br.at[:N_EGROUPS].set(b_rg[layer])
        br = br.at[EXPERT_ROW0:EXPERT_ROW0 + N_EXPERTS].set(b_re[layer].reshape(N_EXPERTS))
        p = dict(
            norm1_w=norm1_w[layer].reshape(1, d),
            wz=wi[:, :i0].astype(BF16), wx=wi[:, i0:i1].astype(BF16),
            wq=(wi[:, i2:i3] * (LOG2E * HD_ATT ** -0.5)).astype(BF16),
            wk=wi[:, i3:i4].astype(BF16), wv=wi[:, i4:i5].astype(BF16),
            ws=w_small.astype(BF16), bs=b_small.reshape(1, LANES),
            conv_w=conv_w[layer], conv_b=conv_b[layer], a_log=a_log[layer], d_skip=d_skip[layer],
            ssd_norm_w=ssd_norm_w[layer], attn_norm_w=attn_norm_w[layer].reshape(1, d_att),
            wo_ssd=w_out[layer][:d_ssd].astype(BF16), wo_att=w_out[layer][d_ssd:].astype(BF16),
            norm2_w=norm2_w[layer].reshape(1, d),
            wr=jnp.concatenate([wr_hi, wr_lo], axis=0), br=br.reshape(ROUTER_ROWS, 1),
            wg=w_gate[layer].astype(BF16), wu=w_up[layer].astype(BF16), wd=w_down[layer].astype(BF16),
        )
        conv0 = jnp.zeros((bp, CONV_W - 1, d_conv), F32)
        ssm0 = jnp.zeros((bp, h_ssd, P_SSD, N_STATE), F32)
        last = layer == depth - 1
        yp, st_p = _layer(yp, mod3, modf3, 0, None, None, None, conv0, ssm0, p, final_w, last)
        ck = cache_k[layer].reshape(bs, -1, d_att)
        cv = cache_v[layer].reshape(bs, -1, d_att)
        ys, st_s = _layer(ys, mod3, modf3, bp, ck, cv, cache_logf[layer], state_conv[layer], state_ssm[layer], p,
                          final_w, last)
        outs_p.append(st_p)
        outs_s.append(st_s)

    def stack(outs, b, l):
        k = jnp.stack([o[0].reshape(b, l, h_att, HD_ATT) for o in outs])
        v = jnp.stack([o[1].reshape(b, l, h_att, HD_ATT) for o in outs])
        return (k, v, jnp.stack([o[2] for o in outs]), jnp.stack([o[3] for o in outs]),
                jnp.stack([o[4] for o in outs]))

    return (yp, ys) + stack(outs_p, bp, lp) + stack(outs_s, bs, x_sample.shape[1])
```

```python
import functools

import jax
import jax.numpy as jnp
import numpy as np
from jax import lax
from jax.experimental import pallas as pl
from jax.experimental.pallas import tpu as pltpu

F32 = jnp.float32
BF16 = jnp.bfloat16

P_SSD = 64
N_STATE = 64
G_SSD = 2
CONV_W = 4
HD_ATT = 64
N_EGROUPS = 4
EXPERTS_PER_GROUP = 4
N_EXPERTS = N_EGROUPS * EXPERTS_PER_GROUP
EPS = 1e-6
NEG_BIG = -1e30

LANES = 128
SEG_ALIGN = 16
SORT_TILE = 256
MOE_BLOCK = 512
MOE_CHUNK = 64
SMALL_W = 16
VMEM_LIMIT = 56 * 1024 * 1024


def _params(*sem):
    return pltpu.CompilerParams(dimension_semantics=sem, vmem_limit_bytes=VMEM_LIMIT)


def _split2(x):
    hi = x.astype(BF16)
    lo = (x - hi.astype(F32)).astype(BF16)
    return hi, lo


def _split3(x):
    hi = x.astype(BF16)
    r = x - hi.astype(F32)
    mid = r.astype(BF16)
    lo = (r - mid.astype(F32)).astype(BF16)
    return hi, mid, lo


def _dot(a, b):
    return jnp.dot(a, b, preferred_element_type=F32)


def _dot_nt(a, b):
    return lax.dot_general(a, b, (((1,), (1,)), ((), ())), preferred_element_type=F32)


def _dot_tn(a, b):
    return lax.dot_general(a, b, (((0,), (0,)), ((), ())), preferred_element_type=F32)


def _silu(x):
    return x / (1.0 + jnp.exp(-x))


def _rms(x, w):
    return x * lax.rsqrt(jnp.mean(x * x, axis=-1, keepdims=True) + EPS) * w


def _pick_tile(n, candidates):
    for c in candidates:
        if n % c == 0:
            return c
    return n


def _mod_kernel(c_ref, w_ref, b_ref, o_ref):
    a = _silu(c_ref[...])
    a_hi, a_lo = _split2(a)
    w_hi, w_lo = _split2(w_ref[...])
    o_ref[...] = _dot(a_hi, w_hi) + _dot(a_lo, w_hi) + _dot(a_hi, w_lo) + b_ref[...]


def _modulation(c, w, b):
    m, d = c.shape
    n = w.shape[1]
    tn = _pick_tile(n, (1024, 512, 256, 128))
    return pl.pallas_call(
        _mod_kernel,
        grid=(n // tn,),
        in_specs=[pl.BlockSpec((m, d), lambda j: (0, 0)),
                  pl.BlockSpec((d, tn), lambda j: (0, j)),
                  pl.BlockSpec((1, tn), lambda j: (0, j))],
        out_specs=pl.BlockSpec((m, tn), lambda j: (0, j)),
        out_shape=jax.ShapeDtypeStruct((m, n), F32),
        compiler_params=_params("parallel"),
        name="adaln_mod",
    )(c, w, b.reshape(1, n))


def _inproj_kernel(x_ref, sh_ref, sc_ref, nw_ref, wz_ref, wx_ref, wq_ref, wk_ref, wv_ref, ws_ref, bs_ref,
                   z_ref, xbc_ref, q_ref, k_ref, v_ref, kb_ref, vb_ref, sm_ref, smt_ref, tail_ref):
    l = pl.program_id(1)
    x = x_ref[0]
    h = (_rms(x, nw_ref[...]) * (1.0 + sc_ref[0]) + sh_ref[0]).astype(BF16)
    z_ref[0] = _dot(h, wz_ref[...]).astype(BF16)
    xbc = _dot(h, wx_ref[...])
    xbc_ref[0] = xbc.astype(BF16)
    q_ref[0] = _dot(h, wq_ref[...]).astype(BF16)
    k = _dot(h, wk_ref[...])
    k_ref[0] = k
    kb_ref[0] = k.astype(BF16)
    v = _dot(h, wv_ref[...])
    v_ref[0] = v
    vb_ref[0] = v.astype(BF16)
    s = _dot(h, ws_ref[...]) + bs_ref[...]
    t = jnp.log(1.0 + jnp.exp(-jnp.abs(s)))
    lane = lax.broadcasted_iota(jnp.int32, s.shape, 1)
    s = jnp.where(lane < SMALL_W // 2, jnp.maximum(s, 0.0) + t, jnp.minimum(s, 0.0) - t)
    sm_ref[0] = s[:, :SMALL_W]
    smt_ref[0] = s.T[:SMALL_W, :]

    @pl.when(l == pl.num_programs(1) - 1)
    def _():
        tl = xbc.shape[0]
        tail_ref[0] = xbc[tl - (CONV_W - 1):, :]


def _inproj(x, mod3, boff, norm_w, wz, wx, wq, wk, wv, ws, bs):
    b, l, d = x.shape
    tl = _pick_tile(l, (512, 256, 128, 64))
    nl = l // tl
    d_ssd, d_conv, d_att = wz.shape[1], wx.shape[1], wq.shape[1]
    row = lambda col: pl.BlockSpec((1, 1, d), lambda i, j, col=col: (i + boff, 0, col))
    full = lambda a: pl.BlockSpec(a.shape, lambda i, j: (0,) * a.ndim)
    tok = lambda w: pl.BlockSpec((1, tl, w), lambda i, j: (i, j, 0))
    out_shape = [
        jax.ShapeDtypeStruct((b, l, d_ssd), BF16),
        jax.ShapeDtypeStruct((b, l, d_conv), BF16),
        jax.ShapeDtypeStruct((b, l, d_att), BF16),
        jax.ShapeDtypeStruct((b, l, d_att), F32),
        jax.ShapeDtypeStruct((b, l, d_att), F32),
        jax.ShapeDtypeStruct((b, l, d_att), BF16),
        jax.ShapeDtypeStruct((b, l, d_att), BF16),
        jax.ShapeDtypeStruct((b, l, SMALL_W), F32),
        jax.ShapeDtypeStruct((b, SMALL_W, l), F32),
        jax.ShapeDtypeStruct((b, CONV_W - 1, d_conv), F32),
    ]
    out_specs = [tok(d_ssd), tok(d_conv), tok(d_att), tok(d_att), tok(d_att), tok(d_att), tok(d_att),
                 tok(SMALL_W), pl.BlockSpec((1, SMALL_W, tl), lambda i, j: (i, 0, j)),
                 pl.BlockSpec((1, CONV_W - 1, d_conv), lambda i, j: (i, 0, 0))]
    return pl.pallas_call(
        _inproj_kernel,
        grid=(b, nl),
        in_specs=[tok(d), row(0), row(1), full(norm_w), full(wz), full(wx), full(wq), full(wk), full(wv),
                  full(ws), full(bs)],
        out_specs=out_specs,
        out_shape=out_shape,
        compiler_params=_params("parallel", "arbitrary"),
        name="inproj",
    )(x, mod3, mod3, norm_w, wz, wx, wq, wk, wv, ws, bs)


LOG2E = 1.4426950408889634
AUG = 3
SPLIT_ROWS = 128


PIECE_COLS = 32


def _aug_select(n_heads):
    pair = 2 * HD_ATT
    sq = np.zeros((n_heads // 2, PIECE_COLS, pair), np.float32)
    sk = np.zeros((n_heads // 2, PIECE_COLS, pair), np.float32)
    one = AUG * n_heads
    for h in range(n_heads):
        slot = ((h ^ 1) % 2) * HD_ATT
        for c in range(AUG):
            sq[h // 2, c * n_heads + h, slot + c] = 1.0
            sq[h // 2, one, slot + AUG + c] = 1.0
            sk[h // 2, one, slot + c] = 1.0
            sk[h // 2, c * n_heads + h, slot + AUG + c] = -1.0
    return sq, sk


def _cumsum_kernel(x_ref, eye_ref, o_ref, carry):
    @pl.when(pl.program_id(1) == 0)
    def _():
        carry[...] = jnp.zeros_like(carry)

    x = x_ref[0]
    h, tc = x.shape
    r = lax.broadcasted_iota(jnp.int32, (tc, tc), 0)
    c = lax.broadcasted_iota(jnp.int32, (tc, tc), 1)
    upper = jnp.where(r <= c, 1.0, 0.0).astype(BF16)
    stack = lambda ps: jnp.concatenate([p.astype(F32) for p in ps], axis=0)
    parts = _dot(stack(_split3(x)).astype(BF16), upper)
    cs = parts[:h] + parts[h:2 * h] + parts[2 * h:] + carry[...]
    carry[...] = cs[:, tc - 1:]
    rows = lax.broadcasted_iota(jnp.int32, (PIECE_COLS - AUG * h, tc), 0)
    pieces = jnp.concatenate([stack(_split3(cs * LOG2E)), jnp.where(rows == 0, 1.0, 0.0)], axis=0)
    o_ref[0] = _dot_tn(pieces.astype(BF16), eye_ref[...]).astype(BF16)


def _forget_cumsum(logf_t):
    b, h, lk = logf_t.shape
    assert AUG * h < PIECE_COLS
    tc = _pick_tile(lk, (512, 384, 256, 128))
    eye = jnp.asarray(np.eye(PIECE_COLS), BF16)
    return pl.pallas_call(
        _cumsum_kernel,
        grid=(b, lk // tc),
        in_specs=[pl.BlockSpec((1, h, tc), lambda i, j: (i, 0, j)),
                  pl.BlockSpec((PIECE_COLS, PIECE_COLS), lambda i, j: (0, 0))],
        out_specs=pl.BlockSpec((1, tc, PIECE_COLS), lambda i, j: (i, j, 0)),
        out_shape=jax.ShapeDtypeStruct((b, lk, PIECE_COLS), BF16),
        scratch_shapes=[pltpu.VMEM((h, 1), F32)],
        compiler_params=_params("parallel", "arbitrary"),
        name="forget_cumsum",
    )(logf_t, eye)


def _ssd_spread(n_heads):
    e = np.zeros((2 * AUG * n_heads, 2 * n_heads * P_SSD), np.float32)
    for v in range(2):
        for c in range(AUG):
            for h in range(n_heads):
                col = v * n_heads * P_SSD + h * P_SSD
                e[(v * AUG + c) * n_heads + h, col:col + P_SSD] = 1.0
    return e


def _ssd_kernel(xbc_ref, z_ref, sm_ref, smt_ref, cprev_ref, sprev_ref, cw_ref, cb_ref, arow_ref, acol_ref,
                dskip_ref, nw_ref, spread_ref, y_ref, snew_ref, hist, state):
    l = pl.program_id(1)
    q = xbc_ref.shape[1]
    d_ssd = z_ref.shape[2]
    n_heads = d_ssd // P_SSD
    pair = 2 * P_SSD
    hist_rows = hist.shape[0]

    @pl.when(l == 0)
    def _():
        hist[...] = jnp.zeros(hist.shape, F32)
        hist[hist_rows - (CONV_W - 1):, :] = cprev_ref[0]
        state[...] = sprev_ref[0]

    xb = xbc_ref[0]
    r3 = lax.broadcasted_iota(jnp.int32, ((CONV_W - 1) * q, q), 0)
    c3 = lax.broadcasted_iota(jnp.int32, ((CONV_W - 1) * q, q), 1)
    shift = jnp.where(r3 % q - c3 == r3 // q + 1, 1.0, 0.0).astype(BF16)
    shifted = _dot(shift, xb)
    cw = cw_ref[...]
    conv = cb_ref[...] + xb.astype(F32) * cw[CONV_W - 1:CONV_W, :]
    for k in range(CONV_W - 1):
        conv = conv + shifted[k * q:(k + 1) * q, :] * cw[CONV_W - 2 - k:CONV_W - 1 - k, :]
    hrow = lambda j: hist[hist_rows - j:hist_rows - j + 1, :]
    frow = lax.broadcasted_iota(jnp.int32, (hist_rows, 1), 0)
    fix = jnp.zeros((hist_rows, conv.shape[1]), F32)
    for t in range(CONV_W - 1):
        acc = 0.0
        for j in range(1, CONV_W - t):
            acc = acc + hrow(j) * cw[CONV_W - 1 - t - j:CONV_W - t - j, :]
        fix = jnp.where(frow == t, acc, fix)
    conv = jnp.concatenate([conv[:hist_rows] + fix, conv[hist_rows:]], axis=0)
    hist[...] = xb[q - hist_rows:, :].astype(F32)
    act = _silu(conv)

    dt = sm_ref[0][:, :n_heads]
    dt_t = smt_ref[0][:n_heads, :]
    a_row = -jnp.exp(arow_ref[...]) * LOG2E
    a_col = -jnp.exp(acol_ref[...]) * LOG2E
    r = lax.broadcasted_iota(jnp.int32, (q, q), 0)
    c = lax.broadcasted_iota(jnp.int32, (q, q), 1)
    causal = r >= c
    lower = jnp.where(causal, 1.0, 0.0).astype(BF16)
    upper = jnp.where(r <= c, 1.0, 0.0).astype(BF16)
    h0, h1, h2 = _split3(dt * a_row)
    acum = _dot(lower, h0) + _dot(lower, h1) + _dot(lower, h2)
    t0, t1, t2 = _split3(dt_t * a_col)
    acum_t = _dot(t0, upper) + _dot(t1, upper) + _dot(t2, upper)
    pieces = [p.astype(F32) for p in _split3(acum) + _split3(dt)]
    wide = _dot(jnp.concatenate(pieces, axis=1).astype(BF16), spread_ref[...])
    acum_x = wide[:, :d_ssd]
    dt_x = wide[:, d_ssd:]
    a_last = acum[q - 1:q, :]
    e_all = jnp.exp2(a_last)
    e_cum_x = jnp.exp2(acum_x)
    e_end_x = jnp.exp2(acum_x[q - 1:q, :] - acum_x)

    xs = act[:, :d_ssd]
    bm = act[:, d_ssd:d_ssd + G_SSD * N_STATE]
    cm = act[:, d_ssd + G_SSD * N_STATE:]
    bm_b = bm.astype(BF16)
    cm_b = cm.astype(BF16)
    xdt = xs * dt_x
    xdt_b = xdt.astype(BF16)
    xe_b = (xdt * e_end_x).astype(BF16)
    lane = lax.broadcasted_iota(jnp.int32, (1, pair), 1)
    first = lane < P_SSD
    srow = lax.broadcasted_iota(jnp.int32, (pair, 1), 0)
    ys = []
    for p in range(n_heads // 2):
        g = (2 * p * G_SSD) // n_heads
        in_group = first if g == 0 else jnp.logical_not(first)
        sl = slice(p * pair, (p + 1) * pair)
        if (2 * p) % (n_heads // G_SSD) == 0:
            cb = _dot_nt(jnp.where(in_group, cm_b, jnp.zeros_like(cm_b)), bm_b)
        ms = []
        for hh in range(2):
            h = 2 * p + hh
            seg = acum[:, h:h + 1] - acum_t[h:h + 1, :]
            ms.append((cb * jnp.exp2(jnp.where(causal, seg, NEG_BIG))).astype(BF16))
        xp = xdt_b[:, sl]
        zero = jnp.zeros_like(xp)
        y = _dot(jnp.concatenate(ms, axis=1),
                 jnp.concatenate([jnp.where(first, xp, zero), jnp.where(first, zero, xp)], axis=0))
        s_in = state[sl, :]
        y = y + _dot_nt(cm_b, s_in.astype(BF16)) * e_cum_x[:, sl]
        upd = _dot_tn(xe_b[:, sl], bm_b)
        keep = jnp.where(srow < P_SSD, e_all[:, 2 * p:2 * p + 1], e_all[:, 2 * p + 1:2 * p + 2])
        state[sl, :] = s_in * keep + jnp.where(in_group, upd, 0.0)
        ys.append(y)
    y_all = jnp.concatenate(ys, axis=1) + dskip_ref[...] * xs
    yg = y_all * _silu(z_ref[0].astype(F32))
    y_ref[0] = _rms(yg, nw_ref[...]).astype(BF16)
    snew_ref[0] = state[...]


def _ssd(xbc, z, sm, smt, conv_prev, ssm_prev, conv_w, conv_b, a_log, d_skip, norm_w):
    b, l, d_conv = xbc.shape
    d_ssd = z.shape[2]
    n_heads = d_ssd // P_SSD
    hg = n_heads // G_SSD
    q = _pick_tile(l, (256, 128, 64))
    tok = lambda w: pl.BlockSpec((1, q, w), lambda i, j: (i, j, 0))
    full = lambda a: pl.BlockSpec(a.shape, lambda i, j: (0,) * a.ndim)
    a_row = a_log.reshape(1, n_heads)
    a_col = a_log.reshape(n_heads, 1)
    conv_b = conv_b.reshape(1, d_conv)
    d_skip = jnp.repeat(d_skip, P_SSD).reshape(1, d_ssd)
    norm_w = norm_w.reshape(1, d_ssd)
    spread = jnp.asarray(_ssd_spread(n_heads), BF16)
    s4 = ssm_prev.reshape(b, G_SSD, hg * P_SSD, N_STATE)
    s_in = jnp.concatenate([jnp.pad(s4[:, g], ((0, 0), (0, 0), (g * N_STATE, (G_SSD - 1 - g) * N_STATE)))
                            for g in range(G_SSD)], axis=1)
    state_spec = pl.BlockSpec((1, n_heads * P_SSD, G_SSD * N_STATE), lambda i, j: (i, 0, 0))
    y, s_out = pl.pallas_call(
        _ssd_kernel,
        grid=(b, l // q),
        in_specs=[tok(d_conv), tok(d_ssd), tok(SMALL_W), pl.BlockSpec((1, SMALL_W, q), lambda i, j: (i, 0, j)),
                  pl.BlockSpec((1, CONV_W - 1, d_conv), lambda i, j: (i, 0, 0)), state_spec,
                  full(conv_w), full(conv_b), full(a_row), full(a_col), full(d_skip), full(norm_w), full(spread)],
        out_specs=[tok(d_ssd), state_spec],
        out_shape=[jax.ShapeDtypeStruct((b, l, d_ssd), BF16),
                   jax.ShapeDtypeStruct((b, n_heads * P_SSD, G_SSD * N_STATE), F32)],
        scratch_shapes=[pltpu.VMEM((8, d_conv), F32),
                        pltpu.VMEM((n_heads * P_SSD, G_SSD * N_STATE), F32)],
        compiler_params=_params("parallel", "arbitrary"),
        name="ssd",
    )(xbc, z, sm, smt, conv_prev, s_in, conv_w, conv_b, a_row, a_col, d_skip, norm_w, spread)
    s_out = s_out.reshape(b, G_SSD, hg * P_SSD, G_SSD * N_STATE)
    s_new = jnp.concatenate([s_out[:, g, :, g * N_STATE:(g + 1) * N_STATE] for g in range(G_SSD)], axis=1)
    return y, s_new.reshape(b, n_heads, P_SSD, N_STATE)


def _attn_kernel(q_ref, pq_ref, k_ref, pk_ref, v_ref, sq_ref, sk_ref, o_ref, kk_s, m_s, acc_s, *, p0, tq, tk):
    i = pl.program_id(2)
    lane = lax.broadcasted_iota(jnp.int32, (1, 2 * HD_ATT), 1)
    first = lane < HD_ATT
    own = (first, jnp.logical_not(first))
    sum_lane = (HD_ATT, 0)

    @pl.when(i == 0)
    def _():
        k = k_ref[0]
        ka = _dot(pk_ref[0], sk_ref[0]).astype(BF16)
        kk_s[0] = jnp.where(first, k, ka)
        kk_s[1] = jnp.where(first, ka, k)

    q = q_ref[0]
    qa = _dot(pq_ref[0], sq_ref[0]).astype(BF16)
    qq = (jnp.where(first, q, qa), jnp.where(first, qa, q))
    m_s[...] = jnp.full(m_s.shape, NEG_BIG, F32)
    acc_s[...] = jnp.zeros(acc_s.shape, F32)
    nc = tk // LANES
    n_split = 2 if tq % (2 * SPLIT_ROWS) == 0 else 1
    rows = tq // n_split

    def step(j, masked):
        off = pl.multiple_of(j * tk, tk)
        v = v_ref[0, pl.ds(off, tk), :]
        vv = [jnp.where(own[hh], v, jnp.where(lane == sum_lane[hh], 1.0, 0.0).astype(BF16)) for hh in range(2)]
        logit = [[_dot_nt(qq[hh][r * rows:(r + 1) * rows], kk_s[hh, pl.ds(off, tk), :]) for hh in range(2)]
                 for r in range(n_split)]
        for r in range(n_split):
            rs = slice(r * rows, (r + 1) * rows)
            if masked:
                q_pos = p0 + i * tq + r * rows + lax.broadcasted_iota(jnp.int32, (rows, tk), 0)
                k_pos = j * tk + lax.broadcasted_iota(jnp.int32, (rows, tk), 1)
                visible = k_pos <= q_pos
            for hh in range(2):
                s = jnp.where(visible, logit[r][hh], NEG_BIG) if masked else logit[r][hh]
                cols = [s[:, c * LANES:(c + 1) * LANES] for c in range(nc)]
                m_cur = functools.reduce(jnp.maximum, cols)
                m_prev = m_s[hh, rs, :]
                m_new = jnp.maximum(m_prev, jnp.max(m_cur, axis=1, keepdims=True))
                alpha = jnp.exp2(m_prev - m_new)
                p = jnp.concatenate([jnp.exp2((col - m_new).astype(BF16)) for col in cols], axis=1)
                m_s[hh, rs, :] = m_new
                acc_s[hh, rs, :] = alpha * acc_s[hh, rs, :] + _dot(p, vv[hh])

    n_full = (p0 + i * tq + 1) // tk
    n_vis = (p0 + i * tq + tq - 1) // tk + 1

    def full_pair(jj, carry):
        step(2 * jj, False)
        step(2 * jj + 1, False)
        return carry

    def masked_body(j, carry):
        step(j, True)
        return carry

    lax.fori_loop(0, n_full // 2, full_pair, 0)
    pl.when(n_full % 2 == 1)(lambda: step(n_full - 1, False))
    lax.fori_loop(n_full, n_vis, masked_body, 0)
    a0 = acc_s[0]
    a1 = acc_s[1]
    o = jnp.where(first, a0 / a0[:, sum_lane[0]:sum_lane[0] + 1], a1 / a1[:, sum_lane[1]:sum_lane[1] + 1])
    o_ref[0] = o.astype(o_ref.dtype)


def _attention(q, k, v, pieces, p0):
    b, lq, d_att = q.shape
    lk = k.shape[1]
    n_heads = d_att // HD_ATT
    tq = _pick_tile(lq, (512, 256, 128, 64))
    tk = lk if lk <= 1536 else _pick_tile(lk, (512, 256, 128))
    assert p0 % tq == 0 and lk >= p0 + lq
    pair = 2 * HD_ATT
    sq, sk = (jnp.asarray(a, BF16) for a in _aug_select(n_heads))
    q_spec = pl.BlockSpec((1, tq, pair), lambda bi, hp, i: (bi, i, hp))
    kv_spec = pl.BlockSpec((1, lk, pair), lambda bi, hp, i: (bi, 0, hp))
    sel_spec = pl.BlockSpec((1, PIECE_COLS, pair), lambda bi, hp, i: (hp, 0, 0))
    return pl.pallas_call(
        functools.partial(_attn_kernel, p0=p0, tq=tq, tk=tk),
        grid=(b, n_heads // 2, lq // tq),
        in_specs=[q_spec, pl.BlockSpec((1, tq, PIECE_COLS), lambda bi, hp, i: (bi, i + p0 // tq, 0)),
                  kv_spec, pl.BlockSpec((1, lk, PIECE_COLS), lambda bi, hp, i: (bi, 0, 0)), kv_spec,
                  sel_spec, sel_spec],
        out_specs=q_spec,
        out_shape=jax.ShapeDtypeStruct((b, lq, d_att), BF16),
        scratch_shapes=[pltpu.VMEM((2, lk, pair), BF16), pltpu.VMEM((2, tq, LANES), F32),
                        pltpu.VMEM((2, tq, pair), F32)],
        compiler_params=_params("parallel", "parallel", "arbitrary"),
        name="fox_attention",
    )(q, pieces, k, pieces, v, sq, sk)


ROUTER_ROWS = 32
EXPERT_ROW0 = 8


def _outproj_kernel(ys_ref, oa_ref, x_ref, g1_ref, sh_ref, sc_ref, anw_ref, wos_ref, woa_ref, n2w_ref, wr_ref,
                    br_ref, x1_ref, h2_ref, ld_ref, cnt_ref, wt_s, g_s):
    ya = _rms(oa_ref[0].astype(F32), anw_ref[...]).astype(BF16)
    m = _dot(ys_ref[0], wos_ref[...]) + _dot(ya, woa_ref[...])
    x1 = x_ref[0] + g1_ref[0] * m
    x1_ref[0] = x1
    h2 = _rms(x1, n2w_ref[...]) * (1.0 + sc_ref[0]) + sh_ref[0]
    h_hi, h_lo = _split2(h2)
    wr = wr_ref[...]
    p1 = _dot_nt(wr, h_hi)
    p2 = _dot_nt(wr[:ROUTER_ROWS], h_lo)
    logit = p1[:ROUTER_ROWS] + p1[ROUTER_ROWS:] + p2 + br_ref[...]

    lg = [logit[g:g + 1, :] for g in range(N_EGROUPS)]
    gmax = jnp.maximum(jnp.maximum(lg[0], lg[1]), jnp.maximum(lg[2], lg[3]))
    denom = sum(jnp.exp(x - gmax) for x in lg)
    p_sel = 1.0 / denom
    is_g = []
    taken = jnp.zeros_like(gmax) > 1.0
    for g in range(N_EGROUPS):
        hit = (lg[g] == gmax) & jnp.logical_not(taken)
        is_g.append(hit)
        taken = taken | hit
    le = []
    for e in range(EXPERTS_PER_GROUP):
        v = logit[EXPERT_ROW0 + 3 * EXPERTS_PER_GROUP + e:EXPERT_ROW0 + 3 * EXPERTS_PER_GROUP + e + 1, :]
        for g in range(N_EGROUPS - 2, -1, -1):
            r0 = EXPERT_ROW0 + g * EXPERTS_PER_GROUP + e
            v = jnp.where(is_g[g], logit[r0:r0 + 1, :], v)
        le.append(v)
    m1 = jnp.maximum(jnp.maximum(le[0], le[1]), jnp.maximum(le[2], le[3]))
    first = []
    taken = jnp.zeros_like(m1) > 1.0
    for e in range(EXPERTS_PER_GROUP):
        hit = (le[e] == m1) & jnp.logical_not(taken)
        first.append(hit)
        taken = taken | hit
    rest = [jnp.where(first[e], -jnp.inf, le[e]) for e in range(EXPERTS_PER_GROUP)]
    m2 = jnp.maximum(jnp.maximum(rest[0], rest[1]), jnp.maximum(rest[2], rest[3]))
    second = []
    taken = jnp.zeros_like(m1) > 1.0
    for e in range(EXPERTS_PER_GROUP):
        hit = (rest[e] == m2) & jnp.logical_not(taken)
        second.append(hit)
        taken = taken | hit
    e2 = jnp.exp(m2 - m1)
    w_a = p_sel / (1.0 + e2)
    w_b = w_a * e2
    d = x1.shape[1]
    wt_s[...] = jnp.zeros(wt_s.shape, F32)
    for e in range(EXPERTS_PER_GROUP):
        w = jnp.where(first[e], w_a, jnp.where(second[e], w_b, 0.0))
        w_hi = w.astype(BF16).astype(F32)
        wt_s[e:e + 1, :] = w_hi
        wt_s[EXPERTS_PER_GROUP + e:EXPERTS_PER_GROUP + e + 1, :] = w - w_hi
    h2_ref[:, :d] = h_hi
    h2_ref[:, d:] = wt_s[...].T.astype(BF16)

    tl = logit.shape[1]
    ts = min(SORT_TILE, tl)
    g_s[...] = jnp.zeros(g_s.shape, F32)
    for g in range(N_EGROUPS):
        g_s[g:g + 1, :] = jnp.where(is_g[g], 1.0, 0.0)
    r = lax.broadcasted_iota(jnp.int32, (tl, tl), 0)
    c = lax.broadcasted_iota(jnp.int32, (tl, tl), 1)
    same_tile = (r // ts) == (c // ts)
    upper = jnp.where((r <= c) & same_tile, 1.0, 0.0).astype(BF16)
    cum = _dot(g_s[...].astype(BF16), upper)
    lane = lax.broadcasted_iota(jnp.int32, (1, tl), 1)
    crow = lax.broadcasted_iota(jnp.int32, (8, LANES), 0)
    clane = lax.broadcasted_iota(jnp.int32, (8, LANES), 1)
    ldest = -1.0
    cnt = jnp.zeros((8, LANES), F32)
    for g in range(N_EGROUPS):
        ldest = ldest + jnp.where(is_g[g], cum[g:g + 1, :], 0.0)
    lo = [0.0] * (tl // ts)
    for g in range(N_EGROUPS):
        lo_row = jnp.zeros((1, tl), F32)
        for sub in range(tl // ts):
            n = cum[g:g + 1, (sub + 1) * ts - 1:(sub + 1) * ts]
            n_pad = jnp.ceil(n / SEG_ALIGN) * SEG_ALIGN
            lo_row = jnp.where(lane // ts == sub, lo[sub], lo_row)
            cnt = jnp.where((crow == sub) & (clane == g), n_pad, cnt)
            lo[sub] = lo[sub] + n_pad
        ldest = ldest + jnp.where(is_g[g], lo_row, 0.0)
    ld_ref[0] = ldest.astype(jnp.int32)
    cnt_ref[0] = cnt.astype(jnp.int32)


def _outproj(y_ssd, o_att, x, mod3, boff, attn_norm_w, wo_ssd, wo_att, norm2_w, wr, br):
    b, l, d = x.shape
    tl = _pick_tile(l, (512, 256, 128, 64))
    nl = l // tl
    d_ssd, d_att = y_ssd.shape[2], o_att.shape[2]
    row = lambda col: pl.BlockSpec((1, 1, d), lambda i, j, col=col: (i + boff, 0, col))
    full = lambda a: pl.BlockSpec(a.shape, lambda i, j: (0,) * a.ndim)
    tok = lambda w: pl.BlockSpec((1, tl, w), lambda i, j: (i, j, 0))
    return pl.pallas_call(
        _outproj_kernel,
        grid=(b, nl),
        in_specs=[tok(d_ssd), tok(d_att), tok(d), row(2), row(3), row(4), full(attn_norm_w), full(wo_ssd),
                  full(wo_att), full(norm2_w), full(wr), full(br)],
        out_specs=[tok(d), pl.BlockSpec((tl, d + LANES), lambda i, j: (i * nl + j, 0)),
                   pl.BlockSpec((1, 1, tl), lambda i, j: (i * nl + j, 0, 0)),
                   pl.BlockSpec((1, 8, LANES), lambda i, j: (i * nl + j, 0, 0))],
        out_shape=[jax.ShapeDtypeStruct((b, l, d), F32),
                   jax.ShapeDtypeStruct((b * l, d + LANES), BF16),
                   jax.ShapeDtypeStruct((b * nl, 1, tl), jnp.int32),
                   jax.ShapeDtypeStruct((b * nl, 8, LANES), jnp.int32)],
        scratch_shapes=[pltpu.VMEM((LANES, tl), F32), pltpu.VMEM((8, tl), F32)],
        compiler_params=_params("parallel", "parallel"),
        name="outproj_router",
    )(y_ssd, o_att, x, mod3, mod3, mod3, attn_norm_w, wo_ssd, wo_att, norm2_w, wr, br)


def _sort_plan(cnt, n_tiles, ts, chunk, n_blocks):
    ns = n_tiles // cnt.shape[0]
    n_pad = cnt[:, :ns, :N_EGROUPS].reshape(n_tiles, N_EGROUPS)
    lo = jnp.cumsum(n_pad, axis=1) - n_pad
    region = (jnp.sum(n_pad, axis=0) + chunk + MOE_BLOCK - 1) // MOE_BLOCK * MOE_BLOCK
    end = jnp.cumsum(region)
    off = (end - region)[None, :] + jnp.cumsum(n_pad, axis=0) - n_pad
    blk = jnp.arange(n_blocks, dtype=jnp.int32) * MOE_BLOCK
    blk_group = jnp.minimum(jnp.sum(blk[:, None] >= end[None, :], axis=1), N_EGROUPS - 1).astype(jnp.int32)
    n_used = (end[-1] // MOE_BLOCK).astype(jnp.int32).reshape(1)
    flat = lambda a: a.astype(jnp.int32).reshape(-1)
    return flat(lo), flat(off), flat(n_pad), blk_group, n_used


def _sort_kernel(lo_ref, off_ref, np_ref, x_ref, ld_ref, init_ref, out_ref, cbuf, sems, *, ts, chunk):
    del init_ref
    i = pl.program_id(0)
    slot = i % 2
    rows = ts + N_EGROUPS * SEG_ALIGN

    @pl.when(i == 0)
    def _():
        cbuf[...] = jnp.zeros(cbuf.shape, BF16)

    r = lax.broadcasted_iota(jnp.int32, (rows, ts), 0)
    perm = jnp.where(r == ld_ref[0], 1.0, 0.0).astype(BF16)
    cbuf[slot, 0:rows, :] = _dot(perm, x_ref[...]).astype(BF16)

    def copies(step, sl):
        out = []
        for g in range(N_EGROUPS):
            lo = pl.multiple_of(lo_ref[step * N_EGROUPS + g], SEG_ALIGN)
            off = pl.multiple_of(off_ref[step * N_EGROUPS + g], SEG_ALIGN)
            n_pad = np_ref[step * N_EGROUPS + g]
            for c in range(ts // chunk):
                cp = pltpu.make_async_copy(cbuf.at[sl, pl.ds(lo + c * chunk, chunk)],
                                           out_ref.at[pl.ds(off + c * chunk, chunk)], sems.at[g, c])
                out.append((None if c == 0 else n_pad > c * chunk, cp))
        return out

    def for_each(step, sl, act):
        for pred, cp in copies(step, sl):
            if pred is None:
                act(cp)
            else:
                pl.when(pred)(functools.partial(act, cp))

    @pl.when(i > 0)
    def _():
        for_each(i - 1, 1 - slot, lambda cp: cp.wait())

    for_each(i, slot, lambda cp: cp.start())

    @pl.when(i == pl.num_programs(0) - 1)
    def _():
        for_each(i, slot, lambda cp: cp.wait())


def _experts_kernel(grp_ref, nb_ref, x_ref, wg_ref, wu_ref, wd_ref, o_ref):
    del grp_ref
    b = pl.program_id(0)
    d = o_ref.shape[1]

    @pl.when(b < nb_ref[0])
    def _():
        blk = x_ref[...]
        x = blk[:, :d]
        wp = blk[:, d:].astype(F32)
        acc = jnp.zeros(o_ref.shape, F32)
        for e in range(EXPERTS_PER_GROUP):
            w_e = wp[:, e:e + 1] + wp[:, EXPERTS_PER_GROUP + e:EXPERTS_PER_GROUP + e + 1]
            hid = _silu(_dot(x, wg_ref[e])) * _dot(x, wu_ref[e]) * w_e
            acc = acc + _dot(hid.astype(BF16), wd_ref[e])
        o_ref[...] = acc.astype(BF16)

    @pl.when(b >= nb_ref[0])
    def _():
        o_ref[...] = jnp.zeros(o_ref.shape, BF16)


def _combine_kernel(lo_ref, off_ref, np_ref, ld_ref, x1_ref, g2_ref, shf_ref, scf_ref, fnw_ref, src_ref, y_ref,
                    seg, sems, *, ts, chunk, final_norm):
    i = pl.program_id(0)
    slot = i % 2

    def fetch(step, sl, act):
        for g in range(N_EGROUPS):
            off = pl.multiple_of(off_ref[step * N_EGROUPS + g], SEG_ALIGN)
            n_pad = np_ref[step * N_EGROUPS + g]
            for c in range(ts // chunk):
                cp = pltpu.make_async_copy(src_ref.at[pl.ds(off + c * chunk, chunk)],
                                           seg.at[sl, g, pl.ds(c * chunk, chunk)], sems.at[sl, g, c])
                if c == 0:
                    act(cp)
                else:
                    pl.when(n_pad > c * chunk)(functools.partial(act, cp))

    @pl.when(i == 0)
    def _():
        seg[...] = jnp.zeros(seg.shape, BF16)
        fetch(0, 0, lambda cp: cp.start())

    @pl.when(i + 1 < pl.num_programs(0))
    def _():
        fetch(i + 1, 1 - slot, lambda cp: cp.start())

    fetch(i, slot, lambda cp: cp.wait())

    ld = ld_ref[0]
    r = lax.broadcasted_iota(jnp.int32, (ts, ts), 0)
    moe = jnp.zeros(y_ref.shape, F32)
    for g in range(N_EGROUPS):
        lo = lo_ref[i * N_EGROUPS + g]
        n_pad = np_ref[i * N_EGROUPS + g]
        perm = jnp.where((r + lo == ld) & (r < n_pad), 1.0, 0.0).astype(BF16)
        moe = moe + _dot_tn(perm, seg[slot, g])
    xo = x1_ref[...] + g2_ref[0] * moe
    if final_norm:
        xo = _rms(xo, fnw_ref[...]) * (1.0 + scf_ref[0]) + shf_ref[0]
    y_ref[...] = xo


def _moe_final(h2ext, ldest, cnt, wg, wu, wd, x1, mod3, modf3, boff, rows_per_batch, final_norm_w, final_norm):
    t, d = x1.shape
    payload = h2ext.shape[1]
    ts = min(SORT_TILE, rows_per_batch)
    chunk = min(MOE_CHUNK, ts)
    n_tiles = t // ts
    ldest = ldest.reshape(n_tiles, 1, ts)
    n_blocks = -(-(t + n_tiles * N_EGROUPS * (SEG_ALIGN - 1) + N_EGROUPS * (chunk + MOE_BLOCK - 1)) // MOE_BLOCK) + 1
    lo, off, n_pad, blk_group, n_used = _sort_plan(cnt, n_tiles, ts, chunk, n_blocks)

    sorted_rows = pl.pallas_call(
        functools.partial(_sort_kernel, ts=ts, chunk=chunk),
        grid_spec=pltpu.PrefetchScalarGridSpec(
            num_scalar_prefetch=3,
            grid=(n_tiles,),
            in_specs=[pl.BlockSpec((ts, payload), lambda i, *_: (i, 0)),
                      pl.BlockSpec((1, 1, ts), lambda i, *_: (i, 0, 0)),
                      pl.BlockSpec(memory_space=pl.ANY)],
            out_specs=pl.BlockSpec(memory_space=pl.ANY),
            scratch_shapes=[pltpu.VMEM((2, ts + N_EGROUPS * SEG_ALIGN + ts, payload), BF16),
                            pltpu.SemaphoreType.DMA((N_EGROUPS, ts // chunk))]),
        out_shape=jax.ShapeDtypeStruct((n_blocks * MOE_BLOCK, payload), BF16),
        input_output_aliases={5: 0},
        compiler_params=_params("arbitrary"),
        name="moe_sort",
    )(lo, off, n_pad, h2ext, ldest, jnp.zeros((n_blocks * MOE_BLOCK, payload), BF16))

    d_e = wg.shape[2]
    live = lambda b, nb: jnp.minimum(b, nb[0] - 1)
    out_sorted = pl.pallas_call(
        _experts_kernel,
        grid_spec=pltpu.PrefetchScalarGridSpec(
            num_scalar_prefetch=2,
            grid=(n_blocks,),
            in_specs=[pl.BlockSpec((MOE_BLOCK, payload), lambda b, grp, nb: (live(b, nb), 0)),
                      pl.BlockSpec((EXPERTS_PER_GROUP, d, d_e), lambda b, grp, nb: (grp[b], 0, 0)),
                      pl.BlockSpec((EXPERTS_PER_GROUP, d, d_e), lambda b, grp, nb: (grp[b], 0, 0)),
                      pl.BlockSpec((EXPERTS_PER_GROUP, d_e, d), lambda b, grp, nb: (grp[b], 0, 0))],
            out_specs=pl.BlockSpec((MOE_BLOCK, d), lambda b, grp, nb: (b, 0))),
        out_shape=jax.ShapeDtypeStruct((n_blocks * MOE_BLOCK, d), BF16),
        compiler_params=_params("arbitrary"),
        name="moe_experts",
    )(blk_group, n_used, sorted_rows, wg, wu, wd)

    per = rows_per_batch // ts
    row = lambda col: pl.BlockSpec((1, 1, d), lambda i, *_, col=col: (i // per + boff, 0, col))
    return pl.pallas_call(
        functools.partial(_combine_kernel, ts=ts, chunk=chunk, final_norm=final_norm),
        grid_spec=pltpu.PrefetchScalarGridSpec(
            num_scalar_prefetch=3,
            grid=(n_tiles,),
            in_specs=[pl.BlockSpec((1, 1, ts), lambda i, *_: (i, 0, 0)),
                      pl.BlockSpec((ts, d), lambda i, *_: (i, 0)),
                      row(5), row(0), row(1),
                      pl.BlockSpec((1, d), lambda i, *_: (0, 0)),
                      pl.BlockSpec(memory_space=pl.ANY)],
            out_specs=pl.BlockSpec((ts, d), lambda i, *_: (i, 0)),
            scratch_shapes=[pltpu.VMEM((2, N_EGROUPS, ts, d), BF16),
                            pltpu.SemaphoreType.DMA((2, N_EGROUPS, ts // chunk))]),
        out_shape=jax.ShapeDtypeStruct((t, d), F32),
        compiler_params=_params("arbitrary"),
        name="moe_combine",
    )(lo, off, n_pad, ldest, x1, mod3, modf3, modf3, final_norm_w, out_sorted)


def _layer(x, mod3, modf3, boff, k_prev, v_prev, logf_prev, conv_prev, ssm_prev, p, final_norm_w, final_norm):
    b, l, d = x.shape
    z, xbc, q, k, v, k_b, v_b, sm, smt, conv_new = _inproj(
        x, mod3, boff, p["norm1_w"], p["wz"], p["wx"], p["wq"], p["wk"], p["wv"], p["ws"], p["bs"])
    n_heads_ssd = p["wz"].shape[1] // P_SSD
    y_ssd, ssm_new = _ssd(xbc, z, sm, smt, conv_prev, ssm_prev, p["conv_w"], p["conv_b"], p["a_log"],
                          p["d_skip"], p["ssd_norm_w"])
    n_heads = p["wq"].shape[1] // HD_ATT
    logf_t = smt[:, n_heads_ssd:n_heads_ssd + n_heads, :]
    if k_prev is None:
        p0, k_all, v_all, lf_all = 0, k_b, v_b, logf_t
    else:
        p0 = k_prev.shape[1]
        pad = (-(p0 + l)) % LANES
        zeros = lambda w: jnp.zeros((b, pad, w), BF16)
        k_all = jnp.concatenate([k_prev.astype(BF16), k_b, zeros(k_b.shape[2])], axis=1)
        v_all = jnp.concatenate([v_prev.astype(BF16), v_b, zeros(v_b.shape[2])], axis=1)
        lf_all = jnp.concatenate([jnp.swapaxes(logf_prev, 1, 2), logf_t, jnp.zeros((b, n_heads, pad), F32)], axis=2)
    o_att = _attention(q, k_all, v_all, _forget_cumsum(lf_all), p0)
    x1, h2ext, ldest, cnt = _outproj(y_ssd, o_att, x, mod3, boff, p["attn_norm_w"], p["wo_ssd"], p["wo_att"],
                                     p["norm2_w"], p["wr"], p["br"])
    y = _moe_final(h2ext, ldest, cnt, p["wg"], p["wu"], p["wd"], x1.reshape(b * l, d), mod3, modf3, boff, l,
                   final_norm_w, final_norm)
    return y.reshape(b, l, d), (k, v, jnp.swapaxes(logf_t, 1, 2), conv_new, ssm_new)


def kernel(x_prompt, x_sample, c_prompt, c_sample, cache_k, cache_v, cache_logf, state_conv, state_ssm, norm1_w, w_ada, b_ada, w_in, conv_w, conv_b, dt_bias, a_log, d_skip, ssd_norm_w, f_bias, attn_norm_w, w_out, norm2_w, w_rg, b_rg, w_re, b_re, w_gate, w_up, w_down, final_norm_w, w_ada_f, b_ada_f):
    depth = w_in.shape[0]
    bp, lp, d = x_prompt.shape
    bs = x_sample.shape[0]
    d_conv = conv_w.shape[2]
    d_ssd = ssd_norm_w.shape[1]
    d_att = attn_norm_w.shape[1]
    h_ssd = dt_bias.shape[1]
    h_att = f_bias.shape[1]
    assert h_ssd + h_att == SMALL_W and d_att // HD_ATT == h_att and d_ssd // P_SSD == h_ssd

    c_all = jnp.concatenate([c_prompt, c_sample], axis=0)
    modf3 = _modulation(c_all, w_ada_f, b_ada_f).reshape(bp + bs, 1, 2 * d)
    final_w = final_norm_w.reshape(1, d)

    i0 = d_ssd
    i1 = i0 + d_conv
    i2 = i1 + h_ssd
    i3 = i2 + d_att
    i4 = i3 + d_att
    i5 = i4 + d_att
    yp, ys = x_prompt, x_sample
    outs_p, outs_s = [], []
    for layer in range(depth):
        mod3 = _modulation(c_all, w_ada[layer], b_ada[layer]).reshape(bp + bs, 1, 6 * d)
        wi = w_in[layer]
        w_small = jnp.concatenate([wi[:, i1:i2], wi[:, i5:], jnp.zeros((d, LANES - SMALL_W), F32)], axis=1)
        b_small = jnp.concatenate([dt_bias[layer], f_bias[layer], jnp.zeros((LANES - SMALL_W,), F32)])
        wr = jnp.zeros((ROUTER_ROWS, d), F32)
        wr = wr.at[:N_EGROUPS].set(w_rg[layer].T)
        wr = wr.at[EXPERT_ROW0:EXPERT_ROW0 + N_EXPERTS].set(
            jnp.transpose(w_re[layer], (0, 2, 1)).reshape(N_EXPERTS, d))
        wr_hi, wr_lo = _split2(wr)
        br = jnp.zeros((ROUTER_ROWS,), F32)
        br = br.at[:N_EGROUPS].set(b_rg[layer])
        br = br.at[EXPERT_ROW0:EXPERT_ROW0 + N_EXPERTS].set(b_re[layer].reshape(N_EXPERTS))
        p = dict(
            norm1_w=norm1_w[layer].reshape(1, d),
            wz=wi[:, :i0].astype(BF16), wx=wi[:, i0:i1].astype(BF16),
            wq=(wi[:, i2:i3] * (LOG2E * HD_ATT ** -0.5)).astype(BF16),
            wk=wi[:, i3:i4].astype(BF16), wv=wi[:, i4:i5].astype(BF16),
            ws=w_small.astype(BF16), bs=b_small.reshape(1, LANES),
            conv_w=conv_w[layer], conv_b=conv_b[layer], a_log=a_log[layer], d_skip=d_skip[layer],
            ssd_norm_w=ssd_norm_w[layer], attn_norm_w=attn_norm_w[layer].reshape(1, d_att),
            wo_ssd=w_out[layer][:d_ssd].astype(BF16), wo_att=w_out[layer][d_ssd:].astype(BF16),
            norm2_w=norm2_w[layer].reshape(1, d),
            wr=jnp.concatenate([wr_hi, wr_lo], axis=0), br=br.reshape(ROUTER_ROWS, 1),
            wg=w_gate[layer].astype(BF16), wu=w_up[layer].astype(BF16), wd=w_down[layer].astype(BF16),
        )
        conv0 = jnp.zeros((bp, CONV_W - 1, d_conv), F32)
        ssm0 = jnp.zeros((bp, h_ssd, P_SSD, N_STATE), F32)
        last = layer == depth - 1
        yp, st_p = _layer(yp, mod3, modf3, 0, None, None, None, conv0, ssm0, p, final_w, last)
        ck = cache_k[layer].reshape(bs, -1, d_att)
        cv = cache_v[layer].reshape(bs, -1, d_att)
        ys, st_s = _layer(ys, mod3, modf3, bp, ck, cv, cache_logf[layer], state_conv[layer], state_ssm[layer], p,
                          final_w, last)
        outs_p.append(st_p)
        outs_s.append(st_s)

    def stack(outs, b, l):
        k = jnp.stack([o[0].reshape(b, l, h_att, HD_ATT) for o in outs])
        v = jnp.stack([o[1].reshape(b, l, h_att, HD_ATT) for o in outs])
        return (k, v, jnp.stack([o[2] for o in outs]), jnp.stack([o[3] for o in outs]),
                jnp.stack([o[4] for o in outs]))

    return (yp, ys) + stack(outs_p, bp, lp) + stack(outs_s, bs, x_sample.shape[1])
```

```python
import functools

import jax
import jax.numpy as jnp
import numpy as np
from jax import lax
from jax.experimental import pallas as pl
from jax.experimental.pallas import tpu as pltpu

F32 = jnp.float32
BF16 = jnp.bfloat16

P_SSD = 64
N_STATE = 64
G_SSD = 2
CONV_W = 4
HD_ATT = 64
N_EGROUPS = 4
EXPERTS_PER_GROUP = 4
N_EXPERTS = N_EGROUPS * EXPERTS_PER_GROUP
EPS = 1e-6
NEG_BIG = -1e30

LANES = 128
SEG_ALIGN = 16
SORT_TILE = 256
MOE_BLOCK = 512
MOE_CHUNK = 64
SMALL_W = 16
VMEM_LIMIT = 56 * 1024 * 1024


def _params(*sem):
    return pltpu.CompilerParams(dimension_semantics=sem, vmem_limit_bytes=VMEM_LIMIT)


def _split2(x):
    hi = x.astype(BF16)
    lo = (x - hi.astype(F32)).astype(BF16)
    return hi, lo


def _split3(x):
    hi = x.astype(BF16)
    r = x - hi.astype(F32)
    mid = r.astype(BF16)
    lo = (r - mid.astype(F32)).astype(BF16)
    return hi, mid, lo


def _dot(a, b):
    return jnp.dot(a, b, preferred_element_type=F32)


def _dot_nt(a, b):
    return lax.dot_general(a, b, (((1,), (1,)), ((), ())), preferred_element_type=F32)


def _dot_tn(a, b):
    return lax.dot_general(a, b, (((0,), (0,)), ((), ())), preferred_element_type=F32)


def _silu(x):
    h = 0.5 * x
    return h + h * jnp.tanh(h)


def _rms(x, w):
    return x * lax.rsqrt(jnp.mean(x * x, axis=-1, keepdims=True) + EPS) * w


def _pick_tile(n, candidates):
    for c in candidates:
        if n % c == 0:
            return c
    return n


def _mod_kernel(c_ref, w_ref, b_ref, o_ref):
    a = _silu(c_ref[...])
    a_hi, a_lo = _split2(a)
    w_hi, w_lo = _split2(w_ref[...])
    o_ref[...] = _dot(a_hi, w_hi) + _dot(a_lo, w_hi) + _dot(a_hi, w_lo) + b_ref[...]


def _modulation(c, w, b):
    m, d = c.shape
    n = w.shape[1]
    tn = _pick_tile(n, (1024, 512, 256, 128))
    return pl.pallas_call(
        _mod_kernel,
        grid=(n // tn,),
        in_specs=[pl.BlockSpec((m, d), lambda j: (0, 0)),
                  pl.BlockSpec((d, tn), lambda j: (0, j)),
                  pl.BlockSpec((1, tn), lambda j: (0, j))],
        out_specs=pl.BlockSpec((m, tn), lambda j: (0, j)),
        out_shape=jax.ShapeDtypeStruct((m, n), F32),
        compiler_params=_params("parallel"),
        name="adaln_mod",
    )(c, w, b.reshape(1, n))


def _inproj_kernel(x_ref, sh_ref, sc_ref, nw_ref, wz_ref, wx_ref, wq_ref, wk_ref, wv_ref, ws_ref, bs_ref,
                   z_ref, xbc_ref, q_ref, k_ref, v_ref, kb_ref, vb_ref, sm_ref, smt_ref, tail_ref):
    l = pl.program_id(1)
    x = x_ref[0]
    h = (_rms(x, nw_ref[...]) * (1.0 + sc_ref[0]) + sh_ref[0]).astype(BF16)
    z_ref[0] = _dot(h, wz_ref[...]).astype(BF16)
    xbc = _dot(h, wx_ref[...])
    xbc_ref[0] = xbc.astype(BF16)
    q_ref[0] = _dot(h, wq_ref[...]).astype(BF16)
    k = _dot(h, wk_ref[...])
    k_ref[0] = k
    kb_ref[0] = k.astype(BF16)
    v = _dot(h, wv_ref[...])
    v_ref[0] = v
    vb_ref[0] = v.astype(BF16)
    s = _dot(h, ws_ref[...]) + bs_ref[...]
    t = jnp.log(1.0 + jnp.exp(-jnp.abs(s)))
    lane = lax.broadcasted_iota(jnp.int32, s.shape, 1)
    s = jnp.where(lane < SMALL_W // 2, jnp.maximum(s, 0.0) + t, jnp.minimum(s, 0.0) - t)
    sm_ref[0] = s[:, :SMALL_W]
    smt_ref[0] = s.T[:SMALL_W, :]

    @pl.when(l == pl.num_programs(1) - 1)
    def _():
        tl = xbc.shape[0]
        tail_ref[0] = xbc[tl - (CONV_W - 1):, :]


def _inproj(x, mod3, boff, norm_w, wz, wx, wq, wk, wv, ws, bs):
    b, l, d = x.shape
    tl = _pick_tile(l, (1024, 512, 256, 128, 64))
    nl = l // tl
    d_ssd, d_conv, d_att = wz.shape[1], wx.shape[1], wq.shape[1]
    row = lambda col: pl.BlockSpec((1, 1, d), lambda i, j, col=col: (i + boff, 0, col))
    full = lambda a: pl.BlockSpec(a.shape, lambda i, j: (0,) * a.ndim)
    tok = lambda w: pl.BlockSpec((1, tl, w), lambda i, j: (i, j, 0))
    out_shape = [
        jax.ShapeDtypeStruct((b, l, d_ssd), BF16),
        jax.ShapeDtypeStruct((b, l, d_conv), BF16),
        jax.ShapeDtypeStruct((b, l, d_att), BF16),
        jax.ShapeDtypeStruct((b, l, d_att), F32),
        jax.ShapeDtypeStruct((b, l, d_att), F32),
        jax.ShapeDtypeStruct((b, l, d_att), BF16),
        jax.ShapeDtypeStruct((b, l, d_att), BF16),
        jax.ShapeDtypeStruct((b, l, SMALL_W), F32),
        jax.ShapeDtypeStruct((b, SMALL_W, l), F32),
        jax.ShapeDtypeStruct((b, CONV_W - 1, d_conv), F32),
    ]
    out_specs = [tok(d_ssd), tok(d_conv), tok(d_att), tok(d_att), tok(d_att), tok(d_att), tok(d_att),
                 tok(SMALL_W), pl.BlockSpec((1, SMALL_W, tl), lambda i, j: (i, 0, j)),
                 pl.BlockSpec((1, CONV_W - 1, d_conv), lambda i, j: (i, 0, 0))]
    return pl.pallas_call(
        _inproj_kernel,
        grid=(b, nl),
        in_specs=[tok(d), row(0), row(1), full(norm_w), full(wz), full(wx), full(wq), full(wk), full(wv),
                  full(ws), full(bs)],
        out_specs=out_specs,
        out_shape=out_shape,
        compiler_params=_params("parallel", "arbitrary"),
        name="inproj",
    )(x, mod3, mod3, norm_w, wz, wx, wq, wk, wv, ws, bs)


LOG2E = 1.4426950408889634
AUG = 3
SPLIT_ROWS = 128


PIECE_COLS = 32


def _aug_select(n_heads):
    pair = 2 * HD_ATT
    sq = np.zeros((n_heads // 2, PIECE_COLS, pair), np.float32)
    sk = np.zeros((n_heads // 2, PIECE_COLS, pair), np.float32)
    one = AUG * n_heads
    for h in range(n_heads):
        slot = ((h ^ 1) % 2) * HD_ATT
        for c in range(AUG):
            sq[h // 2, c * n_heads + h, slot + c] = 1.0
            sq[h // 2, one, slot + AUG + c] = 1.0
            sk[h // 2, one, slot + c] = 1.0
            sk[h // 2, c * n_heads + h, slot + AUG + c] = -1.0
    return sq, sk


def _cumsum_kernel(x_ref, eye_ref, o_ref, carry):
    @pl.when(pl.program_id(1) == 0)
    def _():
        carry[...] = jnp.zeros_like(carry)

    x = x_ref[0]
    h, tc = x.shape
    r = lax.broadcasted_iota(jnp.int32, (tc, tc), 0)
    c = lax.broadcasted_iota(jnp.int32, (tc, tc), 1)
    upper = jnp.where(r <= c, 1.0, 0.0).astype(BF16)
    stack = lambda ps: jnp.concatenate([p.astype(F32) for p in ps], axis=0)
    parts = _dot(stack(_split3(x)).astype(BF16), upper)
    cs = parts[:h] + parts[h:2 * h] + parts[2 * h:] + carry[...]
    carry[...] = cs[:, tc - 1:]
    rows = lax.broadcasted_iota(jnp.int32, (PIECE_COLS - AUG * h, tc), 0)
    pieces = jnp.concatenate([stack(_split3(cs * LOG2E)), jnp.where(rows == 0, 1.0, 0.0)], axis=0)
    o_ref[0] = _dot_tn(pieces.astype(BF16), eye_ref[...]).astype(BF16)


def _forget_cumsum(logf_t):
    b, h, lk = logf_t.shape
    assert AUG * h < PIECE_COLS
    tc = _pick_tile(lk, (1024, 512, 384, 256, 128))
    eye = jnp.asarray(np.eye(PIECE_COLS), BF16)
    return pl.pallas_call(
        _cumsum_kernel,
        grid=(b, lk // tc),
        in_specs=[pl.BlockSpec((1, h, tc), lambda i, j: (i, 0, j)),
                  pl.BlockSpec((PIECE_COLS, PIECE_COLS), lambda i, j: (0, 0))],
        out_specs=pl.BlockSpec((1, tc, PIECE_COLS), lambda i, j: (i, j, 0)),
        out_shape=jax.ShapeDtypeStruct((b, lk, PIECE_COLS), BF16),
        scratch_shapes=[pltpu.VMEM((h, 1), F32)],
        compiler_params=_params("parallel", "arbitrary"),
        name="forget_cumsum",
    )(logf_t, eye)


def _ssd_spread(n_heads):
    e = np.zeros((2 * AUG * n_heads, 2 * n_heads * P_SSD), np.float32)
    for v in range(2):
        for c in range(AUG):
            for h in range(n_heads):
                col = v * n_heads * P_SSD + h * P_SSD
                e[(v * AUG + c) * n_heads + h, col:col + P_SSD] = 1.0
    return e


def _ssd_kernel(xbc_ref, z_ref, sm_ref, smt_ref, cprev_ref, sprev_ref, cw_ref, cb_ref, arow_ref, acol_ref,
                dskip_ref, nw_ref, spread_ref, y_ref, snew_ref, hist, state):
    l = pl.program_id(1)
    q = xbc_ref.shape[1]
    d_ssd = z_ref.shape[2]
    n_heads = d_ssd // P_SSD
    pair = 2 * P_SSD
    hist_rows = hist.shape[0]

    @pl.when(l == 0)
    def _():
        hist[...] = jnp.zeros(hist.shape, F32)
        hist[hist_rows - (CONV_W - 1):, :] = cprev_ref[0]
        state[...] = sprev_ref[0]

    xb = xbc_ref[0]
    r3 = lax.broadcasted_iota(jnp.int32, ((CONV_W - 1) * q, q), 0)
    c3 = lax.broadcasted_iota(jnp.int32, ((CONV_W - 1) * q, q), 1)
    shift = jnp.where(r3 % q - c3 == r3 // q + 1, 1.0, 0.0).astype(BF16)
    shifted = _dot(shift, xb)
    cw = cw_ref[...]
    conv = cb_ref[...] + xb.astype(F32) * cw[CONV_W - 1:CONV_W, :]
    for k in range(CONV_W - 1):
        conv = conv + shifted[k * q:(k + 1) * q, :] * cw[CONV_W - 2 - k:CONV_W - 1 - k, :]
    hrow = lambda j: hist[hist_rows - j:hist_rows - j + 1, :]
    frow = lax.broadcasted_iota(jnp.int32, (hist_rows, 1), 0)
    fix = jnp.zeros((hist_rows, conv.shape[1]), F32)
    for t in range(CONV_W - 1):
        acc = 0.0
        for j in range(1, CONV_W - t):
            acc = acc + hrow(j) * cw[CONV_W - 1 - t - j:CONV_W - t - j, :]
        fix = jnp.where(frow == t, acc, fix)
    conv = jnp.concatenate([conv[:hist_rows] + fix, conv[hist_rows:]], axis=0)
    hist[...] = xb[q - hist_rows:, :].astype(F32)
    act = _silu(conv)

    dt = sm_ref[0][:, :n_heads]
    dt_t = smt_ref[0][:n_heads, :]
    a_row = -jnp.exp(arow_ref[...]) * LOG2E
    a_col = -jnp.exp(acol_ref[...]) * LOG2E
    r = lax.broadcasted_iota(jnp.int32, (q, q), 0)
    c = lax.broadcasted_iota(jnp.int32, (q, q), 1)
    causal = r >= c
    lower = jnp.where(causal, 1.0, 0.0).astype(BF16)
    upper = jnp.where(r <= c, 1.0, 0.0).astype(BF16)
    h0, h1, h2 = _split3(dt * a_row)
    acum = _dot(lower, h0) + _dot(lower, h1) + _dot(lower, h2)
    t0, t1, t2 = _split3(dt_t * a_col)
    acum_t = _dot(t0, upper) + _dot(t1, upper) + _dot(t2, upper)
    pieces = [p.astype(F32) for p in _split3(acum) + _split3(dt)]
    wide = _dot(jnp.concatenate(pieces, axis=1).astype(BF16), spread_ref[...])
    acum_x = wide[:, :d_ssd]
    dt_x = wide[:, d_ssd:]
    a_last = acum[q - 1:q, :]
    e_all = jnp.exp2(a_last)
    e_cum_x = jnp.exp2(acum_x)
    e_end_x = jnp.exp2(acum_x[q - 1:q, :] - acum_x)

    xs = act[:, :d_ssd]
    bm = act[:, d_ssd:d_ssd + G_SSD * N_STATE]
    cm = act[:, d_ssd + G_SSD * N_STATE:]
    bm_b = bm.astype(BF16)
    cm_b = cm.astype(BF16)
    xdt = xs * dt_x
    xdt_b = xdt.astype(BF16)
    xe_b = (xdt * e_end_x).astype(BF16)
    lane = lax.broadcasted_iota(jnp.int32, (1, pair), 1)
    first = lane < P_SSD
    srow = lax.broadcasted_iota(jnp.int32, (pair, 1), 0)
    ys = []
    for p in range(n_heads // 2):
        g = (2 * p * G_SSD) // n_heads
        in_group = first if g == 0 else jnp.logical_not(first)
        sl = slice(p * pair, (p + 1) * pair)
        if (2 * p) % (n_heads // G_SSD) == 0:
            cb = _dot_nt(jnp.where(in_group, cm_b, jnp.zeros_like(cm_b)), bm_b)
        ms = []
        for hh in range(2):
            h = 2 * p + hh
            seg = acum[:, h:h + 1] - acum_t[h:h + 1, :]
            ms.append((cb * jnp.exp2(jnp.where(causal, seg, NEG_BIG))).astype(BF16))
        xp = xdt_b[:, sl]
        zero = jnp.zeros_like(xp)
        y = _dot(jnp.concatenate(ms, axis=1),
                 jnp.concatenate([jnp.where(first, xp, zero), jnp.where(first, zero, xp)], axis=0))
        s_in = state[sl, :]
        y = y + _dot_nt(cm_b, s_in.astype(BF16)) * e_cum_x[:, sl]
        upd = _dot_tn(xe_b[:, sl], bm_b)
        keep = jnp.where(srow < P_SSD, e_all[:, 2 * p:2 * p + 1], e_all[:, 2 * p + 1:2 * p + 2])
        state[sl, :] = s_in * keep + jnp.where(in_group, upd, 0.0)
        ys.append(y)
    y_all = jnp.concatenate(ys, axis=1) + dskip_ref[...] * xs
    yg = y_all * _silu(z_ref[0].astype(F32))
    y_ref[0] = _rms(yg, nw_ref[...]).astype(BF16)
    snew_ref[0] = state[...]


def _ssd(xbc, z, sm, smt, conv_prev, ssm_prev, conv_w, conv_b, a_log, d_skip, norm_w):
    b, l, d_conv = xbc.shape
    d_ssd = z.shape[2]
    n_heads = d_ssd // P_SSD
    hg = n_heads // G_SSD
    q = _pick_tile(l, (256, 128, 64))
    tok = lambda w: pl.BlockSpec((1, q, w), lambda i, j: (i, j, 0))
    full = lambda a: pl.BlockSpec(a.shape, lambda i, j: (0,) * a.ndim)
    a_row = a_log.reshape(1, n_heads)
    a_col = a_log.reshape(n_heads, 1)
    conv_b = conv_b.reshape(1, d_conv)
    d_skip = jnp.repeat(d_skip, P_SSD).reshape(1, d_ssd)
    norm_w = norm_w.reshape(1, d_ssd)
    spread = jnp.asarray(_ssd_spread(n_heads), BF16)
    s4 = ssm_prev.reshape(b, G_SSD, hg * P_SSD, N_STATE)
    s_in = jnp.concatenate([jnp.pad(s4[:, g], ((0, 0), (0, 0), (g * N_STATE, (G_SSD - 1 - g) * N_STATE)))
                            for g in range(G_SSD)], axis=1)
    state_spec = pl.BlockSpec((1, n_heads * P_SSD, G_SSD * N_STATE), lambda i, j: (i, 0, 0))
    y, s_out = pl.pallas_call(
        _ssd_kernel,
        grid=(b, l // q),
        in_specs=[tok(d_conv), tok(d_ssd), tok(SMALL_W), pl.BlockSpec((1, SMALL_W, q), lambda i, j: (i, 0, j)),
                  pl.BlockSpec((1, CONV_W - 1, d_conv), lambda i, j: (i, 0, 0)), state_spec,
                  full(conv_w), full(conv_b), full(a_row), full(a_col), full(d_skip), full(norm_w), full(spread)],
        out_specs=[tok(d_ssd), state_spec],
        out_shape=[jax.ShapeDtypeStruct((b, l, d_ssd), BF16),
                   jax.ShapeDtypeStruct((b, n_heads * P_SSD, G_SSD * N_STATE), F32)],
        scratch_shapes=[pltpu.VMEM((8, d_conv), F32),
                        pltpu.VMEM((n_heads * P_SSD, G_SSD * N_STATE), F32)],
        compiler_params=_params("parallel", "arbitrary"),
        name="ssd",
    )(xbc, z, sm, smt, conv_prev, s_in, conv_w, conv_b, a_row, a_col, d_skip, norm_w, spread)
    s_out = s_out.reshape(b, G_SSD, hg * P_SSD, G_SSD * N_STATE)
    s_new = jnp.concatenate([s_out[:, g, :, g * N_STATE:(g + 1) * N_STATE] for g in range(G_SSD)], axis=1)
    return y, s_new.reshape(b, n_heads, P_SSD, N_STATE)


def _attn_kernel(q_ref, pq_ref, k_ref, pk_ref, v_ref, sq_ref, sk_ref, o_ref, kk_s, m_s, acc_s, *, p0, tq, tk):
    i = pl.program_id(2)
    lane = lax.broadcasted_iota(jnp.int32, (1, 2 * HD_ATT), 1)
    first = lane < HD_ATT
    own = (first, jnp.logical_not(first))
    sum_lane = (HD_ATT, 0)

    @pl.when(i == 0)
    def _():
        k = k_ref[0]
        ka = _dot(pk_ref[0], sk_ref[0]).astype(BF16)
        kk_s[0] = jnp.where(first, k, ka)
        kk_s[1] = jnp.where(first, ka, k)

    q = q_ref[0]
    qa = _dot(pq_ref[0], sq_ref[0]).astype(BF16)
    qq = (jnp.where(first, q, qa), jnp.where(first, qa, q))
    m_s[...] = jnp.full(m_s.shape, NEG_BIG, F32)
    acc_s[...] = jnp.zeros(acc_s.shape, F32)
    nc = tk // LANES
    n_split = 2 if tq % (2 * SPLIT_ROWS) == 0 else 1
    rows = tq // n_split

    def step(j, masked):
        off = pl.multiple_of(j * tk, tk)
        v = v_ref[0, pl.ds(off, tk), :]
        vv = [jnp.where(own[hh], v, jnp.where(lane == sum_lane[hh], 1.0, 0.0).astype(BF16)) for hh in range(2)]
        logit = [[_dot_nt(qq[hh][r * rows:(r + 1) * rows], kk_s[hh, pl.ds(off, tk), :]) for hh in range(2)]
                 for r in range(n_split)]
        for r in range(n_split):
            rs = slice(r * rows, (r + 1) * rows)
            if masked:
                q_pos = p0 + i * tq + r * rows + lax.broadcasted_iota(jnp.int32, (rows, tk), 0)
                k_pos = j * tk + lax.broadcasted_iota(jnp.int32, (rows, tk), 1)
                visible = k_pos <= q_pos
            for hh in range(2):
                s = jnp.where(visible, logit[r][hh], NEG_BIG) if masked else logit[r][hh]
                cols = [s[:, c * LANES:(c + 1) * LANES] for c in range(nc)]
                m_cur = functools.reduce(jnp.maximum, cols)
                m_prev = m_s[hh, rs, :]
                m_new = jnp.maximum(m_prev, jnp.max(m_cur, axis=1, keepdims=True))
                alpha = jnp.exp2(m_prev - m_new)
                p = jnp.concatenate([jnp.exp2((col - m_new).astype(BF16)) for col in cols], axis=1)
                m_s[hh, rs, :] = m_new
                acc_s[hh, rs, :] = alpha * acc_s[hh, rs, :] + _dot(p, vv[hh])

    n_full = (p0 + i * tq + 1) // tk
    n_vis = (p0 + i * tq + tq - 1) // tk + 1

    def full_pair(jj, carry):
        step(2 * jj, False)
        step(2 * jj + 1, False)
        return carry

    def masked_body(j, carry):
        step(j, True)
        return carry

    lax.fori_loop(0, n_full // 2, full_pair, 0)
    pl.when(n_full % 2 == 1)(lambda: step(n_full - 1, False))
    lax.fori_loop(n_full, n_vis, masked_body, 0)
    a0 = acc_s[0]
    a1 = acc_s[1]
    o = jnp.where(first, a0 / a0[:, sum_lane[0]:sum_lane[0] + 1], a1 / a1[:, sum_lane[1]:sum_lane[1] + 1])
    o_ref[0] = o.astype(o_ref.dtype)


def _attention(q, k, v, pieces, p0):
    b, lq, d_att = q.shape
    lk = k.shape[1]
    n_heads = d_att // HD_ATT
    tq = _pick_tile(lq, (512, 256, 128, 64))
    tk = lk if lk <= 1536 else _pick_tile(lk, (512, 256, 128))
    assert p0 % tq == 0 and lk >= p0 + lq
    pair = 2 * HD_ATT
    sq, sk = (jnp.asarray(a, BF16) for a in _aug_select(n_heads))
    q_spec = pl.BlockSpec((1, tq, pair), lambda bi, hp, i: (bi, i, hp))
    kv_spec = pl.BlockSpec((1, lk, pair), lambda bi, hp, i: (bi, 0, hp))
    sel_spec = pl.BlockSpec((1, PIECE_COLS, pair), lambda bi, hp, i: (hp, 0, 0))
    return pl.pallas_call(
        functools.partial(_attn_kernel, p0=p0, tq=tq, tk=tk),
        grid=(b, n_heads // 2, lq // tq),
        in_specs=[q_spec, pl.BlockSpec((1, tq, PIECE_COLS), lambda bi, hp, i: (bi, i + p0 // tq, 0)),
                  kv_spec, pl.BlockSpec((1, lk, PIECE_COLS), lambda bi, hp, i: (bi, 0, 0)), kv_spec,
                  sel_spec, sel_spec],
        out_specs=q_spec,
        out_shape=jax.ShapeDtypeStruct((b, lq, d_att), BF16),
        scratch_shapes=[pltpu.VMEM((2, lk, pair), BF16), pltpu.VMEM((2, tq, LANES), F32),
                        pltpu.VMEM((2, tq, pair), F32)],
        compiler_params=_params("parallel", "parallel", "arbitrary"),
        name="fox_attention",
    )(q, pieces, k, pieces, v, sq, sk)


ROUTER_ROWS = 32
EXPERT_ROW0 = 8


def _outproj_kernel(ys_ref, oa_ref, x_ref, g1_ref, sh_ref, sc_ref, anw_ref, wos_ref, woa_ref, n2w_ref, wr_ref,
                    br_ref, x1_ref, h2_ref, ld_ref, cnt_ref, wt_s, g_s):
    ya = _rms(oa_ref[0].astype(F32), anw_ref[...]).astype(BF16)
    m = _dot(ys_ref[0], wos_ref[...]) + _dot(ya, woa_ref[...])
    x1 = x_ref[0] + g1_ref[0] * m
    x1_ref[0] = x1
    h2 = _rms(x1, n2w_ref[...]) * (1.0 + sc_ref[0]) + sh_ref[0]
    h_hi, h_lo = _split2(h2)
    wr = wr_ref[...]
    p1 = _dot_nt(wr, h_hi)
    p2 = _dot_nt(wr[:ROUTER_ROWS], h_lo)
    logit = p1[:ROUTER_ROWS] + p1[ROUTER_ROWS:] + p2 + br_ref[...]

    lg = [logit[g:g + 1, :] for g in range(N_EGROUPS)]
    gmax = jnp.maximum(jnp.maximum(lg[0], lg[1]), jnp.maximum(lg[2], lg[3]))
    denom = sum(jnp.exp(x - gmax) for x in lg)
    p_sel = 1.0 / denom
    is_g = []
    taken = jnp.zeros_like(gmax) > 1.0
    for g in range(N_EGROUPS):
        hit = (lg[g] == gmax) & jnp.logical_not(taken)
        is_g.append(hit)
        taken = taken | hit
    le = []
    for e in range(EXPERTS_PER_GROUP):
        v = logit[EXPERT_ROW0 + 3 * EXPERTS_PER_GROUP + e:EXPERT_ROW0 + 3 * EXPERTS_PER_GROUP + e + 1, :]
        for g in range(N_EGROUPS - 2, -1, -1):
            r0 = EXPERT_ROW0 + g * EXPERTS_PER_GROUP + e
            v = jnp.where(is_g[g], logit[r0:r0 + 1, :], v)
        le.append(v)
    m1 = jnp.maximum(jnp.maximum(le[0], le[1]), jnp.maximum(le[2], le[3]))
    first = []
    taken = jnp.zeros_like(m1) > 1.0
    for e in range(EXPERTS_PER_GROUP):
        hit = (le[e] == m1) & jnp.logical_not(taken)
        first.append(hit)
        taken = taken | hit
    rest = [jnp.where(first[e], -jnp.inf, le[e]) for e in range(EXPERTS_PER_GROUP)]
    m2 = jnp.maximum(jnp.maximum(rest[0], rest[1]), jnp.maximum(rest[2], rest[3]))
    second = []
    taken = jnp.zeros_like(m1) > 1.0
    for e in range(EXPERTS_PER_GROUP):
        hit = (rest[e] == m2) & jnp.logical_not(taken)
        second.append(hit)
        taken = taken | hit
    e2 = jnp.exp(m2 - m1)
    w_a = p_sel / (1.0 + e2)
    w_b = w_a * e2
    d = x1.shape[1]
    wt_s[...] = jnp.zeros(wt_s.shape, F32)
    for e in range(EXPERTS_PER_GROUP):
        w = jnp.where(first[e], w_a, jnp.where(second[e], w_b, 0.0))
        w_hi = w.astype(BF16).astype(F32)
        wt_s[e:e + 1, :] = w_hi
        wt_s[EXPERTS_PER_GROUP + e:EXPERTS_PER_GROUP + e + 1, :] = w - w_hi
    h2_ref[:, :d] = h_hi
    h2_ref[:, d:] = wt_s[...].T.astype(BF16)

    tl = logit.shape[1]
    ts = min(SORT_TILE, tl)
    g_s[...] = jnp.zeros(g_s.shape, F32)
    for g in range(N_EGROUPS):
        g_s[g:g + 1, :] = jnp.where(is_g[g], 1.0, 0.0)
    r = lax.broadcasted_iota(jnp.int32, (tl, tl), 0)
    c = lax.broadcasted_iota(jnp.int32, (tl, tl), 1)
    same_tile = (r // ts) == (c // ts)
    upper = jnp.where((r <= c) & same_tile, 1.0, 0.0).astype(BF16)
    cum = _dot(g_s[...].astype(BF16), upper)
    lane = lax.broadcasted_iota(jnp.int32, (1, tl), 1)
    crow = lax.broadcasted_iota(jnp.int32, (8, LANES), 0)
    clane = lax.broadcasted_iota(jnp.int32, (8, LANES), 1)
    ldest = -1.0
    cnt = jnp.zeros((8, LANES), F32)
    for g in range(N_EGROUPS):
        ldest = ldest + jnp.where(is_g[g], cum[g:g + 1, :], 0.0)
    lo = [0.0] * (tl // ts)
    for g in range(N_EGROUPS):
        lo_row = jnp.zeros((1, tl), F32)
        for sub in range(tl // ts):
            n = cum[g:g + 1, (sub + 1) * ts - 1:(sub + 1) * ts]
            n_pad = jnp.ceil(n / SEG_ALIGN) * SEG_ALIGN
            lo_row = jnp.where(lane // ts == sub, lo[sub], lo_row)
            cnt = jnp.where((crow == sub) & (clane == g), n_pad, cnt)
            lo[sub] = lo[sub] + n_pad
        ldest = ldest + jnp.where(is_g[g], lo_row, 0.0)
    ld_ref[0] = ldest.astype(jnp.int32)
    cnt_ref[0] = cnt.astype(jnp.int32)


def _outproj(y_ssd, o_att, x, mod3, boff, attn_norm_w, wo_ssd, wo_att, norm2_w, wr, br):
    b, l, d = x.shape
    tl = _pick_tile(l, (512, 256, 128, 64))
    nl = l // tl
    d_ssd, d_att = y_ssd.shape[2], o_att.shape[2]
    row = lambda col: pl.BlockSpec((1, 1, d), lambda i, j, col=col: (i + boff, 0, col))
    full = lambda a: pl.BlockSpec(a.shape, lambda i, j: (0,) * a.ndim)
    tok = lambda w: pl.BlockSpec((1, tl, w), lambda i, j: (i, j, 0))
    return pl.pallas_call(
        _outproj_kernel,
        grid=(b, nl),
        in_specs=[tok(d_ssd), tok(d_att), tok(d), row(2), row(3), row(4), full(attn_norm_w), full(wo_ssd),
                  full(wo_att), full(norm2_w), full(wr), full(br)],
        out_specs=[tok(d), pl.BlockSpec((tl, d + LANES), lambda i, j: (i * nl + j, 0)),
                   pl.BlockSpec((1, 1, tl), lambda i, j: (i * nl + j, 0, 0)),
                   pl.BlockSpec((1, 8, LANES), lambda i, j: (i * nl + j, 0, 0))],
        out_shape=[jax.ShapeDtypeStruct((b, l, d), F32),
                   jax.ShapeDtypeStruct((b * l, d + LANES), BF16),
                   jax.ShapeDtypeStruct((b * nl, 1, tl), jnp.int32),
                   jax.ShapeDtypeStruct((b * nl, 8, LANES), jnp.int32)],
        scratch_shapes=[pltpu.VMEM((LANES, tl), F32), pltpu.VMEM((8, tl), F32)],
        compiler_params=_params("parallel", "parallel"),
        name="outproj_router",
    )(y_ssd, o_att, x, mod3, mod3, mod3, attn_norm_w, wo_ssd, wo_att, norm2_w, wr, br)


def _sort_plan(cnt, n_tiles, ts, chunk, n_blocks):
    ns = n_tiles // cnt.shape[0]
    n_pad = cnt[:, :ns, :N_EGROUPS].reshape(n_tiles, N_EGROUPS)
    lo = jnp.cumsum(n_pad, axis=1) - n_pad
    region = (jnp.sum(n_pad, axis=0) + chunk + MOE_BLOCK - 1) // MOE_BLOCK * MOE_BLOCK
    end = jnp.cumsum(region)
    off = (end - region)[None, :] + jnp.cumsum(n_pad, axis=0) - n_pad
    blk = jnp.arange(n_blocks, dtype=jnp.int32) * MOE_BLOCK
    blk_group = jnp.minimum(jnp.sum(blk[:, None] >= end[None, :], axis=1), N_EGROUPS - 1).astype(jnp.int32)
    n_used = (end[-1] // MOE_BLOCK).astype(jnp.int32).reshape(1)
    flat = lambda a: a.astype(jnp.int32).reshape(-1)
    return flat(lo), flat(off), flat(n_pad), blk_group, n_used


def _sort_kernel(lo_ref, off_ref, np_ref, x_ref, ld_ref, init_ref, out_ref, cbuf, sems, *, ts, tps):
    del init_ref
    i = pl.program_id(0)
    slot = i % 2
    rows = ts + N_EGROUPS * SEG_ALIGN
    sizes = [ts >> k for k in range((ts // SEG_ALIGN).bit_length())]

    r = lax.broadcasted_iota(jnp.int32, (rows, ts), 0)
    for u in range(tps):
        perm = jnp.where(r == ld_ref[u], 1.0, 0.0).astype(BF16)
        cbuf[slot, u] = _dot(perm, x_ref[u * ts:(u + 1) * ts, :]).astype(BF16)

    def for_each(step, sl, act):
        for u in range(tps):
            for g in range(N_EGROUPS):
                at = (step * tps + u) * N_EGROUPS + g
                n_pad = np_ref[at]
                for k, size in enumerate(sizes):
                    done = n_pad & ~(2 * size - 1)
                    src = pl.multiple_of(lo_ref[at] + done, SEG_ALIGN)
                    dst = pl.multiple_of(off_ref[at] + done, SEG_ALIGN)
                    cp = pltpu.make_async_copy(cbuf.at[sl, u, pl.ds(src, size)], out_ref.at[pl.ds(dst, size)],
                                               sems.at[sl, u, g, k])
                    pl.when((n_pad & size) != 0)(functools.partial(act, cp))

    for_each(i, slot, lambda cp: cp.start())

    @pl.when(i > 0)
    def _():
        for_each(i - 1, 1 - slot, lambda cp: cp.wait())

    @pl.when(i == pl.num_programs(0) - 1)
    def _():
        for_each(i, slot, lambda cp: cp.wait())


def _experts_kernel(grp_ref, nb_ref, x_ref, wg_ref, wu_ref, wd_ref, o_ref):
    del grp_ref
    b = pl.program_id(0)
    d = o_ref.shape[1]

    @pl.when(b < nb_ref[0])
    def _():
        blk = x_ref[...]
        x = blk[:, :d]
        wp = blk[:, d:].astype(F32)
        acc = jnp.zeros(o_ref.shape, F32)
        for e in range(EXPERTS_PER_GROUP):
            w_e = wp[:, e:e + 1] + wp[:, EXPERTS_PER_GROUP + e:EXPERTS_PER_GROUP + e + 1]
            hid = _silu(_dot(x, wg_ref[e])) * _dot(x, wu_ref[e]) * w_e
            acc = acc + _dot(hid.astype(BF16), wd_ref[e])
        o_ref[...] = acc.astype(BF16)

    @pl.when(b >= nb_ref[0])
    def _():
        o_ref[...] = jnp.zeros(o_ref.shape, BF16)


def _combine_kernel(lo_ref, off_ref, np_ref, ld_ref, x1_ref, g2_ref, shf_ref, scf_ref, fnw_ref, src_ref, y_ref,
                    seg, sems, *, ts, tps, chunk, final_norm):
    i = pl.program_id(0)
    slot = i % 2

    def fetch(step, sl, act):
        for u in range(tps):
            for g in range(N_EGROUPS):
                at = (step * tps + u) * N_EGROUPS + g
                off = pl.multiple_of(off_ref[at], SEG_ALIGN)
                n_pad = np_ref[at]
                for c in range(ts // chunk):
                    cp = pltpu.make_async_copy(src_ref.at[pl.ds(off + c * chunk, chunk)],
                                               seg.at[sl, u, g, pl.ds(c * chunk, chunk)], sems.at[sl, u, g, c])
                    if c == 0:
                        act(cp)
                    else:
                        pl.when(n_pad > c * chunk)(functools.partial(act, cp))

    @pl.when(i == 0)
    def _():
        seg[...] = jnp.zeros(seg.shape, BF16)
        fetch(0, 0, lambda cp: cp.start())

    @pl.when(i + 1 < pl.num_programs(0))
    def _():
        fetch(i + 1, 1 - slot, lambda cp: cp.start())

    fetch(i, slot, lambda cp: cp.wait())

    r = lax.broadcasted_iota(jnp.int32, (ts, ts), 0)
    for u in range(tps):
        rs = slice(u * ts, (u + 1) * ts)
        ld = ld_ref[u]
        moe = jnp.zeros((ts, y_ref.shape[1]), F32)
        for g in range(N_EGROUPS):
            at = (i * tps + u) * N_EGROUPS + g
            perm = jnp.where((r + lo_ref[at] == ld) & (r < np_ref[at]), 1.0, 0.0).astype(BF16)
            moe = moe + _dot_tn(perm, seg[slot, u, g])
        xo = x1_ref[rs, :] + g2_ref[0] * moe
        if final_norm:
            xo = _rms(xo, fnw_ref[...]) * (1.0 + scf_ref[0]) + shf_ref[0]
        y_ref[rs, :] = xo


def _moe_final(h2ext, ldest, cnt, wg, wu, wd, x1, mod3, modf3, boff, rows_per_batch, final_norm_w, final_norm):
    t, d = x1.shape
    payload = h2ext.shape[1]
    ts = min(SORT_TILE, rows_per_batch)
    tps = 2 if rows_per_batch % (2 * ts) == 0 else 1
    chunk = min(MOE_CHUNK, ts)
    n_tiles = t // ts
    ldest = ldest.reshape(n_tiles, 1, ts)
    n_blocks = -(-(t + n_tiles * N_EGROUPS * (SEG_ALIGN - 1) + N_EGROUPS * (chunk + MOE_BLOCK - 1)) // MOE_BLOCK) + 1
    lo, off, n_pad, blk_group, n_used = _sort_plan(cnt, n_tiles, ts, chunk, n_blocks)
    ld_spec = pl.BlockSpec((tps, 1, ts), lambda i, *_: (i, 0, 0))

    sorted_rows = pl.pallas_call(
        functools.partial(_sort_kernel, ts=ts, tps=tps),
        grid_spec=pltpu.PrefetchScalarGridSpec(
            num_scalar_prefetch=3,
            grid=(n_tiles // tps,),
            in_specs=[pl.BlockSpec((tps * ts, payload), lambda i, *_: (i, 0)), ld_spec,
                      pl.BlockSpec(memory_space=pl.ANY)],
            out_specs=pl.BlockSpec(memory_space=pl.ANY),
            scratch_shapes=[pltpu.VMEM((2, tps, ts + N_EGROUPS * SEG_ALIGN, payload), BF16),
                            pltpu.SemaphoreType.DMA((2, tps, N_EGROUPS, (ts // SEG_ALIGN).bit_length()))]),
        out_shape=jax.ShapeDtypeStruct((n_blocks * MOE_BLOCK, payload), BF16),
        input_output_aliases={5: 0},
        compiler_params=_params("arbitrary"),
        name="moe_sort",
    )(lo, off, n_pad, h2ext, ldest, jnp.zeros((n_blocks * MOE_BLOCK, payload), BF16))

    d_e = wg.shape[2]
    live = lambda b, nb: jnp.minimum(b, nb[0] - 1)
    out_sorted = pl.pallas_call(
        _experts_kernel,
        grid_spec=pltpu.PrefetchScalarGridSpec(
            num_scalar_prefetch=2,
            grid=(n_blocks,),
            in_specs=[pl.BlockSpec((MOE_BLOCK, payload), lambda b, grp, nb: (live(b, nb), 0)),
                      pl.BlockSpec((EXPERTS_PER_GROUP, d, d_e), lambda b, grp, nb: (grp[b], 0, 0)),
                      pl.BlockSpec((EXPERTS_PER_GROUP, d, d_e), lambda b, grp, nb: (grp[b], 0, 0)),
                      pl.BlockSpec((EXPERTS_PER_GROUP, d_e, d), lambda b, grp, nb: (grp[b], 0, 0))],
            out_specs=pl.BlockSpec((MOE_BLOCK, d), lambda b, grp, nb: (b, 0))),
        out_shape=jax.ShapeDtypeStruct((n_blocks * MOE_BLOCK, d), BF16),
        compiler_params=_params("arbitrary"),
        name="moe_experts",
    )(blk_group, n_used, sorted_rows, wg, wu, wd)

    per = rows_per_batch // (tps * ts)
    row = lambda col: pl.BlockSpec((1, 1, d), lambda i, *_, col=col: (i // per + boff, 0, col))
    return pl.pallas_call(
        functools.partial(_combine_kernel, ts=ts, tps=tps, chunk=chunk, final_norm=final_norm),
        grid_spec=pltpu.PrefetchScalarGridSpec(
            num_scalar_prefetch=3,
            grid=(n_tiles // tps,),
            in_specs=[ld_spec,
                      pl.BlockSpec((tps * ts, d), lambda i, *_: (i, 0)),
                      row(5), row(0), row(1),
                      pl.BlockSpec((1, d), lambda i, *_: (0, 0)),
                      pl.BlockSpec(memory_space=pl.ANY)],
            out_specs=pl.BlockSpec((tps * ts, d), lambda i, *_: (i, 0)),
            scratch_shapes=[pltpu.VMEM((2, tps, N_EGROUPS, ts, d), BF16),
                            pltpu.SemaphoreType.DMA((2, tps, N_EGROUPS, ts // chunk))]),
        out_shape=jax.ShapeDtypeStruct((t, d), F32),
        compiler_params=_params("arbitrary"),
        name="moe_combine",
    )(lo, off, n_pad, ldest, x1, mod3, modf3, modf3, final_norm_w, out_sorted)


def _layer(x, mod3, modf3, boff, k_prev, v_prev, logf_prev, conv_prev, ssm_prev, p, final_norm_w, final_norm):
    b, l, d = x.shape
    z, xbc, q, k, v, k_b, v_b, sm, smt, conv_new = _inproj(
        x, mod3, boff, p["norm1_w"], p["wz"], p["wx"], p["wq"], p["wk"], p["wv"], p["ws"], p["bs"])
    n_heads_ssd = p["wz"].shape[1] // P_SSD
    y_ssd, ssm_new = _ssd(xbc, z, sm, smt, conv_prev, ssm_prev, p["conv_w"], p["conv_b"], p["a_log"],
                          p["d_skip"], p["ssd_norm_w"])
    n_heads = p["wq"].shape[1] // HD_ATT
    logf_t = smt[:, n_heads_ssd:n_heads_ssd + n_heads, :]
    if k_prev is None:
        p0, k_all, v_all, lf_all = 0, k_b, v_b, logf_t
    else:
        p0 = k_prev.shape[1]
        pad = (-(p0 + l)) % LANES
        zeros = lambda w: jnp.zeros((b, pad, w), BF16)
        k_all = jnp.concatenate([k_prev.astype(BF16), k_b, zeros(k_b.shape[2])], axis=1)
        v_all = jnp.concatenate([v_prev.astype(BF16), v_b, zeros(v_b.shape[2])], axis=1)
        lf_all = jnp.concatenate([jnp.swapaxes(logf_prev, 1, 2), logf_t, jnp.zeros((b, n_heads, pad), F32)], axis=2)
    o_att = _attention(q, k_all, v_all, _forget_cumsum(lf_all), p0)
    x1, h2ext, ldest, cnt = _outproj(y_ssd, o_att, x, mod3, boff, p["attn_norm_w"], p["wo_ssd"], p["wo_att"],
                                     p["norm2_w"], p["wr"], p["br"])
    y = _moe_final(h2ext, ldest, cnt, p["wg"], p["wu"], p["wd"], x1.reshape(b * l, d), mod3, modf3, boff, l,
                   final_norm_w, final_norm)
    return y.reshape(b, l, d), (k, v, jnp.swapaxes(logf_t, 1, 2), conv_new, ssm_new)


def kernel(x_prompt, x_sample, c_prompt, c_sample, cache_k, cache_v, cache_logf, state_conv, state_ssm, norm1_w, w_ada, b_ada, w_in, conv_w, conv_b, dt_bias, a_log, d_skip, ssd_norm_w, f_bias, attn_norm_w, w_out, norm2_w, w_rg, b_rg, w_re, b_re, w_gate, w_up, w_down, final_norm_w, w_ada_f, b_ada_f):
    depth = w_in.shape[0]
    bp, lp, d = x_prompt.shape
    bs = x_sample.shape[0]
    d_conv = conv_w.shape[2]
    d_ssd = ssd_norm_w.shape[1]
    d_att = attn_norm_w.shape[1]
    h_ssd = dt_bias.shape[1]
    h_att = f_bias.shape[1]
    assert h_ssd + h_att == SMALL_W and d_att // HD_ATT == h_att and d_ssd // P_SSD == h_ssd

    c_all = jnp.concatenate([c_prompt, c_sample], axis=0)
    modf3 = _modulation(c_all, w_ada_f, b_ada_f).reshape(bp + bs, 1, 2 * d)
    final_w = final_norm_w.reshape(1, d)

    i0 = d_ssd
    i1 = i0 + d_conv
    i2 = i1 + h_ssd
    i3 = i2 + d_att
    i4 = i3 + d_att
    i5 = i4 + d_att
    yp, ys = x_prompt, x_sample
    outs_p, outs_s = [], []
    for layer in range(depth):
        mod3 = _modulation(c_all, w_ada[layer], b_ada[layer]).reshape(bp + bs, 1, 6 * d)
        wi = w_in[layer]
        w_small = jnp.concatenate([wi[:, i1:i2], wi[:, i5:], jnp.zeros((d, LANES - SMALL_W), F32)], axis=1)
        b_small = jnp.concatenate([dt_bias[layer], f_bias[layer], jnp.zeros((LANES - SMALL_W,), F32)])
        wr = jnp.zeros((ROUTER_ROWS, d), F32)
        wr = wr.at[:N_EGROUPS].set(w_rg[layer].T)
        wr = wr.at[EXPERT_ROW0:EXPERT_ROW0 + N_EXPERTS].set(
            jnp.transpose(w_re[layer], (0, 2, 1)).reshape(N_EXPERTS, d))
        wr_hi, wr_lo = _split2(wr)
        br = jnp.zeros((ROUTER_ROWS,), F32)
        br = br.at[:N_EGROUPS].set(b_rg[layer])
        br = br.at[EXPERT_ROW0:EXPERT_ROW0 + N_EXPERTS].set(b_re[layer].reshape(N_EXPERTS))
        p = dict(
            norm1_w=norm1_w[layer].reshape(1, d),
            wz=wi[:, :i0].astype(BF16), wx=wi[:, i0:i1].astype(BF16),
            wq=(wi[:, i2:i3] * (LOG2E * HD_ATT ** -0.5)).astype(BF16),
            wk=wi[:, i3:i4].astype(BF16), wv=wi[:, i4:i5].astype(BF16),
            ws=w_small.astype(BF16), bs=b_small.reshape(1, LANES),
            conv_w=conv_w[layer], conv_b=conv_b[layer], a_log=a_log[layer], d_skip=d_skip[layer],
            ssd_norm_w=ssd_norm_w[layer], attn_norm_w=attn_norm_w[layer].reshape(1, d_att),
            wo_ssd=w_out[layer][:d_ssd].astype(BF16), wo_att=w_out[layer][d_ssd:].astype(BF16),
            norm2_w=norm2_w[layer].reshape(1, d),
            wr=jnp.concatenate([wr_hi, wr_lo], axis=0), br=br.reshape(ROUTER_ROWS, 1),
            wg=w_gate[layer].astype(BF16), wu=w_up[layer].astype(BF16), wd=w_down[layer].astype(BF16),
        )
        conv0 = jnp.zeros((bp, CONV_W - 1, d_conv), F32)
        ssm0 = jnp.zeros((bp, h_ssd, P_SSD, N_STATE), F32)
        last = layer == depth - 1
        yp, st_p = _layer(yp, mod3, modf3, 0, None, None, None, conv0, ssm0, p, final_w, last)
        ck = cache_k[layer].reshape(bs, -1, d_att)
        cv = cache_v[layer].reshape(bs, -1, d_att)
        ys, st_s = _layer(ys, mod3, modf3, bp, ck, cv, cache_logf[layer], state_conv[layer], state_ssm[layer], p,
                          final_w, last)
        outs_p.append(st_p)
        outs_s.append(st_s)

    def stack(outs, b, l):
        k = jnp.stack([o[0].reshape(b, l, h_att, HD_ATT) for o in outs])
        v = jnp.stack([o[1].reshape(b, l, h_att, HD_ATT) for o in outs])
        return (k, v, jnp.stack([o[2] for o in outs]), jnp.stack([o[3] for o in outs]),
                jnp.stack([o[4] for o in outs]))

    return (yp, ys) + stack(outs_p, bp, lp) + stack(outs_s, bs, x_sample.shape[1])
```

```python
import functools

import jax
import jax.numpy as jnp
import numpy as np
from jax import lax
from jax.experimental import pallas as pl
from jax.experimental.pallas import tpu as pltpu

F32 = jnp.float32
BF16 = jnp.bfloat16

P_SSD = 64
N_STATE = 64
G_SSD = 2
CONV_W = 4
HD_ATT = 64
N_EGROUPS = 4
EXPERTS_PER_GROUP = 4
N_EXPERTS = N_EGROUPS * EXPERTS_PER_GROUP
EPS = 1e-6
NEG_BIG = -1e30

LANES = 128
SEG_ALIGN = 16
SORT_TILE = 256
MOE_BLOCK = 512
SMALL_W = 16
VMEM_LIMIT = 56 * 1024 * 1024


def _params(*sem):
    return pltpu.CompilerParams(dimension_semantics=sem, vmem_limit_bytes=VMEM_LIMIT)


def _split2(x):
    hi = x.astype(BF16)
    lo = (x - hi.astype(F32)).astype(BF16)
    return hi, lo


def _split3(x):
    hi = x.astype(BF16)
    r = x - hi.astype(F32)
    mid = r.astype(BF16)
    lo = (r - mid.astype(F32)).astype(BF16)
    return hi, mid, lo


def _dot(a, b):
    return jnp.dot(a, b, preferred_element_type=F32)


def _dot_nt(a, b):
    return lax.dot_general(a, b, (((1,), (1,)), ((), ())), preferred_element_type=F32)


def _dot_tn(a, b):
    return lax.dot_general(a, b, (((0,), (0,)), ((), ())), preferred_element_type=F32)


def _silu(x):
    h = 0.5 * x
    return h + h * jnp.tanh(h)


def _rms(x, w):
    return x * lax.rsqrt(jnp.mean(x * x, axis=-1, keepdims=True) + EPS) * w


def _pick_tile(n, candidates):
    for c in candidates:
        if n % c == 0:
            return c
    return n


def _mod_kernel(c_ref, w_ref, b_ref, o_ref):
    a = _silu(c_ref[...])
    a_hi, a_lo = _split2(a)
    w_hi, w_lo = _split2(w_ref[...])
    o_ref[...] = _dot(a_hi, w_hi) + _dot(a_lo, w_hi) + _dot(a_hi, w_lo) + b_ref[...]


def _modulation(c, w, b):
    m, d = c.shape
    n = w.shape[1]
    tn = _pick_tile(n, (1024, 512, 256, 128))
    return pl.pallas_call(
        _mod_kernel,
        grid=(n // tn,),
        in_specs=[pl.BlockSpec((m, d), lambda j: (0, 0)),
                  pl.BlockSpec((d, tn), lambda j: (0, j)),
                  pl.BlockSpec((1, tn), lambda j: (0, j))],
        out_specs=pl.BlockSpec((m, tn), lambda j: (0, j)),
        out_shape=jax.ShapeDtypeStruct((m, n), F32),
        compiler_params=_params("parallel"),
        name="adaln_mod",
    )(c, w, b.reshape(1, n))


def _inproj_kernel(x_ref, sh_ref, sc_ref, nw_ref, wz_ref, wx_ref, wq_ref, wk_ref, wv_ref, ws_ref, bs_ref,
                   z_ref, xbc_ref, q_ref, k_ref, v_ref, kb_ref, vb_ref, sm_ref, smt_ref, tail_ref):
    l = pl.program_id(1)
    x = x_ref[0]
    h = (_rms(x, nw_ref[...]) * (1.0 + sc_ref[0]) + sh_ref[0]).astype(BF16)
    z_ref[0] = _dot(h, wz_ref[...]).astype(BF16)
    xbc = _dot(h, wx_ref[...])
    xbc_ref[0] = xbc.astype(BF16)
    q_ref[0] = _dot(h, wq_ref[...]).astype(BF16)
    k = _dot(h, wk_ref[...])
    k_ref[0] = k
    kb_ref[0] = k.astype(BF16)
    v = _dot(h, wv_ref[...])
    v_ref[0] = v
    vb_ref[0] = v.astype(BF16)
    s = _dot(h, ws_ref[...]) + bs_ref[...]
    t = jnp.log(1.0 + jnp.exp(-jnp.abs(s)))
    lane = lax.broadcasted_iota(jnp.int32, s.shape, 1)
    s = jnp.where(lane < SMALL_W // 2, jnp.maximum(s, 0.0) + t, jnp.minimum(s, 0.0) - t)
    sm_ref[0] = s[:, :SMALL_W]
    smt_ref[0] = s.T[:SMALL_W, :]

    @pl.when(l == pl.num_programs(1) - 1)
    def _():
        tl = xbc.shape[0]
        tail_ref[0] = xbc[tl - (CONV_W - 1):, :]


def _inproj(x, mod3, boff, norm_w, wz, wx, wq, wk, wv, ws, bs):
    b, l, d = x.shape
    tl = _pick_tile(l, (1024, 512, 256, 128, 64))
    nl = l // tl
    d_ssd, d_conv, d_att = wz.shape[1], wx.shape[1], wq.shape[1]
    row = lambda col: pl.BlockSpec((1, 1, d), lambda i, j, col=col: (i + boff, 0, col))
    full = lambda a: pl.BlockSpec(a.shape, lambda i, j: (0,) * a.ndim)
    tok = lambda w: pl.BlockSpec((1, tl, w), lambda i, j: (i, j, 0))
    out_shape = [
        jax.ShapeDtypeStruct((b, l, d_ssd), BF16),
        jax.ShapeDtypeStruct((b, l, d_conv), BF16),
        jax.ShapeDtypeStruct((b, l, d_att), BF16),
        jax.ShapeDtypeStruct((b, l, d_att), F32),
        jax.ShapeDtypeStruct((b, l, d_att), F32),
        jax.ShapeDtypeStruct((b, l, d_att), BF16),
        jax.ShapeDtypeStruct((b, l, d_att), BF16),
        jax.ShapeDtypeStruct((b, l, SMALL_W), F32),
        jax.ShapeDtypeStruct((b, SMALL_W, l), F32),
        jax.ShapeDtypeStruct((b, CONV_W - 1, d_conv), F32),
    ]
    out_specs = [tok(d_ssd), tok(d_conv), tok(d_att), tok(d_att), tok(d_att), tok(d_att), tok(d_att),
                 tok(SMALL_W), pl.BlockSpec((1, SMALL_W, tl), lambda i, j: (i, 0, j)),
                 pl.BlockSpec((1, CONV_W - 1, d_conv), lambda i, j: (i, 0, 0))]
    return pl.pallas_call(
        _inproj_kernel,
        grid=(b, nl),
        in_specs=[tok(d), row(0), row(1), full(norm_w), full(wz), full(wx), full(wq), full(wk), full(wv),
                  full(ws), full(bs)],
        out_specs=out_specs,
        out_shape=out_shape,
        compiler_params=_params("parallel", "arbitrary"),
        name="inproj",
    )(x, mod3, mod3, norm_w, wz, wx, wq, wk, wv, ws, bs)


LOG2E = 1.4426950408889634
AUG = 3
SPLIT_ROWS = 128


PIECE_COLS = 32


def _aug_select(n_heads):
    pair = 2 * HD_ATT
    sq = np.zeros((n_heads // 2, PIECE_COLS, pair), np.float32)
    sk = np.zeros((n_heads // 2, PIECE_COLS, pair), np.float32)
    one = AUG * n_heads
    for h in range(n_heads):
        slot = ((h ^ 1) % 2) * HD_ATT
        for c in range(AUG):
            sq[h // 2, c * n_heads + h, slot + c] = 1.0
            sq[h // 2, one, slot + AUG + c] = 1.0
            sk[h // 2, one, slot + c] = 1.0
            sk[h // 2, c * n_heads + h, slot + AUG + c] = -1.0
    return sq, sk


def _cumsum_kernel(x_ref, eye_ref, o_ref, carry):
    @pl.when(pl.program_id(1) == 0)
    def _():
        carry[...] = jnp.zeros_like(carry)

    x = x_ref[0]
    h, tc = x.shape
    r = lax.broadcasted_iota(jnp.int32, (tc, tc), 0)
    c = lax.broadcasted_iota(jnp.int32, (tc, tc), 1)
    upper = jnp.where(r <= c, 1.0, 0.0).astype(BF16)
    stack = lambda ps: jnp.concatenate([p.astype(F32) for p in ps], axis=0)
    parts = _dot(stack(_split3(x)).astype(BF16), upper)
    cs = parts[:h] + parts[h:2 * h] + parts[2 * h:] + carry[...]
    carry[...] = cs[:, tc - 1:]
    rows = lax.broadcasted_iota(jnp.int32, (PIECE_COLS - AUG * h, tc), 0)
    pieces = jnp.concatenate([stack(_split3(cs * LOG2E)), jnp.where(rows == 0, 1.0, 0.0)], axis=0)
    o_ref[0] = _dot_tn(pieces.astype(BF16), eye_ref[...]).astype(BF16)


def _forget_cumsum(logf_t):
    b, h, lk = logf_t.shape
    assert AUG * h < PIECE_COLS
    tc = _pick_tile(lk, (1024, 512, 384, 256, 128))
    eye = jnp.asarray(np.eye(PIECE_COLS), BF16)
    return pl.pallas_call(
        _cumsum_kernel,
        grid=(b, lk // tc),
        in_specs=[pl.BlockSpec((1, h, tc), lambda i, j: (i, 0, j)),
                  pl.BlockSpec((PIECE_COLS, PIECE_COLS), lambda i, j: (0, 0))],
        out_specs=pl.BlockSpec((1, tc, PIECE_COLS), lambda i, j: (i, j, 0)),
        out_shape=jax.ShapeDtypeStruct((b, lk, PIECE_COLS), BF16),
        scratch_shapes=[pltpu.VMEM((h, 1), F32)],
        compiler_params=_params("parallel", "arbitrary"),
        name="forget_cumsum",
    )(logf_t, eye)


def _ssd_spread(n_heads):
    e = np.zeros((2 * AUG * n_heads, 2 * n_heads * P_SSD), np.float32)
    for v in range(2):
        for c in range(AUG):
            for h in range(n_heads):
                col = v * n_heads * P_SSD + h * P_SSD
                e[(v * AUG + c) * n_heads + h, col:col + P_SSD] = 1.0
    return e


def _ssd_kernel(xbc_ref, z_ref, sm_ref, smt_ref, cprev_ref, sprev_ref, cw_ref, cb_ref, arow_ref, acol_ref,
                dskip_ref, nw_ref, spread_ref, y_ref, snew_ref, hist, state):
    l = pl.program_id(1)
    q = xbc_ref.shape[1]
    d_ssd = z_ref.shape[2]
    n_heads = d_ssd // P_SSD
    pair = 2 * P_SSD
    hist_rows = hist.shape[0]

    @pl.when(l == 0)
    def _():
        hist[...] = jnp.zeros(hist.shape, F32)
        hist[hist_rows - (CONV_W - 1):, :] = cprev_ref[0]
        state[...] = sprev_ref[0]

    xb = xbc_ref[0]
    r3 = lax.broadcasted_iota(jnp.int32, ((CONV_W - 1) * q, q), 0)
    c3 = lax.broadcasted_iota(jnp.int32, ((CONV_W - 1) * q, q), 1)
    shift = jnp.where(r3 % q - c3 == r3 // q + 1, 1.0, 0.0).astype(BF16)
    shifted = _dot(shift, xb)
    cw = cw_ref[...]
    conv = cb_ref[...] + xb.astype(F32) * cw[CONV_W - 1:CONV_W, :]
    for k in range(CONV_W - 1):
        conv = conv + shifted[k * q:(k + 1) * q, :] * cw[CONV_W - 2 - k:CONV_W - 1 - k, :]
    hrow = lambda j: hist[hist_rows - j:hist_rows - j + 1, :]
    frow = lax.broadcasted_iota(jnp.int32, (hist_rows, 1), 0)
    fix = jnp.zeros((hist_rows, conv.shape[1]), F32)
    for t in range(CONV_W - 1):
        acc = 0.0
        for j in range(1, CONV_W - t):
            acc = acc + hrow(j) * cw[CONV_W - 1 - t - j:CONV_W - t - j, :]
        fix = jnp.where(frow == t, acc, fix)
    conv = jnp.concatenate([conv[:hist_rows] + fix, conv[hist_rows:]], axis=0)
    hist[...] = xb[q - hist_rows:, :].astype(F32)
    act = _silu(conv)

    dt = sm_ref[0][:, :n_heads]
    dt_t = smt_ref[0][:n_heads, :]
    a_row = -jnp.exp(arow_ref[...]) * LOG2E
    a_col = -jnp.exp(acol_ref[...]) * LOG2E
    r = lax.broadcasted_iota(jnp.int32, (q, q), 0)
    c = lax.broadcasted_iota(jnp.int32, (q, q), 1)
    causal = r >= c
    lower = jnp.where(causal, 1.0, 0.0).astype(BF16)
    upper = jnp.where(r <= c, 1.0, 0.0).astype(BF16)
    h0, h1, h2 = _split3(dt * a_row)
    acum = _dot(lower, h0) + _dot(lower, h1) + _dot(lower, h2)
    t0, t1, t2 = _split3(dt_t * a_col)
    acum_t = _dot(t0, upper) + _dot(t1, upper) + _dot(t2, upper)
    pieces = [p.astype(F32) for p in _split3(acum) + _split3(dt)]
    wide = _dot(jnp.concatenate(pieces, axis=1).astype(BF16), spread_ref[...])
    acum_x = wide[:, :d_ssd]
    dt_x = wide[:, d_ssd:]
    a_last = acum[q - 1:q, :]
    e_all = jnp.exp2(a_last)
    e_cum_x = jnp.exp2(acum_x)
    e_end_x = jnp.exp2(acum_x[q - 1:q, :] - acum_x)

    xs = act[:, :d_ssd]
    bm = act[:, d_ssd:d_ssd + G_SSD * N_STATE]
    cm = act[:, d_ssd + G_SSD * N_STATE:]
    bm_b = bm.astype(BF16)
    cm_b = cm.astype(BF16)
    xdt = xs * dt_x
    xdt_b = xdt.astype(BF16)
    xe_b = (xdt * e_end_x).astype(BF16)
    lane = lax.broadcasted_iota(jnp.int32, (1, pair), 1)
    first = lane < P_SSD
    srow = lax.broadcasted_iota(jnp.int32, (pair, 1), 0)
    ys = []
    for p in range(n_heads // 2):
        g = (2 * p * G_SSD) // n_heads
        in_group = first if g == 0 else jnp.logical_not(first)
        sl = slice(p * pair, (p + 1) * pair)
        if (2 * p) % (n_heads // G_SSD) == 0:
            cb = _dot_nt(jnp.where(in_group, cm_b, jnp.zeros_like(cm_b)), bm_b)
        ms = []
        for hh in range(2):
            h = 2 * p + hh
            seg = acum[:, h:h + 1] - acum_t[h:h + 1, :]
            ms.append((cb * jnp.exp2(jnp.where(causal, seg, NEG_BIG))).astype(BF16))
        xp = xdt_b[:, sl]
        zero = jnp.zeros_like(xp)
        y = _dot(jnp.concatenate(ms, axis=1),
                 jnp.concatenate([jnp.where(first, xp, zero), jnp.where(first, zero, xp)], axis=0))
        s_in = state[sl, :]
        y = y + _dot_nt(cm_b, s_in.astype(BF16)) * e_cum_x[:, sl]
        upd = _dot_tn(xe_b[:, sl], bm_b)
        keep = jnp.where(srow < P_SSD, e_all[:, 2 * p:2 * p + 1], e_all[:, 2 * p + 1:2 * p + 2])
        state[sl, :] = s_in * keep + jnp.where(in_group, upd, 0.0)
        ys.append(y)
    y_all = jnp.concatenate(ys, axis=1) + dskip_ref[...] * xs
    yg = y_all * _silu(z_ref[0].astype(F32))
    y_ref[0] = _rms(yg, nw_ref[...]).astype(BF16)
    snew_ref[0] = state[...]


def _ssd(xbc, z, sm, smt, conv_prev, ssm_prev, conv_w, conv_b, a_log, d_skip, norm_w):
    b, l, d_conv = xbc.shape
    d_ssd = z.shape[2]
    n_heads = d_ssd // P_SSD
    hg = n_heads // G_SSD
    q = _pick_tile(l, (256, 128, 64))
    tok = lambda w: pl.BlockSpec((1, q, w), lambda i, j: (i, j, 0))
    full = lambda a: pl.BlockSpec(a.shape, lambda i, j: (0,) * a.ndim)
    a_row = a_log.reshape(1, n_heads)
    a_col = a_log.reshape(n_heads, 1)
    conv_b = conv_b.reshape(1, d_conv)
    d_skip = jnp.repeat(d_skip, P_SSD).reshape(1, d_ssd)
    norm_w = norm_w.reshape(1, d_ssd)
    spread = jnp.asarray(_ssd_spread(n_heads), BF16)
    s4 = ssm_prev.reshape(b, G_SSD, hg * P_SSD, N_STATE)
    s_in = jnp.concatenate([jnp.pad(s4[:, g], ((0, 0), (0, 0), (g * N_STATE, (G_SSD - 1 - g) * N_STATE)))
                            for g in range(G_SSD)], axis=1)
    state_spec = pl.BlockSpec((1, n_heads * P_SSD, G_SSD * N_STATE), lambda i, j: (i, 0, 0))
    y, s_out = pl.pallas_call(
        _ssd_kernel,
        grid=(b, l // q),
        in_specs=[tok(d_conv), tok(d_ssd), tok(SMALL_W), pl.BlockSpec((1, SMALL_W, q), lambda i, j: (i, 0, j)),
                  pl.BlockSpec((1, CONV_W - 1, d_conv), lambda i, j: (i, 0, 0)), state_spec,
                  full(conv_w), full(conv_b), full(a_row), full(a_col), full(d_skip), full(norm_w), full(spread)],
        out_specs=[tok(d_ssd), state_spec],
        out_shape=[jax.ShapeDtypeStruct((b, l, d_ssd), BF16),
                   jax.ShapeDtypeStruct((b, n_heads * P_SSD, G_SSD * N_STATE), F32)],
        scratch_shapes=[pltpu.VMEM((8, d_conv), F32),
                        pltpu.VMEM((n_heads * P_SSD, G_SSD * N_STATE), F32)],
        compiler_params=_params("parallel", "arbitrary"),
        name="ssd",
    )(xbc, z, sm, smt, conv_prev, s_in, conv_w, conv_b, a_row, a_col, d_skip, norm_w, spread)
    s_out = s_out.reshape(b, G_SSD, hg * P_SSD, G_SSD * N_STATE)
    s_new = jnp.concatenate([s_out[:, g, :, g * N_STATE:(g + 1) * N_STATE] for g in range(G_SSD)], axis=1)
    return y, s_new.reshape(b, n_heads, P_SSD, N_STATE)


def _attn_kernel(q_ref, pq_ref, k_ref, pk_ref, v_ref, sq_ref, sk_ref, o_ref, kk_s, m_s, acc_s, *, p0, tq, tk):
    i = pl.program_id(2)
    lane = lax.broadcasted_iota(jnp.int32, (1, 2 * HD_ATT), 1)
    first = lane < HD_ATT
    own = (first, jnp.logical_not(first))
    sum_lane = (HD_ATT, 0)

    @pl.when(i == 0)
    def _():
        k = k_ref[0]
        ka = _dot(pk_ref[0], sk_ref[0]).astype(BF16)
        kk_s[0] = jnp.where(first, k, ka)
        kk_s[1] = jnp.where(first, ka, k)

    q = q_ref[0]
    qa = _dot(pq_ref[0], sq_ref[0]).astype(BF16)
    qq = (jnp.where(first, q, qa), jnp.where(first, qa, q))
    m_s[...] = jnp.full(m_s.shape, NEG_BIG, F32)
    acc_s[...] = jnp.zeros(acc_s.shape, F32)
    nc = tk // LANES
    n_split = 2 if tq % (2 * SPLIT_ROWS) == 0 else 1
    rows = tq // n_split

    def step(j, masked):
        off = pl.multiple_of(j * tk, tk)
        v = v_ref[0, pl.ds(off, tk), :]
        vv = [jnp.where(own[hh], v, jnp.where(lane == sum_lane[hh], 1.0, 0.0).astype(BF16)) for hh in range(2)]
        logit = [[_dot_nt(qq[hh][r * rows:(r + 1) * rows], kk_s[hh, pl.ds(off, tk), :]) for hh in range(2)]
                 for r in range(n_split)]
        for r in range(n_split):
            rs = slice(r * rows, (r + 1) * rows)
            if masked:
                q_pos = p0 + i * tq + r * rows + lax.broadcasted_iota(jnp.int32, (rows, tk), 0)
                k_pos = j * tk + lax.broadcasted_iota(jnp.int32, (rows, tk), 1)
                visible = k_pos <= q_pos
            for hh in range(2):
                s = jnp.where(visible, logit[r][hh], NEG_BIG) if masked else logit[r][hh]
                cols = [s[:, c * LANES:(c + 1) * LANES] for c in range(nc)]
                m_cur = functools.reduce(jnp.maximum, cols)
                m_prev = m_s[hh, rs, :]
                m_new = jnp.maximum(m_prev, jnp.max(m_cur, axis=1, keepdims=True))
                alpha = jnp.exp2(m_prev - m_new)
                p = jnp.concatenate([jnp.exp2((col - m_new).astype(BF16)) for col in cols], axis=1)
                m_s[hh, rs, :] = m_new
                acc_s[hh, rs, :] = alpha * acc_s[hh, rs, :] + _dot(p, vv[hh])

    n_full = (p0 + i * tq + 1) // tk
    n_vis = (p0 + i * tq + tq - 1) // tk + 1

    def full_pair(jj, carry):
        step(2 * jj, False)
        step(2 * jj + 1, False)
        return carry

    def masked_body(j, carry):
        step(j, True)
        return carry

    lax.fori_loop(0, n_full // 2, full_pair, 0)
    pl.when(n_full % 2 == 1)(lambda: step(n_full - 1, False))
    lax.fori_loop(n_full, n_vis, masked_body, 0)
    a0 = acc_s[0]
    a1 = acc_s[1]
    o = jnp.where(first, a0 / a0[:, sum_lane[0]:sum_lane[0] + 1], a1 / a1[:, sum_lane[1]:sum_lane[1] + 1])
    o_ref[0] = o.astype(o_ref.dtype)


def _attention(q, k, v, pieces, p0):
    b, lq, d_att = q.shape
    lk = k.shape[1]
    n_heads = d_att // HD_ATT
    tq = _pick_tile(lq, (512, 256, 128, 64))
    tk = lk if lk <= 1536 else _pick_tile(lk, (512, 256, 128))
    assert p0 % tq == 0 and lk >= p0 + lq
    pair = 2 * HD_ATT
    sq, sk = (jnp.asarray(a, BF16) for a in _aug_select(n_heads))
    q_spec = pl.BlockSpec((1, tq, pair), lambda bi, hp, i: (bi, i, hp))
    kv_spec = pl.BlockSpec((1, lk, pair), lambda bi, hp, i: (bi, 0, hp))
    sel_spec = pl.BlockSpec((1, PIECE_COLS, pair), lambda bi, hp, i: (hp, 0, 0))
    return pl.pallas_call(
        functools.partial(_attn_kernel, p0=p0, tq=tq, tk=tk),
        grid=(b, n_heads // 2, lq // tq),
        in_specs=[q_spec, pl.BlockSpec((1, tq, PIECE_COLS), lambda bi, hp, i: (bi, i + p0 // tq, 0)),
                  kv_spec, pl.BlockSpec((1, lk, PIECE_COLS), lambda bi, hp, i: (bi, 0, 0)), kv_spec,
                  sel_spec, sel_spec],
        out_specs=q_spec,
        out_shape=jax.ShapeDtypeStruct((b, lq, d_att), BF16),
        scratch_shapes=[pltpu.VMEM((2, lk, pair), BF16), pltpu.VMEM((2, tq, LANES), F32),
                        pltpu.VMEM((2, tq, pair), F32)],
        compiler_params=_params("parallel", "parallel", "arbitrary"),
        name="fox_attention",
    )(q, pieces, k, pieces, v, sq, sk)


ROUTER_ROWS = 32
EXPERT_ROW0 = 8


def _outproj_kernel(ys_ref, oa_ref, x_ref, g1_ref, sh_ref, sc_ref, anw_ref, wos_ref, woa_ref, n2w_ref, wr_ref,
                    br_ref, x1_ref, h2_ref, ld_ref, cnt_ref, wt_s, g_s):
    ya = _rms(oa_ref[0].astype(F32), anw_ref[...]).astype(BF16)
    m = _dot(ys_ref[0], wos_ref[...]) + _dot(ya, woa_ref[...])
    x1 = x_ref[0] + g1_ref[0] * m
    x1_ref[0] = x1
    h2 = _rms(x1, n2w_ref[...]) * (1.0 + sc_ref[0]) + sh_ref[0]
    h_hi, h_lo = _split2(h2)
    wr = wr_ref[...]
    p1 = _dot_nt(wr, h_hi)
    p2 = _dot_nt(wr[:ROUTER_ROWS], h_lo)
    logit = p1[:ROUTER_ROWS] + p1[ROUTER_ROWS:] + p2 + br_ref[...]

    lg = [logit[g:g + 1, :] for g in range(N_EGROUPS)]
    gmax = jnp.maximum(jnp.maximum(lg[0], lg[1]), jnp.maximum(lg[2], lg[3]))
    denom = sum(jnp.exp(x - gmax) for x in lg)
    p_sel = 1.0 / denom
    is_g = []
    taken = jnp.zeros_like(gmax) > 1.0
    for g in range(N_EGROUPS):
        hit = (lg[g] == gmax) & jnp.logical_not(taken)
        is_g.append(hit)
        taken = taken | hit
    le = []
    for e in range(EXPERTS_PER_GROUP):
        v = logit[EXPERT_ROW0 + 3 * EXPERTS_PER_GROUP + e:EXPERT_ROW0 + 3 * EXPERTS_PER_GROUP + e + 1, :]
        for g in range(N_EGROUPS - 2, -1, -1):
            r0 = EXPERT_ROW0 + g * EXPERTS_PER_GROUP + e
            v = jnp.where(is_g[g], logit[r0:r0 + 1, :], v)
        le.append(v)
    m1 = jnp.maximum(jnp.maximum(le[0], le[1]), jnp.maximum(le[2], le[3]))
    first = []
    taken = jnp.zeros_like(m1) > 1.0
    for e in range(EXPERTS_PER_GROUP):
        hit = (le[e] == m1) & jnp.logical_not(taken)
        first.append(hit)
        taken = taken | hit
    rest = [jnp.where(first[e], -jnp.inf, le[e]) for e in range(EXPERTS_PER_GROUP)]
    m2 = jnp.maximum(jnp.maximum(rest[0], rest[1]), jnp.maximum(rest[2], rest[3]))
    second = []
    taken = jnp.zeros_like(m1) > 1.0
    for e in range(EXPERTS_PER_GROUP):
        hit = (rest[e] == m2) & jnp.logical_not(taken)
        second.append(hit)
        taken = taken | hit
    e2 = jnp.exp(m2 - m1)
    w_a = p_sel / (1.0 + e2)
    w_b = w_a * e2
    d = x1.shape[1]
    wt_s[...] = jnp.zeros(wt_s.shape, F32)
    for e in range(EXPERTS_PER_GROUP):
        w = jnp.where(first[e], w_a, jnp.where(second[e], w_b, 0.0))
        w_hi = w.astype(BF16).astype(F32)
        wt_s[e:e + 1, :] = w_hi
        wt_s[EXPERTS_PER_GROUP + e:EXPERTS_PER_GROUP + e + 1, :] = w - w_hi
    h2_ref[:, :d] = h_hi
    h2_ref[:, d:] = wt_s[...].T.astype(BF16)

    tl = logit.shape[1]
    ts = min(SORT_TILE, tl)
    g_s[...] = jnp.zeros(g_s.shape, F32)
    for g in range(N_EGROUPS):
        g_s[g:g + 1, :] = jnp.where(is_g[g], 1.0, 0.0)
    r = lax.broadcasted_iota(jnp.int32, (tl, tl), 0)
    c = lax.broadcasted_iota(jnp.int32, (tl, tl), 1)
    same_tile = (r // ts) == (c // ts)
    upper = jnp.where((r <= c) & same_tile, 1.0, 0.0).astype(BF16)
    cum = _dot(g_s[...].astype(BF16), upper)
    lane = lax.broadcasted_iota(jnp.int32, (1, tl), 1)
    crow = lax.broadcasted_iota(jnp.int32, (8, LANES), 0)
    clane = lax.broadcasted_iota(jnp.int32, (8, LANES), 1)
    ldest = -1.0
    cnt = jnp.zeros((8, LANES), F32)
    for g in range(N_EGROUPS):
        ldest = ldest + jnp.where(is_g[g], cum[g:g + 1, :], 0.0)
    lo = [0.0] * (tl // ts)
    for g in range(N_EGROUPS):
        lo_row = jnp.zeros((1, tl), F32)
        for sub in range(tl // ts):
            n = cum[g:g + 1, (sub + 1) * ts - 1:(sub + 1) * ts]
            n_pad = jnp.ceil(n / SEG_ALIGN) * SEG_ALIGN
            lo_row = jnp.where(lane // ts == sub, lo[sub], lo_row)
            cnt = jnp.where((crow == sub) & (clane == g), n_pad, cnt)
            lo[sub] = lo[sub] + n_pad
        ldest = ldest + jnp.where(is_g[g], lo_row, 0.0)
    ld_ref[0] = ldest.astype(jnp.int32)
    cnt_ref[0] = cnt.astype(jnp.int32)


def _outproj(y_ssd, o_att, x, mod3, boff, attn_norm_w, wo_ssd, wo_att, norm2_w, wr, br):
    b, l, d = x.shape
    tl = _pick_tile(l, (512, 256, 128, 64))
    nl = l // tl
    d_ssd, d_att = y_ssd.shape[2], o_att.shape[2]
    row = lambda col: pl.BlockSpec((1, 1, d), lambda i, j, col=col: (i + boff, 0, col))
    full = lambda a: pl.BlockSpec(a.shape, lambda i, j: (0,) * a.ndim)
    tok = lambda w: pl.BlockSpec((1, tl, w), lambda i, j: (i, j, 0))
    return pl.pallas_call(
        _outproj_kernel,
        grid=(b, nl),
        in_specs=[tok(d_ssd), tok(d_att), tok(d), row(2), row(3), row(4), full(attn_norm_w), full(wo_ssd),
                  full(wo_att), full(norm2_w), full(wr), full(br)],
        out_specs=[tok(d), pl.BlockSpec((tl, d + LANES), lambda i, j: (i * nl + j, 0)),
                   pl.BlockSpec((1, 1, tl), lambda i, j: (i * nl + j, 0, 0)),
                   pl.BlockSpec((1, 8, LANES), lambda i, j: (i * nl + j, 0, 0))],
        out_shape=[jax.ShapeDtypeStruct((b, l, d), F32),
                   jax.ShapeDtypeStruct((b * l, d + LANES), BF16),
                   jax.ShapeDtypeStruct((b * nl, 1, tl), jnp.int32),
                   jax.ShapeDtypeStruct((b * nl, 8, LANES), jnp.int32)],
        scratch_shapes=[pltpu.VMEM((LANES, tl), F32), pltpu.VMEM((8, tl), F32)],
        compiler_params=_params("parallel", "parallel"),
        name="outproj_router",
    )(y_ssd, o_att, x, mod3, mod3, mod3, attn_norm_w, wo_ssd, wo_att, norm2_w, wr, br)


def _sort_plan(cnt, n_tiles, block, n_blocks):
    ns = n_tiles // cnt.shape[0]
    n_pad = cnt[:, :ns, :N_EGROUPS].reshape(n_tiles, N_EGROUPS)
    lo = jnp.cumsum(n_pad, axis=1) - n_pad
    region = (jnp.sum(n_pad, axis=0) + block - 1) // block * block
    end = jnp.cumsum(region)
    off = (end - region)[None, :] + jnp.cumsum(n_pad, axis=0) - n_pad
    blk = jnp.arange(n_blocks, dtype=jnp.int32) * block
    blk_group = jnp.minimum(jnp.sum(blk[:, None] >= end[None, :], axis=1), N_EGROUPS - 1).astype(jnp.int32)
    n_used = (end[-1] // block).astype(jnp.int32).reshape(1)
    flat = lambda a: a.astype(jnp.int32).reshape(-1)
    return flat(lo), flat(off), flat(n_pad), blk_group, n_used


def _sort_kernel(lo_ref, off_ref, np_ref, x_ref, ld_ref, init_ref, out_ref, cbuf, sems, *, ts, tps):
    del init_ref
    i = pl.program_id(0)
    slot = i % 2
    rows = ts + N_EGROUPS * SEG_ALIGN
    sizes = [ts >> k for k in range((ts // SEG_ALIGN).bit_length())]

    r = lax.broadcasted_iota(jnp.int32, (rows, ts), 0)
    for u in range(tps):
        perm = jnp.where(r == ld_ref[u], 1.0, 0.0).astype(BF16)
        cbuf[slot, u] = _dot(perm, x_ref[u * ts:(u + 1) * ts, :]).astype(BF16)

    def for_each(step, sl, act):
        for u in range(tps):
            for g in range(N_EGROUPS):
                at = (step * tps + u) * N_EGROUPS + g
                n_pad = np_ref[at]
                for k, size in enumerate(sizes):
                    done = n_pad & ~(2 * size - 1)
                    src = pl.multiple_of(lo_ref[at] + done, SEG_ALIGN)
                    dst = pl.multiple_of(off_ref[at] + done, SEG_ALIGN)
                    cp = pltpu.make_async_copy(cbuf.at[sl, u, pl.ds(src, size)], out_ref.at[pl.ds(dst, size)],
                                               sems.at[sl, u, g, k])
                    pl.when((n_pad & size) != 0)(functools.partial(act, cp))

    for_each(i, slot, lambda cp: cp.start())

    @pl.when(i > 0)
    def _():
        for_each(i - 1, 1 - slot, lambda cp: cp.wait())

    @pl.when(i == pl.num_programs(0) - 1)
    def _():
        for_each(i, slot, lambda cp: cp.wait())


def _experts_kernel(grp_ref, nb_ref, x_ref, wg_ref, wu_ref, wd_ref, o_ref):
    del grp_ref
    b = pl.program_id(0)
    d = o_ref.shape[1]

    @pl.when(b < nb_ref[0])
    def _():
        blk = x_ref[...]
        x = blk[:, :d]
        wp = blk[:, d:].astype(F32)
        acc = jnp.zeros(o_ref.shape, F32)
        for e in range(EXPERTS_PER_GROUP):
            w_e = wp[:, e:e + 1] + wp[:, EXPERTS_PER_GROUP + e:EXPERTS_PER_GROUP + e + 1]
            hid = _silu(_dot(x, wg_ref[e])) * _dot(x, wu_ref[e]) * w_e
            acc = acc + _dot(hid.astype(BF16), wd_ref[e])
        o_ref[...] = acc.astype(BF16)

    @pl.when(b >= nb_ref[0])
    def _():
        o_ref[...] = jnp.zeros(o_ref.shape, BF16)


def _combine_kernel(lo_ref, off_ref, np_ref, ld_ref, x1_ref, g2_ref, shf_ref, scf_ref, fnw_ref, src_ref, y_ref,
                    seg, sems, *, ts, tps, final_norm):
    i = pl.program_id(0)
    slot = i % 2
    sizes = [ts >> k for k in range((ts // SEG_ALIGN).bit_length())]

    def fetch(step, sl, act):
        for u in range(tps):
            for g in range(N_EGROUPS):
                at = (step * tps + u) * N_EGROUPS + g
                n_pad = np_ref[at]
                for k, size in enumerate(sizes):
                    done = n_pad & ~(2 * size - 1)
                    src = pl.multiple_of(off_ref[at] + done, SEG_ALIGN)
                    dst = pl.multiple_of(lo_ref[at] + done, SEG_ALIGN)
                    cp = pltpu.make_async_copy(src_ref.at[pl.ds(src, size)], seg.at[sl, u, pl.ds(dst, size)],
                                               sems.at[sl, u, g, k])
                    pl.when((n_pad & size) != 0)(functools.partial(act, cp))

    @pl.when(i == 0)
    def _():
        seg[...] = jnp.zeros(seg.shape, BF16)
        fetch(0, 0, lambda cp: cp.start())

    @pl.when(i + 1 < pl.num_programs(0))
    def _():
        fetch(i + 1, 1 - slot, lambda cp: cp.start())

    fetch(i, slot, lambda cp: cp.wait())

    r = lax.broadcasted_iota(jnp.int32, (seg.shape[2], ts), 0)
    for u in range(tps):
        rs = slice(u * ts, (u + 1) * ts)
        perm = jnp.where(r == ld_ref[u], 1.0, 0.0).astype(BF16)
        xo = x1_ref[rs, :] + g2_ref[0] * _dot_tn(perm, seg[slot, u])
        if final_norm:
            xo = _rms(xo, fnw_ref[...]) * (1.0 + scf_ref[0]) + shf_ref[0]
        y_ref[rs, :] = xo


def _moe_final(h2ext, ldest, cnt, wg, wu, wd, x1, mod3, modf3, boff, rows_per_batch, final_norm_w, final_norm):
    t, d = x1.shape
    payload = h2ext.shape[1]
    ts = min(SORT_TILE, rows_per_batch)
    tps = 2 if rows_per_batch % (2 * ts) == 0 else 1
    block = MOE_BLOCK if t >= 8 * MOE_BLOCK else MOE_BLOCK // 2
    n_tiles = t // ts
    ldest = ldest.reshape(n_tiles, 1, ts)
    n_blocks = -(-(t + n_tiles * N_EGROUPS * (SEG_ALIGN - 1) + N_EGROUPS * (block - 1)) // block)
    lo, off, n_pad, blk_group, n_used = _sort_plan(cnt, n_tiles, block, n_blocks)
    ld_spec = pl.BlockSpec((tps, 1, ts), lambda i, *_: (i, 0, 0))

    sorted_rows = pl.pallas_call(
        functools.partial(_sort_kernel, ts=ts, tps=tps),
        grid_spec=pltpu.PrefetchScalarGridSpec(
            num_scalar_prefetch=3,
            grid=(n_tiles // tps,),
            in_specs=[pl.BlockSpec((tps * ts, payload), lambda i, *_: (i, 0)), ld_spec,
                      pl.BlockSpec(memory_space=pl.ANY)],
            out_specs=pl.BlockSpec(memory_space=pl.ANY),
            scratch_shapes=[pltpu.VMEM((2, tps, ts + N_EGROUPS * SEG_ALIGN, payload), BF16),
                            pltpu.SemaphoreType.DMA((2, tps, N_EGROUPS, (ts // SEG_ALIGN).bit_length()))]),
        out_shape=jax.ShapeDtypeStruct((n_blocks * block, payload), BF16),
        input_output_aliases={5: 0},
        compiler_params=_params("arbitrary"),
        name="moe_sort",
    )(lo, off, n_pad, h2ext, ldest, jnp.zeros((n_blocks * block, payload), BF16))

    d_e = wg.shape[2]
    live = lambda b, nb: jnp.minimum(b, nb[0] - 1)
    out_sorted = pl.pallas_call(
        _experts_kernel,
        grid_spec=pltpu.PrefetchScalarGridSpec(
            num_scalar_prefetch=2,
            grid=(n_blocks,),
            in_specs=[pl.BlockSpec((block, payload), lambda b, grp, nb: (live(b, nb), 0)),
                      pl.BlockSpec((EXPERTS_PER_GROUP, d, d_e), lambda b, grp, nb: (grp[b], 0, 0)),
                      pl.BlockSpec((EXPERTS_PER_GROUP, d, d_e), lambda b, grp, nb: (grp[b], 0, 0)),
                      pl.BlockSpec((EXPERTS_PER_GROUP, d_e, d), lambda b, grp, nb: (grp[b], 0, 0))],
            out_specs=pl.BlockSpec((block, d), lambda b, grp, nb: (b, 0))),
        out_shape=jax.ShapeDtypeStruct((n_blocks * block, d), BF16),
        compiler_params=_params("arbitrary"),
        name="moe_experts",
    )(blk_group, n_used, sorted_rows, wg, wu, wd)

    per = rows_per_batch // (tps * ts)
    row = lambda col: pl.BlockSpec((1, 1, d), lambda i, *_, col=col: (i // per + boff, 0, col))
    return pl.pallas_call(
        functools.partial(_combine_kernel, ts=ts, tps=tps, final_norm=final_norm),
        grid_spec=pltpu.PrefetchScalarGridSpec(
            num_scalar_prefetch=3,
            grid=(n_tiles // tps,),
            in_specs=[ld_spec,
                      pl.BlockSpec((tps * ts, d), lambda i, *_: (i, 0)),
                      row(5), row(0), row(1),
                      pl.BlockSpec((1, d), lambda i, *_: (0, 0)),
                      pl.BlockSpec(memory_space=pl.ANY)],
            out_specs=pl.BlockSpec((tps * ts, d), lambda i, *_: (i, 0)),
            scratch_shapes=[pltpu.VMEM((2, tps, ts + N_EGROUPS * SEG_ALIGN, d), BF16),
                            pltpu.SemaphoreType.DMA((2, tps, N_EGROUPS, (ts // SEG_ALIGN).bit_length()))]),
        out_shape=jax.ShapeDtypeStruct((t, d), F32),
        compiler_params=_params("arbitrary"),
        name="moe_combine",
    )(lo, off, n_pad, ldest, x1, mod3, modf3, modf3, final_norm_w, out_sorted)


def _layer(x, mod3, modf3, boff, k_prev, v_prev, logf_prev, conv_prev, ssm_prev, p, final_norm_w, final_norm):
    b, l, d = x.shape
    z, xbc, q, k, v, k_b, v_b, sm, smt, conv_new = _inproj(
        x, mod3, boff, p["norm1_w"], p["wz"], p["wx"], p["wq"], p["wk"], p["wv"], p["ws"], p["bs"])
    n_heads_ssd = p["wz"].shape[1] // P_SSD
    y_ssd, ssm_new = _ssd(xbc, z, sm, smt, conv_prev, ssm_prev, p["conv_w"], p["conv_b"], p["a_log"],
                          p["d_skip"], p["ssd_norm_w"])
    n_heads = p["wq"].shape[1] // HD_ATT
    logf_t = smt[:, n_heads_ssd:n_heads_ssd + n_heads, :]
    if k_prev is None:
        p0, k_all, v_all, lf_all = 0, k_b, v_b, logf_t
    else:
        p0 = k_prev.shape[1]
        pad = (-(p0 + l)) % LANES
        zeros = lambda w: jnp.zeros((b, pad, w), BF16)
        k_all = jnp.concatenate([k_prev.astype(BF16), k_b, zeros(k_b.shape[2])], axis=1)
        v_all = jnp.concatenate([v_prev.astype(BF16), v_b, zeros(v_b.shape[2])], axis=1)
        lf_all = jnp.concatenate([jnp.swapaxes(logf_prev, 1, 2), logf_t, jnp.zeros((b, n_heads, pad), F32)], axis=2)
    o_att = _attention(q, k_all, v_all, _forget_cumsum(lf_all), p0)
    x1, h2ext, ldest, cnt = _outproj(y_ssd, o_att, x, mod3, boff, p["attn_norm_w"], p["wo_ssd"], p["wo_att"],
                                     p["norm2_w"], p["wr"], p["br"])
    y = _moe_final(h2ext, ldest, cnt, p["wg"], p["wu"], p["wd"], x1.reshape(b * l, d), mod3, modf3, boff, l,
                   final_norm_w, final_norm)
    return y.reshape(b, l, d), (k, v, jnp.swapaxes(logf_t, 1, 2), conv_new, ssm_new)


def kernel(x_prompt, x_sample, c_prompt, c_sample, cache_k, cache_v, cache_logf, state_conv, state_ssm, norm1_w, w_ada, b_ada, w_in, conv_w, conv_b, dt_bias, a_log, d_skip, ssd_norm_w, f_bias, attn_norm_w, w_out, norm2_w, w_rg, b_rg, w_re, b_re, w_gate, w_up, w_down, final_norm_w, w_ada_f, b_ada_f):
    depth = w_in.shape[0]
    bp, lp, d = x_prompt.shape
    bs = x_sample.shape[0]
    d_conv = conv_w.shape[2]
    d_ssd = ssd_norm_w.shape[1]
    d_att = attn_norm_w.shape[1]
    h_ssd = dt_bias.shape[1]
    h_att = f_bias.shape[1]
    assert h_ssd + h_att == SMALL_W and d_att // HD_ATT == h_att and d_ssd // P_SSD == h_ssd

    c_all = jnp.concatenate([c_prompt, c_sample], axis=0)
    modf3 = _modulation(c_all, w_ada_f, b_ada_f).reshape(bp + bs, 1, 2 * d)
    final_w = final_norm_w.reshape(1, d)

    i0 = d_ssd
    i1 = i0 + d_conv
    i2 = i1 + h_ssd
    i3 = i2 + d_att
    i4 = i3 + d_att
    i5 = i4 + d_att
    yp, ys = x_prompt, x_sample
    outs_p, outs_s = [], []
    for layer in range(depth):
        mod3 = _modulation(c_all, w_ada[layer], b_ada[layer]).reshape(bp + bs, 1, 6 * d)
        wi = w_in[layer]
        w_small = jnp.concatenate([wi[:, i1:i2], wi[:, i5:], jnp.zeros((d, LANES - SMALL_W), F32)], axis=1)
        b_small = jnp.concatenate([dt_bias[layer], f_bias[layer], jnp.zeros((LANES - SMALL_W,), F32)])
        wr = jnp.zeros((ROUTER_ROWS, d), F32)
        wr = wr.at[:N_EGROUPS].set(w_rg[layer].T)
        wr = wr.at[EXPERT_ROW0:EXPERT_ROW0 + N_EXPERTS].set(
            jnp.transpose(w_re[layer], (0, 2, 1)).reshape(N_EXPERTS, d))
        wr_hi, wr_lo = _split2(wr)
        br = jnp.zeros((ROUTER_ROWS,), F32)
        br = br.at[:N_EGROUPS].set(b_rg[layer])
        br = br.at[EXPERT_ROW0:EXPERT_ROW0 + N_EXPERTS].set(b_re[layer].reshape(N_EXPERTS))
        p = dict(
            norm1_w=norm1_w[layer].reshape(1, d),
            wz=wi[:, :i0].astype(BF16), wx=wi[:, i0:i1].astype(BF16),
            wq=(wi[:, i2:i3] * (LOG2E * HD_ATT ** -0.5)).astype(BF16),
            wk=wi[:, i3:i4].astype(BF16), wv=wi[:, i4:i5].astype(BF16),
            ws=w_small.astype(BF16), bs=b_small.reshape(1, LANES),
            conv_w=conv_w[layer], conv_b=conv_b[layer], a_log=a_log[layer], d_skip=d_skip[layer],
            ssd_norm_w=ssd_norm_w[layer], attn_norm_w=attn_norm_w[layer].reshape(1, d_att),
            wo_ssd=w_out[layer][:d_ssd].astype(BF16), wo_att=w_out[layer][d_ssd:].astype(BF16),
            norm2_w=norm2_w[layer].reshape(1, d),
            wr=jnp.concatenate([wr_hi, wr_lo], axis=0), br=br.reshape(ROUTER_ROWS, 1),
            wg=w_gate[layer].astype(BF16), wu=w_up[layer].astype(BF16), wd=w_down[layer].astype(BF16),
        )
        conv0 = jnp.zeros((bp, CONV_W - 1, d_conv), F32)
        ssm0 = jnp.zeros((bp, h_ssd, P_SSD, N_STATE), F32)
        last = layer == depth - 1
        yp, st_p = _layer(yp, mod3, modf3, 0, None, None, None, conv0, ssm0, p, final_w, last)
        ck = cache_k[layer].reshape(bs, -1, d_att)
        cv = cache_v[layer].reshape(bs, -1, d_att)
        ys, st_s = _layer(ys, mod3, modf3, bp, ck, cv, cache_logf[layer], state_conv[layer], state_ssm[layer], p,
                          final_w, last)
        outs_p.append(st_p)
        outs_s.append(st_s)

    def stack(outs, b, l):
        k = jnp.stack([o[0].reshape(b, l, h_att, HD_ATT) for o in outs])
        v = jnp.stack([o[1].reshape(b, l, h_att, HD_ATT) for o in outs])
        return (k, v, jnp.stack([o[2] for o in outs]), jnp.stack([o[3] for o in outs]),
                jnp.stack([o[4] for o in outs]))

    return (yp, ys) + stack(outs_p, bp, lp) + stack(outs_s, bs, x_sample.shape[1])
```

```python
import functools

import jax
import jax.numpy as jnp
import numpy as np
from jax import lax
from jax.experimental import pallas as pl
from jax.experimental.pallas import tpu as pltpu

F32 = jnp.float32
BF16 = jnp.bfloat16

P_SSD = 64
N_STATE = 64
G_SSD = 2
CONV_W = 4
HD_ATT = 64
N_EGROUPS = 4
EXPERTS_PER_GROUP = 4
N_EXPERTS = N_EGROUPS * EXPERTS_PER_GROUP
EPS = 1e-6
NEG_BIG = -1e30

LANES = 128
SEG_ALIGN = 16
SORT_TILE = 256
MOE_BLOCK = 512
SMALL_W = 16
VMEM_LIMIT = 56 * 1024 * 1024


def _params(*sem):
    return pltpu.CompilerParams(dimension_semantics=sem, vmem_limit_bytes=VMEM_LIMIT)


def _split2(x):
    hi = x.astype(BF16)
    lo = (x - hi.astype(F32)).astype(BF16)
    return hi, lo


def _split3(x):
    hi = x.astype(BF16)
    r = x - hi.astype(F32)
    mid = r.astype(BF16)
    lo = (r - mid.astype(F32)).astype(BF16)
    return hi, mid, lo


def _dot(a, b):
    return jnp.dot(a, b, preferred_element_type=F32)


def _dot_nt(a, b):
    return lax.dot_general(a, b, (((1,), (1,)), ((), ())), preferred_element_type=F32)


def _dot_tn(a, b):
    return lax.dot_general(a, b, (((0,), (0,)), ((), ())), preferred_element_type=F32)


def _silu(x):
    h = 0.5 * x
    return h + h * jnp.tanh(h)


def _rms(x, w):
    return x * lax.rsqrt(jnp.mean(x * x, axis=-1, keepdims=True) + EPS) * w


def _pick_tile(n, candidates):
    for c in candidates:
        if n % c == 0:
            return c
    return n


def _mod_kernel(c_ref, w_ref, b_ref, o_ref):
    a = _silu(c_ref[...])
    a_hi, a_lo = _split2(a)
    w_hi, w_lo = _split2(w_ref[...])
    o_ref[...] = _dot(a_hi, w_hi) + _dot(a_lo, w_hi) + _dot(a_hi, w_lo) + b_ref[...]


def _modulation(c, w, b):
    m, d = c.shape
    n = w.shape[1]
    tn = _pick_tile(n, (1024, 512, 256, 128))
    return pl.pallas_call(
        _mod_kernel,
        grid=(n // tn,),
        in_specs=[pl.BlockSpec((m, d), lambda j: (0, 0)),
                  pl.BlockSpec((d, tn), lambda j: (0, j)),
                  pl.BlockSpec((1, tn), lambda j: (0, j))],
        out_specs=pl.BlockSpec((m, tn), lambda j: (0, j)),
        out_shape=jax.ShapeDtypeStruct((m, n), F32),
        compiler_params=_params("parallel"),
        name="adaln_mod",
    )(c, w, b.reshape(1, n))


def _inproj_kernel(x_ref, sh_ref, sc_ref, nw_ref, wz_ref, wx_ref, wq_ref, wk_ref, wv_ref, ws_ref, bs_ref,
                   z_ref, xbc_ref, q_ref, k_ref, v_ref, kb_ref, vb_ref, sm_ref, smt_ref, tail_ref):
    l = pl.program_id(1)
    x = x_ref[0]
    h = (_rms(x, nw_ref[...]) * (1.0 + sc_ref[0]) + sh_ref[0]).astype(BF16)
    z_ref[0] = _dot(h, wz_ref[...]).astype(BF16)
    xbc = _dot(h, wx_ref[...])
    xbc_ref[0] = xbc.astype(BF16)
    q_ref[0] = _dot(h, wq_ref[...]).astype(BF16)
    k = _dot(h, wk_ref[...])
    k_ref[0] = k
    kb_ref[0] = k.astype(BF16)
    v = _dot(h, wv_ref[...])
    v_ref[0] = v
    vb_ref[0] = v.astype(BF16)
    s = _dot(h, ws_ref[...]) + bs_ref[...]
    t = jnp.log(1.0 + jnp.exp(-jnp.abs(s)))
    lane = lax.broadcasted_iota(jnp.int32, s.shape, 1)
    s = jnp.where(lane < SMALL_W // 2, jnp.maximum(s, 0.0) + t, jnp.minimum(s, 0.0) - t)
    sm_ref[0] = s[:, :SMALL_W]
    smt_ref[0] = s.T[:SMALL_W, :]

    @pl.when(l == pl.num_programs(1) - 1)
    def _():
        tl = xbc.shape[0]
        tail_ref[0] = xbc[tl - (CONV_W - 1):, :]


def _inproj(x, mod3, boff, norm_w, wz, wx, wq, wk, wv, ws, bs):
    b, l, d = x.shape
    tl = _pick_tile(l, (1024, 512, 256, 128, 64))
    nl = l // tl
    d_ssd, d_conv, d_att = wz.shape[1], wx.shape[1], wq.shape[1]
    row = lambda col: pl.BlockSpec((1, 1, d), lambda i, j, col=col: (i + boff, 0, col))
    full = lambda a: pl.BlockSpec(a.shape, lambda i, j: (0,) * a.ndim)
    tok = lambda w: pl.BlockSpec((1, tl, w), lambda i, j: (i, j, 0))
    out_shape = [
        jax.ShapeDtypeStruct((b, l, d_ssd), BF16),
        jax.ShapeDtypeStruct((b, l, d_conv), BF16),
        jax.ShapeDtypeStruct((b, l, d_att), BF16),
        jax.ShapeDtypeStruct((b, l, d_att), F32),
        jax.ShapeDtypeStruct((b, l, d_att), F32),
        jax.ShapeDtypeStruct((b, l, d_att), BF16),
        jax.ShapeDtypeStruct((b, l, d_att), BF16),
        jax.ShapeDtypeStruct((b, l, SMALL_W), F32),
        jax.ShapeDtypeStruct((b, SMALL_W, l), F32),
        jax.ShapeDtypeStruct((b, CONV_W - 1, d_conv), F32),
    ]
    out_specs = [tok(d_ssd), tok(d_conv), tok(d_att), tok(d_att), tok(d_att), tok(d_att), tok(d_att),
                 tok(SMALL_W), pl.BlockSpec((1, SMALL_W, tl), lambda i, j: (i, 0, j)),
                 pl.BlockSpec((1, CONV_W - 1, d_conv), lambda i, j: (i, 0, 0))]
    return pl.pallas_call(
        _inproj_kernel,
        grid=(b, nl),
        in_specs=[tok(d), row(0), row(1), full(norm_w), full(wz), full(wx), full(wq), full(wk), full(wv),
                  full(ws), full(bs)],
        out_specs=out_specs,
        out_shape=out_shape,
        compiler_params=_params("parallel", "arbitrary"),
        name="inproj",
    )(x, mod3, mod3, norm_w, wz, wx, wq, wk, wv, ws, bs)


LOG2E = 1.4426950408889634
AUG = 3
SPLIT_ROWS = 128


PIECE_COLS = 32


def _aug_select(n_heads):
    pair = 2 * HD_ATT
    sq = np.zeros((n_heads // 2, PIECE_COLS, pair), np.float32)
    sk = np.zeros((n_heads // 2, PIECE_COLS, pair), np.float32)
    one = AUG * n_heads
    for h in range(n_heads):
        slot = ((h ^ 1) % 2) * HD_ATT
        for c in range(AUG):
            sq[h // 2, c * n_heads + h, slot + c] = 1.0
            sq[h // 2, one, slot + AUG + c] = 1.0
            sk[h // 2, one, slot + c] = 1.0
            sk[h // 2, c * n_heads + h, slot + AUG + c] = -1.0
    return sq, sk


def _cumsum_kernel(x_ref, eye_ref, o_ref, carry):
    @pl.when(pl.program_id(1) == 0)
    def _():
        carry[...] = jnp.zeros_like(carry)

    x = x_ref[0]
    h, tc = x.shape
    r = lax.broadcasted_iota(jnp.int32, (tc, tc), 0)
    c = lax.broadcasted_iota(jnp.int32, (tc, tc), 1)
    upper = jnp.where(r <= c, 1.0, 0.0).astype(BF16)
    stack = lambda ps: jnp.concatenate([p.astype(F32) for p in ps], axis=0)
    parts = _dot(stack(_split3(x)).astype(BF16), upper)
    cs = parts[:h] + parts[h:2 * h] + parts[2 * h:] + carry[...]
    carry[...] = cs[:, tc - 1:]
    rows = lax.broadcasted_iota(jnp.int32, (PIECE_COLS - AUG * h, tc), 0)
    pieces = jnp.concatenate([stack(_split3(cs * LOG2E)), jnp.where(rows == 0, 1.0, 0.0)], axis=0)
    o_ref[0] = _dot_tn(pieces.astype(BF16), eye_ref[...]).astype(BF16)


def _forget_cumsum(logf_t):
    b, h, lk = logf_t.shape
    assert AUG * h < PIECE_COLS
    tc = _pick_tile(lk, (1024, 512, 384, 256, 128))
    eye = jnp.asarray(np.eye(PIECE_COLS), BF16)
    return pl.pallas_call(
        _cumsum_kernel,
        grid=(b, lk // tc),
        in_specs=[pl.BlockSpec((1, h, tc), lambda i, j: (i, 0, j)),
                  pl.BlockSpec((PIECE_COLS, PIECE_COLS), lambda i, j: (0, 0))],
        out_specs=pl.BlockSpec((1, tc, PIECE_COLS), lambda i, j: (i, j, 0)),
        out_shape=jax.ShapeDtypeStruct((b, lk, PIECE_COLS), BF16),
        scratch_shapes=[pltpu.VMEM((h, 1), F32)],
        compiler_params=_params("parallel", "arbitrary"),
        name="forget_cumsum",
    )(logf_t, eye)


def _ssd_spread(n_heads):
    e = np.zeros((2 * AUG * n_heads, 2 * n_heads * P_SSD), np.float32)
    for v in range(2):
        for c in range(AUG):
            for h in range(n_heads):
                col = v * n_heads * P_SSD + h * P_SSD
                e[(v * AUG + c) * n_heads + h, col:col + P_SSD] = 1.0
    return e


def _ssd_kernel(xbc_ref, z_ref, sm_ref, smt_ref, cprev_ref, sprev_ref, cw_ref, cb_ref, arow_ref, acol_ref,
                dskip_ref, nw_ref, spread_ref, y_ref, snew_ref, hist, state):
    l = pl.program_id(1)
    q = xbc_ref.shape[1]
    d_ssd = z_ref.shape[2]
    n_heads = d_ssd // P_SSD
    pair = 2 * P_SSD
    hist_rows = hist.shape[0]

    @pl.when(l == 0)
    def _():
        hist[...] = jnp.zeros(hist.shape, F32)
        hist[hist_rows - (CONV_W - 1):, :] = cprev_ref[0]
        state[...] = sprev_ref[0]

    xb = xbc_ref[0]
    r3 = lax.broadcasted_iota(jnp.int32, ((CONV_W - 1) * q, q), 0)
    c3 = lax.broadcasted_iota(jnp.int32, ((CONV_W - 1) * q, q), 1)
    shift = jnp.where(r3 % q - c3 == r3 // q + 1, 1.0, 0.0).astype(BF16)
    shifted = _dot(shift, xb)
    cw = cw_ref[...]
    conv = cb_ref[...] + xb.astype(F32) * cw[CONV_W - 1:CONV_W, :]
    for k in range(CONV_W - 1):
        conv = conv + shifted[k * q:(k + 1) * q, :] * cw[CONV_W - 2 - k:CONV_W - 1 - k, :]
    hrow = lambda j: hist[hist_rows - j:hist_rows - j + 1, :]
    frow = lax.broadcasted_iota(jnp.int32, (hist_rows, 1), 0)
    fix = jnp.zeros((hist_rows, conv.shape[1]), F32)
    for t in range(CONV_W - 1):
        acc = 0.0
        for j in range(1, CONV_W - t):
            acc = acc + hrow(j) * cw[CONV_W - 1 - t - j:CONV_W - t - j, :]
        fix = jnp.where(frow == t, acc, fix)
    conv = jnp.concatenate([conv[:hist_rows] + fix, conv[hist_rows:]], axis=0)
    hist[...] = xb[q - hist_rows:, :].astype(F32)
    act = _silu(conv)

    dt = sm_ref[0][:, :n_heads]
    dt_t = smt_ref[0][:n_heads, :]
    a_row = -jnp.exp(arow_ref[...]) * LOG2E
    a_col = -jnp.exp(acol_ref[...]) * LOG2E
    r = lax.broadcasted_iota(jnp.int32, (q, q), 0)
    c = lax.broadcasted_iota(jnp.int32, (q, q), 1)
    causal = r >= c
    lower = jnp.where(causal, 1.0, 0.0).astype(BF16)
    upper = jnp.where(r <= c, 1.0, 0.0).astype(BF16)
    h0, h1, h2 = _split3(dt * a_row)
    acum = _dot(lower, h0) + _dot(lower, h1) + _dot(lower, h2)
    t0, t1, t2 = _split3(dt_t * a_col)
    acum_t = _dot(t0, upper) + _dot(t1, upper) + _dot(t2, upper)
    pieces = [p.astype(F32) for p in _split3(acum) + _split3(dt)]
    wide = _dot(jnp.concatenate(pieces, axis=1).astype(BF16), spread_ref[...])
    acum_x = wide[:, :d_ssd]
    dt_x = wide[:, d_ssd:]
    a_last = acum[q - 1:q, :]
    e_all = jnp.exp2(a_last)
    e_cum_x = jnp.exp2(acum_x)
    e_end_x = jnp.exp2(acum_x[q - 1:q, :] - acum_x)

    xs = act[:, :d_ssd]
    bm = act[:, d_ssd:d_ssd + G_SSD * N_STATE]
    cm = act[:, d_ssd + G_SSD * N_STATE:]
    bm_b = bm.astype(BF16)
    cm_b = cm.astype(BF16)
    xdt = xs * dt_x
    xdt_b = xdt.astype(BF16)
    xe_b = (xdt * e_end_x).astype(BF16)
    lane = lax.broadcasted_iota(jnp.int32, (1, pair), 1)
    first = lane < P_SSD
    srow = lax.broadcasted_iota(jnp.int32, (pair, 1), 0)
    ys = []
    for p in range(n_heads // 2):
        g = (2 * p * G_SSD) // n_heads
        in_group = first if g == 0 else jnp.logical_not(first)
        sl = slice(p * pair, (p + 1) * pair)
        if (2 * p) % (n_heads // G_SSD) == 0:
            cb = _dot_nt(jnp.where(in_group, cm_b, jnp.zeros_like(cm_b)), bm_b)
        ms = []
        for hh in range(2):
            h = 2 * p + hh
            seg = acum[:, h:h + 1] - acum_t[h:h + 1, :]
            ms.append((cb * jnp.exp2(jnp.where(causal, seg, NEG_BIG))).astype(BF16))
        xp = xdt_b[:, sl]
        zero = jnp.zeros_like(xp)
        y = _dot(jnp.concatenate(ms, axis=1),
                 jnp.concatenate([jnp.where(first, xp, zero), jnp.where(first, zero, xp)], axis=0))
        s_in = state[sl, :]
        y = y + _dot_nt(cm_b, s_in.astype(BF16)) * e_cum_x[:, sl]
        upd = _dot_tn(xe_b[:, sl], bm_b)
        keep = jnp.where(srow < P_SSD, e_all[:, 2 * p:2 * p + 1], e_all[:, 2 * p + 1:2 * p + 2])
        state[sl, :] = s_in * keep + jnp.where(in_group, upd, 0.0)
        ys.append(y)
    y_all = jnp.concatenate(ys, axis=1) + dskip_ref[...] * xs
    yg = y_all * _silu(z_ref[0].astype(F32))
    y_ref[0] = _rms(yg, nw_ref[...]).astype(BF16)
    snew_ref[0] = state[...]


def _ssd(xbc, z, sm, smt, conv_prev, ssm_prev, conv_w, conv_b, a_log, d_skip, norm_w):
    b, l, d_conv = xbc.shape
    d_ssd = z.shape[2]
    n_heads = d_ssd // P_SSD
    hg = n_heads // G_SSD
    q = _pick_tile(l, (256, 128, 64))
    tok = lambda w: pl.BlockSpec((1, q, w), lambda i, j: (i, j, 0))
    full = lambda a: pl.BlockSpec(a.shape, lambda i, j: (0,) * a.ndim)
    a_row = a_log.reshape(1, n_heads)
    a_col = a_log.reshape(n_heads, 1)
    conv_b = conv_b.reshape(1, d_conv)
    d_skip = jnp.repeat(d_skip, P_SSD).reshape(1, d_ssd)
    norm_w = norm_w.reshape(1, d_ssd)
    spread = jnp.asarray(_ssd_spread(n_heads), BF16)
    s4 = ssm_prev.reshape(b, G_SSD, hg * P_SSD, N_STATE)
    s_in = jnp.concatenate([jnp.pad(s4[:, g], ((0, 0), (0, 0), (g * N_STATE, (G_SSD - 1 - g) * N_STATE)))
                            for g in range(G_SSD)], axis=1)
    state_spec = pl.BlockSpec((1, n_heads * P_SSD, G_SSD * N_STATE), lambda i, j: (i, 0, 0))
    y, s_out = pl.pallas_call(
        _ssd_kernel,
        grid=(b, l // q),
        in_specs=[tok(d_conv), tok(d_ssd), tok(SMALL_W), pl.BlockSpec((1, SMALL_W, q), lambda i, j: (i, 0, j)),
                  pl.BlockSpec((1, CONV_W - 1, d_conv), lambda i, j: (i, 0, 0)), state_spec,
                  full(conv_w), full(conv_b), full(a_row), full(a_col), full(d_skip), full(norm_w), full(spread)],
        out_specs=[tok(d_ssd), state_spec],
        out_shape=[jax.ShapeDtypeStruct((b, l, d_ssd), BF16),
                   jax.ShapeDtypeStruct((b, n_heads * P_SSD, G_SSD * N_STATE), F32)],
        scratch_shapes=[pltpu.VMEM((8, d_conv), F32),
                        pltpu.VMEM((n_heads * P_SSD, G_SSD * N_STATE), F32)],
        compiler_params=_params("parallel", "arbitrary"),
        name="ssd",
    )(xbc, z, sm, smt, conv_prev, s_in, conv_w, conv_b, a_row, a_col, d_skip, norm_w, spread)
    s_out = s_out.reshape(b, G_SSD, hg * P_SSD, G_SSD * N_STATE)
    s_new = jnp.concatenate([s_out[:, g, :, g * N_STATE:(g + 1) * N_STATE] for g in range(G_SSD)], axis=1)
    return y, s_new.reshape(b, n_heads, P_SSD, N_STATE)


def _attn_kernel(q_ref, pq_ref, k_ref, pk_ref, v_ref, sq_ref, sk_ref, o_ref, kk_s, m_s, acc_s, *, p0, tq, tk):
    i = pl.program_id(2)
    lane = lax.broadcasted_iota(jnp.int32, (1, 2 * HD_ATT), 1)
    first = lane < HD_ATT
    own = (first, jnp.logical_not(first))
    sum_lane = (HD_ATT, 0)

    @pl.when(i == 0)
    def _():
        k = k_ref[0]
        ka = _dot(pk_ref[0], sk_ref[0]).astype(BF16)
        kk_s[0] = jnp.where(first, k, ka)
        kk_s[1] = jnp.where(first, ka, k)

    q = q_ref[0]
    qa = _dot(pq_ref[0], sq_ref[0]).astype(BF16)
    qq = (jnp.where(first, q, qa), jnp.where(first, qa, q))
    m_s[...] = jnp.full(m_s.shape, NEG_BIG, F32)
    acc_s[...] = jnp.zeros(acc_s.shape, F32)
    nc = tk // LANES
    n_split = 2 if tq % (2 * SPLIT_ROWS) == 0 else 1
    rows = tq // n_split
    aligned = tq == tk and p0 % tk == 0

    def step(j, masked):
        off = pl.multiple_of(j * tk, tk)
        v = v_ref[0, pl.ds(off, tk), :]
        vv = [jnp.where(own[hh], v, jnp.where(lane == sum_lane[hh], 1.0, 0.0).astype(BF16)) for hh in range(2)]
        width = [(r + 1) * rows if masked and aligned else tk for r in range(n_split)]
        logit = [[_dot_nt(qq[hh][r * rows:(r + 1) * rows], kk_s[hh, pl.ds(off, width[r]), :]) for hh in range(2)]
                 for r in range(n_split)]
        for r in range(n_split):
            rs = slice(r * rows, (r + 1) * rows)
            tiles = range(width[r] // LANES)
            if masked and aligned:
                row = lax.broadcasted_iota(jnp.int32, (rows, LANES), 0)
                col = lax.broadcasted_iota(jnp.int32, (rows, LANES), 1)
                seen = [None if (c + 1) * LANES <= r * rows + 1 else col + (c * LANES - r * rows) <= row for c in tiles]
            elif masked:
                q_pos = p0 + i * tq + r * rows + lax.broadcasted_iota(jnp.int32, (rows, LANES), 0)
                k_pos = j * tk + lax.broadcasted_iota(jnp.int32, (rows, LANES), 1)
                seen = [k_pos + c * LANES <= q_pos for c in tiles]
            else:
                seen = [None for _ in tiles]
            for hh in range(2):
                s = logit[r][hh]
                cols = [s[:, c * LANES:(c + 1) * LANES] for c in tiles]
                cols = [x if m is None else jnp.where(m, x, NEG_BIG) for x, m in zip(cols, seen)]
                m_cur = functools.reduce(jnp.maximum, cols)
                m_prev = m_s[hh, rs, :]
                m_new = jnp.maximum(m_prev, jnp.max(m_cur, axis=1, keepdims=True))
                alpha = jnp.exp2(m_prev - m_new)
                p = jnp.concatenate([jnp.exp2((col - m_new).astype(BF16)) for col in cols], axis=1)
                m_s[hh, rs, :] = m_new
                acc_s[hh, rs, :] = alpha * acc_s[hh, rs, :] + _dot(p, vv[hh][:width[r]])

    n_full = (p0 + i * tq + 1) // tk
    n_vis = (p0 + i * tq + tq - 1) // tk + 1

    def full_pair(jj, carry):
        step(2 * jj, False)
        step(2 * jj + 1, False)
        return carry

    def masked_body(j, carry):
        step(j, True)
        return carry

    lax.fori_loop(0, n_full // 2, full_pair, 0)
    pl.when(n_full % 2 == 1)(lambda: step(n_full - 1, False))
    lax.fori_loop(n_full, n_vis, masked_body, 0)
    a0 = acc_s[0]
    a1 = acc_s[1]
    o = jnp.where(first, a0 / a0[:, sum_lane[0]:sum_lane[0] + 1], a1 / a1[:, sum_lane[1]:sum_lane[1] + 1])
    o_ref[0] = o.astype(o_ref.dtype)


def _attention(q, k, v, pieces, p0):
    b, lq, d_att = q.shape
    lk = k.shape[1]
    n_heads = d_att // HD_ATT
    tq = _pick_tile(lq, (512, 256, 128, 64))
    tk = lk if lk <= 1536 else _pick_tile(lk, (512, 256, 128))
    assert p0 % tq == 0 and lk >= p0 + lq
    pair = 2 * HD_ATT
    sq, sk = (jnp.asarray(a, BF16) for a in _aug_select(n_heads))
    q_spec = pl.BlockSpec((1, tq, pair), lambda bi, hp, i: (bi, i, hp))
    kv_spec = pl.BlockSpec((1, lk, pair), lambda bi, hp, i: (bi, 0, hp))
    sel_spec = pl.BlockSpec((1, PIECE_COLS, pair), lambda bi, hp, i: (hp, 0, 0))
    return pl.pallas_call(
        functools.partial(_attn_kernel, p0=p0, tq=tq, tk=tk),
        grid=(b, n_heads // 2, lq // tq),
        in_specs=[q_spec, pl.BlockSpec((1, tq, PIECE_COLS), lambda bi, hp, i: (bi, i + p0 // tq, 0)),
                  kv_spec, pl.BlockSpec((1, lk, PIECE_COLS), lambda bi, hp, i: (bi, 0, 0)), kv_spec,
                  sel_spec, sel_spec],
        out_specs=q_spec,
        out_shape=jax.ShapeDtypeStruct((b, lq, d_att), BF16),
        scratch_shapes=[pltpu.VMEM((2, lk, pair), BF16), pltpu.VMEM((2, tq, LANES), F32),
                        pltpu.VMEM((2, tq, pair), F32)],
        compiler_params=_params("parallel", "parallel", "arbitrary"),
        name="fox_attention",
    )(q, pieces, k, pieces, v, sq, sk)


ROUTER_ROWS = 32
EXPERT_ROW0 = 8


def _outproj_kernel(ys_ref, oa_ref, x_ref, g1_ref, sh_ref, sc_ref, anw_ref, wos_ref, woa_ref, n2w_ref, wr_ref,
                    br_ref, x1_ref, h2_ref, ld_ref, cnt_ref, wt_s, g_s):
    ya = _rms(oa_ref[0].astype(F32), anw_ref[...]).astype(BF16)
    m = _dot(ys_ref[0], wos_ref[...]) + _dot(ya, woa_ref[...])
    x1 = x_ref[0] + g1_ref[0] * m
    x1_ref[0] = x1
    h2 = _rms(x1, n2w_ref[...]) * (1.0 + sc_ref[0]) + sh_ref[0]
    h_hi, h_lo = _split2(h2)
    wr = wr_ref[...]
    p1 = _dot_nt(wr, h_hi)
    p2 = _dot_nt(wr[:ROUTER_ROWS], h_lo)
    logit = p1[:ROUTER_ROWS] + p1[ROUTER_ROWS:] + p2 + br_ref[...]

    lg = [logit[g:g + 1, :] for g in range(N_EGROUPS)]
    gmax = jnp.maximum(jnp.maximum(lg[0], lg[1]), jnp.maximum(lg[2], lg[3]))
    denom = sum(jnp.exp(x - gmax) for x in lg)
    p_sel = 1.0 / denom
    is_g = []
    taken = jnp.zeros_like(gmax) > 1.0
    for g in range(N_EGROUPS):
        hit = (lg[g] == gmax) & jnp.logical_not(taken)
        is_g.append(hit)
        taken = taken | hit
    le = []
    for e in range(EXPERTS_PER_GROUP):
        v = logit[EXPERT_ROW0 + 3 * EXPERTS_PER_GROUP + e:EXPERT_ROW0 + 3 * EXPERTS_PER_GROUP + e + 1, :]
        for g in range(N_EGROUPS - 2, -1, -1):
            r0 = EXPERT_ROW0 + g * EXPERTS_PER_GROUP + e
            v = jnp.where(is_g[g], logit[r0:r0 + 1, :], v)
        le.append(v)
    m1 = jnp.maximum(jnp.maximum(le[0], le[1]), jnp.maximum(le[2], le[3]))
    first = []
    taken = jnp.zeros_like(m1) > 1.0
    for e in range(EXPERTS_PER_GROUP):
        hit = (le[e] == m1) & jnp.logical_not(taken)
        first.append(hit)
        taken = taken | hit
    rest = [jnp.where(first[e], -jnp.inf, le[e]) for e in range(EXPERTS_PER_GROUP)]
    m2 = jnp.maximum(jnp.maximum(rest[0], rest[1]), jnp.maximum(rest[2], rest[3]))
    second = []
    taken = jnp.zeros_like(m1) > 1.0
    for e in range(EXPERTS_PER_GROUP):
        hit = (rest[e] == m2) & jnp.logical_not(taken)
        second.append(hit)
        taken = taken | hit
    e2 = jnp.exp(m2 - m1)
    w_a = p_sel / (1.0 + e2)
    w_b = w_a * e2
    d = x1.shape[1]
    wt_s[...] = jnp.zeros(wt_s.shape, F32)
    for e in range(EXPERTS_PER_GROUP):
        w = jnp.where(first[e], w_a, jnp.where(second[e], w_b, 0.0))
        w_hi = w.astype(BF16).astype(F32)
        wt_s[e:e + 1, :] = w_hi
        wt_s[EXPERTS_PER_GROUP + e:EXPERTS_PER_GROUP + e + 1, :] = w - w_hi
    h2_ref[:, :d] = h_hi
    h2_ref[:, d:] = wt_s[...].T.astype(BF16)

    tl = logit.shape[1]
    ts = min(SORT_TILE, tl)
    g_s[...] = jnp.zeros(g_s.shape, F32)
    for g in range(N_EGROUPS):
        g_s[g:g + 1, :] = jnp.where(is_g[g], 1.0, 0.0)
    r = lax.broadcasted_iota(jnp.int32, (tl, tl), 0)
    c = lax.broadcasted_iota(jnp.int32, (tl, tl), 1)
    same_tile = (r // ts) == (c // ts)
    upper = jnp.where((r <= c) & same_tile, 1.0, 0.0).astype(BF16)
    cum = _dot(g_s[...].astype(BF16), upper)
    lane = lax.broadcasted_iota(jnp.int32, (1, tl), 1)
    crow = lax.broadcasted_iota(jnp.int32, (8, LANES), 0)
    clane = lax.broadcasted_iota(jnp.int32, (8, LANES), 1)
    ldest = -1.0
    cnt = jnp.zeros((8, LANES), F32)
    for g in range(N_EGROUPS):
        ldest = ldest + jnp.where(is_g[g], cum[g:g + 1, :], 0.0)
    lo = [0.0] * (tl // ts)
    for g in range(N_EGROUPS):
        lo_row = jnp.zeros((1, tl), F32)
        for sub in range(tl // ts):
            n = cum[g:g + 1, (sub + 1) * ts - 1:(sub + 1) * ts]
            n_pad = jnp.ceil(n / SEG_ALIGN) * SEG_ALIGN
            lo_row = jnp.where(lane // ts == sub, lo[sub], lo_row)
            cnt = jnp.where((crow == sub) & (clane == g), n_pad, cnt)
            lo[sub] = lo[sub] + n_pad
        ldest = ldest + jnp.where(is_g[g], lo_row, 0.0)
    ld_ref[0] = ldest.astype(jnp.int32)
    cnt_ref[0] = cnt.astype(jnp.int32)


def _outproj(y_ssd, o_att, x, mod3, boff, attn_norm_w, wo_ssd, wo_att, norm2_w, wr, br):
    b, l, d = x.shape
    tl = _pick_tile(l, (512, 256, 128, 64))
    nl = l // tl
    d_ssd, d_att = y_ssd.shape[2], o_att.shape[2]
    row = lambda col: pl.BlockSpec((1, 1, d), lambda i, j, col=col: (i + boff, 0, col))
    full = lambda a: pl.BlockSpec(a.shape, lambda i, j: (0,) * a.ndim)
    tok = lambda w: pl.BlockSpec((1, tl, w), lambda i, j: (i, j, 0))
    return pl.pallas_call(
        _outproj_kernel,
        grid=(b, nl),
        in_specs=[tok(d_ssd), tok(d_att), tok(d), row(2), row(3), row(4), full(attn_norm_w), full(wo_ssd),
                  full(wo_att), full(norm2_w), full(wr), full(br)],
        out_specs=[tok(d), pl.BlockSpec((tl, d + LANES), lambda i, j: (i * nl + j, 0)),
                   pl.BlockSpec((1, 1, tl), lambda i, j: (i * nl + j, 0, 0)),
                   pl.BlockSpec((1, 8, LANES), lambda i, j: (i * nl + j, 0, 0))],
        out_shape=[jax.ShapeDtypeStruct((b, l, d), F32),
                   jax.ShapeDtypeStruct((b * l, d + LANES), BF16),
                   jax.ShapeDtypeStruct((b * nl, 1, tl), jnp.int32),
                   jax.ShapeDtypeStruct((b * nl, 8, LANES), jnp.int32)],
        scratch_shapes=[pltpu.VMEM((LANES, tl), F32), pltpu.VMEM((8, tl), F32)],
        compiler_params=_params("parallel", "parallel"),
        name="outproj_router",
    )(y_ssd, o_att, x, mod3, mod3, mod3, attn_norm_w, wo_ssd, wo_att, norm2_w, wr, br)


def _sort_plan(cnt, n_tiles, block, n_blocks):
    ns = n_tiles // cnt.shape[0]
    n_pad = cnt[:, :ns, :N_EGROUPS].reshape(n_tiles, N_EGROUPS)
    lo = jnp.cumsum(n_pad, axis=1) - n_pad
    region = (jnp.sum(n_pad, axis=0) + block - 1) // block * block
    end = jnp.cumsum(region)
    off = (end - region)[None, :] + jnp.cumsum(n_pad, axis=0) - n_pad
    blk = jnp.arange(n_blocks, dtype=jnp.int32) * block
    blk_group = jnp.minimum(jnp.sum(blk[:, None] >= end[None, :], axis=1), N_EGROUPS - 1).astype(jnp.int32)
    n_used = (end[-1] // block).astype(jnp.int32).reshape(1)
    flat = lambda a: a.astype(jnp.int32).reshape(-1)
    return flat(lo), flat(off), flat(n_pad), blk_group, n_used


def _sort_kernel(lo_ref, off_ref, np_ref, x_ref, ld_ref, init_ref, out_ref, cbuf, sems, *, ts, tps):
    del init_ref
    i = pl.program_id(0)
    slot = i % 2
    rows = ts + N_EGROUPS * SEG_ALIGN
    sizes = [ts >> k for k in range((ts // SEG_ALIGN).bit_length())]

    r = lax.broadcasted_iota(jnp.int32, (rows, ts), 0)
    for u in range(tps):
        perm = jnp.where(r == ld_ref[u], 1.0, 0.0).astype(BF16)
        cbuf[slot, u] = _dot(perm, x_ref[u * ts:(u + 1) * ts, :]).astype(BF16)

    def for_each(step, sl, act):
        for u in range(tps):
            for g in range(N_EGROUPS):
                at = (step * tps + u) * N_EGROUPS + g
                n_pad = np_ref[at]
                for k, size in enumerate(sizes):
                    done = n_pad & ~(2 * size - 1)
                    src = pl.multiple_of(lo_ref[at] + done, SEG_ALIGN)
                    dst = pl.multiple_of(off_ref[at] + done, SEG_ALIGN)
                    cp = pltpu.make_async_copy(cbuf.at[sl, u, pl.ds(src, size)], out_ref.at[pl.ds(dst, size)],
                                               sems.at[sl, u, g, k])
                    pl.when((n_pad & size) != 0)(functools.partial(act, cp))

    for_each(i, slot, lambda cp: cp.start())

    @pl.when(i > 0)
    def _():
        for_each(i - 1, 1 - slot, lambda cp: cp.wait())

    @pl.when(i == pl.num_programs(0) - 1)
    def _():
        for_each(i, slot, lambda cp: cp.wait())


def _experts_kernel(grp_ref, nb_ref, x_ref, wg_ref, wu_ref, wd_ref, o_ref):
    del grp_ref
    b = pl.program_id(0)
    d = o_ref.shape[1]

    @pl.when(b < nb_ref[0])
    def _():
        blk = x_ref[...]
        x = blk[:, :d]
        wp = blk[:, d:].astype(F32)
        acc = jnp.zeros(o_ref.shape, F32)
        for e in range(EXPERTS_PER_GROUP):
            w_e = wp[:, e:e + 1] + wp[:, EXPERTS_PER_GROUP + e:EXPERTS_PER_GROUP + e + 1]
            hid = _silu(_dot(x, wg_ref[e])) * _dot(x, wu_ref[e]) * w_e
            acc = acc + _dot(hid.astype(BF16), wd_ref[e])
        o_ref[...] = acc.astype(BF16)

    @pl.when(b >= nb_ref[0])
    def _():
        o_ref[...] = jnp.zeros(o_ref.shape, BF16)


def _combine_kernel(lo_ref, off_ref, np_ref, ld_ref, x1_ref, g2_ref, shf_ref, scf_ref, fnw_ref, src_ref, y_ref,
                    seg, sems, *, ts, tps, final_norm):
    i = pl.program_id(0)
    slot = i % 2
    sizes = [ts >> k for k in range((ts // SEG_ALIGN).bit_length())]

    def fetch(step, sl, act):
        for u in range(tps):
            for g in range(N_EGROUPS):
                at = (step * tps + u) * N_EGROUPS + g
                n_pad = np_ref[at]
                for k, size in enumerate(sizes):
                    done = n_pad & ~(2 * size - 1)
                    src = pl.multiple_of(off_ref[at] + done, SEG_ALIGN)
                    dst = pl.multiple_of(lo_ref[at] + done, SEG_ALIGN)
                    cp = pltpu.make_async_copy(src_ref.at[pl.ds(src, size)], seg.at[sl, u, pl.ds(dst, size)],
                                               sems.at[sl, u, g, k])
                    pl.when((n_pad & size) != 0)(functools.partial(act, cp))

    @pl.when(i == 0)
    def _():
        seg[...] = jnp.zeros(seg.shape, BF16)
        fetch(0, 0, lambda cp: cp.start())

    @pl.when(i + 1 < pl.num_programs(0))
    def _():
        fetch(i + 1, 1 - slot, lambda cp: cp.start())

    fetch(i, slot, lambda cp: cp.wait())

    r = lax.broadcasted_iota(jnp.int32, (seg.shape[2], ts), 0)
    for u in range(tps):
        rs = slice(u * ts, (u + 1) * ts)
        perm = jnp.where(r == ld_ref[u], 1.0, 0.0).astype(BF16)
        xo = x1_ref[rs, :] + g2_ref[0] * _dot_tn(perm, seg[slot, u])
        if final_norm:
            xo = _rms(xo, fnw_ref[...]) * (1.0 + scf_ref[0]) + shf_ref[0]
        y_ref[rs, :] = xo


def _moe_final(h2ext, ldest, cnt, wg, wu, wd, x1, mod3, modf3, boff, rows_per_batch, final_norm_w, final_norm):
    t, d = x1.shape
    payload = h2ext.shape[1]
    ts = min(SORT_TILE, rows_per_batch)
    tps = 2 if rows_per_batch % (2 * ts) == 0 else 1
    block = MOE_BLOCK if t >= 8 * MOE_BLOCK else MOE_BLOCK // 2
    n_tiles = t // ts
    ldest = ldest.reshape(n_tiles, 1, ts)
    n_blocks = -(-(t + n_tiles * N_EGROUPS * (SEG_ALIGN - 1) + N_EGROUPS * (block - 1)) // block)
    lo, off, n_pad, blk_group, n_used = _sort_plan(cnt, n_tiles, block, n_blocks)
    ld_spec = pl.BlockSpec((tps, 1, ts), lambda i, *_: (i, 0, 0))

    sorted_rows = pl.pallas_call(
        functools.partial(_sort_kernel, ts=ts, tps=tps),
        grid_spec=pltpu.PrefetchScalarGridSpec(
            num_scalar_prefetch=3,
            grid=(n_tiles // tps,),
            in_specs=[pl.BlockSpec((tps * ts, payload), lambda i, *_: (i, 0)), ld_spec,
                      pl.BlockSpec(memory_space=pl.ANY)],
            out_specs=pl.BlockSpec(memory_space=pl.ANY),
            scratch_shapes=[pltpu.VMEM((2, tps, ts + N_EGROUPS * SEG_ALIGN, payload), BF16),
                            pltpu.SemaphoreType.DMA((2, tps, N_EGROUPS, (ts // SEG_ALIGN).bit_length()))]),
        out_shape=jax.ShapeDtypeStruct((n_blocks * block, payload), BF16),
        input_output_aliases={5: 0},
        compiler_params=_params("arbitrary"),
        name="moe_sort",
    )(lo, off, n_pad, h2ext, ldest, jnp.zeros((n_blocks * block, payload), BF16))

    d_e = wg.shape[2]
    live = lambda b, nb: jnp.minimum(b, nb[0] - 1)
    out_sorted = pl.pallas_call(
        _experts_kernel,
        grid_spec=pltpu.PrefetchScalarGridSpec(
            num_scalar_prefetch=2,
            grid=(n_blocks,),
            in_specs=[pl.BlockSpec((block, payload), lambda b, grp, nb: (live(b, nb), 0)),
                      pl.BlockSpec((EXPERTS_PER_GROUP, d, d_e), lambda b, grp, nb: (grp[b], 0, 0)),
                      pl.BlockSpec((EXPERTS_PER_GROUP, d, d_e), lambda b, grp, nb: (grp[b], 0, 0)),
                      pl.BlockSpec((EXPERTS_PER_GROUP, d_e, d), lambda b, grp, nb: (grp[b], 0, 0))],
            out_specs=pl.BlockSpec((block, d), lambda b, grp, nb: (b, 0))),
        out_shape=jax.ShapeDtypeStruct((n_blocks * block, d), BF16),
        compiler_params=_params("arbitrary"),
        name="moe_experts",
    )(blk_group, n_used, sorted_rows, wg, wu, wd)

    per = rows_per_batch // (tps * ts)
    row = lambda col: pl.BlockSpec((1, 1, d), lambda i, *_, col=col: (i // per + boff, 0, col))
    return pl.pallas_call(
        functools.partial(_combine_kernel, ts=ts, tps=tps, final_norm=final_norm),
        grid_spec=pltpu.PrefetchScalarGridSpec(
            num_scalar_prefetch=3,
            grid=(n_tiles // tps,),
            in_specs=[ld_spec,
                      pl.BlockSpec((tps * ts, d), lambda i, *_: (i, 0)),
                      row(5), row(0), row(1),
                      pl.BlockSpec((1, d), lambda i, *_: (0, 0)),
                      pl.BlockSpec(memory_space=pl.ANY)],
            out_specs=pl.BlockSpec((tps * ts, d), lambda i, *_: (i, 0)),
            scratch_shapes=[pltpu.VMEM((2, tps, ts + N_EGROUPS * SEG_ALIGN, d), BF16),
                            pltpu.SemaphoreType.DMA((2, tps, N_EGROUPS, (ts // SEG_ALIGN).bit_length()))]),
        out_shape=jax.ShapeDtypeStruct((t, d), F32),
        compiler_params=_params("arbitrary"),
        name="moe_combine",
    )(lo, off, n_pad, ldest, x1, mod3, modf3, modf3, final_norm_w, out_sorted)


def _layer(x, mod3, modf3, boff, k_prev, v_prev, logf_prev, conv_prev, ssm_prev, p, final_norm_w, final_norm):
    b, l, d = x.shape
    z, xbc, q, k, v, k_b, v_b, sm, smt, conv_new = _inproj(
        x, mod3, boff, p["norm1_w"], p["wz"], p["wx"], p["wq"], p["wk"], p["wv"], p["ws"], p["bs"])
    n_heads_ssd = p["wz"].shape[1] // P_SSD
    y_ssd, ssm_new = _ssd(xbc, z, sm, smt, conv_prev, ssm_prev, p["conv_w"], p["conv_b"], p["a_log"],
                          p["d_skip"], p["ssd_norm_w"])
    n_heads = p["wq"].shape[1] // HD_ATT
    logf_t = smt[:, n_heads_ssd:n_heads_ssd + n_heads, :]
    if k_prev is None:
        p0, k_all, v_all, lf_all = 0, k_b, v_b, logf_t
    else:
        p0 = k_prev.shape[1]
        pad = (-(p0 + l)) % LANES
        zeros = lambda w: jnp.zeros((b, pad, w), BF16)
        k_all = jnp.concatenate([k_prev.astype(BF16), k_b, zeros(k_b.shape[2])], axis=1)
        v_all = jnp.concatenate([v_prev.astype(BF16), v_b, zeros(v_b.shape[2])], axis=1)
        lf_all = jnp.concatenate([jnp.swapaxes(logf_prev, 1, 2), logf_t, jnp.zeros((b, n_heads, pad), F32)], axis=2)
    o_att = _attention(q, k_all, v_all, _forget_cumsum(lf_all), p0)
    x1, h2ext, ldest, cnt = _outproj(y_ssd, o_att, x, mod3, boff, p["attn_norm_w"], p["wo_ssd"], p["wo_att"],
                                     p["norm2_w"], p["wr"], p["br"])
    y = _moe_final(h2ext, ldest, cnt, p["wg"], p["wu"], p["wd"], x1.reshape(b * l, d), mod3, modf3, boff, l,
                   final_norm_w, final_norm)
    return y.reshape(b, l, d), (k, v, jnp.swapaxes(logf_t, 1, 2), conv_new, ssm_new)


def kernel(x_prompt, x_sample, c_prompt, c_sample, cache_k, cache_v, cache_logf, state_conv, state_ssm, norm1_w, w_ada, b_ada, w_in, conv_w, conv_b, dt_bias, a_log, d_skip, ssd_norm_w, f_bias, attn_norm_w, w_out, norm2_w, w_rg, b_rg, w_re, b_re, w_gate, w_up, w_down, final_norm_w, w_ada_f, b_ada_f):
    depth = w_in.shape[0]
    bp, lp, d = x_prompt.shape
    bs = x_sample.shape[0]
    d_conv = conv_w.shape[2]
    d_ssd = ssd_norm_w.shape[1]
    d_att = attn_norm_w.shape[1]
    h_ssd = dt_bias.shape[1]
    h_att = f_bias.shape[1]
    assert h_ssd + h_att == SMALL_W and d_att // HD_ATT == h_att and d_ssd // P_SSD == h_ssd

    c_all = jnp.concatenate([c_prompt, c_sample], axis=0)
    modf3 = _modulation(c_all, w_ada_f, b_ada_f).reshape(bp + bs, 1, 2 * d)
    final_w = final_norm_w.reshape(1, d)

    i0 = d_ssd
    i1 = i0 + d_conv
    i2 = i1 + h_ssd
    i3 = i2 + d_att
    i4 = i3 + d_att
    i5 = i4 + d_att
    yp, ys = x_prompt, x_sample
    outs_p, outs_s = [], []
    for layer in range(depth):
        mod3 = _modulation(c_all, w_ada[layer], b_ada[layer]).reshape(bp + bs, 1, 6 * d)
        wi = w_in[layer]
        w_small = jnp.concatenate([wi[:, i1:i2], wi[:, i5:], jnp.zeros((d, LANES - SMALL_W), F32)], axis=1)
        b_small = jnp.concatenate([dt_bias[layer], f_bias[layer], jnp.zeros((LANES - SMALL_W,), F32)])
        wr = jnp.zeros((ROUTER_ROWS, d), F32)
        wr = wr.at[:N_EGROUPS].set(w_rg[layer].T)
        wr = wr.at[EXPERT_ROW0:EXPERT_ROW0 + N_EXPERTS].set(
            jnp.transpose(w_re[layer], (0, 2, 1)).reshape(N_EXPERTS, d))
        wr_hi, wr_lo = _split2(wr)
        br = jnp.zeros((ROUTER_ROWS,), F32)
        br = br.at[:N_EGROUPS].set(b_rg[layer])
        br = br.at[EXPERT_ROW0:EXPERT_ROW0 + N_EXPERTS].set(b_re[layer].reshape(N_EXPERTS))
        p = dict(
            norm1_w=norm1_w[layer].reshape(1, d),
            wz=wi[:, :i0].astype(BF16), wx=wi[:, i0:i1].astype(BF16),
            wq=(wi[:, i2:i3] * (LOG2E * HD_ATT ** -0.5)).astype(BF16),
            wk=wi[:, i3:i4].astype(BF16), wv=wi[:, i4:i5].astype(BF16),
            ws=w_small.astype(BF16), bs=b_small.reshape(1, LANES),
            conv_w=conv_w[layer], conv_b=conv_b[layer], a_log=a_log[layer], d_skip=d_skip[layer],
            ssd_norm_w=ssd_norm_w[layer], attn_norm_w=attn_norm_w[layer].reshape(1, d_att),
            wo_ssd=w_out[layer][:d_ssd].astype(BF16), wo_att=w_out[layer][d_ssd:].astype(BF16),
            norm2_w=norm2_w[layer].reshape(1, d),
            wr=jnp.concatenate([wr_hi, wr_lo], axis=0), br=br.reshape(ROUTER_ROWS, 1),
            wg=w_gate[layer].astype(BF16), wu=w_up[layer].astype(BF16), wd=w_down[layer].astype(BF16),
        )
        conv0 = jnp.zeros((bp, CONV_W - 1, d_conv), F32)
        ssm0 = jnp.zeros((bp, h_ssd, P_SSD, N_STATE), F32)
        last = layer == depth - 1
        yp, st_p = _layer(yp, mod3, modf3, 0, None, None, None, conv0, ssm0, p, final_w, last)
        ck = cache_k[layer].reshape(bs, -1, d_att)
        cv = cache_v[layer].reshape(bs, -1, d_att)
        ys, st_s = _layer(ys, mod3, modf3, bp, ck, cv, cache_logf[layer], state_conv[layer], state_ssm[layer], p,
                          final_w, last)
        outs_p.append(st_p)
        outs_s.append(st_s)

    def stack(outs, b, l):
        k = jnp.stack([o[0].reshape(b, l, h_att, HD_ATT) for o in outs])
        v = jnp.stack([o[1].reshape(b, l, h_att, HD_ATT) for o in outs])
        return (k, v, jnp.stack([o[2] for o in outs]), jnp.stack([o[3] for o in outs]),
                jnp.stack([o[4] for o in outs]))

    return (yp, ys) + stack(outs_p, bp, lp) + stack(outs_s, bs, x_sample.shape[1])
```

```python
import functools

import jax
import jax.numpy as jnp
import numpy as np
from jax import lax
from jax.experimental import pallas as pl
from jax.experimental.pallas import tpu as pltpu

F32 = jnp.float32
BF16 = jnp.bfloat16

P_SSD = 64
N_STATE = 64
G_SSD = 2
CONV_W = 4
HD_ATT = 64
N_EGROUPS = 4
EXPERTS_PER_GROUP = 4
N_EXPERTS = N_EGROUPS * EXPERTS_PER_GROUP
EPS = 1e-6
NEG_BIG = -1e30

LANES = 128
SEG_ALIGN = 16
SORT_TILE = 256
MOE_BLOCK = 512
SMALL_W = 16
VMEM_LIMIT = 56 * 1024 * 1024


def _params(*sem):
    return pltpu.CompilerParams(dimension_semantics=sem, vmem_limit_bytes=VMEM_LIMIT)


def _split2(x):
    hi = x.astype(BF16)
    lo = (x - hi.astype(F32)).astype(BF16)
    return hi, lo


def _split3(x):
    hi = x.astype(BF16)
    r = x - hi.astype(F32)
    mid = r.astype(BF16)
    lo = (r - mid.astype(F32)).astype(BF16)
    return hi, mid, lo


def _dot(a, b):
    return jnp.dot(a, b, preferred_element_type=F32)


def _dot_nt(a, b):
    return lax.dot_general(a, b, (((1,), (1,)), ((), ())), preferred_element_type=F32)


def _dot_tn(a, b):
    return lax.dot_general(a, b, (((0,), (0,)), ((), ())), preferred_element_type=F32)


def _silu(x):
    h = 0.5 * x
    return h + h * jnp.tanh(h)


def _rms(x, w):
    return x * lax.rsqrt(jnp.mean(x * x, axis=-1, keepdims=True) + EPS) * w


def _pick_tile(n, candidates):
    for c in candidates:
        if n % c == 0:
            return c
    return n


def _mod_kernel(c_ref, w_ref, b_ref, o_ref):
    a = _silu(c_ref[...])
    a_hi, a_lo = _split2(a)
    w_hi, w_lo = _split2(w_ref[...])
    o_ref[...] = _dot(a_hi, w_hi) + _dot(a_lo, w_hi) + _dot(a_hi, w_lo) + b_ref[...]


def _modulation(c, w, b):
    m, d = c.shape
    n = w.shape[1]
    tn = _pick_tile(n, (1024, 512, 256, 128))
    return pl.pallas_call(
        _mod_kernel,
        grid=(n // tn,),
        in_specs=[pl.BlockSpec((m, d), lambda j: (0, 0)),
                  pl.BlockSpec((d, tn), lambda j: (0, j)),
                  pl.BlockSpec((1, tn), lambda j: (0, j))],
        out_specs=pl.BlockSpec((m, tn), lambda j: (0, j)),
        out_shape=jax.ShapeDtypeStruct((m, n), F32),
        compiler_params=_params("parallel"),
        name="adaln_mod",
    )(c, w, b.reshape(1, n))


def _inproj_kernel(x_ref, sh_ref, sc_ref, nw_ref, wz_ref, wx_ref, wq_ref, wk_ref, wv_ref, ws_ref, bs_ref,
                   z_ref, xbc_ref, q_ref, k_ref, v_ref, kb_ref, vb_ref, sm_ref, smt_ref, tail_ref):
    l = pl.program_id(1)
    x = x_ref[0]
    h = (_rms(x, nw_ref[...]) * (1.0 + sc_ref[0]) + sh_ref[0]).astype(BF16)
    z_ref[0] = _dot(h, wz_ref[...]).astype(BF16)
    xbc = _dot(h, wx_ref[...])
    xbc_ref[0] = xbc.astype(BF16)
    q_ref[0] = _dot(h, wq_ref[...]).astype(BF16)
    k = _dot(h, wk_ref[...])
    k_ref[0] = k
    kb_ref[0] = k.astype(BF16)
    v = _dot(h, wv_ref[...])
    v_ref[0] = v
    vb_ref[0] = v.astype(BF16)
    s = _dot(h, ws_ref[...]) + bs_ref[...]
    t = jnp.log(1.0 + jnp.exp(-jnp.abs(s)))
    lane = lax.broadcasted_iota(jnp.int32, s.shape, 1)
    s = jnp.where(lane < SMALL_W // 2, jnp.maximum(s, 0.0) + t, jnp.minimum(s, 0.0) - t)
    sm_ref[0] = s[:, :SMALL_W]
    smt_ref[0] = s.T[:SMALL_W, :]

    @pl.when(l == pl.num_programs(1) - 1)
    def _():
        tl = xbc.shape[0]
        tail_ref[0] = xbc[tl - (CONV_W - 1):, :]


def _inproj(x, mod3, boff, norm_w, wz, wx, wq, wk, wv, ws, bs):
    b, l, d = x.shape
    tl = _pick_tile(l, (1024, 512, 256, 128, 64))
    nl = l // tl
    d_ssd, d_conv, d_att = wz.shape[1], wx.shape[1], wq.shape[1]
    row = lambda col: pl.BlockSpec((1, 1, d), lambda i, j, col=col: (i + boff, 0, col))
    full = lambda a: pl.BlockSpec(a.shape, lambda i, j: (0,) * a.ndim)
    tok = lambda w: pl.BlockSpec((1, tl, w), lambda i, j: (i, j, 0))
    out_shape = [
        jax.ShapeDtypeStruct((b, l, d_ssd), BF16),
        jax.ShapeDtypeStruct((b, l, d_conv), BF16),
        jax.ShapeDtypeStruct((b, l, d_att), BF16),
        jax.ShapeDtypeStruct((b, l, d_att), F32),
        jax.ShapeDtypeStruct((b, l, d_att), F32),
        jax.ShapeDtypeStruct((b, l, d_att), BF16),
        jax.ShapeDtypeStruct((b, l, d_att), BF16),
        jax.ShapeDtypeStruct((b, l, SMALL_W), F32),
        jax.ShapeDtypeStruct((b, SMALL_W, l), F32),
        jax.ShapeDtypeStruct((b, CONV_W - 1, d_conv), F32),
    ]
    out_specs = [tok(d_ssd), tok(d_conv), tok(d_att), tok(d_att), tok(d_att), tok(d_att), tok(d_att),
                 tok(SMALL_W), pl.BlockSpec((1, SMALL_W, tl), lambda i, j: (i, 0, j)),
                 pl.BlockSpec((1, CONV_W - 1, d_conv), lambda i, j: (i, 0, 0))]
    return pl.pallas_call(
        _inproj_kernel,
        grid=(b, nl),
        in_specs=[tok(d), row(0), row(1), full(norm_w), full(wz), full(wx), full(wq), full(wk), full(wv),
                  full(ws), full(bs)],
        out_specs=out_specs,
        out_shape=out_shape,
        compiler_params=_params("parallel", "arbitrary"),
        name="inproj",
    )(x, mod3, mod3, norm_w, wz, wx, wq, wk, wv, ws, bs)


LOG2E = 1.4426950408889634
AUG = 3
SPLIT_ROWS = 128


PIECE_COLS = 32


def _aug_select(n_heads):
    pair = 2 * HD_ATT
    sq = np.zeros((n_heads // 2, PIECE_COLS, pair), np.float32)
    sk = np.zeros((n_heads // 2, PIECE_COLS, pair), np.float32)
    one = AUG * n_heads
    for h in range(n_heads):
        slot = ((h ^ 1) % 2) * HD_ATT
        for c in range(AUG):
            sq[h // 2, c * n_heads + h, slot + c] = 1.0
            sq[h // 2, one, slot + AUG + c] = 1.0
            sk[h // 2, one, slot + c] = 1.0
            sk[h // 2, c * n_heads + h, slot + AUG + c] = -1.0
    return sq, sk


def _cumsum_kernel(x_ref, eye_ref, o_ref, carry):
    @pl.when(pl.program_id(1) == 0)
    def _():
        carry[...] = jnp.zeros_like(carry)

    x = x_ref[0]
    h, tc = x.shape
    r = lax.broadcasted_iota(jnp.int32, (tc, tc), 0)
    c = lax.broadcasted_iota(jnp.int32, (tc, tc), 1)
    upper = jnp.where(r <= c, 1.0, 0.0).astype(BF16)
    stack = lambda ps: jnp.concatenate([p.astype(F32) for p in ps], axis=0)
    parts = _dot(stack(_split3(x)).astype(BF16), upper)
    cs = parts[:h] + parts[h:2 * h] + parts[2 * h:] + carry[...]
    carry[...] = cs[:, tc - 1:]
    rows = lax.broadcasted_iota(jnp.int32, (PIECE_COLS - AUG * h, tc), 0)
    pieces = jnp.concatenate([stack(_split3(cs * LOG2E)), jnp.where(rows == 0, 1.0, 0.0)], axis=0)
    o_ref[0] = _dot_tn(pieces.astype(BF16), eye_ref[...]).astype(BF16)


def _forget_cumsum(logf_t):
    b, h, lk = logf_t.shape
    assert AUG * h < PIECE_COLS
    tc = _pick_tile(lk, (1024, 512, 384, 256, 128))
    eye = jnp.asarray(np.eye(PIECE_COLS), BF16)
    return pl.pallas_call(
        _cumsum_kernel,
        grid=(b, lk // tc),
        in_specs=[pl.BlockSpec((1, h, tc), lambda i, j: (i, 0, j)),
                  pl.BlockSpec((PIECE_COLS, PIECE_COLS), lambda i, j: (0, 0))],
        out_specs=pl.BlockSpec((1, tc, PIECE_COLS), lambda i, j: (i, j, 0)),
        out_shape=jax.ShapeDtypeStruct((b, lk, PIECE_COLS), BF16),
        scratch_shapes=[pltpu.VMEM((h, 1), F32)],
        compiler_params=_params("parallel", "arbitrary"),
        name="forget_cumsum",
    )(logf_t, eye)


def _ssd_spread(n_heads):
    e = np.zeros((2 * AUG * n_heads, 2 * n_heads * P_SSD), np.float32)
    for v in range(2):
        for c in range(AUG):
            for h in range(n_heads):
                col = v * n_heads * P_SSD + h * P_SSD
                e[(v * AUG + c) * n_heads + h, col:col + P_SSD] = 1.0
    return e


def _ssd_kernel(xbc_ref, z_ref, sm_ref, smt_ref, cprev_ref, sprev_ref, cw_ref, cb_ref, arow_ref, acol_ref,
                dskip_ref, nw_ref, spread_ref, y_ref, snew_ref, hist, state):
    l = pl.program_id(1)
    q = xbc_ref.shape[1]
    d_ssd = z_ref.shape[2]
    n_heads = d_ssd // P_SSD
    pair = 2 * P_SSD
    hist_rows = hist.shape[0]

    @pl.when(l == 0)
    def _():
        hist[...] = jnp.zeros(hist.shape, F32)
        hist[hist_rows - (CONV_W - 1):, :] = cprev_ref[0]
        state[...] = sprev_ref[0]

    xb = xbc_ref[0]
    r3 = lax.broadcasted_iota(jnp.int32, ((CONV_W - 1) * q, q), 0)
    c3 = lax.broadcasted_iota(jnp.int32, ((CONV_W - 1) * q, q), 1)
    shift = jnp.where(r3 % q - c3 == r3 // q + 1, 1.0, 0.0).astype(BF16)
    shifted = _dot(shift, xb)
    cw = cw_ref[...]
    conv = cb_ref[...] + xb.astype(F32) * cw[CONV_W - 1:CONV_W, :]
    for k in range(CONV_W - 1):
        conv = conv + shifted[k * q:(k + 1) * q, :] * cw[CONV_W - 2 - k:CONV_W - 1 - k, :]
    hrow = lambda j: hist[hist_rows - j:hist_rows - j + 1, :]
    frow = lax.broadcasted_iota(jnp.int32, (hist_rows, 1), 0)
    fix = jnp.zeros((hist_rows, conv.shape[1]), F32)
    for t in range(CONV_W - 1):
        acc = 0.0
        for j in range(1, CONV_W - t):
            acc = acc + hrow(j) * cw[CONV_W - 1 - t - j:CONV_W - t - j, :]
        fix = jnp.where(frow == t, acc, fix)
    conv = jnp.concatenate([conv[:hist_rows] + fix, conv[hist_rows:]], axis=0)
    hist[...] = xb[q - hist_rows:, :].astype(F32)
    act = _silu(conv)

    dt = sm_ref[0][:, :n_heads]
    dt_t = smt_ref[0][:n_heads, :]
    a_row = -jnp.exp(arow_ref[...]) * LOG2E
    a_col = -jnp.exp(acol_ref[...]) * LOG2E
    r = lax.broadcasted_iota(jnp.int32, (q, q), 0)
    c = lax.broadcasted_iota(jnp.int32, (q, q), 1)
    causal = r >= c
    lower = jnp.where(causal, 1.0, 0.0).astype(BF16)
    upper = jnp.where(r <= c, 1.0, 0.0).astype(BF16)
    h0, h1, h2 = _split3(dt * a_row)
    acum = _dot(lower, h0) + _dot(lower, h1) + _dot(lower, h2)
    t0, t1, t2 = _split3(dt_t * a_col)
    acum_t = _dot(t0, upper) + _dot(t1, upper) + _dot(t2, upper)
    pieces = [p.astype(F32) for p in _split3(acum) + _split3(dt)]
    wide = _dot(jnp.concatenate(pieces, axis=1).astype(BF16), spread_ref[...])
    acum_x = wide[:, :d_ssd]
    dt_x = wide[:, d_ssd:]
    a_last = acum[q - 1:q, :]
    e_all = jnp.exp2(a_last)
    e_cum_x = jnp.exp2(acum_x)
    e_end_x = jnp.exp2(acum_x[q - 1:q, :] - acum_x)

    xs = act[:, :d_ssd]
    bm = act[:, d_ssd:d_ssd + G_SSD * N_STATE]
    cm = act[:, d_ssd + G_SSD * N_STATE:]
    bm_b = bm.astype(BF16)
    cm_b = cm.astype(BF16)
    xdt = xs * dt_x
    xdt_b = xdt.astype(BF16)
    xe_b = (xdt * e_end_x).astype(BF16)
    lane = lax.broadcasted_iota(jnp.int32, (1, pair), 1)
    first = lane < P_SSD
    srow = lax.broadcasted_iota(jnp.int32, (pair, 1), 0)
    ys = []
    for p in range(n_heads // 2):
        g = (2 * p * G_SSD) // n_heads
        in_group = first if g == 0 else jnp.logical_not(first)
        sl = slice(p * pair, (p + 1) * pair)
        if (2 * p) % (n_heads // G_SSD) == 0:
            cb = _dot_nt(jnp.where(in_group, cm_b, jnp.zeros_like(cm_b)), bm_b)
        ms = []
        for hh in range(2):
            h = 2 * p + hh
            seg = acum[:, h:h + 1] - acum_t[h:h + 1, :]
            ms.append((cb * jnp.exp2(jnp.where(causal, seg, NEG_BIG))).astype(BF16))
        xp = xdt_b[:, sl]
        zero = jnp.zeros_like(xp)
        y = _dot(jnp.concatenate(ms, axis=1),
                 jnp.concatenate([jnp.where(first, xp, zero), jnp.where(first, zero, xp)], axis=0))
        s_in = state[sl, :]
        y = y + _dot_nt(cm_b, s_in.astype(BF16)) * e_cum_x[:, sl]
        upd = _dot_tn(xe_b[:, sl], bm_b)
        keep = jnp.where(srow < P_SSD, e_all[:, 2 * p:2 * p + 1], e_all[:, 2 * p + 1:2 * p + 2])
        state[sl, :] = s_in * keep + jnp.where(in_group, upd, 0.0)
        ys.append(y)
    y_all = jnp.concatenate(ys, axis=1) + dskip_ref[...] * xs
    yg = y_all * _silu(z_ref[0].astype(F32))
    y_ref[0] = _rms(yg, nw_ref[...]).astype(BF16)
    snew_ref[0] = state[...]


def _ssd(xbc, z, sm, smt, conv_prev, ssm_prev, conv_w, conv_b, a_log, d_skip, norm_w):
    b, l, d_conv = xbc.shape
    d_ssd = z.shape[2]
    n_heads = d_ssd // P_SSD
    hg = n_heads // G_SSD
    q = _pick_tile(l, (256, 128, 64))
    tok = lambda w: pl.BlockSpec((1, q, w), lambda i, j: (i, j, 0))
    full = lambda a: pl.BlockSpec(a.shape, lambda i, j: (0,) * a.ndim)
    a_row = a_log.reshape(1, n_heads)
    a_col = a_log.reshape(n_heads, 1)
    conv_b = conv_b.reshape(1, d_conv)
    d_skip = jnp.repeat(d_skip, P_SSD).reshape(1, d_ssd)
    norm_w = norm_w.reshape(1, d_ssd)
    spread = jnp.asarray(_ssd_spread(n_heads), BF16)
    s4 = ssm_prev.reshape(b, G_SSD, hg * P_SSD, N_STATE)
    s_in = jnp.concatenate([jnp.pad(s4[:, g], ((0, 0), (0, 0), (g * N_STATE, (G_SSD - 1 - g) * N_STATE)))
                            for g in range(G_SSD)], axis=1)
    state_spec = pl.BlockSpec((1, n_heads * P_SSD, G_SSD * N_STATE), lambda i, j: (i, 0, 0))
    y, s_out = pl.pallas_call(
        _ssd_kernel,
        grid=(b, l // q),
        in_specs=[tok(d_conv), tok(d_ssd), tok(SMALL_W), pl.BlockSpec((1, SMALL_W, q), lambda i, j: (i, 0, j)),
                  pl.BlockSpec((1, CONV_W - 1, d_conv), lambda i, j: (i, 0, 0)), state_spec,
                  full(conv_w), full(conv_b), full(a_row), full(a_col), full(d_skip), full(norm_w), full(spread)],
        out_specs=[tok(d_ssd), state_spec],
        out_shape=[jax.ShapeDtypeStruct((b, l, d_ssd), BF16),
                   jax.ShapeDtypeStruct((b, n_heads * P_SSD, G_SSD * N_STATE), F32)],
        scratch_shapes=[pltpu.VMEM((8, d_conv), F32),
                        pltpu.VMEM((n_heads * P_SSD, G_SSD * N_STATE), F32)],
        compiler_params=_params("parallel", "arbitrary"),
        name="ssd",
    )(xbc, z, sm, smt, conv_prev, s_in, conv_w, conv_b, a_row, a_col, d_skip, norm_w, spread)
    s_out = s_out.reshape(b, G_SSD, hg * P_SSD, G_SSD * N_STATE)
    s_new = jnp.concatenate([s_out[:, g, :, g * N_STATE:(g + 1) * N_STATE] for g in range(G_SSD)], axis=1)
    return y, s_new.reshape(b, n_heads, P_SSD, N_STATE)


def _attn_kernel(q_ref, pq_ref, k_ref, pk_ref, v_ref, sq_ref, sk_ref, o_ref, kk_s, m_s, acc_s, *, p0, tq, tk):
    i = pl.program_id(2)
    lane = lax.broadcasted_iota(jnp.int32, (1, 2 * HD_ATT), 1)
    first = lane < HD_ATT
    own = (first, jnp.logical_not(first))
    sum_lane = (HD_ATT, 0)

    @pl.when(i == 0)
    def _():
        k = k_ref[0]
        ka = _dot(pk_ref[0], sk_ref[0]).astype(BF16)
        kk_s[0] = jnp.where(first, k, ka)
        kk_s[1] = jnp.where(first, ka, k)

    q = q_ref[0]
    qa = _dot(pq_ref[0], sq_ref[0]).astype(BF16)
    qq = (jnp.where(first, q, qa), jnp.where(first, qa, q))
    m_s[...] = jnp.full(m_s.shape, NEG_BIG, F32)
    acc_s[...] = jnp.zeros(acc_s.shape, F32)
    nc = tk // LANES
    n_split = 2 if tq % (2 * SPLIT_ROWS) == 0 else 1
    rows = tq // n_split
    aligned = tq == tk and p0 % tk == 0

    def step(j, masked):
        off = pl.multiple_of(j * tk, tk)
        v = v_ref[0, pl.ds(off, tk), :]
        vv = [jnp.where(own[hh], v, jnp.where(lane == sum_lane[hh], 1.0, 0.0).astype(BF16)) for hh in range(2)]
        width = [(r + 1) * rows if masked and aligned else tk for r in range(n_split)]
        logit = [[_dot_nt(qq[hh][r * rows:(r + 1) * rows], kk_s[hh, pl.ds(off, width[r]), :]) for hh in range(2)]
                 for r in range(n_split)]
        for r in range(n_split):
            rs = slice(r * rows, (r + 1) * rows)
            tiles = range(width[r] // LANES)
            if masked and aligned:
                row = lax.broadcasted_iota(jnp.int32, (rows, LANES), 0)
                col = lax.broadcasted_iota(jnp.int32, (rows, LANES), 1)
                seen = [None if (c + 1) * LANES <= r * rows + 1 else col + (c * LANES - r * rows) <= row for c in tiles]
            elif masked:
                q_pos = p0 + i * tq + r * rows + lax.broadcasted_iota(jnp.int32, (rows, LANES), 0)
                k_pos = j * tk + lax.broadcasted_iota(jnp.int32, (rows, LANES), 1)
                seen = [k_pos + c * LANES <= q_pos for c in tiles]
            else:
                seen = [None for _ in tiles]
            for hh in range(2):
                s = logit[r][hh]
                cols = [s[:, c * LANES:(c + 1) * LANES] for c in tiles]
                cols = [x if m is None else jnp.where(m, x, NEG_BIG) for x, m in zip(cols, seen)]
                m_cur = functools.reduce(jnp.maximum, cols)
                m_prev = m_s[hh, rs, :]
                m_new = jnp.maximum(m_prev, jnp.max(m_cur, axis=1, keepdims=True))
                alpha = jnp.exp2(m_prev - m_new)
                p = jnp.concatenate([jnp.exp2((col - m_new).astype(BF16)) for col in cols], axis=1)
                m_s[hh, rs, :] = m_new
                acc_s[hh, rs, :] = alpha * acc_s[hh, rs, :] + _dot(p, vv[hh][:width[r]])

    n_full = (p0 + i * tq + 1) // tk
    n_vis = (p0 + i * tq + tq - 1) // tk + 1

    def full_pair(jj, carry):
        step(2 * jj, False)
        step(2 * jj + 1, False)
        return carry

    def masked_body(j, carry):
        step(j, True)
        return carry

    lax.fori_loop(0, n_full // 2, full_pair, 0)
    odd = n_full % 2 == 1
    both = jnp.logical_and(odd, n_vis > n_full)

    @pl.when(both)
    def _():
        step(n_full - 1, False)
        step(n_full, True)

    pl.when(jnp.logical_and(odd, n_vis <= n_full))(lambda: step(n_full - 1, False))
    lax.fori_loop(n_full + both.astype(jnp.int32), n_vis, masked_body, 0)
    a0 = acc_s[0]
    a1 = acc_s[1]
    inv0 = 1.0 / a0[:, sum_lane[0]:sum_lane[0] + 1]
    inv1 = 1.0 / a1[:, sum_lane[1]:sum_lane[1] + 1]
    o = jnp.where(first, a0 * inv0, a1 * inv1)
    o_ref[0] = o.astype(o_ref.dtype)


def _attention(q, k, v, pieces, p0):
    b, lq, d_att = q.shape
    lk = k.shape[1]
    n_heads = d_att // HD_ATT
    tq = _pick_tile(lq, (512, 256, 128, 64))
    tk = lk if lk <= 1536 else _pick_tile(lk, (512, 256, 128))
    assert p0 % tq == 0 and lk >= p0 + lq
    pair = 2 * HD_ATT
    sq, sk = (jnp.asarray(a, BF16) for a in _aug_select(n_heads))
    q_spec = pl.BlockSpec((1, tq, pair), lambda bi, hp, i: (bi, i, hp))
    kv_spec = pl.BlockSpec((1, lk, pair), lambda bi, hp, i: (bi, 0, hp))
    sel_spec = pl.BlockSpec((1, PIECE_COLS, pair), lambda bi, hp, i: (hp, 0, 0))
    return pl.pallas_call(
        functools.partial(_attn_kernel, p0=p0, tq=tq, tk=tk),
        grid=(b, n_heads // 2, lq // tq),
        in_specs=[q_spec, pl.BlockSpec((1, tq, PIECE_COLS), lambda bi, hp, i: (bi, i + p0 // tq, 0)),
                  kv_spec, pl.BlockSpec((1, lk, PIECE_COLS), lambda bi, hp, i: (bi, 0, 0)), kv_spec,
                  sel_spec, sel_spec],
        out_specs=q_spec,
        out_shape=jax.ShapeDtypeStruct((b, lq, d_att), BF16),
        scratch_shapes=[pltpu.VMEM((2, lk, pair), BF16), pltpu.VMEM((2, tq, LANES), F32),
                        pltpu.VMEM((2, tq, pair), F32)],
        compiler_params=_params("parallel", "parallel", "arbitrary"),
        name="fox_attention",
    )(q, pieces, k, pieces, v, sq, sk)


ROUTER_ROWS = 32
EXPERT_ROW0 = 8


def _outproj_kernel(ys_ref, oa_ref, x_ref, g1_ref, sh_ref, sc_ref, anw_ref, wos_ref, woa_ref, n2w_ref, wr_ref,
                    br_ref, x1_ref, h2_ref, ld_ref, cnt_ref, wt_s, g_s):
    ya = _rms(oa_ref[0].astype(F32), anw_ref[...]).astype(BF16)
    m = _dot(ys_ref[0], wos_ref[...]) + _dot(ya, woa_ref[...])
    x1 = x_ref[0] + g1_ref[0] * m
    x1_ref[0] = x1
    h2 = _rms(x1, n2w_ref[...]) * (1.0 + sc_ref[0]) + sh_ref[0]
    h_hi, h_lo = _split2(h2)
    wr = wr_ref[...]
    p1 = _dot_nt(wr, h_hi)
    p2 = _dot_nt(wr[:ROUTER_ROWS], h_lo)
    logit = p1[:ROUTER_ROWS] + p1[ROUTER_ROWS:] + p2 + br_ref[...]

    lg = [logit[g:g + 1, :] for g in range(N_EGROUPS)]
    gmax = jnp.maximum(jnp.maximum(lg[0], lg[1]), jnp.maximum(lg[2], lg[3]))
    denom = sum(jnp.exp(x - gmax) for x in lg)
    p_sel = 1.0 / denom
    is_g = []
    taken = jnp.zeros_like(gmax) > 1.0
    for g in range(N_EGROUPS):
        hit = (lg[g] == gmax) & jnp.logical_not(taken)
        is_g.append(hit)
        taken = taken | hit
    le = []
    for e in range(EXPERTS_PER_GROUP):
        v = logit[EXPERT_ROW0 + 3 * EXPERTS_PER_GROUP + e:EXPERT_ROW0 + 3 * EXPERTS_PER_GROUP + e + 1, :]
        for g in range(N_EGROUPS - 2, -1, -1):
            r0 = EXPERT_ROW0 + g * EXPERTS_PER_GROUP + e
            v = jnp.where(is_g[g], logit[r0:r0 + 1, :], v)
        le.append(v)
    m1 = jnp.maximum(jnp.maximum(le[0], le[1]), jnp.maximum(le[2], le[3]))
    first = []
    taken = jnp.zeros_like(m1) > 1.0
    for e in range(EXPERTS_PER_GROUP):
        hit = (le[e] == m1) & jnp.logical_not(taken)
        first.append(hit)
        taken = taken | hit
    rest = [jnp.where(first[e], -jnp.inf, le[e]) for e in range(EXPERTS_PER_GROUP)]
    m2 = jnp.maximum(jnp.maximum(rest[0], rest[1]), jnp.maximum(rest[2], rest[3]))
    second = []
    taken = jnp.zeros_like(m1) > 1.0
    for e in range(EXPERTS_PER_GROUP):
        hit = (rest[e] == m2) & jnp.logical_not(taken)
        second.append(hit)
        taken = taken | hit
    e2 = jnp.exp(m2 - m1)
    w_a = p_sel / (1.0 + e2)
    w_b = w_a * e2
    d = x1.shape[1]
    wt_s[...] = jnp.zeros(wt_s.shape, F32)
    for e in range(EXPERTS_PER_GROUP):
        w = jnp.where(first[e], w_a, jnp.where(second[e], w_b, 0.0))
        w_hi = w.astype(BF16).astype(F32)
        wt_s[e:e + 1, :] = w_hi
        wt_s[EXPERTS_PER_GROUP + e:EXPERTS_PER_GROUP + e + 1, :] = w - w_hi
    h2_ref[:, :d] = h_hi
    h2_ref[:, d:] = wt_s[...].T.astype(BF16)

    tl = logit.shape[1]
    ts = min(SORT_TILE, tl)
    g_s[...] = jnp.zeros(g_s.shape, F32)
    for g in range(N_EGROUPS):
        g_s[g:g + 1, :] = jnp.where(is_g[g], 1.0, 0.0)
    r = lax.broadcasted_iota(jnp.int32, (tl, tl), 0)
    c = lax.broadcasted_iota(jnp.int32, (tl, tl), 1)
    same_tile = (r // ts) == (c // ts)
    upper = jnp.where((r <= c) & same_tile, 1.0, 0.0).astype(BF16)
    cum = _dot(g_s[...].astype(BF16), upper)
    lane = lax.broadcasted_iota(jnp.int32, (1, tl), 1)
    crow = lax.broadcasted_iota(jnp.int32, (8, LANES), 0)
    clane = lax.broadcasted_iota(jnp.int32, (8, LANES), 1)
    ldest = -1.0
    cnt = jnp.zeros((8, LANES), F32)
    for g in range(N_EGROUPS):
        ldest = ldest + jnp.where(is_g[g], cum[g:g + 1, :], 0.0)
    lo = [0.0] * (tl // ts)
    for g in range(N_EGROUPS):
        lo_row = jnp.zeros((1, tl), F32)
        for sub in range(tl // ts):
            n = cum[g:g + 1, (sub + 1) * ts - 1:(sub + 1) * ts]
            n_pad = jnp.ceil(n / SEG_ALIGN) * SEG_ALIGN
            lo_row = jnp.where(lane // ts == sub, lo[sub], lo_row)
            cnt = jnp.where((crow == sub) & (clane == g), n_pad, cnt)
            lo[sub] = lo[sub] + n_pad
        ldest = ldest + jnp.where(is_g[g], lo_row, 0.0)
    ld_ref[0] = ldest.astype(jnp.int32)
    cnt_ref[0] = cnt.astype(jnp.int32)


def _outproj(y_ssd, o_att, x, mod3, boff, attn_norm_w, wo_ssd, wo_att, norm2_w, wr, br):
    b, l, d = x.shape
    tl = _pick_tile(l, (512, 256, 128, 64))
    nl = l // tl
    d_ssd, d_att = y_ssd.shape[2], o_att.shape[2]
    row = lambda col: pl.BlockSpec((1, 1, d), lambda i, j, col=col: (i + boff, 0, col))
    full = lambda a: pl.BlockSpec(a.shape, lambda i, j: (0,) * a.ndim)
    tok = lambda w: pl.BlockSpec((1, tl, w), lambda i, j: (i, j, 0))
    return pl.pallas_call(
        _outproj_kernel,
        grid=(b, nl),
        in_specs=[tok(d_ssd), tok(d_att), tok(d), row(2), row(3), row(4), full(attn_norm_w), full(wo_ssd),
                  full(wo_att), full(norm2_w), full(wr), full(br)],
        out_specs=[tok(d), pl.BlockSpec((tl, d + LANES), lambda i, j: (i * nl + j, 0)),
                   pl.BlockSpec((1, 1, tl), lambda i, j: (i * nl + j, 0, 0)),
                   pl.BlockSpec((1, 8, LANES), lambda i, j: (i * nl + j, 0, 0))],
        out_shape=[jax.ShapeDtypeStruct((b, l, d), F32),
                   jax.ShapeDtypeStruct((b * l, d + LANES), BF16),
                   jax.ShapeDtypeStruct((b * nl, 1, tl), jnp.int32),
                   jax.ShapeDtypeStruct((b * nl, 8, LANES), jnp.int32)],
        scratch_shapes=[pltpu.VMEM((LANES, tl), F32), pltpu.VMEM((8, tl), F32)],
        compiler_params=_params("parallel", "parallel"),
        name="outproj_router",
    )(y_ssd, o_att, x, mod3, mod3, mod3, attn_norm_w, wo_ssd, wo_att, norm2_w, wr, br)


def _sort_plan(cnt, n_tiles, block, n_blocks):
    ns = n_tiles // cnt.shape[0]
    n_pad = cnt[:, :ns, :N_EGROUPS].reshape(n_tiles, N_EGROUPS)
    lo = jnp.cumsum(n_pad, axis=1) - n_pad
    region = (jnp.sum(n_pad, axis=0) + block - 1) // block * block
    end = jnp.cumsum(region)
    off = (end - region)[None, :] + jnp.cumsum(n_pad, axis=0) - n_pad
    blk = jnp.arange(n_blocks, dtype=jnp.int32) * block
    blk_group = jnp.minimum(jnp.sum(blk[:, None] >= end[None, :], axis=1), N_EGROUPS - 1).astype(jnp.int32)
    n_used = (end[-1] // block).astype(jnp.int32).reshape(1)
    flat = lambda a: a.astype(jnp.int32).reshape(-1)
    return flat(lo), flat(off), flat(n_pad), blk_group, n_used


def _sort_kernel(lo_ref, off_ref, np_ref, x_ref, ld_ref, init_ref, out_ref, cbuf, sems, *, ts, tps):
    del init_ref
    i = pl.program_id(0)
    slot = i % 2
    rows = ts + N_EGROUPS * SEG_ALIGN
    sizes = [ts >> k for k in range((ts // SEG_ALIGN).bit_length())]

    r = lax.broadcasted_iota(jnp.int32, (rows, ts), 0)
    for u in range(tps):
        perm = jnp.where(r == ld_ref[u], 1.0, 0.0).astype(BF16)
        cbuf[slot, u] = _dot(perm, x_ref[u * ts:(u + 1) * ts, :]).astype(BF16)

    def for_each(step, sl, act):
        for u in range(tps):
            for g in range(N_EGROUPS):
                at = (step * tps + u) * N_EGROUPS + g
                n_pad = np_ref[at]
                for k, size in enumerate(sizes):
                    done = n_pad & ~(2 * size - 1)
                    src = pl.multiple_of(lo_ref[at] + done, SEG_ALIGN)
                    dst = pl.multiple_of(off_ref[at] + done, SEG_ALIGN)
                    cp = pltpu.make_async_copy(cbuf.at[sl, u, pl.ds(src, size)], out_ref.at[pl.ds(dst, size)],
                                               sems.at[sl, u, g, k])
                    pl.when((n_pad & size) != 0)(functools.partial(act, cp))

    for_each(i, slot, lambda cp: cp.start())

    @pl.when(i > 0)
    def _():
        for_each(i - 1, 1 - slot, lambda cp: cp.wait())

    @pl.when(i == pl.num_programs(0) - 1)
    def _():
        for_each(i, slot, lambda cp: cp.wait())


def _experts_kernel(grp_ref, nb_ref, x_ref, wg_ref, wu_ref, wd_ref, o_ref):
    del grp_ref
    b = pl.program_id(0)
    d = o_ref.shape[1]

    @pl.when(b < nb_ref[0])
    def _():
        blk = x_ref[...]
        x = blk[:, :d]
        wp = blk[:, d:].astype(F32)
        acc = jnp.zeros(o_ref.shape, F32)
        for e in range(EXPERTS_PER_GROUP):
            w_e = wp[:, e:e + 1] + wp[:, EXPERTS_PER_GROUP + e:EXPERTS_PER_GROUP + e + 1]
            hid = _silu(_dot(x, wg_ref[e])) * _dot(x, wu_ref[e]) * w_e
            acc = acc + _dot(hid.astype(BF16), wd_ref[e])
        o_ref[...] = acc.astype(BF16)

    @pl.when(b >= nb_ref[0])
    def _():
        o_ref[...] = jnp.zeros(o_ref.shape, BF16)


def _combine_kernel(lo_ref, off_ref, np_ref, ld_ref, x1_ref, g2_ref, shf_ref, scf_ref, fnw_ref, src_ref, y_ref,
                    seg, sems, *, ts, tps, final_norm):
    i = pl.program_id(0)
    slot = i % 2
    sizes = [ts >> k for k in range((ts // SEG_ALIGN).bit_length())]

    def fetch(step, sl, act):
        for u in range(tps):
            for g in range(N_EGROUPS):
                at = (step * tps + u) * N_EGROUPS + g
                n_pad = np_ref[at]
                for k, size in enumerate(sizes):
                    done = n_pad & ~(2 * size - 1)
                    src = pl.multiple_of(off_ref[at] + done, SEG_ALIGN)
                    dst = pl.multiple_of(lo_ref[at] + done, SEG_ALIGN)
                    cp = pltpu.make_async_copy(src_ref.at[pl.ds(src, size)], seg.at[sl, u, pl.ds(dst, size)],
                                               sems.at[sl, u, g, k])
                    pl.when((n_pad & size) != 0)(functools.partial(act, cp))

    @pl.when(i == 0)
    def _():
        seg[...] = jnp.zeros(seg.shape, BF16)
        fetch(0, 0, lambda cp: cp.start())

    @pl.when(i + 1 < pl.num_programs(0))
    def _():
        fetch(i + 1, 1 - slot, lambda cp: cp.start())

    fetch(i, slot, lambda cp: cp.wait())

    r = lax.broadcasted_iota(jnp.int32, (seg.shape[2], ts), 0)
    for u in range(tps):
        rs = slice(u * ts, (u + 1) * ts)
        perm = jnp.where(r == ld_ref[u], 1.0, 0.0).astype(BF16)
        xo = x1_ref[rs, :] + g2_ref[0] * _dot_tn(perm, seg[slot, u])
        if final_norm:
            xo = _rms(xo, fnw_ref[...]) * (1.0 + scf_ref[0]) + shf_ref[0]
        y_ref[rs, :] = xo


def _moe_final(h2ext, ldest, cnt, wg, wu, wd, x1, mod3, modf3, boff, rows_per_batch, final_norm_w, final_norm):
    t, d = x1.shape
    payload = h2ext.shape[1]
    ts = min(SORT_TILE, rows_per_batch)
    tps = 2 if rows_per_batch % (2 * ts) == 0 else 1
    block = MOE_BLOCK if t >= 8 * MOE_BLOCK else MOE_BLOCK // 2
    n_tiles = t // ts
    ldest = ldest.reshape(n_tiles, 1, ts)
    n_blocks = -(-(t + n_tiles * N_EGROUPS * (SEG_ALIGN - 1) + N_EGROUPS * (block - 1)) // block)
    lo, off, n_pad, blk_group, n_used = _sort_plan(cnt, n_tiles, block, n_blocks)
    ld_spec = pl.BlockSpec((tps, 1, ts), lambda i, *_: (i, 0, 0))

    sorted_rows = pl.pallas_call(
        functools.partial(_sort_kernel, ts=ts, tps=tps),
        grid_spec=pltpu.PrefetchScalarGridSpec(
            num_scalar_prefetch=3,
            grid=(n_tiles // tps,),
            in_specs=[pl.BlockSpec((tps * ts, payload), lambda i, *_: (i, 0)), ld_spec,
                      pl.BlockSpec(memory_space=pl.ANY)],
            out_specs=pl.BlockSpec(memory_space=pl.ANY),
            scratch_shapes=[pltpu.VMEM((2, tps, ts + N_EGROUPS * SEG_ALIGN, payload), BF16),
                            pltpu.SemaphoreType.DMA((2, tps, N_EGROUPS, (ts // SEG_ALIGN).bit_length()))]),
        out_shape=jax.ShapeDtypeStruct((n_blocks * block, payload), BF16),
        input_output_aliases={5: 0},
        compiler_params=_params("arbitrary"),
        name="moe_sort",
    )(lo, off, n_pad, h2ext, ldest, jnp.zeros((n_blocks * block, payload), BF16))

    d_e = wg.shape[2]
    live = lambda b, nb: jnp.minimum(b, nb[0] - 1)
    out_sorted = pl.pallas_call(
        _experts_kernel,
        grid_spec=pltpu.PrefetchScalarGridSpec(
            num_scalar_prefetch=2,
            grid=(n_blocks,),
            in_specs=[pl.BlockSpec((block, payload), lambda b, grp, nb: (live(b, nb), 0)),
                      pl.BlockSpec((EXPERTS_PER_GROUP, d, d_e), lambda b, grp, nb: (grp[b], 0, 0)),
                      pl.BlockSpec((EXPERTS_PER_GROUP, d, d_e), lambda b, grp, nb: (grp[b], 0, 0)),
                      pl.BlockSpec((EXPERTS_PER_GROUP, d_e, d), lambda b, grp, nb: (grp[b], 0, 0))],
            out_specs=pl.BlockSpec((block, d), lambda b, grp, nb: (b, 0))),
        out_shape=jax.ShapeDtypeStruct((n_blocks * block, d), BF16),
        compiler_params=_params("arbitrary"),
        name="moe_experts",
    )(blk_group, n_used, sorted_rows, wg, wu, wd)

    per = rows_per_batch // (tps * ts)
    row = lambda col: pl.BlockSpec((1, 1, d), lambda i, *_, col=col: (i // per + boff, 0, col))
    return pl.pallas_call(
        functools.partial(_combine_kernel, ts=ts, tps=tps, final_norm=final_norm),
        grid_spec=pltpu.PrefetchScalarGridSpec(
            num_scalar_prefetch=3,
            grid=(n_tiles // tps,),
            in_specs=[ld_spec,
                      pl.BlockSpec((tps * ts, d), lambda i, *_: (i, 0)),
                      row(5), row(0), row(1),
                      pl.BlockSpec((1, d), lambda i, *_: (0, 0)),
                      pl.BlockSpec(memory_space=pl.ANY)],
            out_specs=pl.BlockSpec((tps * ts, d), lambda i, *_: (i, 0)),
            scratch_shapes=[pltpu.VMEM((2, tps, ts + N_EGROUPS * SEG_ALIGN, d), BF16),
                            pltpu.SemaphoreType.DMA((2, tps, N_EGROUPS, (ts // SEG_ALIGN).bit_length()))]),
        out_shape=jax.ShapeDtypeStruct((t, d), F32),
        compiler_params=_params("arbitrary"),
        name="moe_combine",
    )(lo, off, n_pad, ldest, x1, mod3, modf3, modf3, final_norm_w, out_sorted)


def _layer(x, mod3, modf3, boff, k_prev, v_prev, logf_prev, conv_prev, ssm_prev, p, final_norm_w, final_norm):
    b, l, d = x.shape
    z, xbc, q, k, v, k_b, v_b, sm, smt, conv_new = _inproj(
        x, mod3, boff, p["norm1_w"], p["wz"], p["wx"], p["wq"], p["wk"], p["wv"], p["ws"], p["bs"])
    n_heads_ssd = p["wz"].shape[1] // P_SSD
    y_ssd, ssm_new = _ssd(xbc, z, sm, smt, conv_prev, ssm_prev, p["conv_w"], p["conv_b"], p["a_log"],
                          p["d_skip"], p["ssd_norm_w"])
    n_heads = p["wq"].shape[1] // HD_ATT
    logf_t = smt[:, n_heads_ssd:n_heads_ssd + n_heads, :]
    if k_prev is None:
        p0, k_all, v_all, lf_all = 0, k_b, v_b, logf_t
    else:
        p0 = k_prev.shape[1]
        pad = (-(p0 + l)) % LANES
        zeros = lambda w: jnp.zeros((b, pad, w), BF16)
        k_all = jnp.concatenate([k_prev.astype(BF16), k_b, zeros(k_b.shape[2])], axis=1)
        v_all = jnp.concatenate([v_prev.astype(BF16), v_b, zeros(v_b.shape[2])], axis=1)
        lf_all = jnp.concatenate([jnp.swapaxes(logf_prev, 1, 2), logf_t, jnp.zeros((b, n_heads, pad), F32)], axis=2)
    o_att = _attention(q, k_all, v_all, _forget_cumsum(lf_all), p0)
    x1, h2ext, ldest, cnt = _outproj(y_ssd, o_att, x, mod3, boff, p["attn_norm_w"], p["wo_ssd"], p["wo_att"],
                                     p["norm2_w"], p["wr"], p["br"])
    y = _moe_final(h2ext, ldest, cnt, p["wg"], p["wu"], p["wd"], x1.reshape(b * l, d), mod3, modf3, boff, l,
                   final_norm_w, final_norm)
    return y.reshape(b, l, d), (k, v, jnp.swapaxes(logf_t, 1, 2), conv_new, ssm_new)


def kernel(x_prompt, x_sample, c_prompt, c_sample, cache_k, cache_v, cache_logf, state_conv, state_ssm, norm1_w, w_ada, b_ada, w_in, conv_w, conv_b, dt_bias, a_log, d_skip, ssd_norm_w, f_bias, attn_norm_w, w_out, norm2_w, w_rg, b_rg, w_re, b_re, w_gate, w_up, w_down, final_norm_w, w_ada_f, b_ada_f):
    depth = w_in.shape[0]
    bp, lp, d = x_prompt.shape
    bs = x_sample.shape[0]
    d_conv = conv_w.shape[2]
    d_ssd = ssd_norm_w.shape[1]
    d_att = attn_norm_w.shape[1]
    h_ssd = dt_bias.shape[1]
    h_att = f_bias.shape[1]
    assert h_ssd + h_att == SMALL_W and d_att // HD_ATT == h_att and d_ssd // P_SSD == h_ssd

    c_all = jnp.concatenate([c_prompt, c_sample], axis=0)
    modf3 = _modulation(c_all, w_ada_f, b_ada_f).reshape(bp + bs, 1, 2 * d)
    final_w = final_norm_w.reshape(1, d)

    i0 = d_ssd
    i1 = i0 + d_conv
    i2 = i1 + h_ssd
    i3 = i2 + d_att
    i4 = i3 + d_att
    i5 = i4 + d_att
    yp, ys = x_prompt, x_sample
    outs_p, outs_s = [], []
    for layer in range(depth):
        mod3 = _modulation(c_all, w_ada[layer], b_ada[layer]).reshape(bp + bs, 1, 6 * d)
        wi = w_in[layer]
        w_small = jnp.concatenate([wi[:, i1:i2], wi[:, i5:], jnp.zeros((d, LANES - SMALL_W), F32)], axis=1)
        b_small = jnp.concatenate([dt_bias[layer], f_bias[layer], jnp.zeros((LANES - SMALL_W,), F32)])
        wr = jnp.zeros((ROUTER_ROWS, d), F32)
        wr = wr.at[:N_EGROUPS].set(w_rg[layer].T)
        wr = wr.at[EXPERT_ROW0:EXPERT_ROW0 + N_EXPERTS].set(
            jnp.transpose(w_re[layer], (0, 2, 1)).reshape(N_EXPERTS, d))
        wr_hi, wr_lo = _split2(wr)
        br = jnp.zeros((ROUTER_ROWS,), F32)
        br = br.at[:N_EGROUPS].set(b_rg[layer])
        br = br.at[EXPERT_ROW0:EXPERT_ROW0 + N_EXPERTS].set(b_re[layer].reshape(N_EXPERTS))
        p = dict(
            norm1_w=norm1_w[layer].reshape(1, d),
            wz=wi[:, :i0].astype(BF16), wx=wi[:, i0:i1].astype(BF16),
            wq=(wi[:, i2:i3] * (LOG2E * HD_ATT ** -0.5)).astype(BF16),
            wk=wi[:, i3:i4].astype(BF16), wv=wi[:, i4:i5].astype(BF16),
            ws=w_small.astype(BF16), bs=b_small.reshape(1, LANES),
            conv_w=conv_w[layer], conv_b=conv_b[layer], a_log=a_log[layer], d_skip=d_skip[layer],
            ssd_norm_w=ssd_norm_w[layer], attn_norm_w=attn_norm_w[layer].reshape(1, d_att),
            wo_ssd=w_out[layer][:d_ssd].astype(BF16), wo_att=w_out[layer][d_ssd:].astype(BF16),
            norm2_w=norm2_w[layer].reshape(1, d),
            wr=jnp.concatenate([wr_hi, wr_lo], axis=0), br=br.reshape(ROUTER_ROWS, 1),
            wg=w_gate[layer].astype(BF16), wu=w_up[layer].astype(BF16), wd=w_down[layer].astype(BF16),
        )
        conv0 = jnp.zeros((bp, CONV_W - 1, d_conv), F32)
        ssm0 = jnp.zeros((bp, h_ssd, P_SSD, N_STATE), F32)
        last = layer == depth - 1
        yp, st_p = _layer(yp, mod3, modf3, 0, None, None, None, conv0, ssm0, p, final_w, last)
        ck = cache_k[layer].reshape(bs, -1, d_att)
        cv = cache_v[layer].reshape(bs, -1, d_att)
        ys, st_s = _layer(ys, mod3, modf3, bp, ck, cv, cache_logf[layer], state_conv[layer], state_ssm[layer], p,
                          final_w, last)
        outs_p.append(st_p)
        outs_s.append(st_s)

    def stack(outs, b, l):
        k = jnp.stack([o[0].reshape(b, l, h_att, HD_ATT) for o in outs])
        v = jnp.stack([o[1].reshape(b, l, h_att, HD_ATT) for o in outs])
        return (k, v, jnp.stack([o[2] for o in outs]), jnp.stack([o[3] for o in outs]),
                jnp.stack([o[4] for o in outs]))

    return (yp, ys) + stack(outs_p, bp, lp) + stack(outs_s, bs, x_sample.shape[1])
```

```python
import functools

import jax
import jax.numpy as jnp
import numpy as np
from jax import lax
from jax.experimental import pallas as pl
from jax.experimental.pallas import tpu as pltpu

F32 = jnp.float32
BF16 = jnp.bfloat16

P_SSD = 64
N_STATE = 64
G_SSD = 2
CONV_W = 4
HD_ATT = 64
N_EGROUPS = 4
EXPERTS_PER_GROUP = 4
N_EXPERTS = N_EGROUPS * EXPERTS_PER_GROUP
EPS = 1e-6
NEG_BIG = -1e30

LANES = 128
SEG_ALIGN = 16
SORT_TILE = 256
MOE_BLOCK = 512
SMALL_W = 16
VMEM_LIMIT = 56 * 1024 * 1024


def _params(*sem):
    return pltpu.CompilerParams(dimension_semantics=sem, vmem_limit_bytes=VMEM_LIMIT)


def _split2(x):
    hi = x.astype(BF16)
    lo = (x - hi.astype(F32)).astype(BF16)
    return hi, lo


def _split3(x):
    hi = x.astype(BF16)
    r = x - hi.astype(F32)
    mid = r.astype(BF16)
    lo = (r - mid.astype(F32)).astype(BF16)
    return hi, mid, lo


def _dot(a, b):
    return jnp.dot(a, b, preferred_element_type=F32)


def _dot_nt(a, b):
    return lax.dot_general(a, b, (((1,), (1,)), ((), ())), preferred_element_type=F32)


def _dot_tn(a, b):
    return lax.dot_general(a, b, (((0,), (0,)), ((), ())), preferred_element_type=F32)


def _silu(x):
    h = 0.5 * x
    return h + h * jnp.tanh(h)


def _rms(x, w):
    return x * lax.rsqrt(jnp.mean(x * x, axis=-1, keepdims=True) + EPS) * w


def _pick_tile(n, candidates):
    for c in candidates:
        if n % c == 0:
            return c
    return n


def _mod_kernel(c_ref, w_ref, b_ref, o_ref):
    a = _silu(c_ref[...])
    a_hi, a_lo = _split2(a)
    w_hi, w_lo = _split2(w_ref[...])
    o_ref[...] = _dot(a_hi, w_hi) + _dot(a_lo, w_hi) + _dot(a_hi, w_lo) + b_ref[...]


def _modulation(c, w, b):
    m, d = c.shape
    n = w.shape[1]
    tn = _pick_tile(n, (1024, 512, 256, 128))
    return pl.pallas_call(
        _mod_kernel,
        grid=(n // tn,),
        in_specs=[pl.BlockSpec((m, d), lambda j: (0, 0)),
                  pl.BlockSpec((d, tn), lambda j: (0, j)),
                  pl.BlockSpec((1, tn), lambda j: (0, j))],
        out_specs=pl.BlockSpec((m, tn), lambda j: (0, j)),
        out_shape=jax.ShapeDtypeStruct((m, n), F32),
        compiler_params=_params("parallel"),
        name="adaln_mod",
    )(c, w, b.reshape(1, n))


def _inproj_kernel(x_ref, sh_ref, sc_ref, nw_ref, wz_ref, wx_ref, wq_ref, wk_ref, wv_ref, ws_ref, bs_ref,
                   z_ref, xbc_ref, q_ref, k_ref, v_ref, kb_ref, vb_ref, sm_ref, smt_ref, tail_ref):
    l = pl.program_id(1)
    x = x_ref[0]
    h = (_rms(x, nw_ref[...]) * (1.0 + sc_ref[0]) + sh_ref[0]).astype(BF16)
    z_ref[0] = _dot(h, wz_ref[...]).astype(BF16)
    xbc = _dot(h, wx_ref[...])
    xbc_ref[0] = xbc.astype(BF16)
    q_ref[0] = _dot(h, wq_ref[...]).astype(BF16)
    k = _dot(h, wk_ref[...])
    k_ref[0] = k
    kb_ref[0] = k.astype(BF16)
    v = _dot(h, wv_ref[...])
    v_ref[0] = v
    vb_ref[0] = v.astype(BF16)
    s = _dot(h, ws_ref[...]) + bs_ref[...]
    t = jnp.log(1.0 + jnp.exp(-jnp.abs(s)))
    lane = lax.broadcasted_iota(jnp.int32, s.shape, 1)
    s = jnp.where(lane < SMALL_W // 2, jnp.maximum(s, 0.0) + t, jnp.minimum(s, 0.0) - t)
    sm_ref[0] = s[:, :SMALL_W]
    smt_ref[0] = s.T[:SMALL_W, :]

    @pl.when(l == pl.num_programs(1) - 1)
    def _():
        tl = xbc.shape[0]
        tail_ref[0] = xbc[tl - (CONV_W - 1):, :]


def _inproj(x, mod3, boff, norm_w, wz, wx, wq, wk, wv, ws, bs):
    b, l, d = x.shape
    tl = _pick_tile(l, (1024, 512, 256, 128, 64))
    nl = l // tl
    d_ssd, d_conv, d_att = wz.shape[1], wx.shape[1], wq.shape[1]
    row = lambda col: pl.BlockSpec((1, 1, d), lambda i, j, col=col: (i + boff, 0, col))
    full = lambda a: pl.BlockSpec(a.shape, lambda i, j: (0,) * a.ndim)
    tok = lambda w: pl.BlockSpec((1, tl, w), lambda i, j: (i, j, 0))
    out_shape = [
        jax.ShapeDtypeStruct((b, l, d_ssd), BF16),
        jax.ShapeDtypeStruct((b, l, d_conv), BF16),
        jax.ShapeDtypeStruct((b, l, d_att), BF16),
        jax.ShapeDtypeStruct((b, l, d_att), F32),
        jax.ShapeDtypeStruct((b, l, d_att), F32),
        jax.ShapeDtypeStruct((b, l, d_att), BF16),
        jax.ShapeDtypeStruct((b, l, d_att), BF16),
        jax.ShapeDtypeStruct((b, l, SMALL_W), F32),
        jax.ShapeDtypeStruct((b, SMALL_W, l), F32),
        jax.ShapeDtypeStruct((b, CONV_W - 1, d_conv), F32),
    ]
    out_specs = [tok(d_ssd), tok(d_conv), tok(d_att), tok(d_att), tok(d_att), tok(d_att), tok(d_att),
                 tok(SMALL_W), pl.BlockSpec((1, SMALL_W, tl), lambda i, j: (i, 0, j)),
                 pl.BlockSpec((1, CONV_W - 1, d_conv), lambda i, j: (i, 0, 0))]
    return pl.pallas_call(
        _inproj_kernel,
        grid=(b, nl),
        in_specs=[tok(d), row(0), row(1), full(norm_w), full(wz), full(wx), full(wq), full(wk), full(wv),
                  full(ws), full(bs)],
        out_specs=out_specs,
        out_shape=out_shape,
        compiler_params=_params("parallel", "arbitrary"),
        name="inproj",
    )(x, mod3, mod3, norm_w, wz, wx, wq, wk, wv, ws, bs)


LOG2E = 1.4426950408889634
AUG = 3
SPLIT_ROWS = 128


PIECE_COLS = 32


def _aug_select(n_heads):
    pair = 2 * HD_ATT
    sq = np.zeros((n_heads // 2, PIECE_COLS, pair), np.float32)
    sk = np.zeros((n_heads // 2, PIECE_COLS, pair), np.float32)
    one = AUG * n_heads
    for h in range(n_heads):
        slot = ((h ^ 1) % 2) * HD_ATT
        for c in range(AUG):
            sq[h // 2, c * n_heads + h, slot + c] = 1.0
            sq[h // 2, one, slot + AUG + c] = 1.0
            sk[h // 2, one, slot + c] = 1.0
            sk[h // 2, c * n_heads + h, slot + AUG + c] = -1.0
    return sq, sk


def _cumsum_kernel(x_ref, eye_ref, o_ref, carry):
    @pl.when(pl.program_id(1) == 0)
    def _():
        carry[...] = jnp.zeros_like(carry)

    x = x_ref[0]
    h, tc = x.shape
    r = lax.broadcasted_iota(jnp.int32, (tc, tc), 0)
    c = lax.broadcasted_iota(jnp.int32, (tc, tc), 1)
    upper = jnp.where(r <= c, 1.0, 0.0).astype(BF16)
    stack = lambda ps: jnp.concatenate([p.astype(F32) for p in ps], axis=0)
    parts = _dot(stack(_split3(x)).astype(BF16), upper)
    cs = parts[:h] + parts[h:2 * h] + parts[2 * h:] + carry[...]
    carry[...] = cs[:, tc - 1:]
    rows = lax.broadcasted_iota(jnp.int32, (PIECE_COLS - AUG * h, tc), 0)
    pieces = jnp.concatenate([stack(_split3(cs * LOG2E)), jnp.where(rows == 0, 1.0, 0.0)], axis=0)
    o_ref[0] = _dot_tn(pieces.astype(BF16), eye_ref[...]).astype(BF16)


def _forget_cumsum(logf_t):
    b, h, lk = logf_t.shape
    assert AUG * h < PIECE_COLS
    tc = _pick_tile(lk, (1024, 512, 384, 256, 128))
    eye = jnp.asarray(np.eye(PIECE_COLS), BF16)
    return pl.pallas_call(
        _cumsum_kernel,
        grid=(b, lk // tc),
        in_specs=[pl.BlockSpec((1, h, tc), lambda i, j: (i, 0, j)),
                  pl.BlockSpec((PIECE_COLS, PIECE_COLS), lambda i, j: (0, 0))],
        out_specs=pl.BlockSpec((1, tc, PIECE_COLS), lambda i, j: (i, j, 0)),
        out_shape=jax.ShapeDtypeStruct((b, lk, PIECE_COLS), BF16),
        scratch_shapes=[pltpu.VMEM((h, 1), F32)],
        compiler_params=_params("parallel", "arbitrary"),
        name="forget_cumsum",
    )(logf_t, eye)


def _ssd_spread(n_heads):
    e = np.zeros((2 * AUG * n_heads, 2 * n_heads * P_SSD), np.float32)
    for v in range(2):
        for c in range(AUG):
            for h in range(n_heads):
                col = v * n_heads * P_SSD + h * P_SSD
                e[(v * AUG + c) * n_heads + h, col:col + P_SSD] = 1.0
    return e


def _ssd_kernel(xbc_ref, z_ref, sm_ref, smt_ref, cprev_ref, sprev_ref, cw_ref, cb_ref, arow_ref, acol_ref,
                dskip_ref, nw_ref, spread_ref, y_ref, snew_ref, hist, state):
    l = pl.program_id(1)
    q = xbc_ref.shape[1]
    d_ssd = z_ref.shape[2]
    n_heads = d_ssd // P_SSD
    pair = 2 * P_SSD
    hist_rows = hist.shape[0]

    @pl.when(l == 0)
    def _():
        hist[...] = jnp.zeros(hist.shape, F32)
        hist[hist_rows - (CONV_W - 1):, :] = cprev_ref[0]
        state[...] = sprev_ref[0]

    xb = xbc_ref[0]
    r3 = lax.broadcasted_iota(jnp.int32, ((CONV_W - 1) * q, q), 0)
    c3 = lax.broadcasted_iota(jnp.int32, ((CONV_W - 1) * q, q), 1)
    shift = jnp.where(r3 % q - c3 == r3 // q + 1, 1.0, 0.0).astype(BF16)
    shifted = _dot(shift, xb)
    cw = cw_ref[...]
    conv = cb_ref[...] + xb.astype(F32) * cw[CONV_W - 1:CONV_W, :]
    for k in range(CONV_W - 1):
        conv = conv + shifted[k * q:(k + 1) * q, :] * cw[CONV_W - 2 - k:CONV_W - 1 - k, :]
    hrow = lambda j: hist[hist_rows - j:hist_rows - j + 1, :]
    frow = lax.broadcasted_iota(jnp.int32, (hist_rows, 1), 0)
    fix = jnp.zeros((hist_rows, conv.shape[1]), F32)
    for t in range(CONV_W - 1):
        acc = 0.0
        for j in range(1, CONV_W - t):
            acc = acc + hrow(j) * cw[CONV_W - 1 - t - j:CONV_W - t - j, :]
        fix = jnp.where(frow == t, acc, fix)
    conv = jnp.concatenate([conv[:hist_rows] + fix, conv[hist_rows:]], axis=0)
    hist[...] = xb[q - hist_rows:, :].astype(F32)
    act = _silu(conv)

    dt = sm_ref[0][:, :n_heads]
    dt_t = smt_ref[0][:n_heads, :]
    a_row = -jnp.exp(arow_ref[...]) * LOG2E
    a_col = -jnp.exp(acol_ref[...]) * LOG2E
    r = lax.broadcasted_iota(jnp.int32, (q, q), 0)
    c = lax.broadcasted_iota(jnp.int32, (q, q), 1)
    causal = r >= c
    lower = jnp.where(causal, 1.0, 0.0).astype(BF16)
    upper = jnp.where(r <= c, 1.0, 0.0).astype(BF16)
    h0, h1, h2 = _split3(dt * a_row)
    acum = _dot(lower, h0) + _dot(lower, h1) + _dot(lower, h2)
    t0, t1, t2 = _split3(dt_t * a_col)
    acum_t = _dot(t0, upper) + _dot(t1, upper) + _dot(t2, upper)
    pieces = [p.astype(F32) for p in _split3(acum) + _split3(dt)]
    wide = _dot(jnp.concatenate(pieces, axis=1).astype(BF16), spread_ref[...])
    acum_x = wide[:, :d_ssd]
    dt_x = wide[:, d_ssd:]
    a_last = acum[q - 1:q, :]
    e_all = jnp.exp2(a_last)
    e_cum_x = jnp.exp2(acum_x)
    e_end_x = jnp.exp2(acum_x[q - 1:q, :] - acum_x)

    xs = act[:, :d_ssd]
    bm = act[:, d_ssd:d_ssd + G_SSD * N_STATE]
    cm = act[:, d_ssd + G_SSD * N_STATE:]
    bm_b = bm.astype(BF16)
    cm_b = cm.astype(BF16)
    xdt = xs * dt_x
    xdt_b = xdt.astype(BF16)
    xe_b = (xdt * e_end_x).astype(BF16)
    lane = lax.broadcasted_iota(jnp.int32, (1, pair), 1)
    first = lane < P_SSD
    srow = lax.broadcasted_iota(jnp.int32, (pair, 1), 0)
    ys = []
    for p in range(n_heads // 2):
        g = (2 * p * G_SSD) // n_heads
        in_group = first if g == 0 else jnp.logical_not(first)
        sl = slice(p * pair, (p + 1) * pair)
        if (2 * p) % (n_heads // G_SSD) == 0:
            cb = _dot_nt(jnp.where(in_group, cm_b, jnp.zeros_like(cm_b)), bm_b)
        ms = []
        for hh in range(2):
            h = 2 * p + hh
            seg = acum[:, h:h + 1] - acum_t[h:h + 1, :]
            ms.append((cb * jnp.exp2(jnp.where(causal, seg, NEG_BIG))).astype(BF16))
        xp = xdt_b[:, sl]
        zero = jnp.zeros_like(xp)
        y = _dot(jnp.concatenate(ms, axis=1),
                 jnp.concatenate([jnp.where(first, xp, zero), jnp.where(first, zero, xp)], axis=0))
        s_in = state[sl, :]
        y = y + _dot_nt(cm_b, s_in.astype(BF16)) * e_cum_x[:, sl]
        upd = _dot_tn(xe_b[:, sl], bm_b)
        keep = jnp.where(srow < P_SSD, e_all[:, 2 * p:2 * p + 1], e_all[:, 2 * p + 1:2 * p + 2])
        state[sl, :] = s_in * keep + jnp.where(in_group, upd, 0.0)
        ys.append(y)
    y_all = jnp.concatenate(ys, axis=1) + dskip_ref[...] * xs
    yg = y_all * _silu(z_ref[0].astype(F32))
    y_ref[0] = _rms(yg, nw_ref[...]).astype(BF16)
    snew_ref[0] = state[...]


def _ssd(xbc, z, sm, smt, conv_prev, ssm_prev, conv_w, conv_b, a_log, d_skip, norm_w):
    b, l, d_conv = xbc.shape
    d_ssd = z.shape[2]
    n_heads = d_ssd // P_SSD
    hg = n_heads // G_SSD
    q = _pick_tile(l, (256, 128, 64))
    tok = lambda w: pl.BlockSpec((1, q, w), lambda i, j: (i, j, 0))
    full = lambda a: pl.BlockSpec(a.shape, lambda i, j: (0,) * a.ndim)
    a_row = a_log.reshape(1, n_heads)
    a_col = a_log.reshape(n_heads, 1)
    conv_b = conv_b.reshape(1, d_conv)
    d_skip = jnp.repeat(d_skip, P_SSD).reshape(1, d_ssd)
    norm_w = norm_w.reshape(1, d_ssd)
    spread = jnp.asarray(_ssd_spread(n_heads), BF16)
    s4 = ssm_prev.reshape(b, G_SSD, hg * P_SSD, N_STATE)
    s_in = jnp.concatenate([jnp.pad(s4[:, g], ((0, 0), (0, 0), (g * N_STATE, (G_SSD - 1 - g) * N_STATE)))
                            for g in range(G_SSD)], axis=1)
    state_spec = pl.BlockSpec((1, n_heads * P_SSD, G_SSD * N_STATE), lambda i, j: (i, 0, 0))
    y, s_out = pl.pallas_call(
        _ssd_kernel,
        grid=(b, l // q),
        in_specs=[tok(d_conv), tok(d_ssd), tok(SMALL_W), pl.BlockSpec((1, SMALL_W, q), lambda i, j: (i, 0, j)),
                  pl.BlockSpec((1, CONV_W - 1, d_conv), lambda i, j: (i, 0, 0)), state_spec,
                  full(conv_w), full(conv_b), full(a_row), full(a_col), full(d_skip), full(norm_w), full(spread)],
        out_specs=[tok(d_ssd), state_spec],
        out_shape=[jax.ShapeDtypeStruct((b, l, d_ssd), BF16),
                   jax.ShapeDtypeStruct((b, n_heads * P_SSD, G_SSD * N_STATE), F32)],
        scratch_shapes=[pltpu.VMEM((8, d_conv), F32),
                        pltpu.VMEM((n_heads * P_SSD, G_SSD * N_STATE), F32)],
        compiler_params=_params("parallel", "arbitrary"),
        name="ssd",
    )(xbc, z, sm, smt, conv_prev, s_in, conv_w, conv_b, a_row, a_col, d_skip, norm_w, spread)
    s_out = s_out.reshape(b, G_SSD, hg * P_SSD, G_SSD * N_STATE)
    s_new = jnp.concatenate([s_out[:, g, :, g * N_STATE:(g + 1) * N_STATE] for g in range(G_SSD)], axis=1)
    return y, s_new.reshape(b, n_heads, P_SSD, N_STATE)


def _attn_kernel(q_ref, pq_ref, k_ref, pk_ref, v_ref, sq_ref, sk_ref, o_ref, kk_s, m_s, acc_s, *, p0, tq, tk):
    i = pl.program_id(2)
    lane = lax.broadcasted_iota(jnp.int32, (1, 2 * HD_ATT), 1)
    first = lane < HD_ATT
    own = (first, jnp.logical_not(first))
    sum_lane = (HD_ATT, 0)

    @pl.when(i == 0)
    def _():
        k = k_ref[0]
        ka = _dot(pk_ref[0], sk_ref[0]).astype(BF16)
        kk_s[0] = jnp.where(first, k, ka)
        kk_s[1] = jnp.where(first, ka, k)

    q = q_ref[0]
    qa = _dot(pq_ref[0], sq_ref[0]).astype(BF16)
    qq = (jnp.where(first, q, qa), jnp.where(first, qa, q))
    m_s[...] = jnp.full(m_s.shape, NEG_BIG, F32)
    acc_s[...] = jnp.zeros(acc_s.shape, F32)
    nc = tk // LANES
    n_split = 2 if tq % (2 * SPLIT_ROWS) == 0 else 1
    rows = tq // n_split
    aligned = tq == tk and p0 % tk == 0

    def step(j, masked):
        off = pl.multiple_of(j * tk, tk)
        v = v_ref[0, pl.ds(off, tk), :]
        vv = [jnp.where(own[hh], v, jnp.where(lane == sum_lane[hh], 1.0, 0.0).astype(BF16)) for hh in range(2)]
        width = [(r + 1) * rows if masked and aligned else tk for r in range(n_split)]
        logit = [[_dot_nt(qq[hh][r * rows:(r + 1) * rows], kk_s[hh, pl.ds(off, width[r]), :]) for hh in range(2)]
                 for r in range(n_split)]
        for r in range(n_split):
            rs = slice(r * rows, (r + 1) * rows)
            tiles = range(width[r] // LANES)
            if masked and aligned:
                row = lax.broadcasted_iota(jnp.int32, (rows, LANES), 0)
                col = lax.broadcasted_iota(jnp.int32, (rows, LANES), 1)
                seen = [None if (c + 1) * LANES <= r * rows + 1 else col + (c * LANES - r * rows) <= row for c in tiles]
            elif masked:
                q_pos = p0 + i * tq + r * rows + lax.broadcasted_iota(jnp.int32, (rows, LANES), 0)
                k_pos = j * tk + lax.broadcasted_iota(jnp.int32, (rows, LANES), 1)
                seen = [k_pos + c * LANES <= q_pos for c in tiles]
            else:
                seen = [None for _ in tiles]
            for hh in range(2):
                s = logit[r][hh]
                cols = [s[:, c * LANES:(c + 1) * LANES] for c in tiles]
                cols = [x if m is None else jnp.where(m, x, NEG_BIG) for x, m in zip(cols, seen)]
                m_cur = functools.reduce(jnp.maximum, cols)
                m_prev = m_s[hh, rs, :]
                m_new = jnp.maximum(m_prev, jnp.max(m_cur, axis=1, keepdims=True))
                alpha = jnp.exp2(m_prev - m_new)
                p = jnp.concatenate([jnp.exp2((col - m_new).astype(BF16)) for col in cols], axis=1)
                m_s[hh, rs, :] = m_new
                acc_s[hh, rs, :] = alpha * acc_s[hh, rs, :] + _dot(p, vv[hh][:width[r]])

    n_full = (p0 + i * tq + 1) // tk
    n_vis = (p0 + i * tq + tq - 1) // tk + 1

    def full_pair(jj, carry):
        step(2 * jj, False)
        step(2 * jj + 1, False)
        return carry

    def masked_body(j, carry):
        step(j, True)
        return carry

    lax.fori_loop(0, n_full // 2, full_pair, 0)
    odd = n_full % 2 == 1
    both = jnp.logical_and(odd, n_vis > n_full)

    @pl.when(both)
    def _():
        step(n_full - 1, False)
        step(n_full, True)

    pl.when(jnp.logical_and(odd, n_vis <= n_full))(lambda: step(n_full - 1, False))
    lax.fori_loop(n_full + both.astype(jnp.int32), n_vis, masked_body, 0)
    a0 = acc_s[0]
    a1 = acc_s[1]
    inv0 = 1.0 / a0[:, sum_lane[0]:sum_lane[0] + 1]
    inv1 = 1.0 / a1[:, sum_lane[1]:sum_lane[1] + 1]
    o = jnp.where(first, a0 * inv0, a1 * inv1)
    o_ref[0] = o.astype(o_ref.dtype)


def _attention(q, k, v, pieces, p0):
    b, lq, d_att = q.shape
    lk = k.shape[1]
    n_heads = d_att // HD_ATT
    tq = _pick_tile(lq, (512, 256, 128, 64))
    tk = lk if lk <= 1536 else _pick_tile(lk, (512, 256, 128))
    assert p0 % tq == 0 and lk >= p0 + lq
    pair = 2 * HD_ATT
    sq, sk = (jnp.asarray(a, BF16) for a in _aug_select(n_heads))
    q_spec = pl.BlockSpec((1, tq, pair), lambda bi, hp, i: (bi, i, hp))
    kv_spec = pl.BlockSpec((1, lk, pair), lambda bi, hp, i: (bi, 0, hp))
    sel_spec = pl.BlockSpec((1, PIECE_COLS, pair), lambda bi, hp, i: (hp, 0, 0))
    return pl.pallas_call(
        functools.partial(_attn_kernel, p0=p0, tq=tq, tk=tk),
        grid=(b, n_heads // 2, lq // tq),
        in_specs=[q_spec, pl.BlockSpec((1, tq, PIECE_COLS), lambda bi, hp, i: (bi, i + p0 // tq, 0)),
                  kv_spec, pl.BlockSpec((1, lk, PIECE_COLS), lambda bi, hp, i: (bi, 0, 0)), kv_spec,
                  sel_spec, sel_spec],
        out_specs=q_spec,
        out_shape=jax.ShapeDtypeStruct((b, lq, d_att), BF16),
        scratch_shapes=[pltpu.VMEM((2, lk, pair), BF16), pltpu.VMEM((2, tq, LANES), F32),
                        pltpu.VMEM((2, tq, pair), F32)],
        compiler_params=_params("parallel", "parallel", "arbitrary"),
        name="fox_attention",
    )(q, pieces, k, pieces, v, sq, sk)


ROUTER_ROWS = 32
EXPERT_ROW0 = 8


def _outproj_kernel(ys_ref, oa_ref, x_ref, g1_ref, sh_ref, sc_ref, anw_ref, wos_ref, woa_ref, n2w_ref, wr_ref,
                    br_ref, x1_ref, h2_ref, ld_ref, cnt_ref, wt_s, g_s):
    ya = _rms(oa_ref[0].astype(F32), anw_ref[...]).astype(BF16)
    m = _dot(ys_ref[0], wos_ref[...]) + _dot(ya, woa_ref[...])
    x1 = x_ref[0] + g1_ref[0] * m
    x1_ref[0] = x1
    h2 = _rms(x1, n2w_ref[...]) * (1.0 + sc_ref[0]) + sh_ref[0]
    h_hi, h_lo = _split2(h2)
    wr = wr_ref[...]
    p1 = _dot_nt(wr, h_hi)
    p2 = _dot_nt(wr[:ROUTER_ROWS], h_lo)
    logit = p1[:ROUTER_ROWS] + p1[ROUTER_ROWS:] + p2 + br_ref[...]

    lg = [logit[g:g + 1, :] for g in range(N_EGROUPS)]
    gmax = jnp.maximum(jnp.maximum(lg[0], lg[1]), jnp.maximum(lg[2], lg[3]))
    denom = sum(jnp.exp(x - gmax) for x in lg)
    p_sel = 1.0 / denom
    is_g = []
    taken = jnp.zeros_like(gmax) > 1.0
    for g in range(N_EGROUPS):
        hit = (lg[g] == gmax) & jnp.logical_not(taken)
        is_g.append(hit)
        taken = taken | hit
    le = []
    for e in range(EXPERTS_PER_GROUP):
        v = logit[EXPERT_ROW0 + 3 * EXPERTS_PER_GROUP + e:EXPERT_ROW0 + 3 * EXPERTS_PER_GROUP + e + 1, :]
        for g in range(N_EGROUPS - 2, -1, -1):
            r0 = EXPERT_ROW0 + g * EXPERTS_PER_GROUP + e
            v = jnp.where(is_g[g], logit[r0:r0 + 1, :], v)
        le.append(v)
    m1 = jnp.maximum(jnp.maximum(le[0], le[1]), jnp.maximum(le[2], le[3]))
    first = []
    taken = jnp.zeros_like(m1) > 1.0
    for e in range(EXPERTS_PER_GROUP):
        hit = (le[e] == m1) & jnp.logical_not(taken)
        first.append(hit)
        taken = taken | hit
    rest = [jnp.where(first[e], -jnp.inf, le[e]) for e in range(EXPERTS_PER_GROUP)]
    m2 = jnp.maximum(jnp.maximum(rest[0], rest[1]), jnp.maximum(rest[2], rest[3]))
    second = []
    taken = jnp.zeros_like(m1) > 1.0
    for e in range(EXPERTS_PER_GROUP):
        hit = (rest[e] == m2) & jnp.logical_not(taken)
        second.append(hit)
        taken = taken | hit
    e2 = jnp.exp(m2 - m1)
    w_a = p_sel / (1.0 + e2)
    w_b = w_a * e2
    d = x1.shape[1]
    wt_s[...] = jnp.zeros(wt_s.shape, F32)
    for e in range(EXPERTS_PER_GROUP):
        w = jnp.where(first[e], w_a, jnp.where(second[e], w_b, 0.0))
        w_hi = w.astype(BF16).astype(F32)
        wt_s[e:e + 1, :] = w_hi
        wt_s[EXPERTS_PER_GROUP + e:EXPERTS_PER_GROUP + e + 1, :] = w - w_hi
    h2_ref[:, :d] = h_hi
    h2_ref[:, d:] = wt_s[...].T.astype(BF16)

    tl = logit.shape[1]
    ts = min(SORT_TILE, tl)
    g_s[...] = jnp.zeros(g_s.shape, F32)
    for g in range(N_EGROUPS):
        g_s[g:g + 1, :] = jnp.where(is_g[g], 1.0, 0.0)
    r = lax.broadcasted_iota(jnp.int32, (tl, tl), 0)
    c = lax.broadcasted_iota(jnp.int32, (tl, tl), 1)
    same_tile = (r // ts) == (c // ts)
    upper = jnp.where((r <= c) & same_tile, 1.0, 0.0).astype(BF16)
    cum = _dot(g_s[...].astype(BF16), upper)
    lane = lax.broadcasted_iota(jnp.int32, (1, tl), 1)
    crow = lax.broadcasted_iota(jnp.int32, (8, LANES), 0)
    clane = lax.broadcasted_iota(jnp.int32, (8, LANES), 1)
    ldest = -1.0
    cnt = jnp.zeros((8, LANES), F32)
    for g in range(N_EGROUPS):
        ldest = ldest + jnp.where(is_g[g], cum[g:g + 1, :], 0.0)
    lo = [0.0] * (tl // ts)
    for g in range(N_EGROUPS):
        lo_row = jnp.zeros((1, tl), F32)
        for sub in range(tl // ts):
            n = cum[g:g + 1, (sub + 1) * ts - 1:(sub + 1) * ts]
            n_pad = jnp.ceil(n / SEG_ALIGN) * SEG_ALIGN
            lo_row = jnp.where(lane // ts == sub, lo[sub], lo_row)
            cnt = jnp.where((crow == sub) & (clane == g), n_pad, cnt)
            lo[sub] = lo[sub] + n_pad
        ldest = ldest + jnp.where(is_g[g], lo_row, 0.0)
    ld_ref[0] = ldest.astype(jnp.int32)
    cnt_ref[0] = cnt.astype(jnp.int32)


def _outproj(y_ssd, o_att, x, mod3, boff, attn_norm_w, wo_ssd, wo_att, norm2_w, wr, br):
    b, l, d = x.shape
    tl = _pick_tile(l, (512, 256, 128, 64))
    nl = l // tl
    d_ssd, d_att = y_ssd.shape[2], o_att.shape[2]
    row = lambda col: pl.BlockSpec((1, 1, d), lambda i, j, col=col: (i + boff, 0, col))
    full = lambda a: pl.BlockSpec(a.shape, lambda i, j: (0,) * a.ndim)
    tok = lambda w: pl.BlockSpec((1, tl, w), lambda i, j: (i, j, 0))
    return pl.pallas_call(
        _outproj_kernel,
        grid=(b, nl),
        in_specs=[tok(d_ssd), tok(d_att), tok(d), row(2), row(3), row(4), full(attn_norm_w), full(wo_ssd),
                  full(wo_att), full(norm2_w), full(wr), full(br)],
        out_specs=[tok(d), pl.BlockSpec((tl, d + LANES), lambda i, j: (i * nl + j, 0)),
                   pl.BlockSpec((1, 1, tl), lambda i, j: (i * nl + j, 0, 0)),
                   pl.BlockSpec((1, 8, LANES), lambda i, j: (i * nl + j, 0, 0))],
        out_shape=[jax.ShapeDtypeStruct((b, l, d), F32),
                   jax.ShapeDtypeStruct((b * l, d + LANES), BF16),
                   jax.ShapeDtypeStruct((b * nl, 1, tl), jnp.int32),
                   jax.ShapeDtypeStruct((b * nl, 8, LANES), jnp.int32)],
        scratch_shapes=[pltpu.VMEM((LANES, tl), F32), pltpu.VMEM((8, tl), F32)],
        compiler_params=_params("parallel", "parallel"),
        name="outproj_router",
    )(y_ssd, o_att, x, mod3, mod3, mod3, attn_norm_w, wo_ssd, wo_att, norm2_w, wr, br)


def _sort_plan(cnt, n_tiles, block, n_blocks):
    ns = n_tiles // cnt.shape[0]
    n_pad = cnt[:, :ns, :N_EGROUPS].reshape(n_tiles, N_EGROUPS)
    lo = jnp.cumsum(n_pad, axis=1) - n_pad
    region = (jnp.sum(n_pad, axis=0) + block - 1) // block * block
    end = jnp.cumsum(region)
    off = (end - region)[None, :] + jnp.cumsum(n_pad, axis=0) - n_pad
    blk = jnp.arange(n_blocks, dtype=jnp.int32) * block
    blk_group = jnp.minimum(jnp.sum(blk[:, None] >= end[None, :], axis=1), N_EGROUPS - 1).astype(jnp.int32)
    n_used = (end[-1] // block).astype(jnp.int32).reshape(1)
    flat = lambda a: a.astype(jnp.int32).reshape(-1)
    return flat(lo), flat(off), flat(n_pad), blk_group, n_used


def _sort_kernel(lo_ref, off_ref, np_ref, x_ref, ld_ref, init_ref, out_ref, cbuf, sems, *, ts, tps):
    del init_ref
    i = pl.program_id(0)
    slot = i % 2
    rows = ts + N_EGROUPS * SEG_ALIGN
    sizes = [ts >> k for k in range((ts // SEG_ALIGN).bit_length())]

    r = lax.broadcasted_iota(jnp.int32, (rows, ts), 0)
    for u in range(tps):
        perm = jnp.where(r == ld_ref[u], 1.0, 0.0).astype(BF16)
        cbuf[slot, u] = _dot(perm, x_ref[u * ts:(u + 1) * ts, :]).astype(BF16)

    def for_each(step, sl, act):
        for u in range(tps):
            for g in range(N_EGROUPS):
                at = (step * tps + u) * N_EGROUPS + g
                n_pad = np_ref[at]
                for k, size in enumerate(sizes):
                    done = n_pad & ~(2 * size - 1)
                    src = pl.multiple_of(lo_ref[at] + done, SEG_ALIGN)
                    dst = pl.multiple_of(off_ref[at] + done, SEG_ALIGN)
                    cp = pltpu.make_async_copy(cbuf.at[sl, u, pl.ds(src, size)], out_ref.at[pl.ds(dst, size)],
                                               sems.at[sl, u, g, k])
                    pl.when((n_pad & size) != 0)(functools.partial(act, cp))

    for_each(i, slot, lambda cp: cp.start())

    @pl.when(i > 0)
    def _():
        for_each(i - 1, 1 - slot, lambda cp: cp.wait())

    @pl.when(i == pl.num_programs(0) - 1)
    def _():
        for_each(i, slot, lambda cp: cp.wait())


def _experts_kernel(grp_ref, nb_ref, x_ref, wg_ref, wu_ref, wd_ref, o_ref):
    del grp_ref
    b = pl.program_id(0)
    d = o_ref.shape[1]

    @pl.when(b < nb_ref[0])
    def _():
        blk = x_ref[...]
        x = blk[:, :d]
        wp = blk[:, d:].astype(F32)
        acc = jnp.zeros(o_ref.shape, F32)
        for e in range(EXPERTS_PER_GROUP):
            w_e = wp[:, e:e + 1] + wp[:, EXPERTS_PER_GROUP + e:EXPERTS_PER_GROUP + e + 1]
            hid = _silu(_dot(x, wg_ref[e].astype(BF16))) * _dot(x, wu_ref[e].astype(BF16)) * w_e
            acc = acc + _dot(hid.astype(BF16), wd_ref[e].astype(BF16))
        o_ref[...] = acc.astype(BF16)

    @pl.when(b >= nb_ref[0])
    def _():
        o_ref[...] = jnp.zeros(o_ref.shape, BF16)


def _combine_kernel(lo_ref, off_ref, np_ref, ld_ref, x1_ref, g2_ref, shf_ref, scf_ref, fnw_ref, src_ref, y_ref,
                    seg, sems, *, ts, tps, final_norm):
    i = pl.program_id(0)
    slot = i % 2
    sizes = [ts >> k for k in range((ts // SEG_ALIGN).bit_length())]

    def fetch(step, sl, act):
        for u in range(tps):
            for g in range(N_EGROUPS):
                at = (step * tps + u) * N_EGROUPS + g
                n_pad = np_ref[at]
                for k, size in enumerate(sizes):
                    done = n_pad & ~(2 * size - 1)
                    src = pl.multiple_of(off_ref[at] + done, SEG_ALIGN)
                    dst = pl.multiple_of(lo_ref[at] + done, SEG_ALIGN)
                    cp = pltpu.make_async_copy(src_ref.at[pl.ds(src, size)], seg.at[sl, u, pl.ds(dst, size)],
                                               sems.at[sl, u, g, k])
                    pl.when((n_pad & size) != 0)(functools.partial(act, cp))

    @pl.when(i == 0)
    def _():
        seg[...] = jnp.zeros(seg.shape, BF16)
        fetch(0, 0, lambda cp: cp.start())

    @pl.when(i + 1 < pl.num_programs(0))
    def _():
        fetch(i + 1, 1 - slot, lambda cp: cp.start())

    fetch(i, slot, lambda cp: cp.wait())

    r = lax.broadcasted_iota(jnp.int32, (seg.shape[2], ts), 0)
    for u in range(tps):
        rs = slice(u * ts, (u + 1) * ts)
        perm = jnp.where(r == ld_ref[u], 1.0, 0.0).astype(BF16)
        xo = x1_ref[rs, :] + g2_ref[0] * _dot_tn(perm, seg[slot, u])
        if final_norm:
            xo = _rms(xo, fnw_ref[...]) * (1.0 + scf_ref[0]) + shf_ref[0]
        y_ref[rs, :] = xo


def _moe_final(h2ext, ldest, cnt, wg, wu, wd, x1, mod3, modf3, boff, rows_per_batch, final_norm_w, final_norm):
    t, d = x1.shape
    payload = h2ext.shape[1]
    ts = min(SORT_TILE, rows_per_batch)
    tps = 2 if rows_per_batch % (2 * ts) == 0 else 1
    block = MOE_BLOCK if t >= 8 * MOE_BLOCK else MOE_BLOCK // 2
    n_tiles = t // ts
    ldest = ldest.reshape(n_tiles, 1, ts)
    n_blocks = -(-(t + n_tiles * N_EGROUPS * (SEG_ALIGN - 1) + N_EGROUPS * (block - 1)) // block)
    lo, off, n_pad, blk_group, n_used = _sort_plan(cnt, n_tiles, block, n_blocks)
    ld_spec = pl.BlockSpec((tps, 1, ts), lambda i, *_: (i, 0, 0))

    sorted_rows = pl.pallas_call(
        functools.partial(_sort_kernel, ts=ts, tps=tps),
        grid_spec=pltpu.PrefetchScalarGridSpec(
            num_scalar_prefetch=3,
            grid=(n_tiles // tps,),
            in_specs=[pl.BlockSpec((tps * ts, payload), lambda i, *_: (i, 0)), ld_spec,
                      pl.BlockSpec(memory_space=pl.ANY)],
            out_specs=pl.BlockSpec(memory_space=pl.ANY),
            scratch_shapes=[pltpu.VMEM((2, tps, ts + N_EGROUPS * SEG_ALIGN, payload), BF16),
                            pltpu.SemaphoreType.DMA((2, tps, N_EGROUPS, (ts // SEG_ALIGN).bit_length()))]),
        out_shape=jax.ShapeDtypeStruct((n_blocks * block, payload), BF16),
        input_output_aliases={5: 0},
        compiler_params=_params("arbitrary"),
        name="moe_sort",
    )(lo, off, n_pad, h2ext, ldest, jnp.zeros((n_blocks * block, payload), BF16))

    d_e = wg.shape[2]
    once = pl.Buffered(1)
    live = lambda b, nb: jnp.minimum(b, nb[0] - 1)
    out_sorted = pl.pallas_call(
        _experts_kernel,
        grid_spec=pltpu.PrefetchScalarGridSpec(
            num_scalar_prefetch=2,
            grid=(n_blocks,),
            in_specs=[pl.BlockSpec((block, payload), lambda b, grp, nb: (live(b, nb), 0)),
                      pl.BlockSpec((EXPERTS_PER_GROUP, d, d_e), lambda b, grp, nb: (grp[b], 0, 0), once),
                      pl.BlockSpec((EXPERTS_PER_GROUP, d, d_e), lambda b, grp, nb: (grp[b], 0, 0), once),
                      pl.BlockSpec((EXPERTS_PER_GROUP, d_e, d), lambda b, grp, nb: (grp[b], 0, 0), once)],
            out_specs=pl.BlockSpec((block, d), lambda b, grp, nb: (b, 0))),
        out_shape=jax.ShapeDtypeStruct((n_blocks * block, d), BF16),
        compiler_params=_params("arbitrary"),
        name="moe_experts",
    )(blk_group, n_used, sorted_rows, wg, wu, wd)

    per = rows_per_batch // (tps * ts)
    row = lambda col: pl.BlockSpec((1, 1, d), lambda i, *_, col=col: (i // per + boff, 0, col))
    return pl.pallas_call(
        functools.partial(_combine_kernel, ts=ts, tps=tps, final_norm=final_norm),
        grid_spec=pltpu.PrefetchScalarGridSpec(
            num_scalar_prefetch=3,
            grid=(n_tiles // tps,),
            in_specs=[ld_spec,
                      pl.BlockSpec((tps * ts, d), lambda i, *_: (i, 0)),
                      row(5), row(0), row(1),
                      pl.BlockSpec((1, d), lambda i, *_: (0, 0)),
                      pl.BlockSpec(memory_space=pl.ANY)],
            out_specs=pl.BlockSpec((tps * ts, d), lambda i, *_: (i, 0)),
            scratch_shapes=[pltpu.VMEM((2, tps, ts + N_EGROUPS * SEG_ALIGN, d), BF16),
                            pltpu.SemaphoreType.DMA((2, tps, N_EGROUPS, (ts // SEG_ALIGN).bit_length()))]),
        out_shape=jax.ShapeDtypeStruct((t, d), F32),
        compiler_params=_params("arbitrary"),
        name="moe_combine",
    )(lo, off, n_pad, ldest, x1, mod3, modf3, modf3, final_norm_w, out_sorted)


def _layer(x, mod3, modf3, boff, k_prev, v_prev, logf_prev, conv_prev, ssm_prev, p, final_norm_w, final_norm):
    b, l, d = x.shape
    z, xbc, q, k, v, k_b, v_b, sm, smt, conv_new = _inproj(
        x, mod3, boff, p["norm1_w"], p["wz"], p["wx"], p["wq"], p["wk"], p["wv"], p["ws"], p["bs"])
    n_heads_ssd = p["wz"].shape[1] // P_SSD
    y_ssd, ssm_new = _ssd(xbc, z, sm, smt, conv_prev, ssm_prev, p["conv_w"], p["conv_b"], p["a_log"],
                          p["d_skip"], p["ssd_norm_w"])
    n_heads = p["wq"].shape[1] // HD_ATT
    logf_t = smt[:, n_heads_ssd:n_heads_ssd + n_heads, :]
    if k_prev is None:
        p0, k_all, v_all, lf_all = 0, k_b, v_b, logf_t
    else:
        p0 = k_prev.shape[1]
        pad = (-(p0 + l)) % LANES
        zeros = lambda w: jnp.zeros((b, pad, w), BF16)
        k_all = jnp.concatenate([k_prev.astype(BF16), k_b, zeros(k_b.shape[2])], axis=1)
        v_all = jnp.concatenate([v_prev.astype(BF16), v_b, zeros(v_b.shape[2])], axis=1)
        lf_all = jnp.concatenate([jnp.swapaxes(logf_prev, 1, 2), logf_t, jnp.zeros((b, n_heads, pad), F32)], axis=2)
    o_att = _attention(q, k_all, v_all, _forget_cumsum(lf_all), p0)
    x1, h2ext, ldest, cnt = _outproj(y_ssd, o_att, x, mod3, boff, p["attn_norm_w"], p["wo_ssd"], p["wo_att"],
                                     p["norm2_w"], p["wr"], p["br"])
    y = _moe_final(h2ext, ldest, cnt, p["wg"], p["wu"], p["wd"], x1.reshape(b * l, d), mod3, modf3, boff, l,
                   final_norm_w, final_norm)
    return y.reshape(b, l, d), (k, v, jnp.swapaxes(logf_t, 1, 2), conv_new, ssm_new)


def kernel(x_prompt, x_sample, c_prompt, c_sample, cache_k, cache_v, cache_logf, state_conv, state_ssm, norm1_w, w_ada, b_ada, w_in, conv_w, conv_b, dt_bias, a_log, d_skip, ssd_norm_w, f_bias, attn_norm_w, w_out, norm2_w, w_rg, b_rg, w_re, b_re, w_gate, w_up, w_down, final_norm_w, w_ada_f, b_ada_f):
    depth = w_in.shape[0]
    bp, lp, d = x_prompt.shape
    bs = x_sample.shape[0]
    d_conv = conv_w.shape[2]
    d_ssd = ssd_norm_w.shape[1]
    d_att = attn_norm_w.shape[1]
    h_ssd = dt_bias.shape[1]
    h_att = f_bias.shape[1]
    assert h_ssd + h_att == SMALL_W and d_att // HD_ATT == h_att and d_ssd // P_SSD == h_ssd

    c_all = jnp.concatenate([c_prompt, c_sample], axis=0)
    modf3 = _modulation(c_all, w_ada_f, b_ada_f).reshape(bp + bs, 1, 2 * d)
    final_w = final_norm_w.reshape(1, d)

    i0 = d_ssd
    i1 = i0 + d_conv
    i2 = i1 + h_ssd
    i3 = i2 + d_att
    i4 = i3 + d_att
    i5 = i4 + d_att
    yp, ys = x_prompt, x_sample
    outs_p, outs_s = [], []
    for layer in range(depth):
        mod3 = _modulation(c_all, w_ada[layer], b_ada[layer]).reshape(bp + bs, 1, 6 * d)
        wi = w_in[layer]
        w_small = jnp.concatenate([wi[:, i1:i2], wi[:, i5:], jnp.zeros((d, LANES - SMALL_W), F32)], axis=1)
        b_small = jnp.concatenate([dt_bias[layer], f_bias[layer], jnp.zeros((LANES - SMALL_W,), F32)])
        wr = jnp.zeros((ROUTER_ROWS, d), F32)
        wr = wr.at[:N_EGROUPS].set(w_rg[layer].T)
        wr = wr.at[EXPERT_ROW0:EXPERT_ROW0 + N_EXPERTS].set(
            jnp.transpose(w_re[layer], (0, 2, 1)).reshape(N_EXPERTS, d))
        wr_hi, wr_lo = _split2(wr)
        br = jnp.zeros((ROUTER_ROWS,), F32)
        br = br.at[:N_EGROUPS].set(b_rg[layer])
        br = br.at[EXPERT_ROW0:EXPERT_ROW0 + N_EXPERTS].set(b_re[layer].reshape(N_EXPERTS))
        p = dict(
            norm1_w=norm1_w[layer].reshape(1, d),
            wz=wi[:, :i0].astype(BF16), wx=wi[:, i0:i1].astype(BF16),
            wq=(wi[:, i2:i3] * (LOG2E * HD_ATT ** -0.5)).astype(BF16),
            wk=wi[:, i3:i4].astype(BF16), wv=wi[:, i4:i5].astype(BF16),
            ws=w_small.astype(BF16), bs=b_small.reshape(1, LANES),
            conv_w=conv_w[layer], conv_b=conv_b[layer], a_log=a_log[layer], d_skip=d_skip[layer],
            ssd_norm_w=ssd_norm_w[layer], attn_norm_w=attn_norm_w[layer].reshape(1, d_att),
            wo_ssd=w_out[layer][:d_ssd].astype(BF16), wo_att=w_out[layer][d_ssd:].astype(BF16),
            norm2_w=norm2_w[layer].reshape(1, d),
            wr=jnp.concatenate([wr_hi, wr_lo], axis=0), br=br.reshape(ROUTER_ROWS, 1),
            wg=w_gate[layer], wu=w_up[layer], wd=w_down[layer],
        )
        conv0 = jnp.zeros((bp, CONV_W - 1, d_conv), F32)
        ssm0 = jnp.zeros((bp, h_ssd, P_SSD, N_STATE), F32)
        last = layer == depth - 1
        yp, st_p = _layer(yp, mod3, modf3, 0, None, None, None, conv0, ssm0, p, final_w, last)
        ck = cache_k[layer].reshape(bs, -1, d_att)
        cv = cache_v[layer].reshape(bs, -1, d_att)
        ys, st_s = _layer(ys, mod3, modf3, bp, ck, cv, cache_logf[layer], state_conv[layer], state_ssm[layer], p,
                          final_w, last)
        outs_p.append(st_p)
        outs_s.append(st_s)

    def stack(outs, b, l):
        k = jnp.stack([o[0].reshape(b, l, h_att, HD_ATT) for o in outs])
        v = jnp.stack([o[1].reshape(b, l, h_att, HD_ATT) for o in outs])
        return (k, v, jnp.stack([o[2] for o in outs]), jnp.stack([o[3] for o in outs]),
                jnp.stack([o[4] for o in outs]))

    return (yp, ys) + stack(outs_p, bp, lp) + stack(outs_s, bs, x_sample.shape[1])
```

```python
import functools

import jax
import jax.numpy as jnp
import numpy as np
from jax import lax
from jax.experimental import pallas as pl
from jax.experimental.pallas import tpu as pltpu

F32 = jnp.float32
BF16 = jnp.bfloat16

P_SSD = 64
N_STATE = 64
G_SSD = 2
CONV_W = 4
HD_ATT = 64
N_EGROUPS = 4
EXPERTS_PER_GROUP = 4
N_EXPERTS = N_EGROUPS * EXPERTS_PER_GROUP
EPS = 1e-6
NEG_BIG = -1e30

LANES = 128
SUBLANES = 8
SEG_ALIGN = 16
SORT_TILE = 256
MOE_BLOCK = 512
COMMON_ROWS = 64
SMALL_W = 16
VMEM_LIMIT = 56 * 1024 * 1024


def _params(*sem):
    return pltpu.CompilerParams(dimension_semantics=sem, vmem_limit_bytes=VMEM_LIMIT)


def _split2(x):
    hi = x.astype(BF16)
    lo = (x - hi.astype(F32)).astype(BF16)
    return hi, lo


def _split3(x):
    hi = x.astype(BF16)
    r = x - hi.astype(F32)
    mid = r.astype(BF16)
    lo = (r - mid.astype(F32)).astype(BF16)
    return hi, mid, lo


def _dot(a, b):
    return jnp.dot(a, b, preferred_element_type=F32)


def _dot_nt(a, b):
    return lax.dot_general(a, b, (((1,), (1,)), ((), ())), preferred_element_type=F32)


def _dot_tn(a, b):
    return lax.dot_general(a, b, (((0,), (0,)), ((), ())), preferred_element_type=F32)


def _silu(x):
    h = 0.5 * x
    return h + h * jnp.tanh(h)


def _rms(x, w):
    return x * lax.rsqrt(jnp.mean(x * x, axis=-1, keepdims=True) + EPS) * w


def _pick_tile(n, candidates):
    for c in candidates:
        if n % c == 0:
            return c
    return n


def _mod_kernel(c_ref, w_ref, b_ref, o_ref):
    a = _silu(c_ref[...])
    a_hi, a_lo = _split2(a)
    w_hi, w_lo = _split2(w_ref[...])
    o_ref[...] = _dot(a_hi, w_hi) + _dot(a_lo, w_hi) + _dot(a_hi, w_lo) + b_ref[...]


def _modulation(c, w, b):
    m, d = c.shape
    n = w.shape[1]
    tn = _pick_tile(n, (1024, 512, 256, 128))
    return pl.pallas_call(
        _mod_kernel,
        grid=(n // tn,),
        in_specs=[pl.BlockSpec((m, d), lambda j: (0, 0)),
                  pl.BlockSpec((d, tn), lambda j: (0, j)),
                  pl.BlockSpec((1, tn), lambda j: (0, j))],
        out_specs=pl.BlockSpec((m, tn), lambda j: (0, j)),
        out_shape=jax.ShapeDtypeStruct((m, n), F32),
        compiler_params=_params("parallel"),
        name="adaln_mod",
    )(c, w, b.reshape(1, n))


def _inproj_kernel(x_ref, sh_ref, sc_ref, nw_ref, wz_ref, wx_ref, wq_ref, wk_ref, wv_ref, ws_ref, bs_ref,
                   z_ref, xbc_ref, q_ref, k_ref, v_ref, kb_ref, vb_ref, sm_ref, smt_ref, tail_ref):
    l = pl.program_id(1)
    x = x_ref[0]
    h = (_rms(x, nw_ref[...]) * (1.0 + sc_ref[0]) + sh_ref[0]).astype(BF16)
    z_ref[0] = _dot(h, wz_ref[...]).astype(BF16)
    xbc = _dot(h, wx_ref[...])
    xbc_ref[0] = xbc.astype(BF16)
    q_ref[0] = _dot(h, wq_ref[...]).astype(BF16)
    k = _dot(h, wk_ref[...])
    k_ref[0] = k
    kb_ref[0] = k.astype(BF16)
    v = _dot(h, wv_ref[...])
    v_ref[0] = v
    vb_ref[0] = v.astype(BF16)
    s = _dot(h, ws_ref[...]) + bs_ref[...]
    t = jnp.log(1.0 + jnp.exp(-jnp.abs(s)))
    lane = lax.broadcasted_iota(jnp.int32, s.shape, 1)
    s = jnp.where(lane < SMALL_W // 2, jnp.maximum(s, 0.0) + t, jnp.minimum(s, 0.0) - t)
    sm_ref[0] = s[:, :SMALL_W]
    smt_ref[0] = s.T[:SMALL_W, :]

    @pl.when(l == pl.num_programs(1) - 1)
    def _():
        tl = xbc.shape[0]
        tail_ref[0] = xbc[tl - (CONV_W - 1):, :]


def _inproj(x, mod3, boff, norm_w, wz, wx, wq, wk, wv, ws, bs):
    b, l, d = x.shape
    tl = _pick_tile(l, (1024, 512, 256, 128, 64))
    nl = l // tl
    d_ssd, d_conv, d_att = wz.shape[1], wx.shape[1], wq.shape[1]
    row = lambda col: pl.BlockSpec((1, 1, d), lambda i, j, col=col: (i + boff, 0, col))
    full = lambda a: pl.BlockSpec(a.shape, lambda i, j: (0,) * a.ndim)
    tok = lambda w: pl.BlockSpec((1, tl, w), lambda i, j: (i, j, 0))
    out_shape = [
        jax.ShapeDtypeStruct((b, l, d_ssd), BF16),
        jax.ShapeDtypeStruct((b, l, d_conv), BF16),
        jax.ShapeDtypeStruct((b, l, d_att), BF16),
        jax.ShapeDtypeStruct((b, l, d_att), F32),
        jax.ShapeDtypeStruct((b, l, d_att), F32),
        jax.ShapeDtypeStruct((b, l, d_att), BF16),
        jax.ShapeDtypeStruct((b, l, d_att), BF16),
        jax.ShapeDtypeStruct((b, l, SMALL_W), F32),
        jax.ShapeDtypeStruct((b, SMALL_W, l), F32),
        jax.ShapeDtypeStruct((b, CONV_W - 1, d_conv), F32),
    ]
    out_specs = [tok(d_ssd), tok(d_conv), tok(d_att), tok(d_att), tok(d_att), tok(d_att), tok(d_att),
                 tok(SMALL_W), pl.BlockSpec((1, SMALL_W, tl), lambda i, j: (i, 0, j)),
                 pl.BlockSpec((1, CONV_W - 1, d_conv), lambda i, j: (i, 0, 0))]
    return pl.pallas_call(
        _inproj_kernel,
        grid=(b, nl),
        in_specs=[tok(d), row(0), row(1), full(norm_w), full(wz), full(wx), full(wq), full(wk), full(wv),
                  full(ws), full(bs)],
        out_specs=out_specs,
        out_shape=out_shape,
        compiler_params=_params("parallel", "arbitrary"),
        name="inproj",
    )(x, mod3, mod3, norm_w, wz, wx, wq, wk, wv, ws, bs)


LOG2E = 1.4426950408889634
AUG = 3
SPLIT_ROWS = 128


PIECE_COLS = 32


def _aug_select(n_heads):
    pair = 2 * HD_ATT
    sq = np.zeros((n_heads // 2, PIECE_COLS, pair), np.float32)
    sk = np.zeros((n_heads // 2, PIECE_COLS, pair), np.float32)
    one = AUG * n_heads
    for h in range(n_heads):
        slot = ((h ^ 1) % 2) * HD_ATT
        for c in range(AUG):
            sq[h // 2, c * n_heads + h, slot + c] = 1.0
            sq[h // 2, one, slot + AUG + c] = 1.0
            sk[h // 2, one, slot + c] = 1.0
            sk[h // 2, c * n_heads + h, slot + AUG + c] = -1.0
    return sq, sk


def _cumsum_kernel(x_ref, eye_ref, o_ref, carry):
    @pl.when(pl.program_id(1) == 0)
    def _():
        carry[...] = jnp.zeros_like(carry)

    x = x_ref[0]
    h, tc = x.shape
    r = lax.broadcasted_iota(jnp.int32, (tc, tc), 0)
    c = lax.broadcasted_iota(jnp.int32, (tc, tc), 1)
    upper = jnp.where(r <= c, 1.0, 0.0).astype(BF16)
    stack = lambda ps: jnp.concatenate([p.astype(F32) for p in ps], axis=0)
    parts = _dot(stack(_split3(x)).astype(BF16), upper)
    cs = parts[:h] + parts[h:2 * h] + parts[2 * h:] + carry[...]
    carry[...] = cs[:, tc - 1:]
    rows = lax.broadcasted_iota(jnp.int32, (PIECE_COLS - AUG * h, tc), 0)
    pieces = jnp.concatenate([stack(_split3(cs * LOG2E)), jnp.where(rows == 0, 1.0, 0.0)], axis=0)
    o_ref[0] = _dot_tn(pieces.astype(BF16), eye_ref[...]).astype(BF16)


def _forget_cumsum(logf_t):
    b, h, lk = logf_t.shape
    assert AUG * h < PIECE_COLS
    tc = _pick_tile(lk, (1024, 512, 384, 256, 128))
    eye = jnp.asarray(np.eye(PIECE_COLS), BF16)
    return pl.pallas_call(
        _cumsum_kernel,
        grid=(b, lk // tc),
        in_specs=[pl.BlockSpec((1, h, tc), lambda i, j: (i, 0, j)),
                  pl.BlockSpec((PIECE_COLS, PIECE_COLS), lambda i, j: (0, 0))],
        out_specs=pl.BlockSpec((1, tc, PIECE_COLS), lambda i, j: (i, j, 0)),
        out_shape=jax.ShapeDtypeStruct((b, lk, PIECE_COLS), BF16),
        scratch_shapes=[pltpu.VMEM((h, 1), F32)],
        compiler_params=_params("parallel", "arbitrary"),
        name="forget_cumsum",
    )(logf_t, eye)


def _ssd_spread(n_heads):
    e = np.zeros((2 * AUG * n_heads, 2 * n_heads * P_SSD), np.float32)
    for v in range(2):
        for c in range(AUG):
            for h in range(n_heads):
                col = v * n_heads * P_SSD + h * P_SSD
                e[(v * AUG + c) * n_heads + h, col:col + P_SSD] = 1.0
    return e


def _ssd_kernel(xbc_ref, z_ref, sm_ref, smt_ref, cprev_ref, sprev_ref, cw_ref, cb_ref, arow_ref, acol_ref,
                dskip_ref, nw_ref, spread_ref, y_ref, snew_ref, hist, state):
    l = pl.program_id(1)
    q = xbc_ref.shape[1]
    d_ssd = z_ref.shape[2]
    n_heads = d_ssd // P_SSD
    pair = 2 * P_SSD
    hist_rows = hist.shape[0]

    @pl.when(l == 0)
    def _():
        hist[...] = jnp.zeros(hist.shape, F32)
        hist[hist_rows - (CONV_W - 1):, :] = cprev_ref[0]
        state[...] = sprev_ref[0]

    xb = xbc_ref[0]
    r3 = lax.broadcasted_iota(jnp.int32, ((CONV_W - 1) * q, q), 0)
    c3 = lax.broadcasted_iota(jnp.int32, ((CONV_W - 1) * q, q), 1)
    shift = jnp.where(r3 % q - c3 == r3 // q + 1, 1.0, 0.0).astype(BF16)
    shifted = _dot(shift, xb)
    cw = cw_ref[...]
    conv = cb_ref[...] + xb.astype(F32) * cw[CONV_W - 1:CONV_W, :]
    for k in range(CONV_W - 1):
        conv = conv + shifted[k * q:(k + 1) * q, :] * cw[CONV_W - 2 - k:CONV_W - 1 - k, :]
    hrow = lambda j: hist[hist_rows - j:hist_rows - j + 1, :]
    frow = lax.broadcasted_iota(jnp.int32, (hist_rows, 1), 0)
    fix = jnp.zeros((hist_rows, conv.shape[1]), F32)
    for t in range(CONV_W - 1):
        acc = 0.0
        for j in range(1, CONV_W - t):
            acc = acc + hrow(j) * cw[CONV_W - 1 - t - j:CONV_W - t - j, :]
        fix = jnp.where(frow == t, acc, fix)
    conv = jnp.concatenate([conv[:hist_rows] + fix, conv[hist_rows:]], axis=0)
    hist[...] = xb[q - hist_rows:, :].astype(F32)
    act = _silu(conv)

    dt = sm_ref[0][:, :n_heads]
    dt_t = smt_ref[0][:n_heads, :]
    a_row = -jnp.exp(arow_ref[...]) * LOG2E
    a_col = -jnp.exp(acol_ref[...]) * LOG2E
    r = lax.broadcasted_iota(jnp.int32, (q, q), 0)
    c = lax.broadcasted_iota(jnp.int32, (q, q), 1)
    causal = r >= c
    lower = jnp.where(causal, 1.0, 0.0).astype(BF16)
    upper = jnp.where(r <= c, 1.0, 0.0).astype(BF16)
    h0, h1, h2 = _split3(dt * a_row)
    acum = _dot(lower, h0) + _dot(lower, h1) + _dot(lower, h2)
    t0, t1, t2 = _split3(dt_t * a_col)
    acum_t = _dot(t0, upper) + _dot(t1, upper) + _dot(t2, upper)
    pieces = [p.astype(F32) for p in _split3(acum) + _split3(dt)]
    wide = _dot(jnp.concatenate(pieces, axis=1).astype(BF16), spread_ref[...])
    acum_x = wide[:, :d_ssd]
    dt_x = wide[:, d_ssd:]
    a_last = acum[q - 1:q, :]
    e_all = jnp.exp2(a_last)
    e_cum_x = jnp.exp2(acum_x)
    e_end_x = jnp.exp2(acum_x[q - 1:q, :] - acum_x)

    xs = act[:, :d_ssd]
    bm = act[:, d_ssd:d_ssd + G_SSD * N_STATE]
    cm = act[:, d_ssd + G_SSD * N_STATE:]
    bm_b = bm.astype(BF16)
    cm_b = cm.astype(BF16)
    xdt = xs * dt_x
    xdt_b = xdt.astype(BF16)
    xe_b = (xdt * e_end_x).astype(BF16)
    lane = lax.broadcasted_iota(jnp.int32, (1, pair), 1)
    first = lane < P_SSD
    srow = lax.broadcasted_iota(jnp.int32, (pair, 1), 0)
    ys = []
    for p in range(n_heads // 2):
        g = (2 * p * G_SSD) // n_heads
        in_group = first if g == 0 else jnp.logical_not(first)
        sl = slice(p * pair, (p + 1) * pair)
        if (2 * p) % (n_heads // G_SSD) == 0:
            cb = _dot_nt(jnp.where(in_group, cm_b, jnp.zeros_like(cm_b)), bm_b)
        ms = []
        for hh in range(2):
            h = 2 * p + hh
            seg = acum[:, h:h + 1] - acum_t[h:h + 1, :]
            ms.append((cb * jnp.exp2(jnp.where(causal, seg, NEG_BIG))).astype(BF16))
        xp = xdt_b[:, sl]
        zero = jnp.zeros_like(xp)
        y = _dot(jnp.concatenate(ms, axis=1),
                 jnp.concatenate([jnp.where(first, xp, zero), jnp.where(first, zero, xp)], axis=0))
        s_in = state[sl, :]
        y = y + _dot_nt(cm_b, s_in.astype(BF16)) * e_cum_x[:, sl]
        upd = _dot_tn(xe_b[:, sl], bm_b)
        keep = jnp.where(srow < P_SSD, e_all[:, 2 * p:2 * p + 1], e_all[:, 2 * p + 1:2 * p + 2])
        state[sl, :] = s_in * keep + jnp.where(in_group, upd, 0.0)
        ys.append(y)
    y_all = jnp.concatenate(ys, axis=1) + dskip_ref[...] * xs
    yg = y_all * _silu(z_ref[0].astype(F32))
    y_ref[0] = _rms(yg, nw_ref[...]).astype(BF16)
    snew_ref[0] = state[...]


def _ssd(xbc, z, sm, smt, conv_prev, ssm_prev, conv_w, conv_b, a_log, d_skip, norm_w):
    b, l, d_conv = xbc.shape
    d_ssd = z.shape[2]
    n_heads = d_ssd // P_SSD
    hg = n_heads // G_SSD
    q = _pick_tile(l, (256, 128, 64))
    tok = lambda w: pl.BlockSpec((1, q, w), lambda i, j: (i, j, 0))
    full = lambda a: pl.BlockSpec(a.shape, lambda i, j: (0,) * a.ndim)
    a_row = a_log.reshape(1, n_heads)
    a_col = a_log.reshape(n_heads, 1)
    conv_b = conv_b.reshape(1, d_conv)
    d_skip = jnp.repeat(d_skip, P_SSD).reshape(1, d_ssd)
    norm_w = norm_w.reshape(1, d_ssd)
    spread = jnp.asarray(_ssd_spread(n_heads), BF16)
    s4 = ssm_prev.reshape(b, G_SSD, hg * P_SSD, N_STATE)
    s_in = jnp.concatenate([jnp.pad(s4[:, g], ((0, 0), (0, 0), (g * N_STATE, (G_SSD - 1 - g) * N_STATE)))
                            for g in range(G_SSD)], axis=1)
    state_spec = pl.BlockSpec((1, n_heads * P_SSD, G_SSD * N_STATE), lambda i, j: (i, 0, 0))
    y, s_out = pl.pallas_call(
        _ssd_kernel,
        grid=(b, l // q),
        in_specs=[tok(d_conv), tok(d_ssd), tok(SMALL_W), pl.BlockSpec((1, SMALL_W, q), lambda i, j: (i, 0, j)),
                  pl.BlockSpec((1, CONV_W - 1, d_conv), lambda i, j: (i, 0, 0)), state_spec,
                  full(conv_w), full(conv_b), full(a_row), full(a_col), full(d_skip), full(norm_w), full(spread)],
        out_specs=[tok(d_ssd), state_spec],
        out_shape=[jax.ShapeDtypeStruct((b, l, d_ssd), BF16),
                   jax.ShapeDtypeStruct((b, n_heads * P_SSD, G_SSD * N_STATE), F32)],
        scratch_shapes=[pltpu.VMEM((SUBLANES, d_conv), F32),
                        pltpu.VMEM((n_heads * P_SSD, G_SSD * N_STATE), F32)],
        compiler_params=_params("parallel", "arbitrary"),
        name="ssd",
    )(xbc, z, sm, smt, conv_prev, s_in, conv_w, conv_b, a_row, a_col, d_skip, norm_w, spread)
    s_out = s_out.reshape(b, G_SSD, hg * P_SSD, G_SSD * N_STATE)
    s_new = jnp.concatenate([s_out[:, g, :, g * N_STATE:(g + 1) * N_STATE] for g in range(G_SSD)], axis=1)
    return y, s_new.reshape(b, n_heads, P_SSD, N_STATE)


def _attn_kernel(q_ref, pq_ref, k_ref, pk_ref, v_ref, sq_ref, sk_ref, o_ref, kk_s, m_s, acc_s, *, p0, tq, tk):
    i = pl.program_id(2)
    lane = lax.broadcasted_iota(jnp.int32, (1, 2 * HD_ATT), 1)
    first = lane < HD_ATT
    own = (first, jnp.logical_not(first))
    sum_lane = (HD_ATT, 0)

    @pl.when(i == 0)
    def _():
        k = k_ref[0]
        ka = _dot(pk_ref[0], sk_ref[0]).astype(BF16)
        kk_s[0] = jnp.where(first, k, ka)
        kk_s[1] = jnp.where(first, ka, k)

    q = q_ref[0]
    qa = _dot(pq_ref[0], sq_ref[0]).astype(BF16)
    qq = (jnp.where(first, q, qa), jnp.where(first, qa, q))
    m_s[...] = jnp.full(m_s.shape, NEG_BIG, F32)
    acc_s[...] = jnp.zeros(acc_s.shape, F32)
    nc = tk // LANES
    n_split = 2 if tq % (2 * SPLIT_ROWS) == 0 else 1
    rows = tq // n_split
    aligned = tq == tk and p0 % tk == 0

    def step(j, masked):
        off = pl.multiple_of(j * tk, tk)
        v = v_ref[0, pl.ds(off, tk), :]
        vv = [jnp.where(own[hh], v, jnp.where(lane == sum_lane[hh], 1.0, 0.0).astype(BF16)) for hh in range(2)]
        width = [(r + 1) * rows if masked and aligned else tk for r in range(n_split)]
        logit = [[_dot_nt(qq[hh][r * rows:(r + 1) * rows], kk_s[hh, pl.ds(off, width[r]), :]) for hh in range(2)]
                 for r in range(n_split)]
        for r in range(n_split):
            rs = slice(r * rows, (r + 1) * rows)
            tiles = range(width[r] // LANES)
            if masked and aligned:
                row = lax.broadcasted_iota(jnp.int32, (rows, LANES), 0)
                col = lax.broadcasted_iota(jnp.int32, (rows, LANES), 1)
                seen = [None if (c + 1) * LANES <= r * rows + 1 else col + (c * LANES - r * rows) <= row for c in tiles]
            elif masked:
                q_pos = p0 + i * tq + r * rows + lax.broadcasted_iota(jnp.int32, (rows, LANES), 0)
                k_pos = j * tk + lax.broadcasted_iota(jnp.int32, (rows, LANES), 1)
                seen = [k_pos + c * LANES <= q_pos for c in tiles]
            else:
                seen = [None for _ in tiles]
            for hh in range(2):
                s = logit[r][hh]
                cols = [s[:, c * LANES:(c + 1) * LANES] for c in tiles]
                cols = [x if m is None else jnp.where(m, x, NEG_BIG) for x, m in zip(cols, seen)]
                m_cur = functools.reduce(jnp.maximum, cols)
                m_prev = m_s[hh, rs, :]
                m_new = jnp.maximum(m_prev, jnp.max(m_cur, axis=1, keepdims=True))
                alpha = jnp.exp2(m_prev - m_new)
                p = jnp.concatenate([jnp.exp2((col - m_new).astype(BF16)) for col in cols], axis=1)
                m_s[hh, rs, :] = m_new
                acc_s[hh, rs, :] = alpha * acc_s[hh, rs, :] + _dot(p, vv[hh][:width[r]])

    n_full = (p0 + i * tq + 1) // tk
    n_vis = (p0 + i * tq + tq - 1) // tk + 1

    def full_pair(jj, carry):
        step(2 * jj, False)
        step(2 * jj + 1, False)
        return carry

    def masked_body(j, carry):
        step(j, True)
        return carry

    lax.fori_loop(0, n_full // 2, full_pair, 0)
    odd = n_full % 2 == 1
    both = jnp.logical_and(odd, n_vis > n_full)

    @pl.when(both)
    def _():
        step(n_full - 1, False)
        step(n_full, True)

    pl.when(jnp.logical_and(odd, n_vis <= n_full))(lambda: step(n_full - 1, False))
    lax.fori_loop(n_full + both.astype(jnp.int32), n_vis, masked_body, 0)
    a0 = acc_s[0]
    a1 = acc_s[1]
    inv0 = 1.0 / a0[:, sum_lane[0]:sum_lane[0] + 1]
    inv1 = 1.0 / a1[:, sum_lane[1]:sum_lane[1] + 1]
    o = jnp.where(first, a0 * inv0, a1 * inv1)
    o_ref[0] = o.astype(o_ref.dtype)


def _attention(q, k, v, pieces, p0):
    b, lq, d_att = q.shape
    lk = k.shape[1]
    n_heads = d_att // HD_ATT
    tq = _pick_tile(lq, (512, 256, 128, 64))
    tk = lk if lk <= 1536 else _pick_tile(lk, (512, 256, 128))
    assert p0 % tq == 0 and lk >= p0 + lq
    pair = 2 * HD_ATT
    sq, sk = (jnp.asarray(a, BF16) for a in _aug_select(n_heads))
    q_spec = pl.BlockSpec((1, tq, pair), lambda bi, hp, i: (bi, i, hp))
    kv_spec = pl.BlockSpec((1, lk, pair), lambda bi, hp, i: (bi, 0, hp))
    sel_spec = pl.BlockSpec((1, PIECE_COLS, pair), lambda bi, hp, i: (hp, 0, 0))
    return pl.pallas_call(
        functools.partial(_attn_kernel, p0=p0, tq=tq, tk=tk),
        grid=(b, n_heads // 2, lq // tq),
        in_specs=[q_spec, pl.BlockSpec((1, tq, PIECE_COLS), lambda bi, hp, i: (bi, i + p0 // tq, 0)),
                  kv_spec, pl.BlockSpec((1, lk, PIECE_COLS), lambda bi, hp, i: (bi, 0, 0)), kv_spec,
                  sel_spec, sel_spec],
        out_specs=q_spec,
        out_shape=jax.ShapeDtypeStruct((b, lq, d_att), BF16),
        scratch_shapes=[pltpu.VMEM((2, lk, pair), BF16), pltpu.VMEM((2, tq, LANES), F32),
                        pltpu.VMEM((2, tq, pair), F32)],
        compiler_params=_params("parallel", "parallel", "arbitrary"),
        name="fox_attention",
    )(q, pieces, k, pieces, v, sq, sk)


ROUTER_ROWS = 32
EXPERT_ROW0 = 8


def _outproj_kernel(ys_ref, oa_ref, x_ref, g1_ref, sh_ref, sc_ref, anw_ref, wos_ref, woa_ref, n2w_ref, wr_ref,
                    br_ref, x1_ref, h2_ref, ld_ref, cnt_ref, wt_s, g_s):
    ya = _rms(oa_ref[0].astype(F32), anw_ref[...]).astype(BF16)
    m = _dot(ys_ref[0], wos_ref[...]) + _dot(ya, woa_ref[...])
    x1 = x_ref[0] + g1_ref[0] * m
    x1_ref[0] = x1
    h2 = _rms(x1, n2w_ref[...]) * (1.0 + sc_ref[0]) + sh_ref[0]
    h_hi, h_lo = _split2(h2)
    wr = wr_ref[...]
    p1 = _dot_nt(wr, h_hi)
    p2 = _dot_nt(wr[:ROUTER_ROWS], h_lo)
    logit = p1[:ROUTER_ROWS] + p1[ROUTER_ROWS:] + p2 + br_ref[...]

    lg = [logit[g:g + 1, :] for g in range(N_EGROUPS)]
    gmax = jnp.maximum(jnp.maximum(lg[0], lg[1]), jnp.maximum(lg[2], lg[3]))
    denom = sum(jnp.exp(x - gmax) for x in lg)
    p_sel = 1.0 / denom
    is_g = []
    taken = jnp.zeros_like(gmax) > 1.0
    for g in range(N_EGROUPS):
        hit = (lg[g] == gmax) & jnp.logical_not(taken)
        is_g.append(hit)
        taken = taken | hit
    le = []
    for e in range(EXPERTS_PER_GROUP):
        v = logit[EXPERT_ROW0 + 3 * EXPERTS_PER_GROUP + e:EXPERT_ROW0 + 3 * EXPERTS_PER_GROUP + e + 1, :]
        for g in range(N_EGROUPS - 2, -1, -1):
            r0 = EXPERT_ROW0 + g * EXPERTS_PER_GROUP + e
            v = jnp.where(is_g[g], logit[r0:r0 + 1, :], v)
        le.append(v)
    m1 = jnp.maximum(jnp.maximum(le[0], le[1]), jnp.maximum(le[2], le[3]))
    first = []
    taken = jnp.zeros_like(m1) > 1.0
    for e in range(EXPERTS_PER_GROUP):
        hit = (le[e] == m1) & jnp.logical_not(taken)
        first.append(hit)
        taken = taken | hit
    rest = [jnp.where(first[e], -jnp.inf, le[e]) for e in range(EXPERTS_PER_GROUP)]
    m2 = jnp.maximum(jnp.maximum(rest[0], rest[1]), jnp.maximum(rest[2], rest[3]))
    second = []
    taken = jnp.zeros_like(m1) > 1.0
    for e in range(EXPERTS_PER_GROUP):
        hit = (rest[e] == m2) & jnp.logical_not(taken)
        second.append(hit)
        taken = taken | hit
    e2 = jnp.exp(m2 - m1)
    w_a = p_sel / (1.0 + e2)
    w_b = w_a * e2
    d = x1.shape[1]
    wt_s[...] = jnp.zeros(wt_s.shape, F32)
    for e in range(EXPERTS_PER_GROUP):
        w = jnp.where(first[e], w_a, jnp.where(second[e], w_b, 0.0))
        w_hi = w.astype(BF16).astype(F32)
        wt_s[e:e + 1, :] = w_hi
        wt_s[EXPERTS_PER_GROUP + e:EXPERTS_PER_GROUP + e + 1, :] = w - w_hi
    h2_ref[:, :d] = h_hi
    h2_ref[:, d:] = wt_s[...].T.astype(BF16)

    tl = logit.shape[1]
    ts = min(SORT_TILE, tl)
    g_s[...] = jnp.zeros(g_s.shape, F32)
    for g in range(N_EGROUPS):
        g_s[g:g + 1, :] = jnp.where(is_g[g], 1.0, 0.0)
    r = lax.broadcasted_iota(jnp.int32, (tl, tl), 0)
    c = lax.broadcasted_iota(jnp.int32, (tl, tl), 1)
    same_tile = (r // ts) == (c // ts)
    upper = jnp.where((r <= c) & same_tile, 1.0, 0.0).astype(BF16)
    cum = _dot(g_s[...].astype(BF16), upper)
    lane = lax.broadcasted_iota(jnp.int32, (1, tl), 1)
    crow = lax.broadcasted_iota(jnp.int32, (SUBLANES, LANES), 0)
    clane = lax.broadcasted_iota(jnp.int32, (SUBLANES, LANES), 1)
    ldest = -1.0
    cnt = jnp.zeros((SUBLANES, LANES), F32)
    for g in range(N_EGROUPS):
        ldest = ldest + jnp.where(is_g[g], cum[g:g + 1, :], 0.0)
    lo = [0.0] * (tl // ts)
    for g in range(N_EGROUPS):
        lo_row = jnp.zeros((1, tl), F32)
        for sub in range(tl // ts):
            n = cum[g:g + 1, (sub + 1) * ts - 1:(sub + 1) * ts]
            n_pad = jnp.ceil(n / SEG_ALIGN) * SEG_ALIGN
            lo_row = jnp.where(lane // ts == sub, lo[sub], lo_row)
            cnt = jnp.where((crow == sub) & (clane == g), n_pad, cnt)
            lo[sub] = lo[sub] + n_pad
        ldest = ldest + jnp.where(is_g[g], lo_row, 0.0)
    ld_ref[0] = ldest.astype(jnp.int32)
    cnt_ref[0] = cnt.astype(jnp.int32)


def _outproj(y_ssd, o_att, x, mod3, boff, attn_norm_w, wo_ssd, wo_att, norm2_w, wr, br):
    b, l, d = x.shape
    tl = _pick_tile(l, (512, 256, 128, 64))
    nl = l // tl
    d_ssd, d_att = y_ssd.shape[2], o_att.shape[2]
    row = lambda col: pl.BlockSpec((1, 1, d), lambda i, j, col=col: (i + boff, 0, col))
    full = lambda a: pl.BlockSpec(a.shape, lambda i, j: (0,) * a.ndim)
    tok = lambda w: pl.BlockSpec((1, tl, w), lambda i, j: (i, j, 0))
    return pl.pallas_call(
        _outproj_kernel,
        grid=(b, nl),
        in_specs=[tok(d_ssd), tok(d_att), tok(d), row(2), row(3), row(4), full(attn_norm_w), full(wo_ssd),
                  full(wo_att), full(norm2_w), full(wr), full(br)],
        out_specs=[tok(d), pl.BlockSpec((tl, d + LANES), lambda i, j: (i * nl + j, 0)),
                   pl.BlockSpec((1, 1, tl), lambda i, j: (i * nl + j, 0, 0)),
                   pl.BlockSpec((1, SUBLANES, LANES), lambda i, j: (i * nl + j, 0, 0))],
        out_shape=[jax.ShapeDtypeStruct((b, l, d), F32),
                   jax.ShapeDtypeStruct((b * l, d + LANES), BF16),
                   jax.ShapeDtypeStruct((b * nl, 1, tl), jnp.int32),
                   jax.ShapeDtypeStruct((b * nl, SUBLANES, LANES), jnp.int32)],
        scratch_shapes=[pltpu.VMEM((LANES, tl), F32), pltpu.VMEM((SUBLANES, tl), F32)],
        compiler_params=_params("parallel", "parallel"),
        name="outproj_router",
    )(y_ssd, o_att, x, mod3, mod3, mod3, attn_norm_w, wo_ssd, wo_att, norm2_w, wr, br)


def _sort_plan(cnt, n_tiles, block, n_blocks):
    ns = n_tiles // cnt.shape[0]
    n_pad = cnt[:, :ns, :N_EGROUPS].reshape(n_tiles, N_EGROUPS)
    lo = jnp.cumsum(n_pad, axis=1) - n_pad
    region = (jnp.sum(n_pad, axis=0) + block - 1) // block * block
    end = jnp.cumsum(region)
    off = (end - region)[None, :] + jnp.cumsum(n_pad, axis=0) - n_pad
    blk = jnp.arange(n_blocks, dtype=jnp.int32) * block
    blk_group = jnp.minimum(jnp.sum(blk[:, None] >= end[None, :], axis=1), N_EGROUPS - 1).astype(jnp.int32)
    n_used = (end[-1] // block).astype(jnp.int32).reshape(1)
    flat = lambda a: a.astype(jnp.int32).reshape(-1)
    return flat(lo), flat(off), flat(n_pad), blk_group, n_used


def _piece_copies(n_pad, sizes, make, act):
    def piece(k, size):
        done = n_pad & ~(2 * size - 1)
        pl.when((n_pad & size) != 0)(lambda: act(make(k, size, done)))

    rare = [(k, size) for k, size in enumerate(sizes) if size > COMMON_ROWS]
    if rare:
        @pl.when(n_pad > COMMON_ROWS)
        def _():
            for k, size in rare:
                piece(k, size)
    for k, size in enumerate(sizes):
        if size <= COMMON_ROWS:
            piece(k, size)


def _sort_kernel(lo_ref, off_ref, np_ref, x_ref, ld_ref, init_ref, out_ref, cbuf, sems, *, ts, tps):
    del init_ref
    i = pl.program_id(0)
    slot = i % 2
    rows = ts + N_EGROUPS * SEG_ALIGN
    sizes = [ts >> k for k in range((ts // SEG_ALIGN).bit_length())]

    r = lax.broadcasted_iota(jnp.int32, (rows, ts), 0)
    for u in range(tps):
        perm = jnp.where(r == ld_ref[u], 1.0, 0.0).astype(BF16)
        cbuf[slot, u] = _dot(perm, x_ref[u * ts:(u + 1) * ts, :]).astype(BF16)

    def for_each(step, sl, act):
        for u in range(tps):
            for g in range(N_EGROUPS):
                at = (step * tps + u) * N_EGROUPS + g

                def make(k, size, done, u=u, g=g, at=at):
                    src = pl.multiple_of(lo_ref[at] + done, SEG_ALIGN)
                    dst = pl.multiple_of(off_ref[at] + done, SEG_ALIGN)
                    return pltpu.make_async_copy(cbuf.at[sl, u, pl.ds(src, size)], out_ref.at[pl.ds(dst, size)],
                                                 sems.at[sl, u, g, k])

                _piece_copies(np_ref[at], sizes, make, act)

    for_each(i, slot, lambda cp: cp.start())

    @pl.when(i > 0)
    def _():
        for_each(i - 1, 1 - slot, lambda cp: cp.wait())

    @pl.when(i == pl.num_programs(0) - 1)
    def _():
        for_each(i, slot, lambda cp: cp.wait())


def _experts_kernel(grp_ref, nb_ref, x_ref, wg_ref, wu_ref, wd_ref, o_ref):
    del grp_ref
    b = pl.program_id(0)
    d = o_ref.shape[1]

    @pl.when(b < nb_ref[0])
    def _():
        blk = x_ref[...]
        x = blk[:, :d]
        wp = blk[:, d:].astype(F32)
        acc = jnp.zeros(o_ref.shape, F32)
        for e in range(EXPERTS_PER_GROUP):
            w_e = wp[:, e:e + 1] + wp[:, EXPERTS_PER_GROUP + e:EXPERTS_PER_GROUP + e + 1]
            hid = _silu(_dot(x, wg_ref[e])) * _dot(x, wu_ref[e]) * w_e
            acc = acc + _dot(hid.astype(BF16), wd_ref[e])
        o_ref[...] = acc.astype(BF16)

    @pl.when(b >= nb_ref[0])
    def _():
        o_ref[...] = jnp.zeros(o_ref.shape, BF16)


def _combine_kernel(lo_ref, off_ref, np_ref, ld_ref, x1_ref, g2_ref, shf_ref, scf_ref, fnw_ref, src_ref, y_ref,
                    seg, sems, *, ts, tps, final_norm):
    i = pl.program_id(0)
    slot = i % 2
    sizes = [ts >> k for k in range((ts // SEG_ALIGN).bit_length())]

    def fetch(step, sl, act):
        for u in range(tps):
            for g in range(N_EGROUPS):
                at = (step * tps + u) * N_EGROUPS + g

                def make(k, size, done, u=u, g=g, at=at):
                    src = pl.multiple_of(off_ref[at] + done, SEG_ALIGN)
                    dst = pl.multiple_of(lo_ref[at] + done, SEG_ALIGN)
                    return pltpu.make_async_copy(src_ref.at[pl.ds(src, size)], seg.at[sl, u, pl.ds(dst, size)],
                                                 sems.at[sl, u, g, k])

                _piece_copies(np_ref[at], sizes, make, act)

    @pl.when(i == 0)
    def _():
        seg[...] = jnp.zeros(seg.shape, BF16)
        fetch(0, 0, lambda cp: cp.start())

    @pl.when(i + 1 < pl.num_programs(0))
    def _():
        fetch(i + 1, 1 - slot, lambda cp: cp.start())

    fetch(i, slot, lambda cp: cp.wait())

    r = lax.broadcasted_iota(jnp.int32, (seg.shape[2], ts), 0)
    for u in range(tps):
        rs = slice(u * ts, (u + 1) * ts)
        perm = jnp.where(r == ld_ref[u], 1.0, 0.0).astype(BF16)
        xo = x1_ref[rs, :] + g2_ref[0] * _dot_tn(perm, seg[slot, u])
        if final_norm:
            xo = _rms(xo, fnw_ref[...]) * (1.0 + scf_ref[0]) + shf_ref[0]
        y_ref[rs, :] = xo


def _moe_final(h2ext, ldest, cnt, wg, wu, wd, x1, mod3, modf3, boff, rows_per_batch, final_norm_w, final_norm):
    t, d = x1.shape
    payload = h2ext.shape[1]
    ts = min(SORT_TILE, rows_per_batch)
    tps = 2 if rows_per_batch % (2 * ts) == 0 else 1
    block = MOE_BLOCK if t >= 8 * MOE_BLOCK else MOE_BLOCK // 2
    n_tiles = t // ts
    ldest = ldest.reshape(n_tiles, 1, ts)
    n_blocks = -(-(t + n_tiles * N_EGROUPS * (SEG_ALIGN - 1) + N_EGROUPS * (block - 1)) // block)
    lo, off, n_pad, blk_group, n_used = _sort_plan(cnt, n_tiles, block, n_blocks)
    ld_spec = pl.BlockSpec((tps, 1, ts), lambda i, *_: (i, 0, 0))

    sorted_rows = pl.pallas_call(
        functools.partial(_sort_kernel, ts=ts, tps=tps),
        grid_spec=pltpu.PrefetchScalarGridSpec(
            num_scalar_prefetch=3,
            grid=(n_tiles // tps,),
            in_specs=[pl.BlockSpec((tps * ts, payload), lambda i, *_: (i, 0)), ld_spec,
                      pl.BlockSpec(memory_space=pl.ANY)],
            out_specs=pl.BlockSpec(memory_space=pl.ANY),
            scratch_shapes=[pltpu.VMEM((2, tps, ts + N_EGROUPS * SEG_ALIGN, payload), BF16),
                            pltpu.SemaphoreType.DMA((2, tps, N_EGROUPS, (ts // SEG_ALIGN).bit_length()))]),
        out_shape=jax.ShapeDtypeStruct((n_blocks * block, payload), BF16),
        input_output_aliases={5: 0},
        compiler_params=_params("arbitrary"),
        name="moe_sort",
    )(lo, off, n_pad, h2ext, ldest, jnp.zeros((n_blocks * block, payload), BF16))

    d_e = wg.shape[2]
    live = lambda b, nb: jnp.minimum(b, nb[0] - 1)
    out_sorted = pl.pallas_call(
        _experts_kernel,
        grid_spec=pltpu.PrefetchScalarGridSpec(
            num_scalar_prefetch=2,
            grid=(n_blocks,),
            in_specs=[pl.BlockSpec((block, payload), lambda b, grp, nb: (live(b, nb), 0)),
                      pl.BlockSpec((EXPERTS_PER_GROUP, d, d_e), lambda b, grp, nb: (grp[b], 0, 0)),
                      pl.BlockSpec((EXPERTS_PER_GROUP, d, d_e), lambda b, grp, nb: (grp[b], 0, 0)),
                      pl.BlockSpec((EXPERTS_PER_GROUP, d_e, d), lambda b, grp, nb: (grp[b], 0, 0))],
            out_specs=pl.BlockSpec((block, d), lambda b, grp, nb: (b, 0))),
        out_shape=jax.ShapeDtypeStruct((n_blocks * block, d), BF16),
        compiler_params=_params("arbitrary"),
        name="moe_experts",
    )(blk_group, n_used, sorted_rows, wg, wu, wd)

    per = rows_per_batch // (tps * ts)
    row = lambda col: pl.BlockSpec((1, 1, d), lambda i, *_, col=col: (i // per + boff, 0, col))
    return pl.pallas_call(
        functools.partial(_combine_kernel, ts=ts, tps=tps, final_norm=final_norm),
        grid_spec=pltpu.PrefetchScalarGridSpec(
            num_scalar_prefetch=3,
            grid=(n_tiles // tps,),
            in_specs=[ld_spec,
                      pl.BlockSpec((tps * ts, d), lambda i, *_: (i, 0)),
                      row(5), row(0), row(1),
                      pl.BlockSpec((1, d), lambda i, *_: (0, 0)),
                      pl.BlockSpec(memory_space=pl.ANY)],
            out_specs=pl.BlockSpec((tps * ts, d), lambda i, *_: (i, 0)),
            scratch_shapes=[pltpu.VMEM((2, tps, ts + N_EGROUPS * SEG_ALIGN, d), BF16),
                            pltpu.SemaphoreType.DMA((2, tps, N_EGROUPS, (ts // SEG_ALIGN).bit_length()))]),
        out_shape=jax.ShapeDtypeStruct((t, d), F32),
        compiler_params=_params("arbitrary"),
        name="moe_combine",
    )(lo, off, n_pad, ldest, x1, mod3, modf3, modf3, final_norm_w, out_sorted)


def _layer(x, mod3, modf3, boff, k_prev, v_prev, logf_prev, conv_prev, ssm_prev, p, final_norm_w, final_norm):
    b, l, d = x.shape
    z, xbc, q, k, v, k_b, v_b, sm, smt, conv_new = _inproj(
        x, mod3, boff, p["norm1_w"], p["wz"], p["wx"], p["wq"], p["wk"], p["wv"], p["ws"], p["bs"])
    n_heads_ssd = p["wz"].shape[1] // P_SSD
    y_ssd, ssm_new = _ssd(xbc, z, sm, smt, conv_prev, ssm_prev, p["conv_w"], p["conv_b"], p["a_log"],
                          p["d_skip"], p["ssd_norm_w"])
    n_heads = p["wq"].shape[1] // HD_ATT
    logf_t = smt[:, n_heads_ssd:n_heads_ssd + n_heads, :]
    if k_prev is None:
        p0, k_all, v_all, lf_all = 0, k_b, v_b, logf_t
    else:
        p0 = k_prev.shape[1]
        pad = (-(p0 + l)) % LANES
        zeros = lambda w: jnp.zeros((b, pad, w), BF16)
        k_all = jnp.concatenate([k_prev.astype(BF16), k_b, zeros(k_b.shape[2])], axis=1)
        v_all = jnp.concatenate([v_prev.astype(BF16), v_b, zeros(v_b.shape[2])], axis=1)
        lf_all = jnp.concatenate([jnp.swapaxes(logf_prev, 1, 2), logf_t, jnp.zeros((b, n_heads, pad), F32)], axis=2)
    o_att = _attention(q, k_all, v_all, _forget_cumsum(lf_all), p0)
    x1, h2ext, ldest, cnt = _outproj(y_ssd, o_att, x, mod3, boff, p["attn_norm_w"], p["wo_ssd"], p["wo_att"],
                                     p["norm2_w"], p["wr"], p["br"])
    y = _moe_final(h2ext, ldest, cnt, p["wg"], p["wu"], p["wd"], x1.reshape(b * l, d), mod3, modf3, boff, l,
                   final_norm_w, final_norm)
    return y.reshape(b, l, d), (k, v, jnp.swapaxes(logf_t, 1, 2), conv_new, ssm_new)


def kernel(x_prompt, x_sample, c_prompt, c_sample, cache_k, cache_v, cache_logf, state_conv, state_ssm, norm1_w, w_ada, b_ada, w_in, conv_w, conv_b, dt_bias, a_log, d_skip, ssd_norm_w, f_bias, attn_norm_w, w_out, norm2_w, w_rg, b_rg, w_re, b_re, w_gate, w_up, w_down, final_norm_w, w_ada_f, b_ada_f):
    depth = w_in.shape[0]
    bp, lp, d = x_prompt.shape
    bs = x_sample.shape[0]
    d_conv = conv_w.shape[2]
    d_ssd = ssd_norm_w.shape[1]
    d_att = attn_norm_w.shape[1]
    h_ssd = dt_bias.shape[1]
    h_att = f_bias.shape[1]
    assert h_ssd + h_att == SMALL_W and d_att // HD_ATT == h_att and d_ssd // P_SSD == h_ssd

    c_all = jnp.concatenate([c_prompt, c_sample], axis=0)
    modf3 = _modulation(c_all, w_ada_f, b_ada_f).reshape(bp + bs, 1, 2 * d)
    final_w = final_norm_w.reshape(1, d)

    i0 = d_ssd
    i1 = i0 + d_conv
    i2 = i1 + h_ssd
    i3 = i2 + d_att
    i4 = i3 + d_att
    i5 = i4 + d_att
    yp, ys = x_prompt, x_sample
    outs_p, outs_s = [], []
    for layer in range(depth):
        mod3 = _modulation(c_all, w_ada[layer], b_ada[layer]).reshape(bp + bs, 1, 6 * d)
        wi = w_in[layer]
        w_small = jnp.concatenate([wi[:, i1:i2], wi[:, i5:], jnp.zeros((d, LANES - SMALL_W), F32)], axis=1)
        b_small = jnp.concatenate([dt_bias[layer], f_bias[layer], jnp.zeros((LANES - SMALL_W,), F32)])
        wr = jnp.zeros((ROUTER_ROWS, d), F32)
        wr = wr.at[:N_EGROUPS].set(w_rg[layer].T)
        wr = wr.at[EXPERT_ROW0:EXPERT_ROW0 + N_EXPERTS].set(
            jnp.transpose(w_re[layer], (0, 2, 1)).reshape(N_EXPERTS, d))
        wr_hi, wr_lo = _split2(wr)
        br = jnp.zeros((ROUTER_ROWS,), F32)
        br = br.at[:N_EGROUPS].set(b_rg[layer])
        br = br.at[EXPERT_ROW0:EXPERT_ROW0 + N_EXPERTS].set(b_re[layer].reshape(N_EXPERTS))
        p = dict(
            norm1_w=norm1_w[layer].reshape(1, d),
            wz=wi[:, :i0].astype(BF16), wx=wi[:, i0:i1].astype(BF16),
            wq=(wi[:, i2:i3] * (LOG2E * HD_ATT ** -0.5)).astype(BF16),
            wk=wi[:, i3:i4].astype(BF16), wv=wi[:, i4:i5].astype(BF16),
            ws=w_small.astype(BF16), bs=b_small.reshape(1, LANES),
            conv_w=conv_w[layer], conv_b=conv_b[layer], a_log=a_log[layer], d_skip=d_skip[layer],
            ssd_norm_w=ssd_norm_w[layer], attn_norm_w=attn_norm_w[layer].reshape(1, d_att),
            wo_ssd=w_out[layer][:d_ssd].astype(BF16), wo_att=w_out[layer][d_ssd:].astype(BF16),
            norm2_w=norm2_w[layer].reshape(1, d),
            wr=jnp.concatenate([wr_hi, wr_lo], axis=0), br=br.reshape(ROUTER_ROWS, 1),
            wg=w_gate[layer].astype(BF16), wu=w_up[layer].astype(BF16), wd=w_down[layer].astype(BF16),
        )
        conv0 = jnp.zeros((bp, CONV_W - 1, d_conv), F32)
        ssm0 = jnp.zeros((bp, h_ssd, P_SSD, N_STATE), F32)
        last = layer == depth - 1
        yp, st_p = _layer(yp, mod3, modf3, 0, None, None, None, conv0, ssm0, p, final_w, last)
        ck = cache_k[layer].reshape(bs, -1, d_att)
        cv = cache_v[layer].reshape(bs, -1, d_att)
        ys, st_s = _layer(ys, mod3, modf3, bp, ck, cv, cache_logf[layer], state_conv[layer], state_ssm[layer], p,
                          final_w, last)
        outs_p.append(st_p)
        outs_s.append(st_s)

    def stack(outs, b, l):
        k = jnp.stack([o[0].reshape(b, l, h_att, HD_ATT) for o in outs])
        v = jnp.stack([o[1].reshape(b, l, h_att, HD_ATT) for o in outs])
        return (k, v, jnp.stack([o[2] for o in outs]), jnp.stack([o[3] for o in outs]),
                jnp.stack([o[4] for o in outs]))

    return (yp, ys) + stack(outs_p, bp, lp) + stack(outs_s, bs, x_sample.shape[1])
```

```python
import functools

import jax
import jax.numpy as jnp
import numpy as np
from jax import lax
from jax.experimental import pallas as pl
from jax.experimental.pallas import tpu as pltpu

F32 = jnp.float32
BF16 = jnp.bfloat16

P_SSD = 64
N_STATE = 64
G_SSD = 2
CONV_W = 4
HD_ATT = 64
N_EGROUPS = 4
EXPERTS_PER_GROUP = 4
N_EXPERTS = N_EGROUPS * EXPERTS_PER_GROUP
EPS = 1e-6
NEG_BIG = -1e30

LANES = 128
SUBLANES = 8
SEG_ALIGN = 16
SORT_TILE = 256
MOE_BLOCK = 512
SMALL_INPUT_BLOCKS = 8
SMALL_W = 16
VMEM_LIMIT = 56 * 1024 * 1024


def _params(*sem):
    return pltpu.CompilerParams(dimension_semantics=sem, vmem_limit_bytes=VMEM_LIMIT)


def _split2(x):
    hi = x.astype(BF16)
    lo = (x - hi.astype(F32)).astype(BF16)
    return hi, lo


def _split3(x):
    hi = x.astype(BF16)
    r = x - hi.astype(F32)
    mid = r.astype(BF16)
    lo = (r - mid.astype(F32)).astype(BF16)
    return hi, mid, lo


def _dot(a, b):
    return jnp.dot(a, b, preferred_element_type=F32)


def _dot_nt(a, b):
    return lax.dot_general(a, b, (((1,), (1,)), ((), ())), preferred_element_type=F32)


def _dot_tn(a, b):
    return lax.dot_general(a, b, (((0,), (0,)), ((), ())), preferred_element_type=F32)


def _silu(x):
    h = 0.5 * x
    return h + h * jnp.tanh(h)


def _rms(x, w):
    return x * lax.rsqrt(jnp.mean(x * x, axis=-1, keepdims=True) + EPS) * w


def _pick_tile(n, candidates):
    for c in candidates:
        if n % c == 0:
            return c
    return n


def _mod_kernel(c_ref, w_ref, b_ref, o_ref):
    a = _silu(c_ref[...])
    a_hi, a_lo = _split2(a)
    w_hi, w_lo = _split2(w_ref[...])
    o_ref[...] = _dot(a_hi, w_hi) + _dot(a_lo, w_hi) + _dot(a_hi, w_lo) + b_ref[...]


def _modulation(c, w, b):
    m, d = c.shape
    n = w.shape[1]
    tn = _pick_tile(n, (1024, 512, 256, 128))
    return pl.pallas_call(
        _mod_kernel,
        grid=(n // tn,),
        in_specs=[pl.BlockSpec((m, d), lambda j: (0, 0)),
                  pl.BlockSpec((d, tn), lambda j: (0, j)),
                  pl.BlockSpec((1, tn), lambda j: (0, j))],
        out_specs=pl.BlockSpec((m, tn), lambda j: (0, j)),
        out_shape=jax.ShapeDtypeStruct((m, n), F32),
        compiler_params=_params("parallel"),
        name="adaln_mod",
    )(c, w, b.reshape(1, n))


def _inproj_kernel(x_ref, sh_ref, sc_ref, nw_ref, wz_ref, wx_ref, wq_ref, wk_ref, wv_ref, ws_ref, bs_ref,
                   z_ref, xbc_ref, q_ref, k_ref, v_ref, kb_ref, vb_ref, sm_ref, smt_ref, tail_ref):
    l = pl.program_id(1)
    x = x_ref[0]
    h = (_rms(x, nw_ref[...]) * (1.0 + sc_ref[0]) + sh_ref[0]).astype(BF16)
    z_ref[0] = _dot(h, wz_ref[...]).astype(BF16)
    xbc = _dot(h, wx_ref[...])
    xbc_ref[0] = xbc.astype(BF16)
    q_ref[0] = _dot(h, wq_ref[...]).astype(BF16)
    k = _dot(h, wk_ref[...])
    k_ref[0] = k
    kb_ref[0] = k.astype(BF16)
    v = _dot(h, wv_ref[...])
    v_ref[0] = v
    vb_ref[0] = v.astype(BF16)
    s = _dot(h, ws_ref[...]) + bs_ref[...]
    t = jnp.log(1.0 + jnp.exp(-jnp.abs(s)))
    lane = lax.broadcasted_iota(jnp.int32, s.shape, 1)
    s = jnp.where(lane < SMALL_W // 2, jnp.maximum(s, 0.0) + t, jnp.minimum(s, 0.0) - t)
    sm_ref[0] = s[:, :SMALL_W]
    smt_ref[0] = s.T[:SMALL_W, :]

    @pl.when(l == pl.num_programs(1) - 1)
    def _():
        tl = xbc.shape[0]
        tail_ref[0] = xbc[tl - (CONV_W - 1):, :]


def _inproj(x, mod3, boff, norm_w, wz, wx, wq, wk, wv, ws, bs):
    b, l, d = x.shape
    tl = _pick_tile(l, (1024, 512, 256, 128, 64))
    nl = l // tl
    d_ssd, d_conv, d_att = wz.shape[1], wx.shape[1], wq.shape[1]
    row = lambda col: pl.BlockSpec((1, 1, d), lambda i, j, col=col: (i + boff, 0, col))
    full = lambda a: pl.BlockSpec(a.shape, lambda i, j: (0,) * a.ndim)
    tok = lambda w: pl.BlockSpec((1, tl, w), lambda i, j: (i, j, 0))
    out_shape = [
        jax.ShapeDtypeStruct((b, l, d_ssd), BF16),
        jax.ShapeDtypeStruct((b, l, d_conv), BF16),
        jax.ShapeDtypeStruct((b, l, d_att), BF16),
        jax.ShapeDtypeStruct((b, l, d_att), F32),
        jax.ShapeDtypeStruct((b, l, d_att), F32),
        jax.ShapeDtypeStruct((b, l, d_att), BF16),
        jax.ShapeDtypeStruct((b, l, d_att), BF16),
        jax.ShapeDtypeStruct((b, l, SMALL_W), F32),
        jax.ShapeDtypeStruct((b, SMALL_W, l), F32),
        jax.ShapeDtypeStruct((b, CONV_W - 1, d_conv), F32),
    ]
    out_specs = [tok(d_ssd), tok(d_conv), tok(d_att), tok(d_att), tok(d_att), tok(d_att), tok(d_att),
                 tok(SMALL_W), pl.BlockSpec((1, SMALL_W, tl), lambda i, j: (i, 0, j)),
                 pl.BlockSpec((1, CONV_W - 1, d_conv), lambda i, j: (i, 0, 0))]
    return pl.pallas_call(
        _inproj_kernel,
        grid=(b, nl),
        in_specs=[tok(d), row(0), row(1), full(norm_w), full(wz), full(wx), full(wq), full(wk), full(wv),
                  full(ws), full(bs)],
        out_specs=out_specs,
        out_shape=out_shape,
        compiler_params=_params("parallel", "arbitrary"),
        name="inproj",
    )(x, mod3, mod3, norm_w, wz, wx, wq, wk, wv, ws, bs)


LOG2E = 1.4426950408889634
AUG = 3
SPLIT_ROWS = 128


PIECE_COLS = 32


def _aug_select(n_heads):
    pair = 2 * HD_ATT
    sq = np.zeros((n_heads // 2, PIECE_COLS, pair), np.float32)
    sk = np.zeros((n_heads // 2, PIECE_COLS, pair), np.float32)
    one = AUG * n_heads
    for h in range(n_heads):
        slot = ((h ^ 1) % 2) * HD_ATT
        for c in range(AUG):
            sq[h // 2, c * n_heads + h, slot + c] = 1.0
            sq[h // 2, one, slot + AUG + c] = 1.0
            sk[h // 2, one, slot + c] = 1.0
            sk[h // 2, c * n_heads + h, slot + AUG + c] = -1.0
    return sq, sk


def _cumsum_kernel(x_ref, eye_ref, o_ref, carry):
    @pl.when(pl.program_id(1) == 0)
    def _():
        carry[...] = jnp.zeros_like(carry)

    x = x_ref[0]
    h, tc = x.shape
    r = lax.broadcasted_iota(jnp.int32, (tc, tc), 0)
    c = lax.broadcasted_iota(jnp.int32, (tc, tc), 1)
    upper = jnp.where(r <= c, 1.0, 0.0).astype(BF16)
    stack = lambda ps: jnp.concatenate([p.astype(F32) for p in ps], axis=0)
    parts = _dot(stack(_split3(x)).astype(BF16), upper)
    cs = parts[:h] + parts[h:2 * h] + parts[2 * h:] + carry[...]
    carry[...] = cs[:, tc - 1:]
    rows = lax.broadcasted_iota(jnp.int32, (PIECE_COLS - AUG * h, tc), 0)
    pieces = jnp.concatenate([stack(_split3(cs * LOG2E)), jnp.where(rows == 0, 1.0, 0.0)], axis=0)
    o_ref[0] = _dot_tn(pieces.astype(BF16), eye_ref[...]).astype(BF16)


def _forget_cumsum(logf_t):
    b, h, lk = logf_t.shape
    assert AUG * h < PIECE_COLS
    tc = _pick_tile(lk, (1024, 512, 384, 256, 128))
    eye = jnp.asarray(np.eye(PIECE_COLS), BF16)
    return pl.pallas_call(
        _cumsum_kernel,
        grid=(b, lk // tc),
        in_specs=[pl.BlockSpec((1, h, tc), lambda i, j: (i, 0, j)),
                  pl.BlockSpec((PIECE_COLS, PIECE_COLS), lambda i, j: (0, 0))],
        out_specs=pl.BlockSpec((1, tc, PIECE_COLS), lambda i, j: (i, j, 0)),
        out_shape=jax.ShapeDtypeStruct((b, lk, PIECE_COLS), BF16),
        scratch_shapes=[pltpu.VMEM((h, 1), F32)],
        compiler_params=_params("parallel", "arbitrary"),
        name="forget_cumsum",
    )(logf_t, eye)


def _ssd_spread(n_heads):
    e = np.zeros((2 * AUG * n_heads, 2 * n_heads * P_SSD), np.float32)
    for v in range(2):
        for c in range(AUG):
            for h in range(n_heads):
                col = v * n_heads * P_SSD + h * P_SSD
                e[(v * AUG + c) * n_heads + h, col:col + P_SSD] = 1.0
    return e


def _ssd_kernel(xbc_ref, z_ref, sm_ref, smt_ref, cprev_ref, sprev_ref, cw_ref, cb_ref, arow_ref, acol_ref,
                dskip_ref, nw_ref, spread_ref, y_ref, snew_ref, hist, state):
    l = pl.program_id(1)
    q = xbc_ref.shape[1]
    d_ssd = z_ref.shape[2]
    n_heads = d_ssd // P_SSD
    pair = 2 * P_SSD
    hist_rows = hist.shape[0]

    @pl.when(l == 0)
    def _():
        hist[...] = jnp.zeros(hist.shape, F32)
        hist[hist_rows - (CONV_W - 1):, :] = cprev_ref[0]
        state[...] = sprev_ref[0]

    xb = xbc_ref[0]
    r3 = lax.broadcasted_iota(jnp.int32, ((CONV_W - 1) * q, q), 0)
    c3 = lax.broadcasted_iota(jnp.int32, ((CONV_W - 1) * q, q), 1)
    shift = jnp.where(r3 % q - c3 == r3 // q + 1, 1.0, 0.0).astype(BF16)
    shifted = _dot(shift, xb)
    cw = cw_ref[...]
    conv = cb_ref[...] + xb.astype(F32) * cw[CONV_W - 1:CONV_W, :]
    for k in range(CONV_W - 1):
        conv = conv + shifted[k * q:(k + 1) * q, :] * cw[CONV_W - 2 - k:CONV_W - 1 - k, :]
    hrow = lambda j: hist[hist_rows - j:hist_rows - j + 1, :]
    frow = lax.broadcasted_iota(jnp.int32, (hist_rows, 1), 0)
    fix = jnp.zeros((hist_rows, conv.shape[1]), F32)
    for t in range(CONV_W - 1):
        acc = 0.0
        for j in range(1, CONV_W - t):
            acc = acc + hrow(j) * cw[CONV_W - 1 - t - j:CONV_W - t - j, :]
        fix = jnp.where(frow == t, acc, fix)
    conv = jnp.concatenate([conv[:hist_rows] + fix, conv[hist_rows:]], axis=0)
    hist[...] = xb[q - hist_rows:, :].astype(F32)
    act = _silu(conv)

    dt = sm_ref[0][:, :n_heads]
    dt_t = smt_ref[0][:n_heads, :]
    a_row = -jnp.exp(arow_ref[...]) * LOG2E
    a_col = -jnp.exp(acol_ref[...]) * LOG2E
    r = lax.broadcasted_iota(jnp.int32, (q, q), 0)
    c = lax.broadcasted_iota(jnp.int32, (q, q), 1)
    causal = r >= c
    lower = jnp.where(causal, 1.0, 0.0).astype(BF16)
    upper = jnp.where(r <= c, 1.0, 0.0).astype(BF16)
    h0, h1, h2 = _split3(dt * a_row)
    acum = _dot(lower, h0) + _dot(lower, h1) + _dot(lower, h2)
    t0, t1, t2 = _split3(dt_t * a_col)
    acum_t = _dot(t0, upper) + _dot(t1, upper) + _dot(t2, upper)
    pieces = [p.astype(F32) for p in _split3(acum) + _split3(dt)]
    wide = _dot(jnp.concatenate(pieces, axis=1).astype(BF16), spread_ref[...])
    acum_x = wide[:, :d_ssd]
    dt_x = wide[:, d_ssd:]
    a_last = acum[q - 1:q, :]
    e_all = jnp.exp2(a_last)
    e_cum_x = jnp.exp2(acum_x)
    e_end_x = jnp.exp2(acum_x[q - 1:q, :] - acum_x)

    xs = act[:, :d_ssd]
    bm = act[:, d_ssd:d_ssd + G_SSD * N_STATE]
    cm = act[:, d_ssd + G_SSD * N_STATE:]
    bm_b = bm.astype(BF16)
    cm_b = cm.astype(BF16)
    xdt = xs * dt_x
    xdt_b = xdt.astype(BF16)
    xe_b = (xdt * e_end_x).astype(BF16)
    lane = lax.broadcasted_iota(jnp.int32, (1, pair), 1)
    first = lane < P_SSD
    srow = lax.broadcasted_iota(jnp.int32, (pair, 1), 0)
    ys = []
    for p in range(n_heads // 2):
        g = (2 * p * G_SSD) // n_heads
        in_group = first if g == 0 else jnp.logical_not(first)
        sl = slice(p * pair, (p + 1) * pair)
        if (2 * p) % (n_heads // G_SSD) == 0:
            cb = _dot_nt(jnp.where(in_group, cm_b, jnp.zeros_like(cm_b)), bm_b)
        ms = []
        for hh in range(2):
            h = 2 * p + hh
            seg = acum[:, h:h + 1] - acum_t[h:h + 1, :]
            ms.append((cb * jnp.exp2(jnp.where(causal, seg, NEG_BIG))).astype(BF16))
        xp = xdt_b[:, sl]
        zero = jnp.zeros_like(xp)
        y = _dot(jnp.concatenate(ms, axis=1),
                 jnp.concatenate([jnp.where(first, xp, zero), jnp.where(first, zero, xp)], axis=0))
        s_in = state[sl, :]
        y = y + _dot_nt(cm_b, s_in.astype(BF16)) * e_cum_x[:, sl]
        upd = _dot_tn(xe_b[:, sl], bm_b)
        keep = jnp.where(srow < P_SSD, e_all[:, 2 * p:2 * p + 1], e_all[:, 2 * p + 1:2 * p + 2])
        state[sl, :] = s_in * keep + jnp.where(in_group, upd, 0.0)
        ys.append(y)
    y_all = jnp.concatenate(ys, axis=1) + dskip_ref[...] * xs
    yg = y_all * _silu(z_ref[0].astype(F32))
    y_ref[0] = _rms(yg, nw_ref[...]).astype(BF16)
    snew_ref[0] = state[...]


def _ssd(xbc, z, sm, smt, conv_prev, ssm_prev, conv_w, conv_b, a_log, d_skip, norm_w):
    b, l, d_conv = xbc.shape
    d_ssd = z.shape[2]
    n_heads = d_ssd // P_SSD
    hg = n_heads // G_SSD
    q = _pick_tile(l, (256, 128, 64))
    tok = lambda w: pl.BlockSpec((1, q, w), lambda i, j: (i, j, 0))
    full = lambda a: pl.BlockSpec(a.shape, lambda i, j: (0,) * a.ndim)
    a_row = a_log.reshape(1, n_heads)
    a_col = a_log.reshape(n_heads, 1)
    conv_b = conv_b.reshape(1, d_conv)
    d_skip = jnp.repeat(d_skip, P_SSD).reshape(1, d_ssd)
    norm_w = norm_w.reshape(1, d_ssd)
    spread = jnp.asarray(_ssd_spread(n_heads), BF16)
    s4 = ssm_prev.reshape(b, G_SSD, hg * P_SSD, N_STATE)
    s_in = jnp.concatenate([jnp.pad(s4[:, g], ((0, 0), (0, 0), (g * N_STATE, (G_SSD - 1 - g) * N_STATE)))
                            for g in range(G_SSD)], axis=1)
    state_spec = pl.BlockSpec((1, n_heads * P_SSD, G_SSD * N_STATE), lambda i, j: (i, 0, 0))
    y, s_out = pl.pallas_call(
        _ssd_kernel,
        grid=(b, l // q),
        in_specs=[tok(d_conv), tok(d_ssd), tok(SMALL_W), pl.BlockSpec((1, SMALL_W, q), lambda i, j: (i, 0, j)),
                  pl.BlockSpec((1, CONV_W - 1, d_conv), lambda i, j: (i, 0, 0)), state_spec,
                  full(conv_w), full(conv_b), full(a_row), full(a_col), full(d_skip), full(norm_w), full(spread)],
        out_specs=[tok(d_ssd), state_spec],
        out_shape=[jax.ShapeDtypeStruct((b, l, d_ssd), BF16),
                   jax.ShapeDtypeStruct((b, n_heads * P_SSD, G_SSD * N_STATE), F32)],
        scratch_shapes=[pltpu.VMEM((SUBLANES, d_conv), F32),
                        pltpu.VMEM((n_heads * P_SSD, G_SSD * N_STATE), F32)],
        compiler_params=_params("parallel", "arbitrary"),
        name="ssd",
    )(xbc, z, sm, smt, conv_prev, s_in, conv_w, conv_b, a_row, a_col, d_skip, norm_w, spread)
    s_out = s_out.reshape(b, G_SSD, hg * P_SSD, G_SSD * N_STATE)
    s_new = jnp.concatenate([s_out[:, g, :, g * N_STATE:(g + 1) * N_STATE] for g in range(G_SSD)], axis=1)
    return y, s_new.reshape(b, n_heads, P_SSD, N_STATE)


def _attn_kernel(q_ref, pq_ref, k_ref, pk_ref, v_ref, sq_ref, sk_ref, o_ref, kk_s, m_s, acc_s, *, p0, tq, tk):
    i = pl.program_id(2)
    lane = lax.broadcasted_iota(jnp.int32, (1, 2 * HD_ATT), 1)
    first = lane < HD_ATT
    own = (first, jnp.logical_not(first))
    sum_lane = (HD_ATT, 0)

    @pl.when(i == 0)
    def _():
        k = k_ref[0]
        ka = _dot(pk_ref[0], sk_ref[0]).astype(BF16)
        kk_s[0] = jnp.where(first, k, ka)
        kk_s[1] = jnp.where(first, ka, k)

    q = q_ref[0]
    qa = _dot(pq_ref[0], sq_ref[0]).astype(BF16)
    qq = (jnp.where(first, q, qa), jnp.where(first, qa, q))
    m_s[...] = jnp.full(m_s.shape, NEG_BIG, F32)
    acc_s[...] = jnp.zeros(acc_s.shape, F32)
    nc = tk // LANES
    n_split = 2 if tq % (2 * SPLIT_ROWS) == 0 else 1
    rows = tq // n_split
    aligned = tq == tk and p0 % tk == 0

    def step(j, masked):
        off = pl.multiple_of(j * tk, tk)
        v = v_ref[0, pl.ds(off, tk), :]
        vv = [jnp.where(own[hh], v, jnp.where(lane == sum_lane[hh], 1.0, 0.0).astype(BF16)) for hh in range(2)]
        width = [(r + 1) * rows if masked and aligned else tk for r in range(n_split)]
        logit = [[_dot_nt(qq[hh][r * rows:(r + 1) * rows], kk_s[hh, pl.ds(off, width[r]), :]) for hh in range(2)]
                 for r in range(n_split)]
        for r in range(n_split):
            rs = slice(r * rows, (r + 1) * rows)
            tiles = range(width[r] // LANES)
            if masked and aligned:
                row = lax.broadcasted_iota(jnp.int32, (rows, LANES), 0)
                col = lax.broadcasted_iota(jnp.int32, (rows, LANES), 1)
                seen = [None if (c + 1) * LANES <= r * rows + 1 else col + (c * LANES - r * rows) <= row for c in tiles]
            elif masked:
                q_pos = p0 + i * tq + r * rows + lax.broadcasted_iota(jnp.int32, (rows, LANES), 0)
                k_pos = j * tk + lax.broadcasted_iota(jnp.int32, (rows, LANES), 1)
                seen = [k_pos + c * LANES <= q_pos for c in tiles]
            else:
                seen = [None for _ in tiles]
            for hh in range(2):
                s = logit[r][hh]
                cols = [s[:, c * LANES:(c + 1) * LANES] for c in tiles]
                cols = [x if m is None else jnp.where(m, x, NEG_BIG) for x, m in zip(cols, seen)]
                m_cur = functools.reduce(jnp.maximum, cols)
                m_prev = m_s[hh, rs, :]
                m_new = jnp.maximum(m_prev, jnp.max(m_cur, axis=1, keepdims=True))
                alpha = jnp.exp2(m_prev - m_new)
                p = jnp.concatenate([jnp.exp2((col - m_new).astype(BF16)) for col in cols], axis=1)
                m_s[hh, rs, :] = m_new
                acc_s[hh, rs, :] = alpha * acc_s[hh, rs, :] + _dot(p, vv[hh][:width[r]])

    n_full = (p0 + i * tq + 1) // tk
    n_vis = (p0 + i * tq + tq - 1) // tk + 1

    def full_pair(jj, carry):
        step(2 * jj, False)
        step(2 * jj + 1, False)
        return carry

    def masked_body(j, carry):
        step(j, True)
        return carry

    lax.fori_loop(0, n_full // 2, full_pair, 0)
    odd = n_full % 2 == 1
    both = jnp.logical_and(odd, n_vis > n_full)

    @pl.when(both)
    def _():
        step(n_full - 1, False)
        step(n_full, True)

    pl.when(jnp.logical_and(odd, n_vis <= n_full))(lambda: step(n_full - 1, False))
    lax.fori_loop(n_full + both.astype(jnp.int32), n_vis, masked_body, 0)
    a0 = acc_s[0]
    a1 = acc_s[1]
    inv0 = 1.0 / a0[:, sum_lane[0]:sum_lane[0] + 1]
    inv1 = 1.0 / a1[:, sum_lane[1]:sum_lane[1] + 1]
    o = jnp.where(first, a0 * inv0, a1 * inv1)
    o_ref[0] = o.astype(o_ref.dtype)


def _attention(q, k, v, pieces, p0):
    b, lq, d_att = q.shape
    lk = k.shape[1]
    n_heads = d_att // HD_ATT
    tq = _pick_tile(lq, (512, 256, 128, 64))
    tk = lk if lk <= 1536 else _pick_tile(lk, (512, 256, 128))
    assert p0 % tq == 0 and lk >= p0 + lq
    pair = 2 * HD_ATT
    sq, sk = (jnp.asarray(a, BF16) for a in _aug_select(n_heads))
    q_spec = pl.BlockSpec((1, tq, pair), lambda bi, hp, i: (bi, i, hp))
    kv_spec = pl.BlockSpec((1, lk, pair), lambda bi, hp, i: (bi, 0, hp))
    sel_spec = pl.BlockSpec((1, PIECE_COLS, pair), lambda bi, hp, i: (hp, 0, 0))
    return pl.pallas_call(
        functools.partial(_attn_kernel, p0=p0, tq=tq, tk=tk),
        grid=(b, n_heads // 2, lq // tq),
        in_specs=[q_spec, pl.BlockSpec((1, tq, PIECE_COLS), lambda bi, hp, i: (bi, i + p0 // tq, 0)),
                  kv_spec, pl.BlockSpec((1, lk, PIECE_COLS), lambda bi, hp, i: (bi, 0, 0)), kv_spec,
                  sel_spec, sel_spec],
        out_specs=q_spec,
        out_shape=jax.ShapeDtypeStruct((b, lq, d_att), BF16),
        scratch_shapes=[pltpu.VMEM((2, lk, pair), BF16), pltpu.VMEM((2, tq, LANES), F32),
                        pltpu.VMEM((2, tq, pair), F32)],
        compiler_params=_params("parallel", "parallel", "arbitrary"),
        name="fox_attention",
    )(q, pieces, k, pieces, v, sq, sk)


ROUTER_ROWS = 32
EXPERT_ROW0 = 8


def _outproj_kernel(ys_ref, oa_ref, x_ref, g1_ref, sh_ref, sc_ref, anw_ref, wos_ref, woa_ref, n2w_ref, wr_ref,
                    br_ref, x1_ref, h2_ref, ld_ref, cnt_ref, wt_s, g_s):
    ya = _rms(oa_ref[0].astype(F32), anw_ref[...]).astype(BF16)
    m = _dot(ys_ref[0], wos_ref[...]) + _dot(ya, woa_ref[...])
    x1 = x_ref[0] + g1_ref[0] * m
    x1_ref[0] = x1
    h2 = _rms(x1, n2w_ref[...]) * (1.0 + sc_ref[0]) + sh_ref[0]
    h_hi, h_lo = _split2(h2)
    wr = wr_ref[...]
    p1 = _dot_nt(wr, h_hi)
    p2 = _dot_nt(wr[:ROUTER_ROWS], h_lo)
    logit = p1[:ROUTER_ROWS] + p1[ROUTER_ROWS:] + p2 + br_ref[...]

    lg = [logit[g:g + 1, :] for g in range(N_EGROUPS)]
    gmax = jnp.maximum(jnp.maximum(lg[0], lg[1]), jnp.maximum(lg[2], lg[3]))
    denom = sum(jnp.exp(x - gmax) for x in lg)
    p_sel = 1.0 / denom
    is_g = []
    taken = jnp.zeros_like(gmax) > 1.0
    for g in range(N_EGROUPS):
        hit = (lg[g] == gmax) & jnp.logical_not(taken)
        is_g.append(hit)
        taken = taken | hit
    le = []
    for e in range(EXPERTS_PER_GROUP):
        v = logit[EXPERT_ROW0 + 3 * EXPERTS_PER_GROUP + e:EXPERT_ROW0 + 3 * EXPERTS_PER_GROUP + e + 1, :]
        for g in range(N_EGROUPS - 2, -1, -1):
            r0 = EXPERT_ROW0 + g * EXPERTS_PER_GROUP + e
            v = jnp.where(is_g[g], logit[r0:r0 + 1, :], v)
        le.append(v)
    m1 = jnp.maximum(jnp.maximum(le[0], le[1]), jnp.maximum(le[2], le[3]))
    first = []
    taken = jnp.zeros_like(m1) > 1.0
    for e in range(EXPERTS_PER_GROUP):
        hit = (le[e] == m1) & jnp.logical_not(taken)
        first.append(hit)
        taken = taken | hit
    rest = [jnp.where(first[e], -jnp.inf, le[e]) for e in range(EXPERTS_PER_GROUP)]
    m2 = jnp.maximum(jnp.maximum(rest[0], rest[1]), jnp.maximum(rest[2], rest[3]))
    second = []
    taken = jnp.zeros_like(m1) > 1.0
    for e in range(EXPERTS_PER_GROUP):
        hit = (rest[e] == m2) & jnp.logical_not(taken)
        second.append(hit)
        taken = taken | hit
    e2 = jnp.exp(m2 - m1)
    w_a = p_sel / (1.0 + e2)
    w_b = w_a * e2
    d = x1.shape[1]
    wt_s[...] = jnp.zeros(wt_s.shape, F32)
    for e in range(EXPERTS_PER_GROUP):
        w = jnp.where(first[e], w_a, jnp.where(second[e], w_b, 0.0))
        w_hi = w.astype(BF16).astype(F32)
        wt_s[e:e + 1, :] = w_hi
        wt_s[EXPERTS_PER_GROUP + e:EXPERTS_PER_GROUP + e + 1, :] = w - w_hi
    h2_ref[:, :d] = h_hi
    h2_ref[:, d:] = wt_s[...].T.astype(BF16)

    tl = logit.shape[1]
    ts = min(SORT_TILE, tl)
    g_s[...] = jnp.zeros(g_s.shape, F32)
    for g in range(N_EGROUPS):
        g_s[g:g + 1, :] = jnp.where(is_g[g], 1.0, 0.0)
    r = lax.broadcasted_iota(jnp.int32, (tl, tl), 0)
    c = lax.broadcasted_iota(jnp.int32, (tl, tl), 1)
    same_tile = (r // ts) == (c // ts)
    upper = jnp.where((r <= c) & same_tile, 1.0, 0.0).astype(BF16)
    cum = _dot(g_s[...].astype(BF16), upper)
    lane = lax.broadcasted_iota(jnp.int32, (1, tl), 1)
    crow = lax.broadcasted_iota(jnp.int32, (SUBLANES, LANES), 0)
    clane = lax.broadcasted_iota(jnp.int32, (SUBLANES, LANES), 1)
    ldest = -1.0
    cnt = jnp.zeros((SUBLANES, LANES), F32)
    for g in range(N_EGROUPS):
        ldest = ldest + jnp.where(is_g[g], cum[g:g + 1, :], 0.0)
    lo = [0.0] * (tl // ts)
    for g in range(N_EGROUPS):
        lo_row = jnp.zeros((1, tl), F32)
        for sub in range(tl // ts):
            n = cum[g:g + 1, (sub + 1) * ts - 1:(sub + 1) * ts]
            n_pad = jnp.ceil(n / SEG_ALIGN) * SEG_ALIGN
            lo_row = jnp.where(lane // ts == sub, lo[sub], lo_row)
            cnt = jnp.where((crow == sub) & (clane == g), n_pad, cnt)
            lo[sub] = lo[sub] + n_pad
        ldest = ldest + jnp.where(is_g[g], lo_row, 0.0)
    ld_ref[0] = ldest.astype(jnp.int32)
    cnt_ref[0] = cnt.astype(jnp.int32)


def _outproj(y_ssd, o_att, x, mod3, boff, attn_norm_w, wo_ssd, wo_att, norm2_w, wr, br):
    b, l, d = x.shape
    tl = _pick_tile(l, (512, 256, 128, 64))
    nl = l // tl
    d_ssd, d_att = y_ssd.shape[2], o_att.shape[2]
    row = lambda col: pl.BlockSpec((1, 1, d), lambda i, j, col=col: (i + boff, 0, col))
    full = lambda a: pl.BlockSpec(a.shape, lambda i, j: (0,) * a.ndim)
    tok = lambda w: pl.BlockSpec((1, tl, w), lambda i, j: (i, j, 0))
    return pl.pallas_call(
        _outproj_kernel,
        grid=(b, nl),
        in_specs=[tok(d_ssd), tok(d_att), tok(d), row(2), row(3), row(4), full(attn_norm_w), full(wo_ssd),
                  full(wo_att), full(norm2_w), full(wr), full(br)],
        out_specs=[tok(d), pl.BlockSpec((tl, d + LANES), lambda i, j: (i * nl + j, 0)),
                   pl.BlockSpec((1, 1, tl), lambda i, j: (i * nl + j, 0, 0)),
                   pl.BlockSpec((1, SUBLANES, LANES), lambda i, j: (i * nl + j, 0, 0))],
        out_shape=[jax.ShapeDtypeStruct((b, l, d), F32),
                   jax.ShapeDtypeStruct((b * l, d + LANES), BF16),
                   jax.ShapeDtypeStruct((b * nl, 1, tl), jnp.int32),
                   jax.ShapeDtypeStruct((b * nl, SUBLANES, LANES), jnp.int32)],
        scratch_shapes=[pltpu.VMEM((LANES, tl), F32), pltpu.VMEM((SUBLANES, tl), F32)],
        compiler_params=_params("parallel", "parallel"),
        name="outproj_router",
    )(y_ssd, o_att, x, mod3, mod3, mod3, attn_norm_w, wo_ssd, wo_att, norm2_w, wr, br)


def _sort_plan(cnt, n_tiles, block, n_blocks):
    ns = n_tiles // cnt.shape[0]
    n_pad = cnt[:, :ns, :N_EGROUPS].reshape(n_tiles, N_EGROUPS)
    lo = jnp.cumsum(n_pad, axis=1) - n_pad
    region = (jnp.sum(n_pad, axis=0) + block - 1) // block * block
    end = jnp.cumsum(region)
    off = (end - region)[None, :] + jnp.cumsum(n_pad, axis=0) - n_pad
    blk = jnp.arange(n_blocks, dtype=jnp.int32) * block
    blk_group = jnp.minimum(jnp.sum(blk[:, None] >= end[None, :], axis=1), N_EGROUPS - 1).astype(jnp.int32)
    n_used = (end[-1] // block).astype(jnp.int32).reshape(1)
    flat = lambda a: a.astype(jnp.int32).reshape(-1)
    return flat(lo), flat(off), flat(n_pad), blk_group, n_used


def _piece_copies(n_pad, sizes, make, act):
    for k, size in enumerate(sizes):
        done = n_pad & ~(2 * size - 1)

        @pl.when((n_pad & size) != 0)
        def _(k=k, size=size, done=done):
            act(make(k, size, done))


def _sort_kernel(lo_ref, off_ref, np_ref, x_ref, ld_ref, init_ref, out_ref, cbuf, sems, *, ts, tps):
    del init_ref
    i = pl.program_id(0)
    slot = i % 2
    rows = ts + N_EGROUPS * SEG_ALIGN
    sizes = [ts >> k for k in range((ts // SEG_ALIGN).bit_length())]

    r = lax.broadcasted_iota(jnp.int32, (rows, ts), 0)
    for u in range(tps):
        perm = jnp.where(r == ld_ref[u], 1.0, 0.0).astype(BF16)
        cbuf[slot, u] = _dot(perm, x_ref[u * ts:(u + 1) * ts, :]).astype(BF16)

    def for_each(step, sl, act):
        for u in range(tps):
            for g in range(N_EGROUPS):
                at = (step * tps + u) * N_EGROUPS + g

                def make(k, size, done, u=u, g=g, at=at):
                    src = pl.multiple_of(lo_ref[at] + done, SEG_ALIGN)
                    dst = pl.multiple_of(off_ref[at] + done, SEG_ALIGN)
                    return pltpu.make_async_copy(cbuf.at[sl, u, pl.ds(src, size)], out_ref.at[pl.ds(dst, size)],
                                                 sems.at[sl, u, g, k])

                _piece_copies(np_ref[at], sizes, make, act)

    for_each(i, slot, lambda cp: cp.start())

    @pl.when(i > 0)
    def _():
        for_each(i - 1, 1 - slot, lambda cp: cp.wait())

    @pl.when(i == pl.num_programs(0) - 1)
    def _():
        for_each(i, slot, lambda cp: cp.wait())


def _experts_kernel(grp_ref, nb_ref, x_ref, wg_ref, wu_ref, wd_ref, o_ref):
    del grp_ref
    b = pl.program_id(0)
    d = o_ref.shape[1]

    @pl.when(b < nb_ref[0])
    def _():
        blk = x_ref[...]
        x = blk[:, :d]
        wp = blk[:, d:].astype(F32)
        acc = jnp.zeros(o_ref.shape, F32)
        for e in range(EXPERTS_PER_GROUP):
            w_e = wp[:, e:e + 1] + wp[:, EXPERTS_PER_GROUP + e:EXPERTS_PER_GROUP + e + 1]
            hid = _silu(_dot(x, wg_ref[e])) * _dot(x, wu_ref[e]) * w_e
            acc = acc + _dot(hid.astype(BF16), wd_ref[e])
        o_ref[...] = acc.astype(BF16)

    @pl.when(b >= nb_ref[0])
    def _():
        o_ref[...] = jnp.zeros(o_ref.shape, BF16)


def _combine_kernel(lo_ref, off_ref, np_ref, ld_ref, x1_ref, g2_ref, shf_ref, scf_ref, fnw_ref, src_ref, y_ref,
                    seg, sems, *, ts, tps, final_norm):
    i = pl.program_id(0)
    slot = i % 2
    sizes = [ts >> k for k in range((ts // SEG_ALIGN).bit_length())]

    def fetch(step, sl, act):
        for u in range(tps):
            for g in range(N_EGROUPS):
                at = (step * tps + u) * N_EGROUPS + g

                def make(k, size, done, u=u, g=g, at=at):
                    src = pl.multiple_of(off_ref[at] + done, SEG_ALIGN)
                    dst = pl.multiple_of(lo_ref[at] + done, SEG_ALIGN)
                    return pltpu.make_async_copy(src_ref.at[pl.ds(src, size)], seg.at[sl, u, pl.ds(dst, size)],
                                                 sems.at[sl, u, g, k])

                _piece_copies(np_ref[at], sizes, make, act)

    @pl.when(i == 0)
    def _():
        seg[...] = jnp.zeros(seg.shape, BF16)
        fetch(0, 0, lambda cp: cp.start())

    @pl.when(i + 1 < pl.num_programs(0))
    def _():
        fetch(i + 1, 1 - slot, lambda cp: cp.start())

    fetch(i, slot, lambda cp: cp.wait())

    r = lax.broadcasted_iota(jnp.int32, (seg.shape[2], ts), 0)
    for u in range(tps):
        rs = slice(u * ts, (u + 1) * ts)
        perm = jnp.where(r == ld_ref[u], 1.0, 0.0).astype(BF16)
        xo = x1_ref[rs, :] + g2_ref[0] * _dot_tn(perm, seg[slot, u])
        if final_norm:
            xo = _rms(xo, fnw_ref[...]) * (1.0 + scf_ref[0]) + shf_ref[0]
        y_ref[rs, :] = xo


def _moe_final(h2ext, ldest, cnt, wg, wu, wd, x1, mod3, modf3, boff, rows_per_batch, final_norm_w, final_norm):
    t, d = x1.shape
    payload = h2ext.shape[1]
    ts = min(SORT_TILE, rows_per_batch)
    tps = 2 if rows_per_batch % (2 * ts) == 0 else 1
    block = MOE_BLOCK if t >= SMALL_INPUT_BLOCKS * MOE_BLOCK else MOE_BLOCK // 2
    n_tiles = t // ts
    ldest = ldest.reshape(n_tiles, 1, ts)
    n_blocks = -(-(t + n_tiles * N_EGROUPS * (SEG_ALIGN - 1) + N_EGROUPS * (block - 1)) // block)
    lo, off, n_pad, blk_group, n_used = _sort_plan(cnt, n_tiles, block, n_blocks)
    ld_spec = pl.BlockSpec((tps, 1, ts), lambda i, *_: (i, 0, 0))

    sorted_rows = pl.pallas_call(
        functools.partial(_sort_kernel, ts=ts, tps=tps),
        grid_spec=pltpu.PrefetchScalarGridSpec(
            num_scalar_prefetch=3,
            grid=(n_tiles // tps,),
            in_specs=[pl.BlockSpec((tps * ts, payload), lambda i, *_: (i, 0)), ld_spec,
                      pl.BlockSpec(memory_space=pl.ANY)],
            out_specs=pl.BlockSpec(memory_space=pl.ANY),
            scratch_shapes=[pltpu.VMEM((2, tps, ts + N_EGROUPS * SEG_ALIGN, payload), BF16),
                            pltpu.SemaphoreType.DMA((2, tps, N_EGROUPS, (ts // SEG_ALIGN).bit_length()))]),
        out_shape=jax.ShapeDtypeStruct((n_blocks * block, payload), BF16),
        input_output_aliases={5: 0},
        compiler_params=_params("arbitrary"),
        name="moe_sort",
    )(lo, off, n_pad, h2ext, ldest, jnp.zeros((n_blocks * block, payload), BF16))

    d_e = wg.shape[2]
    live = lambda b, nb: jnp.minimum(b, nb[0] - 1)
    out_sorted = pl.pallas_call(
        _experts_kernel,
        grid_spec=pltpu.PrefetchScalarGridSpec(
            num_scalar_prefetch=2,
            grid=(n_blocks,),
            in_specs=[pl.BlockSpec((block, payload), lambda b, grp, nb: (live(b, nb), 0)),
                      pl.BlockSpec((EXPERTS_PER_GROUP, d, d_e), lambda b, grp, nb: (grp[b], 0, 0)),
                      pl.BlockSpec((EXPERTS_PER_GROUP, d, d_e), lambda b, grp, nb: (grp[b], 0, 0)),
                      pl.BlockSpec((EXPERTS_PER_GROUP, d_e, d), lambda b, grp, nb: (grp[b], 0, 0))],
            out_specs=pl.BlockSpec((block, d), lambda b, grp, nb: (b, 0))),
        out_shape=jax.ShapeDtypeStruct((n_blocks * block, d), BF16),
        compiler_params=_params("arbitrary"),
        name="moe_experts",
    )(blk_group, n_used, sorted_rows, wg, wu, wd)

    per = rows_per_batch // (tps * ts)
    row = lambda col: pl.BlockSpec((1, 1, d), lambda i, *_, col=col: (i // per + boff, 0, col))
    return pl.pallas_call(
        functools.partial(_combine_kernel, ts=ts, tps=tps, final_norm=final_norm),
        grid_spec=pltpu.PrefetchScalarGridSpec(
            num_scalar_prefetch=3,
            grid=(n_tiles // tps,),
            in_specs=[ld_spec,
                      pl.BlockSpec((tps * ts, d), lambda i, *_: (i, 0)),
                      row(5), row(0), row(1),
                      pl.BlockSpec((1, d), lambda i, *_: (0, 0)),
                      pl.BlockSpec(memory_space=pl.ANY)],
            out_specs=pl.BlockSpec((tps * ts, d), lambda i, *_: (i, 0)),
            scratch_shapes=[pltpu.VMEM((2, tps, ts + N_EGROUPS * SEG_ALIGN, d), BF16),
                            pltpu.SemaphoreType.DMA((2, tps, N_EGROUPS, (ts // SEG_ALIGN).bit_length()))]),
        out_shape=jax.ShapeDtypeStruct((t, d), F32),
        compiler_params=_params("arbitrary"),
        name="moe_combine",
    )(lo, off, n_pad, ldest, x1, mod3, modf3, modf3, final_norm_w, out_sorted)


def _layer(x, mod3, modf3, boff, k_prev, v_prev, logf_prev, conv_prev, ssm_prev, p, final_norm_w, final_norm):
    b, l, d = x.shape
    z, xbc, q, k, v, k_b, v_b, sm, smt, conv_new = _inproj(
        x, mod3, boff, p["norm1_w"], p["wz"], p["wx"], p["wq"], p["wk"], p["wv"], p["ws"], p["bs"])
    n_heads_ssd = p["wz"].shape[1] // P_SSD
    y_ssd, ssm_new = _ssd(xbc, z, sm, smt, conv_prev, ssm_prev, p["conv_w"], p["conv_b"], p["a_log"],
                          p["d_skip"], p["ssd_norm_w"])
    n_heads = p["wq"].shape[1] // HD_ATT
    logf_t = smt[:, n_heads_ssd:n_heads_ssd + n_heads, :]
    if k_prev is None:
        p0, k_all, v_all, lf_all = 0, k_b, v_b, logf_t
    else:
        p0 = k_prev.shape[1]
        pad = (-(p0 + l)) % LANES
        zeros = lambda w: jnp.zeros((b, pad, w), BF16)
        k_all = jnp.concatenate([k_prev.astype(BF16), k_b, zeros(k_b.shape[2])], axis=1)
        v_all = jnp.concatenate([v_prev.astype(BF16), v_b, zeros(v_b.shape[2])], axis=1)
        lf_all = jnp.concatenate([jnp.swapaxes(logf_prev, 1, 2), logf_t, jnp.zeros((b, n_heads, pad), F32)], axis=2)
    o_att = _attention(q, k_all, v_all, _forget_cumsum(lf_all), p0)
    x1, h2ext, ldest, cnt = _outproj(y_ssd, o_att, x, mod3, boff, p["attn_norm_w"], p["wo_ssd"], p["wo_att"],
                                     p["norm2_w"], p["wr"], p["br"])
    y = _moe_final(h2ext, ldest, cnt, p["wg"], p["wu"], p["wd"], x1.reshape(b * l, d), mod3, modf3, boff, l,
                   final_norm_w, final_norm)
    return y.reshape(b, l, d), (k, v, jnp.swapaxes(logf_t, 1, 2), conv_new, ssm_new)


def kernel(x_prompt, x_sample, c_prompt, c_sample, cache_k, cache_v, cache_logf, state_conv, state_ssm, norm1_w, w_ada, b_ada, w_in, conv_w, conv_b, dt_bias, a_log, d_skip, ssd_norm_w, f_bias, attn_norm_w, w_out, norm2_w, w_rg, b_rg, w_re, b_re, w_gate, w_up, w_down, final_norm_w, w_ada_f, b_ada_f):
    depth = w_in.shape[0]
    bp, lp, d = x_prompt.shape
    bs = x_sample.shape[0]
    d_conv = conv_w.shape[2]
    d_ssd = ssd_norm_w.shape[1]
    d_att = attn_norm_w.shape[1]
    h_ssd = dt_bias.shape[1]
    h_att = f_bias.shape[1]
    assert h_ssd + h_att == SMALL_W and d_att // HD_ATT == h_att and d_ssd // P_SSD == h_ssd

    c_all = jnp.concatenate([c_prompt, c_sample], axis=0)
    modf3 = _modulation(c_all, w_ada_f, b_ada_f).reshape(bp + bs, 1, 2 * d)
    final_w = final_norm_w.reshape(1, d)

    i0 = d_ssd
    i1 = i0 + d_conv
    i2 = i1 + h_ssd
    i3 = i2 + d_att
    i4 = i3 + d_att
    i5 = i4 + d_att
    yp, ys = x_prompt, x_sample
    outs_p, outs_s = [], []
    for layer in range(depth):
        mod3 = _modulation(c_all, w_ada[layer], b_ada[layer]).reshape(bp + bs, 1, 6 * d)
        wi = w_in[layer]
        w_small = jnp.concatenate([wi[:, i1:i2], wi[:, i5:], jnp.zeros((d, LANES - SMALL_W), F32)], axis=1)
        b_small = jnp.concatenate([dt_bias[layer], f_bias[layer], jnp.zeros((LANES - SMALL_W,), F32)])
        wr = jnp.zeros((ROUTER_ROWS, d), F32)
        wr = wr.at[:N_EGROUPS].set(w_rg[layer].T)
        wr = wr.at[EXPERT_ROW0:EXPERT_ROW0 + N_EXPERTS].set(
            jnp.transpose(w_re[layer], (0, 2, 1)).reshape(N_EXPERTS, d))
        wr_hi, wr_lo = _split2(wr)
        br = jnp.zeros((ROUTER_ROWS,), F32)
        br = br.at[:N_EGROUPS].set(b_rg[layer])
        br = br.at[EXPERT_ROW0:EXPERT_ROW0 + N_EXPERTS].set(b_re[layer].reshape(N_EXPERTS))
        p = dict(
            norm1_w=norm1_w[layer].reshape(1, d),
            wz=wi[:, :i0].astype(BF16), wx=wi[:, i0:i1].astype(BF16),
            wq=(wi[:, i2:i3] * (LOG2E * HD_ATT ** -0.5)).astype(BF16),
            wk=wi[:, i3:i4].astype(BF16), wv=wi[:, i4:i5].astype(BF16),
            ws=w_small.astype(BF16), bs=b_small.reshape(1, LANES),
            conv_w=conv_w[layer], conv_b=conv_b[layer], a_log=a_log[layer], d_skip=d_skip[layer],
            ssd_norm_w=ssd_norm_w[layer], attn_norm_w=attn_norm_w[layer].reshape(1, d_att),
            wo_ssd=w_out[layer][:d_ssd].astype(BF16), wo_att=w_out[layer][d_ssd:].astype(BF16),
            norm2_w=norm2_w[layer].reshape(1, d),
            wr=jnp.concatenate([wr_hi, wr_lo], axis=0), br=br.reshape(ROUTER_ROWS, 1),
            wg=w_gate[layer].astype(BF16), wu=w_up[layer].astype(BF16), wd=w_down[layer].astype(BF16),
        )
        conv0 = jnp.zeros((bp, CONV_W - 1, d_conv), F32)
        ssm0 = jnp.zeros((bp, h_ssd, P_SSD, N_STATE), F32)
        last = layer == depth - 1
        yp, st_p = _layer(yp, mod3, modf3, 0, None, None, None, conv0, ssm0, p, final_w, last)
        ck = cache_k[layer].reshape(bs, -1, d_att)
        cv = cache_v[layer].reshape(bs, -1, d_att)
        ys, st_s = _layer(ys, mod3, modf3, bp, ck, cv, cache_logf[layer], state_conv[layer], state_ssm[layer], p,
                          final_w, last)
        outs_p.append(st_p)
        outs_s.append(st_s)

    def stack(outs, b, l):
        k = jnp.stack([o[0].reshape(b, l, h_att, HD_ATT) for o in outs])
        v = jnp.stack([o[1].reshape(b, l, h_att, HD_ATT) for o in outs])
        return (k, v, jnp.stack([o[2] for o in outs]), jnp.stack([o[3] for o in outs]),
                jnp.stack([o[4] for o in outs]))

    return (yp, ys) + stack(outs_p, bp, lp) + stack(outs_s, bs, x_sample.shape[1])
```

```python
import functools

import jax
import jax.numpy as jnp
import numpy as np
from jax import lax
from jax.experimental import pallas as pl
from jax.experimental.pallas import tpu as pltpu

F32 = jnp.float32
BF16 = jnp.bfloat16

P_SSD = 64
N_STATE = 64
G_SSD = 2
CONV_W = 4
HD_ATT = 64
N_EGROUPS = 4
EXPERTS_PER_GROUP = 4
N_EXPERTS = N_EGROUPS * EXPERTS_PER_GROUP
EPS = 1e-6
NEG_BIG = -1e30

LANES = 128
SUBLANES = 8
SEG_ALIGN = 16
SORT_TILE = 256
MOE_BLOCK = 512
SMALL_INPUT_BLOCKS = 8
SMALL_W = 16
VMEM_LIMIT = 56 * 1024 * 1024


def _params(*sem):
    return pltpu.CompilerParams(dimension_semantics=sem, vmem_limit_bytes=VMEM_LIMIT)


def _split2(x):
    hi = x.astype(BF16)
    lo = (x - hi.astype(F32)).astype(BF16)
    return hi, lo


def _split3(x):
    hi = x.astype(BF16)
    r = x - hi.astype(F32)
    mid = r.astype(BF16)
    lo = (r - mid.astype(F32)).astype(BF16)
    return hi, mid, lo


def _dot(a, b):
    return jnp.dot(a, b, preferred_element_type=F32)


def _dot_nt(a, b):
    return lax.dot_general(a, b, (((1,), (1,)), ((), ())), preferred_element_type=F32)


def _dot_tn(a, b):
    return lax.dot_general(a, b, (((0,), (0,)), ((), ())), preferred_element_type=F32)


def _silu(x):
    h = 0.5 * x
    return h + h * jnp.tanh(h)


def _rms(x, w):
    return x * lax.rsqrt(jnp.mean(x * x, axis=-1, keepdims=True) + EPS) * w


def _pick_tile(n, candidates):
    for c in candidates:
        if n % c == 0:
            return c
    return n


def _mod_kernel(c_ref, w_ref, b_ref, o_ref):
    a = _silu(c_ref[...])
    a_hi, a_lo = _split2(a)
    w_hi, w_lo = _split2(w_ref[...])
    o_ref[...] = _dot(a_hi, w_hi) + _dot(a_lo, w_hi) + _dot(a_hi, w_lo) + b_ref[...]


def _modulation(c, w, b):
    m, d = c.shape
    n = w.shape[1]
    tn = _pick_tile(n, (1024, 512, 256, 128))
    return pl.pallas_call(
        _mod_kernel,
        grid=(n // tn,),
        in_specs=[pl.BlockSpec((m, d), lambda j: (0, 0)),
                  pl.BlockSpec((d, tn), lambda j: (0, j)),
                  pl.BlockSpec((1, tn), lambda j: (0, j))],
        out_specs=pl.BlockSpec((m, tn), lambda j: (0, j)),
        out_shape=jax.ShapeDtypeStruct((m, n), F32),
        compiler_params=_params("parallel"),
        name="adaln_mod",
    )(c, w, b.reshape(1, n))


def _inproj_kernel(x_ref, sh_ref, sc_ref, nw_ref, wz_ref, wx_ref, wq_ref, wk_ref, wv_ref, ws_ref, bs_ref,
                   z_ref, xbc_ref, q_ref, k_ref, v_ref, kb_ref, vb_ref, sm_ref, smt_ref, tail_ref):
    l = pl.program_id(1)
    x = x_ref[0]
    h = (_rms(x, nw_ref[...]) * (1.0 + sc_ref[0]) + sh_ref[0]).astype(BF16)
    z_ref[0] = _dot(h, wz_ref[...]).astype(BF16)
    xbc = _dot(h, wx_ref[...])
    xbc_ref[0] = xbc.astype(BF16)
    q_ref[0] = _dot(h, wq_ref[...]).astype(BF16)
    k = _dot(h, wk_ref[...])
    k_ref[0] = k
    kb_ref[0] = k.astype(BF16)
    v = _dot(h, wv_ref[...])
    v_ref[0] = v
    vb_ref[0] = v.astype(BF16)
    s = _dot(h, ws_ref[...]) + bs_ref[...]
    t = jnp.log(1.0 + jnp.exp(-jnp.abs(s)))
    lane = lax.broadcasted_iota(jnp.int32, s.shape, 1)
    s = jnp.where(lane < SMALL_W // 2, jnp.maximum(s, 0.0) + t, jnp.minimum(s, 0.0) - t)
    sm_ref[0] = s[:, :SMALL_W]
    smt_ref[0] = s.T[:SMALL_W, :]

    @pl.when(l == pl.num_programs(1) - 1)
    def _():
        tl = xbc.shape[0]
        tail_ref[0] = xbc[tl - (CONV_W - 1):, :]


def _inproj(x, mod3, boff, norm_w, wz, wx, wq, wk, wv, ws, bs):
    b, l, d = x.shape
    tl = _pick_tile(l, (1024, 512, 256, 128, 64))
    nl = l // tl
    d_ssd, d_conv, d_att = wz.shape[1], wx.shape[1], wq.shape[1]
    row = lambda col: pl.BlockSpec((1, 1, d), lambda i, j, col=col: (i + boff, 0, col))
    full = lambda a: pl.BlockSpec(a.shape, lambda i, j: (0,) * a.ndim)
    tok = lambda w: pl.BlockSpec((1, tl, w), lambda i, j: (i, j, 0))
    out_shape = [
        jax.ShapeDtypeStruct((b, l, d_ssd), BF16),
        jax.ShapeDtypeStruct((b, l, d_conv), BF16),
        jax.ShapeDtypeStruct((b, l, d_att), BF16),
        jax.ShapeDtypeStruct((b, l, d_att), F32),
        jax.ShapeDtypeStruct((b, l, d_att), F32),
        jax.ShapeDtypeStruct((b, l, d_att), BF16),
        jax.ShapeDtypeStruct((b, l, d_att), BF16),
        jax.ShapeDtypeStruct((b, l, SMALL_W), F32),
        jax.ShapeDtypeStruct((b, SMALL_W, l), F32),
        jax.ShapeDtypeStruct((b, CONV_W - 1, d_conv), F32),
    ]
    out_specs = [tok(d_ssd), tok(d_conv), tok(d_att), tok(d_att), tok(d_att), tok(d_att), tok(d_att),
                 tok(SMALL_W), pl.BlockSpec((1, SMALL_W, tl), lambda i, j: (i, 0, j)),
                 pl.BlockSpec((1, CONV_W - 1, d_conv), lambda i, j: (i, 0, 0))]
    return pl.pallas_call(
        _inproj_kernel,
        grid=(b, nl),
        in_specs=[tok(d), row(0), row(1), full(norm_w), full(wz), full(wx), full(wq), full(wk), full(wv),
                  full(ws), full(bs)],
        out_specs=out_specs,
        out_shape=out_shape,
        compiler_params=_params("parallel", "arbitrary"),
        name="inproj",
    )(x, mod3, mod3, norm_w, wz, wx, wq, wk, wv, ws, bs)


LOG2E = 1.4426950408889634
AUG = 3
SPLIT_ROWS = 128


PIECE_COLS = 32


def _aug_select(n_heads):
    pair = 2 * HD_ATT
    sq = np.zeros((n_heads // 2, PIECE_COLS, pair), np.float32)
    sk = np.zeros((n_heads // 2, PIECE_COLS, pair), np.float32)
    one = AUG * n_heads
    for h in range(n_heads):
        slot = ((h ^ 1) % 2) * HD_ATT
        for c in range(AUG):
            sq[h // 2, c * n_heads + h, slot + c] = 1.0
            sq[h // 2, one, slot + AUG + c] = 1.0
            sk[h // 2, one, slot + c] = 1.0
            sk[h // 2, c * n_heads + h, slot + AUG + c] = -1.0
    return sq, sk


def _cumsum_kernel(x_ref, eye_ref, o_ref, carry):
    @pl.when(pl.program_id(1) == 0)
    def _():
        carry[...] = jnp.zeros_like(carry)

    x = x_ref[0]
    h, tc = x.shape
    r = lax.broadcasted_iota(jnp.int32, (tc, tc), 0)
    c = lax.broadcasted_iota(jnp.int32, (tc, tc), 1)
    upper = jnp.where(r <= c, 1.0, 0.0).astype(BF16)
    stack = lambda ps: jnp.concatenate([p.astype(F32) for p in ps], axis=0)
    parts = _dot(stack(_split3(x)).astype(BF16), upper)
    cs = parts[:h] + parts[h:2 * h] + parts[2 * h:] + carry[...]
    carry[...] = cs[:, tc - 1:]
    rows = lax.broadcasted_iota(jnp.int32, (PIECE_COLS - AUG * h, tc), 0)
    pieces = jnp.concatenate([stack(_split3(cs * LOG2E)), jnp.where(rows == 0, 1.0, 0.0)], axis=0)
    o_ref[0] = _dot_tn(pieces.astype(BF16), eye_ref[...]).astype(BF16)


def _forget_cumsum(logf_t):
    b, h, lk = logf_t.shape
    assert AUG * h < PIECE_COLS
    tc = _pick_tile(lk, (1024, 512, 384, 256, 128))
    eye = jnp.asarray(np.eye(PIECE_COLS), BF16)
    return pl.pallas_call(
        _cumsum_kernel,
        grid=(b, lk // tc),
        in_specs=[pl.BlockSpec((1, h, tc), lambda i, j: (i, 0, j)),
                  pl.BlockSpec((PIECE_COLS, PIECE_COLS), lambda i, j: (0, 0))],
        out_specs=pl.BlockSpec((1, tc, PIECE_COLS), lambda i, j: (i, j, 0)),
        out_shape=jax.ShapeDtypeStruct((b, lk, PIECE_COLS), BF16),
        scratch_shapes=[pltpu.VMEM((h, 1), F32)],
        compiler_params=_params("parallel", "arbitrary"),
        name="forget_cumsum",
    )(logf_t, eye)


def _ssd_spread(n_heads):
    e = np.zeros((2 * AUG * n_heads, 2 * n_heads * P_SSD), np.float32)
    for v in range(2):
        for c in range(AUG):
            for h in range(n_heads):
                col = v * n_heads * P_SSD + h * P_SSD
                e[(v * AUG + c) * n_heads + h, col:col + P_SSD] = 1.0
    return e


def _ssd_kernel(xbc_ref, z_ref, sm_ref, smt_ref, cprev_ref, sprev_ref, cw_ref, cb_ref, arow_ref, acol_ref,
                dskip_ref, nw_ref, spread_ref, y_ref, snew_ref, hist, state):
    l = pl.program_id(1)
    q = xbc_ref.shape[1]
    d_ssd = z_ref.shape[2]
    n_heads = d_ssd // P_SSD
    pair = 2 * P_SSD
    hist_rows = hist.shape[0]

    @pl.when(l == 0)
    def _():
        hist[...] = jnp.zeros(hist.shape, F32)
        hist[hist_rows - (CONV_W - 1):, :] = cprev_ref[0]
        state[...] = sprev_ref[0]

    xb = xbc_ref[0]
    r3 = lax.broadcasted_iota(jnp.int32, ((CONV_W - 1) * q, q), 0)
    c3 = lax.broadcasted_iota(jnp.int32, ((CONV_W - 1) * q, q), 1)
    shift = jnp.where(r3 % q - c3 == r3 // q + 1, 1.0, 0.0).astype(BF16)
    shifted = _dot(shift, xb)
    cw = cw_ref[...]
    conv = cb_ref[...] + xb.astype(F32) * cw[CONV_W - 1:CONV_W, :]
    for k in range(CONV_W - 1):
        conv = conv + shifted[k * q:(k + 1) * q, :] * cw[CONV_W - 2 - k:CONV_W - 1 - k, :]
    hrow = lambda j: hist[hist_rows - j:hist_rows - j + 1, :]
    frow = lax.broadcasted_iota(jnp.int32, (hist_rows, 1), 0)
    fix = jnp.zeros((hist_rows, conv.shape[1]), F32)
    for t in range(CONV_W - 1):
        acc = 0.0
        for j in range(1, CONV_W - t):
            acc = acc + hrow(j) * cw[CONV_W - 1 - t - j:CONV_W - t - j, :]
        fix = jnp.where(frow == t, acc, fix)
    conv = jnp.concatenate([conv[:hist_rows] + fix, conv[hist_rows:]], axis=0)
    hist[...] = xb[q - hist_rows:, :].astype(F32)
    act = _silu(conv)

    dt = sm_ref[0][:, :n_heads]
    dt_t = smt_ref[0][:n_heads, :]
    a_row = -jnp.exp(arow_ref[...]) * LOG2E
    a_col = -jnp.exp(acol_ref[...]) * LOG2E
    r = lax.broadcasted_iota(jnp.int32, (q, q), 0)
    c = lax.broadcasted_iota(jnp.int32, (q, q), 1)
    causal = r >= c
    lower = jnp.where(causal, 1.0, 0.0).astype(BF16)
    upper = jnp.where(r <= c, 1.0, 0.0).astype(BF16)
    h0, h1, h2 = _split3(dt * a_row)
    acum = _dot(lower, h0) + _dot(lower, h1) + _dot(lower, h2)
    t0, t1, t2 = _split3(dt_t * a_col)
    acum_t = _dot(t0, upper) + _dot(t1, upper) + _dot(t2, upper)
    pieces = [p.astype(F32) for p in _split3(acum) + _split3(dt)]
    wide = _dot(jnp.concatenate(pieces, axis=1).astype(BF16), spread_ref[...])
    acum_x = wide[:, :d_ssd]
    dt_x = wide[:, d_ssd:]
    a_last = acum[q - 1:q, :]
    e_all = jnp.exp2(a_last)
    e_cum_x = jnp.exp2(acum_x)
    e_end_x = jnp.exp2(acum_x[q - 1:q, :] - acum_x)

    xs = act[:, :d_ssd]
    bm = act[:, d_ssd:d_ssd + G_SSD * N_STATE]
    cm = act[:, d_ssd + G_SSD * N_STATE:]
    bm_b = bm.astype(BF16)
    cm_b = cm.astype(BF16)
    xdt = xs * dt_x
    xdt_b = xdt.astype(BF16)
    xe_b = (xdt * e_end_x).astype(BF16)
    lane = lax.broadcasted_iota(jnp.int32, (1, pair), 1)
    first = lane < P_SSD
    srow = lax.broadcasted_iota(jnp.int32, (pair, 1), 0)
    ys = []
    for p in range(n_heads // 2):
        g = (2 * p * G_SSD) // n_heads
        in_group = first if g == 0 else jnp.logical_not(first)
        sl = slice(p * pair, (p + 1) * pair)
        if (2 * p) % (n_heads // G_SSD) == 0:
            cb = _dot_nt(jnp.where(in_group, cm_b, jnp.zeros_like(cm_b)), bm_b)
        ms = []
        for hh in range(2):
            h = 2 * p + hh
            seg = acum[:, h:h + 1] - acum_t[h:h + 1, :]
            ms.append((cb * jnp.exp2(jnp.where(causal, seg, NEG_BIG))).astype(BF16))
        xp = xdt_b[:, sl]
        zero = jnp.zeros_like(xp)
        y = _dot(jnp.concatenate(ms, axis=1),
                 jnp.concatenate([jnp.where(first, xp, zero), jnp.where(first, zero, xp)], axis=0))
        s_in = state[sl, :]
        y = y + _dot_nt(cm_b, s_in.astype(BF16)) * e_cum_x[:, sl]
        upd = _dot_tn(xe_b[:, sl], bm_b)
        keep = jnp.where(srow < P_SSD, e_all[:, 2 * p:2 * p + 1], e_all[:, 2 * p + 1:2 * p + 2])
        state[sl, :] = s_in * keep + jnp.where(in_group, upd, 0.0)
        ys.append(y)
    y_all = jnp.concatenate(ys, axis=1) + dskip_ref[...] * xs
    yg = y_all * _silu(z_ref[0].astype(F32))
    y_ref[0] = _rms(yg, nw_ref[...]).astype(BF16)
    snew_ref[0] = state[...]


def _ssd(xbc, z, sm, smt, conv_prev, ssm_prev, conv_w, conv_b, a_log, d_skip, norm_w):
    b, l, d_conv = xbc.shape
    d_ssd = z.shape[2]
    n_heads = d_ssd // P_SSD
    hg = n_heads // G_SSD
    q = _pick_tile(l, (256, 128, 64))
    tok = lambda w: pl.BlockSpec((1, q, w), lambda i, j: (i, j, 0))
    full = lambda a: pl.BlockSpec(a.shape, lambda i, j: (0,) * a.ndim)
    a_row = a_log.reshape(1, n_heads)
    a_col = a_log.reshape(n_heads, 1)
    conv_b = conv_b.reshape(1, d_conv)
    d_skip = jnp.repeat(d_skip, P_SSD).reshape(1, d_ssd)
    norm_w = norm_w.reshape(1, d_ssd)
    spread = jnp.asarray(_ssd_spread(n_heads), BF16)
    s4 = ssm_prev.reshape(b, G_SSD, hg * P_SSD, N_STATE)
    s_in = jnp.concatenate([jnp.pad(s4[:, g], ((0, 0), (0, 0), (g * N_STATE, (G_SSD - 1 - g) * N_STATE)))
                            for g in range(G_SSD)], axis=1)
    state_spec = pl.BlockSpec((1, n_heads * P_SSD, G_SSD * N_STATE), lambda i, j: (i, 0, 0))
    y, s_out = pl.pallas_call(
        _ssd_kernel,
        grid=(b, l // q),
        in_specs=[tok(d_conv), tok(d_ssd), tok(SMALL_W), pl.BlockSpec((1, SMALL_W, q), lambda i, j: (i, 0, j)),
                  pl.BlockSpec((1, CONV_W - 1, d_conv), lambda i, j: (i, 0, 0)), state_spec,
                  full(conv_w), full(conv_b), full(a_row), full(a_col), full(d_skip), full(norm_w), full(spread)],
        out_specs=[tok(d_ssd), state_spec],
        out_shape=[jax.ShapeDtypeStruct((b, l, d_ssd), BF16),
                   jax.ShapeDtypeStruct((b, n_heads * P_SSD, G_SSD * N_STATE), F32)],
        scratch_shapes=[pltpu.VMEM((SUBLANES, d_conv), F32),
                        pltpu.VMEM((n_heads * P_SSD, G_SSD * N_STATE), F32)],
        compiler_params=_params("parallel", "arbitrary"),
        name="ssd",
    )(xbc, z, sm, smt, conv_prev, s_in, conv_w, conv_b, a_row, a_col, d_skip, norm_w, spread)
    s_out = s_out.reshape(b, G_SSD, hg * P_SSD, G_SSD * N_STATE)
    s_new = jnp.concatenate([s_out[:, g, :, g * N_STATE:(g + 1) * N_STATE] for g in range(G_SSD)], axis=1)
    return y, s_new.reshape(b, n_heads, P_SSD, N_STATE)


def _attn_kernel(q_ref, pq_ref, k_ref, pk_ref, v_ref, sq_ref, sk_ref, o_ref, kk_s, m_s, acc_s, *, p0, tq, tk):
    i = pl.program_id(2)
    lane = lax.broadcasted_iota(jnp.int32, (1, 2 * HD_ATT), 1)
    first = lane < HD_ATT
    own = (first, jnp.logical_not(first))
    sum_lane = (HD_ATT, 0)

    @pl.when(i == 0)
    def _():
        k = k_ref[0]
        ka = _dot(pk_ref[0], sk_ref[0]).astype(BF16)
        kk_s[0] = jnp.where(first, k, ka)
        kk_s[1] = jnp.where(first, ka, k)

    q = q_ref[0]
    qa = _dot(pq_ref[0], sq_ref[0]).astype(BF16)
    qq = (jnp.where(first, q, qa), jnp.where(first, qa, q))
    m_s[...] = jnp.full(m_s.shape, NEG_BIG, F32)
    acc_s[...] = jnp.zeros(acc_s.shape, F32)
    nc = tk // LANES
    n_split = 2 if tq % (2 * SPLIT_ROWS) == 0 else 1
    rows = tq // n_split
    aligned = tq == tk and p0 % tk == 0

    def step(j, masked):
        off = pl.multiple_of(j * tk, tk)
        v = v_ref[0, pl.ds(off, tk), :]
        vv = [jnp.where(own[hh], v, jnp.where(lane == sum_lane[hh], 1.0, 0.0).astype(BF16)) for hh in range(2)]
        width = [(r + 1) * rows if masked and aligned else tk for r in range(n_split)]
        logit = [[_dot_nt(qq[hh][r * rows:(r + 1) * rows], kk_s[hh, pl.ds(off, width[r]), :]) for hh in range(2)]
                 for r in range(n_split)]
        for r in range(n_split):
            rs = slice(r * rows, (r + 1) * rows)
            tiles = range(width[r] // LANES)
            if masked and aligned:
                row = lax.broadcasted_iota(jnp.int32, (rows, LANES), 0)
                col = lax.broadcasted_iota(jnp.int32, (rows, LANES), 1)
                seen = [None if (c + 1) * LANES <= r * rows + 1 else col + (c * LANES - r * rows) <= row for c in tiles]
            elif masked:
                q_pos = p0 + i * tq + r * rows + lax.broadcasted_iota(jnp.int32, (rows, LANES), 0)
                k_pos = j * tk + lax.broadcasted_iota(jnp.int32, (rows, LANES), 1)
                seen = [k_pos + c * LANES <= q_pos for c in tiles]
            else:
                seen = [None for _ in tiles]
            for hh in range(2):
                s = logit[r][hh]
                cols = [s[:, c * LANES:(c + 1) * LANES] for c in tiles]
                cols = [x if m is None else jnp.where(m, x, NEG_BIG) for x, m in zip(cols, seen)]
                m_cur = functools.reduce(jnp.maximum, cols)
                m_prev = m_s[hh, rs, :]
                m_new = jnp.maximum(m_prev, jnp.max(m_cur, axis=1, keepdims=True))
                alpha = jnp.exp2(m_prev - m_new)
                p = jnp.concatenate([jnp.exp2((col - m_new).astype(BF16)) for col in cols], axis=1)
                m_s[hh, rs, :] = m_new
                acc_s[hh, rs, :] = alpha * acc_s[hh, rs, :] + _dot(p, vv[hh][:width[r]])

    n_full = (p0 + i * tq + 1) // tk
    n_vis = (p0 + i * tq + tq - 1) // tk + 1

    def full_pair(jj, carry):
        step(2 * jj, False)
        step(2 * jj + 1, False)
        return carry

    def masked_body(j, carry):
        step(j, True)
        return carry

    lax.fori_loop(0, n_full // 2, full_pair, 0)
    odd = n_full % 2 == 1
    both = jnp.logical_and(odd, n_vis > n_full)

    @pl.when(both)
    def _():
        step(n_full - 1, False)
        step(n_full, True)

    pl.when(jnp.logical_and(odd, n_vis <= n_full))(lambda: step(n_full - 1, False))
    lax.fori_loop(n_full + both.astype(jnp.int32), n_vis, masked_body, 0)
    a0 = acc_s[0]
    a1 = acc_s[1]
    inv0 = 1.0 / a0[:, sum_lane[0]:sum_lane[0] + 1]
    inv1 = 1.0 / a1[:, sum_lane[1]:sum_lane[1] + 1]
    o = jnp.where(first, a0 * inv0, a1 * inv1)
    o_ref[0] = o.astype(o_ref.dtype)


def _attention(q, k, v, pieces, p0):
    b, lq, d_att = q.shape
    lk = k.shape[1]
    n_heads = d_att // HD_ATT
    tq = _pick_tile(lq, (512, 256, 128, 64))
    tk = lk if lk <= 1536 else _pick_tile(lk, (512, 256, 128))
    assert p0 % tq == 0 and lk >= p0 + lq
    pair = 2 * HD_ATT
    sq, sk = (jnp.asarray(a, BF16) for a in _aug_select(n_heads))
    q_spec = pl.BlockSpec((1, tq, pair), lambda bi, hp, i: (bi, i, hp))
    kv_spec = pl.BlockSpec((1, lk, pair), lambda bi, hp, i: (bi, 0, hp))
    sel_spec = pl.BlockSpec((1, PIECE_COLS, pair), lambda bi, hp, i: (hp, 0, 0))
    return pl.pallas_call(
        functools.partial(_attn_kernel, p0=p0, tq=tq, tk=tk),
        grid=(b, n_heads // 2, lq // tq),
        in_specs=[q_spec, pl.BlockSpec((1, tq, PIECE_COLS), lambda bi, hp, i: (bi, i + p0 // tq, 0)),
                  kv_spec, pl.BlockSpec((1, lk, PIECE_COLS), lambda bi, hp, i: (bi, 0, 0)), kv_spec,
                  sel_spec, sel_spec],
        out_specs=q_spec,
        out_shape=jax.ShapeDtypeStruct((b, lq, d_att), BF16),
        scratch_shapes=[pltpu.VMEM((2, lk, pair), BF16), pltpu.VMEM((2, tq, LANES), F32),
                        pltpu.VMEM((2, tq, pair), F32)],
        compiler_params=_params("parallel", "parallel", "arbitrary"),
        name="fox_attention",
    )(q, pieces, k, pieces, v, sq, sk)


ROUTER_ROWS = 32
EXPERT_ROW0 = 8


def _outproj_kernel(ys_ref, oa_ref, x_ref, g1_ref, sh_ref, sc_ref, anw_ref, wos_ref, woa_ref, n2w_ref, wr_ref,
                    br_ref, x1_ref, h2_ref, ld_ref, cnt_ref, wt_s, g_s):
    ya = _rms(oa_ref[0].astype(F32), anw_ref[...]).astype(BF16)
    m = _dot(ys_ref[0], wos_ref[...]) + _dot(ya, woa_ref[...])
    x1 = x_ref[0] + g1_ref[0] * m
    x1_ref[0] = x1
    h2 = _rms(x1, n2w_ref[...]) * (1.0 + sc_ref[0]) + sh_ref[0]
    h_hi, h_lo = _split2(h2)
    wr = wr_ref[...]
    p1 = _dot_nt(wr, h_hi)
    p2 = _dot_nt(wr[:ROUTER_ROWS], h_lo)
    logit = p1[:ROUTER_ROWS] + p1[ROUTER_ROWS:] + p2 + br_ref[...]

    lg = [logit[g:g + 1, :] for g in range(N_EGROUPS)]
    gmax = jnp.maximum(jnp.maximum(lg[0], lg[1]), jnp.maximum(lg[2], lg[3]))
    denom = sum(jnp.exp(x - gmax) for x in lg)
    p_sel = 1.0 / denom
    is_g = []
    taken = jnp.zeros_like(gmax) > 1.0
    for g in range(N_EGROUPS):
        hit = (lg[g] == gmax) & jnp.logical_not(taken)
        is_g.append(hit)
        taken = taken | hit
    le = []
    for e in range(EXPERTS_PER_GROUP):
        v = logit[EXPERT_ROW0 + 3 * EXPERTS_PER_GROUP + e:EXPERT_ROW0 + 3 * EXPERTS_PER_GROUP + e + 1, :]
        for g in range(N_EGROUPS - 2, -1, -1):
            r0 = EXPERT_ROW0 + g * EXPERTS_PER_GROUP + e
            v = jnp.where(is_g[g], logit[r0:r0 + 1, :], v)
        le.append(v)
    m1 = jnp.maximum(jnp.maximum(le[0], le[1]), jnp.maximum(le[2], le[3]))
    first = []
    taken = jnp.zeros_like(m1) > 1.0
    for e in range(EXPERTS_PER_GROUP):
        hit = (le[e] == m1) & jnp.logical_not(taken)
        first.append(hit)
        taken = taken | hit
    rest = [jnp.where(first[e], -jnp.inf, le[e]) for e in range(EXPERTS_PER_GROUP)]
    m2 = jnp.maximum(jnp.maximum(rest[0], rest[1]), jnp.maximum(rest[2], rest[3]))
    second = []
    taken = jnp.zeros_like(m1) > 1.0
    for e in range(EXPERTS_PER_GROUP):
        hit = (rest[e] == m2) & jnp.logical_not(taken)
        second.append(hit)
        taken = taken | hit
    e2 = jnp.exp(m2 - m1)
    w_a = p_sel / (1.0 + e2)
    w_b = w_a * e2
    d = x1.shape[1]
    wt_s[...] = jnp.zeros(wt_s.shape, F32)
    for e in range(EXPERTS_PER_GROUP):
        w = jnp.where(first[e], w_a, jnp.where(second[e], w_b, 0.0))
        w_hi = w.astype(BF16).astype(F32)
        wt_s[e:e + 1, :] = w_hi
        wt_s[EXPERTS_PER_GROUP + e:EXPERTS_PER_GROUP + e + 1, :] = w - w_hi
    h2_ref[:, :d] = h_hi
    h2_ref[:, d:] = wt_s[...].T.astype(BF16)

    tl = logit.shape[1]
    ts = min(SORT_TILE, tl)
    g_s[...] = jnp.zeros(g_s.shape, F32)
    for g in range(N_EGROUPS):
        g_s[g:g + 1, :] = jnp.where(is_g[g], 1.0, 0.0)
    r = lax.broadcasted_iota(jnp.int32, (ts, ts), 0)
    c = lax.broadcasted_iota(jnp.int32, (ts, ts), 1)
    upper = jnp.where(r <= c, 1.0, 0.0).astype(BF16)
    onehot = g_s[...].astype(BF16)
    cum = jnp.concatenate([_dot(onehot[:, sub * ts:(sub + 1) * ts], upper) for sub in range(tl // ts)],
                          axis=1)
    lane = lax.broadcasted_iota(jnp.int32, (1, tl), 1)
    crow = lax.broadcasted_iota(jnp.int32, (SUBLANES, LANES), 0)
    clane = lax.broadcasted_iota(jnp.int32, (SUBLANES, LANES), 1)
    ldest = -1.0
    cnt = jnp.zeros((SUBLANES, LANES), F32)
    for g in range(N_EGROUPS):
        ldest = ldest + jnp.where(is_g[g], cum[g:g + 1, :], 0.0)
    lo = [0.0] * (tl // ts)
    for g in range(N_EGROUPS):
        lo_row = jnp.zeros((1, tl), F32)
        for sub in range(tl // ts):
            n = cum[g:g + 1, (sub + 1) * ts - 1:(sub + 1) * ts]
            n_pad = jnp.ceil(n / SEG_ALIGN) * SEG_ALIGN
            lo_row = jnp.where(lane // ts == sub, lo[sub], lo_row)
            cnt = jnp.where((crow == sub) & (clane == g), n_pad, cnt)
            lo[sub] = lo[sub] + n_pad
        ldest = ldest + jnp.where(is_g[g], lo_row, 0.0)
    ld_ref[0] = ldest.astype(jnp.int32)
    cnt_ref[0] = cnt.astype(jnp.int32)


def _outproj(y_ssd, o_att, x, mod3, boff, attn_norm_w, wo_ssd, wo_att, norm2_w, wr, br):
    b, l, d = x.shape
    tl = _pick_tile(l, (1024, 512, 256, 128, 64))
    nl = l // tl
    d_ssd, d_att = y_ssd.shape[2], o_att.shape[2]
    row = lambda col: pl.BlockSpec((1, 1, d), lambda i, j, col=col: (i + boff, 0, col))
    full = lambda a: pl.BlockSpec(a.shape, lambda i, j: (0,) * a.ndim)
    tok = lambda w: pl.BlockSpec((1, tl, w), lambda i, j: (i, j, 0))
    return pl.pallas_call(
        _outproj_kernel,
        grid=(b, nl),
        in_specs=[tok(d_ssd), tok(d_att), tok(d), row(2), row(3), row(4), full(attn_norm_w), full(wo_ssd),
                  full(wo_att), full(norm2_w), full(wr), full(br)],
        out_specs=[tok(d), pl.BlockSpec((tl, d + LANES), lambda i, j: (i * nl + j, 0)),
                   pl.BlockSpec((1, 1, tl), lambda i, j: (i * nl + j, 0, 0)),
                   pl.BlockSpec((1, SUBLANES, LANES), lambda i, j: (i * nl + j, 0, 0))],
        out_shape=[jax.ShapeDtypeStruct((b, l, d), F32),
                   jax.ShapeDtypeStruct((b * l, d + LANES), BF16),
                   jax.ShapeDtypeStruct((b * nl, 1, tl), jnp.int32),
                   jax.ShapeDtypeStruct((b * nl, SUBLANES, LANES), jnp.int32)],
        scratch_shapes=[pltpu.VMEM((LANES, tl), F32), pltpu.VMEM((SUBLANES, tl), F32)],
        compiler_params=_params("parallel", "parallel"),
        name="outproj_router",
    )(y_ssd, o_att, x, mod3, mod3, mod3, attn_norm_w, wo_ssd, wo_att, norm2_w, wr, br)


def _sort_plan(cnt, n_tiles, block, n_blocks):
    ns = n_tiles // cnt.shape[0]
    n_pad = cnt[:, :ns, :N_EGROUPS].reshape(n_tiles, N_EGROUPS)
    lo = jnp.cumsum(n_pad, axis=1) - n_pad
    region = (jnp.sum(n_pad, axis=0) + block - 1) // block * block
    end = jnp.cumsum(region)
    off = (end - region)[None, :] + jnp.cumsum(n_pad, axis=0) - n_pad
    blk = jnp.arange(n_blocks, dtype=jnp.int32) * block
    blk_group = jnp.minimum(jnp.sum(blk[:, None] >= end[None, :], axis=1), N_EGROUPS - 1).astype(jnp.int32)
    n_used = (end[-1] // block).astype(jnp.int32).reshape(1)
    flat = lambda a: a.astype(jnp.int32).reshape(-1)
    return flat(lo), flat(off), flat(n_pad), blk_group, n_used


def _piece_copies(n_pad, sizes, make, act):
    for k, size in enumerate(sizes):
        done = n_pad & ~(2 * size - 1)

        @pl.when((n_pad & size) != 0)
        def _(k=k, size=size, done=done):
            act(make(k, size, done))


def _sort_kernel(lo_ref, off_ref, np_ref, x_ref, ld_ref, init_ref, out_ref, cbuf, sems, *, ts, tps):
    del init_ref
    i = pl.program_id(0)
    slot = i % 2
    rows = ts + N_EGROUPS * SEG_ALIGN
    sizes = [ts >> k for k in range((ts // SEG_ALIGN).bit_length())]

    r = lax.broadcasted_iota(jnp.int32, (rows, ts), 0)
    for u in range(tps):
        perm = jnp.where(r == ld_ref[u], 1.0, 0.0).astype(BF16)
        cbuf[slot, u] = _dot(perm, x_ref[u * ts:(u + 1) * ts, :]).astype(BF16)

    def for_each(step, sl, act):
        for u in range(tps):
            for g in range(N_EGROUPS):
                at = (step * tps + u) * N_EGROUPS + g

                def make(k, size, done, u=u, g=g, at=at):
                    src = pl.multiple_of(lo_ref[at] + done, SEG_ALIGN)
                    dst = pl.multiple_of(off_ref[at] + done, SEG_ALIGN)
                    return pltpu.make_async_copy(cbuf.at[sl, u, pl.ds(src, size)], out_ref.at[pl.ds(dst, size)],
                                                 sems.at[sl, u, g, k])

                _piece_copies(np_ref[at], sizes, make, act)

    for_each(i, slot, lambda cp: cp.start())

    @pl.when(i > 0)
    def _():
        for_each(i - 1, 1 - slot, lambda cp: cp.wait())

    @pl.when(i == pl.num_programs(0) - 1)
    def _():
        for_each(i, slot, lambda cp: cp.wait())


def _experts_kernel(grp_ref, nb_ref, x_ref, wg_ref, wu_ref, wd_ref, o_ref):
    del grp_ref
    b = pl.program_id(0)
    d = o_ref.shape[1]

    @pl.when(b < nb_ref[0])
    def _():
        blk = x_ref[...]
        x = blk[:, :d]
        wp = blk[:, d:].astype(F32)
        acc = jnp.zeros(o_ref.shape, F32)
        for e in range(EXPERTS_PER_GROUP):
            w_e = wp[:, e:e + 1] + wp[:, EXPERTS_PER_GROUP + e:EXPERTS_PER_GROUP + e + 1]
            hid = _silu(_dot(x, wg_ref[e])) * _dot(x, wu_ref[e]) * w_e
            acc = acc + _dot(hid.astype(BF16), wd_ref[e])
        o_ref[...] = acc.astype(BF16)

    @pl.when(b >= nb_ref[0])
    def _():
        o_ref[...] = jnp.zeros(o_ref.shape, BF16)


def _combine_kernel(lo_ref, off_ref, np_ref, ld_ref, x1_ref, g2_ref, shf_ref, scf_ref, fnw_ref, src_ref, y_ref,
                    seg, sems, *, ts, tps, final_norm):
    i = pl.program_id(0)
    slot = i % 2
    sizes = [ts >> k for k in range((ts // SEG_ALIGN).bit_length())]

    def fetch(step, sl, act):
        for u in range(tps):
            for g in range(N_EGROUPS):
                at = (step * tps + u) * N_EGROUPS + g

                def make(k, size, done, u=u, g=g, at=at):
                    src = pl.multiple_of(off_ref[at] + done, SEG_ALIGN)
                    dst = pl.multiple_of(lo_ref[at] + done, SEG_ALIGN)
                    return pltpu.make_async_copy(src_ref.at[pl.ds(src, size)], seg.at[sl, u, pl.ds(dst, size)],
                                                 sems.at[sl, u, g, k])

                _piece_copies(np_ref[at], sizes, make, act)

    @pl.when(i == 0)
    def _():
        seg[...] = jnp.zeros(seg.shape, BF16)
        fetch(0, 0, lambda cp: cp.start())

    @pl.when(i + 1 < pl.num_programs(0))
    def _():
        fetch(i + 1, 1 - slot, lambda cp: cp.start())

    fetch(i, slot, lambda cp: cp.wait())

    r = lax.broadcasted_iota(jnp.int32, (seg.shape[2], ts), 0)
    for u in range(tps):
        rs = slice(u * ts, (u + 1) * ts)
        perm = jnp.where(r == ld_ref[u], 1.0, 0.0).astype(BF16)
        xo = x1_ref[rs, :] + g2_ref[0] * _dot_tn(perm, seg[slot, u])
        if final_norm:
            xo = _rms(xo, fnw_ref[...]) * (1.0 + scf_ref[0]) + shf_ref[0]
        y_ref[rs, :] = xo


def _moe_final(h2ext, ldest, cnt, wg, wu, wd, x1, mod3, modf3, boff, rows_per_batch, final_norm_w, final_norm):
    t, d = x1.shape
    payload = h2ext.shape[1]
    ts = min(SORT_TILE, rows_per_batch)
    tps = 2 if rows_per_batch % (2 * ts) == 0 else 1
    block = MOE_BLOCK if t >= SMALL_INPUT_BLOCKS * MOE_BLOCK else MOE_BLOCK // 2
    n_tiles = t // ts
    ldest = ldest.reshape(n_tiles, 1, ts)
    n_blocks = -(-(t + n_tiles * N_EGROUPS * (SEG_ALIGN - 1) + N_EGROUPS * (block - 1)) // block)
    lo, off, n_pad, blk_group, n_used = _sort_plan(cnt, n_tiles, block, n_blocks)
    ld_spec = pl.BlockSpec((tps, 1, ts), lambda i, *_: (i, 0, 0))

    sorted_rows = pl.pallas_call(
        functools.partial(_sort_kernel, ts=ts, tps=tps),
        grid_spec=pltpu.PrefetchScalarGridSpec(
            num_scalar_prefetch=3,
            grid=(n_tiles // tps,),
            in_specs=[pl.BlockSpec((tps * ts, payload), lambda i, *_: (i, 0)), ld_spec,
                      pl.BlockSpec(memory_space=pl.ANY)],
            out_specs=pl.BlockSpec(memory_space=pl.ANY),
            scratch_shapes=[pltpu.VMEM((2, tps, ts + N_EGROUPS * SEG_ALIGN, payload), BF16),
                            pltpu.SemaphoreType.DMA((2, tps, N_EGROUPS, (ts // SEG_ALIGN).bit_length()))]),
        out_shape=jax.ShapeDtypeStruct((n_blocks * block, payload), BF16),
        input_output_aliases={5: 0},
        compiler_params=_params("arbitrary"),
        name="moe_sort",
    )(lo, off, n_pad, h2ext, ldest, jnp.zeros((n_blocks * block, payload), BF16))

    d_e = wg.shape[2]
    live = lambda b, nb: jnp.minimum(b, nb[0] - 1)
    out_sorted = pl.pallas_call(
        _experts_kernel,
        grid_spec=pltpu.PrefetchScalarGridSpec(
            num_scalar_prefetch=2,
            grid=(n_blocks,),
            in_specs=[pl.BlockSpec((block, payload), lambda b, grp, nb: (live(b, nb), 0)),
                      pl.BlockSpec((EXPERTS_PER_GROUP, d, d_e), lambda b, grp, nb: (grp[b], 0, 0)),
                      pl.BlockSpec((EXPERTS_PER_GROUP, d, d_e), lambda b, grp, nb: (grp[b], 0, 0)),
                      pl.BlockSpec((EXPERTS_PER_GROUP, d_e, d), lambda b, grp, nb: (grp[b], 0, 0))],
            out_specs=pl.BlockSpec((block, d), lambda b, grp, nb: (b, 0))),
        out_shape=jax.ShapeDtypeStruct((n_blocks * block, d), BF16),
        compiler_params=_params("arbitrary"),
        name="moe_experts",
    )(blk_group, n_used, sorted_rows, wg, wu, wd)

    per = rows_per_batch // (tps * ts)
    row = lambda col: pl.BlockSpec((1, 1, d), lambda i, *_, col=col: (i // per + boff, 0, col))
    return pl.pallas_call(
        functools.partial(_combine_kernel, ts=ts, tps=tps, final_norm=final_norm),
        grid_spec=pltpu.PrefetchScalarGridSpec(
            num_scalar_prefetch=3,
            grid=(n_tiles // tps,),
            in_specs=[ld_spec,
                      pl.BlockSpec((tps * ts, d), lambda i, *_: (i, 0)),
                      row(5), row(0), row(1),
                      pl.BlockSpec((1, d), lambda i, *_: (0, 0)),
                      pl.BlockSpec(memory_space=pl.ANY)],
            out_specs=pl.BlockSpec((tps * ts, d), lambda i, *_: (i, 0)),
            scratch_shapes=[pltpu.VMEM((2, tps, ts + N_EGROUPS * SEG_ALIGN, d), BF16),
                            pltpu.SemaphoreType.DMA((2, tps, N_EGROUPS, (ts // SEG_ALIGN).bit_length()))]),
        out_shape=jax.ShapeDtypeStruct((t, d), F32),
        compiler_params=_params("arbitrary"),
        name="moe_combine",
    )(lo, off, n_pad, ldest, x1, mod3, modf3, modf3, final_norm_w, out_sorted)


def _layer(x, mod3, modf3, boff, k_prev, v_prev, logf_prev, conv_prev, ssm_prev, p, final_norm_w, final_norm):
    b, l, d = x.shape
    z, xbc, q, k, v, k_b, v_b, sm, smt, conv_new = _inproj(
        x, mod3, boff, p["norm1_w"], p["wz"], p["wx"], p["wq"], p["wk"], p["wv"], p["ws"], p["bs"])
    n_heads_ssd = p["wz"].shape[1] // P_SSD
    y_ssd, ssm_new = _ssd(xbc, z, sm, smt, conv_prev, ssm_prev, p["conv_w"], p["conv_b"], p["a_log"],
                          p["d_skip"], p["ssd_norm_w"])
    n_heads = p["wq"].shape[1] // HD_ATT
    logf_t = smt[:, n_heads_ssd:n_heads_ssd + n_heads, :]
    if k_prev is None:
        p0, k_all, v_all, lf_all = 0, k_b, v_b, logf_t
    else:
        p0 = k_prev.shape[1]
        pad = (-(p0 + l)) % LANES
        zeros = lambda w: jnp.zeros((b, pad, w), BF16)
        k_all = jnp.concatenate([k_prev.astype(BF16), k_b, zeros(k_b.shape[2])], axis=1)
        v_all = jnp.concatenate([v_prev.astype(BF16), v_b, zeros(v_b.shape[2])], axis=1)
        lf_all = jnp.concatenate([jnp.swapaxes(logf_prev, 1, 2), logf_t, jnp.zeros((b, n_heads, pad), F32)], axis=2)
    o_att = _attention(q, k_all, v_all, _forget_cumsum(lf_all), p0)
    x1, h2ext, ldest, cnt = _outproj(y_ssd, o_att, x, mod3, boff, p["attn_norm_w"], p["wo_ssd"], p["wo_att"],
                                     p["norm2_w"], p["wr"], p["br"])
    y = _moe_final(h2ext, ldest, cnt, p["wg"], p["wu"], p["wd"], x1.reshape(b * l, d), mod3, modf3, boff, l,
                   final_norm_w, final_norm)
    return y.reshape(b, l, d), (k, v, jnp.swapaxes(logf_t, 1, 2), conv_new, ssm_new)


def kernel(x_prompt, x_sample, c_prompt, c_sample, cache_k, cache_v, cache_logf, state_conv, state_ssm, norm1_w, w_ada, b_ada, w_in, conv_w, conv_b, dt_bias, a_log, d_skip, ssd_norm_w, f_bias, attn_norm_w, w_out, norm2_w, w_rg, b_rg, w_re, b_re, w_gate, w_up, w_down, final_norm_w, w_ada_f, b_ada_f):
    depth = w_in.shape[0]
    bp, lp, d = x_prompt.shape
    bs = x_sample.shape[0]
    d_conv = conv_w.shape[2]
    d_ssd = ssd_norm_w.shape[1]
    d_att = attn_norm_w.shape[1]
    h_ssd = dt_bias.shape[1]
    h_att = f_bias.shape[1]
    assert h_ssd + h_att == SMALL_W and d_att // HD_ATT == h_att and d_ssd // P_SSD == h_ssd

    c_all = jnp.concatenate([c_prompt, c_sample], axis=0)
    modf3 = _modulation(c_all, w_ada_f, b_ada_f).reshape(bp + bs, 1, 2 * d)
    final_w = final_norm_w.reshape(1, d)

    i0 = d_ssd
    i1 = i0 + d_conv
    i2 = i1 + h_ssd
    i3 = i2 + d_att
    i4 = i3 + d_att
    i5 = i4 + d_att
    yp, ys = x_prompt, x_sample
    outs_p, outs_s = [], []
    for layer in range(depth):
        mod3 = _modulation(c_all, w_ada[layer], b_ada[layer]).reshape(bp + bs, 1, 6 * d)
        wi = w_in[layer]
        w_small = jnp.concatenate([wi[:, i1:i2], wi[:, i5:], jnp.zeros((d, LANES - SMALL_W), F32)], axis=1)
        b_small = jnp.concatenate([dt_bias[layer], f_bias[layer], jnp.zeros((LANES - SMALL_W,), F32)])
        wr = jnp.zeros((ROUTER_ROWS, d), F32)
        wr = wr.at[:N_EGROUPS].set(w_rg[layer].T)
        wr = wr.at[EXPERT_ROW0:EXPERT_ROW0 + N_EXPERTS].set(
            jnp.transpose(w_re[layer], (0, 2, 1)).reshape(N_EXPERTS, d))
        wr_hi, wr_lo = _split2(wr)
        br = jnp.zeros((ROUTER_ROWS,), F32)
        br = br.at[:N_EGROUPS].set(b_rg[layer])
        br = br.at[EXPERT_ROW0:EXPERT_ROW0 + N_EXPERTS].set(b_re[layer].reshape(N_EXPERTS))
        p = dict(
            norm1_w=norm1_w[layer].reshape(1, d),
            wz=wi[:, :i0].astype(BF16), wx=wi[:, i0:i1].astype(BF16),
            wq=(wi[:, i2:i3] * (LOG2E * HD_ATT ** -0.5)).astype(BF16),
            wk=wi[:, i3:i4].astype(BF16), wv=wi[:, i4:i5].astype(BF16),
            ws=w_small.astype(BF16), bs=b_small.reshape(1, LANES),
            conv_w=conv_w[layer], conv_b=conv_b[layer], a_log=a_log[layer], d_skip=d_skip[layer],
            ssd_norm_w=ssd_norm_w[layer], attn_norm_w=attn_norm_w[layer].reshape(1, d_att),
            wo_ssd=w_out[layer][:d_ssd].astype(BF16), wo_att=w_out[layer][d_ssd:].astype(BF16),
            norm2_w=norm2_w[layer].reshape(1, d),
            wr=jnp.concatenate([wr_hi, wr_lo], axis=0), br=br.reshape(ROUTER_ROWS, 1),
            wg=w_gate[layer].astype(BF16), wu=w_up[layer].astype(BF16), wd=w_down[layer].astype(BF16),
        )
        conv0 = jnp.zeros((bp, CONV_W - 1, d_conv), F32)
        ssm0 = jnp.zeros((bp, h_ssd, P_SSD, N_STATE), F32)
        last = layer == depth - 1
        yp, st_p = _layer(yp, mod3, modf3, 0, None, None, None, conv0, ssm0, p, final_w, last)
        ck = cache_k[layer].reshape(bs, -1, d_att)
        cv = cache_v[layer].reshape(bs, -1, d_att)
        ys, st_s = _layer(ys, mod3, modf3, bp, ck, cv, cache_logf[layer], state_conv[layer], state_ssm[layer], p,
                          final_w, last)
        outs_p.append(st_p)
        outs_s.append(st_s)

    def stack(outs, b, l):
        k = jnp.stack([o[0].reshape(b, l, h_att, HD_ATT) for o in outs])
        v = jnp.stack([o[1].reshape(b, l, h_att, HD_ATT) for o in outs])
        return (k, v, jnp.stack([o[2] for o in outs]), jnp.stack([o[3] for o in outs]),
                jnp.stack([o[4] for o in outs]))

    return (yp, ys) + stack(outs_p, bp, lp) + stack(outs_s, bs, x_sample.shape[1])
```

```python
import functools

import jax
import jax.numpy as jnp
import numpy as np
from jax import lax
from jax.experimental import pallas as pl
from jax.experimental.pallas import tpu as pltpu

F32 = jnp.float32
BF16 = jnp.bfloat16

P_SSD = 64
N_STATE = 64
G_SSD = 2
CONV_W = 4
HD_ATT = 64
N_EGROUPS = 4
EXPERTS_PER_GROUP = 4
N_EXPERTS = N_EGROUPS * EXPERTS_PER_GROUP
EPS = 1e-6
NEG_BIG = -1e30

LANES = 128
SUBLANES = 8
SEG_ALIGN = 16
SORT_TILE = 256
MOE_BLOCK = 512
SMALL_INPUT_BLOCKS = 8
SMALL_W = 16
VMEM_LIMIT = 56 * 1024 * 1024


def _params(*sem):
    return pltpu.CompilerParams(dimension_semantics=sem, vmem_limit_bytes=VMEM_LIMIT)


def _split2(x):
    hi = x.astype(BF16)
    lo = (x - hi.astype(F32)).astype(BF16)
    return hi, lo


def _split3(x):
    hi = x.astype(BF16)
    r = x - hi.astype(F32)
    mid = r.astype(BF16)
    lo = (r - mid.astype(F32)).astype(BF16)
    return hi, mid, lo


def _dot(a, b):
    return jnp.dot(a, b, preferred_element_type=F32)


def _dot_nt(a, b):
    return lax.dot_general(a, b, (((1,), (1,)), ((), ())), preferred_element_type=F32)


def _dot_tn(a, b):
    return lax.dot_general(a, b, (((0,), (0,)), ((), ())), preferred_element_type=F32)


def _silu(x):
    h = 0.5 * x
    return h + h * jnp.tanh(h)


def _rms(x, w):
    return x * lax.rsqrt(jnp.mean(x * x, axis=-1, keepdims=True) + EPS) * w


def _pick_tile(n, candidates):
    for c in candidates:
        if n % c == 0:
            return c
    return n


def _mod_kernel(c_ref, w_ref, b_ref, o_ref):
    a = _silu(c_ref[...])
    a_hi, a_lo = _split2(a)
    w_hi, w_lo = _split2(w_ref[...])
    o_ref[...] = _dot(a_hi, w_hi) + _dot(a_lo, w_hi) + _dot(a_hi, w_lo) + b_ref[...]


def _modulation(c, w, b):
    m, d = c.shape
    n = w.shape[1]
    tn = _pick_tile(n, (1024, 512, 256, 128))
    return pl.pallas_call(
        _mod_kernel,
        grid=(n // tn,),
        in_specs=[pl.BlockSpec((m, d), lambda j: (0, 0)),
                  pl.BlockSpec((d, tn), lambda j: (0, j)),
                  pl.BlockSpec((1, tn), lambda j: (0, j))],
        out_specs=pl.BlockSpec((m, tn), lambda j: (0, j)),
        out_shape=jax.ShapeDtypeStruct((m, n), F32),
        compiler_params=_params("parallel"),
        name="adaln_mod",
    )(c, w, b.reshape(1, n))


def _inproj_kernel(x_ref, sh_ref, sc_ref, nw_ref, wz_ref, wx_ref, wq_ref, wk_ref, wv_ref, ws_ref, bs_ref,
                   z_ref, xbc_ref, q_ref, k_ref, v_ref, kb_ref, vb_ref, sm_ref, smt_ref, tail_ref):
    l = pl.program_id(1)
    x = x_ref[0]
    h = (_rms(x, nw_ref[...]) * (1.0 + sc_ref[0]) + sh_ref[0]).astype(BF16)
    z_ref[0] = _dot(h, wz_ref[...]).astype(BF16)
    xbc = _dot(h, wx_ref[...])
    xbc_ref[0] = xbc.astype(BF16)
    q_ref[0] = _dot(h, wq_ref[...]).astype(BF16)
    k = _dot(h, wk_ref[...])
    k_ref[0] = k
    kb_ref[0] = k.astype(BF16)
    v = _dot(h, wv_ref[...])
    v_ref[0] = v
    vb_ref[0] = v.astype(BF16)
    s = _dot(h, ws_ref[...]) + bs_ref[...]
    t = jnp.log(1.0 + jnp.exp(-jnp.abs(s)))
    lane = lax.broadcasted_iota(jnp.int32, s.shape, 1)
    s = jnp.where(lane < SMALL_W // 2, jnp.maximum(s, 0.0) + t, jnp.minimum(s, 0.0) - t)
    sm_ref[0] = s[:, :SMALL_W]
    smt_ref[0] = s.T[:SMALL_W, :]

    @pl.when(l == pl.num_programs(1) - 1)
    def _():
        tl = xbc.shape[0]
        tail_ref[0] = xbc[tl - (CONV_W - 1):, :]


def _inproj(x, mod3, boff, norm_w, wz, wx, wq, wk, wv, ws, bs):
    b, l, d = x.shape
    tl = _pick_tile(l, (1024, 512, 256, 128, 64))
    nl = l // tl
    d_ssd, d_conv, d_att = wz.shape[1], wx.shape[1], wq.shape[1]
    row = lambda col: pl.BlockSpec((1, 1, d), lambda i, j, col=col: (i + boff, 0, col))
    full = lambda a: pl.BlockSpec(a.shape, lambda i, j: (0,) * a.ndim)
    tok = lambda w: pl.BlockSpec((1, tl, w), lambda i, j: (i, j, 0))
    out_shape = [
        jax.ShapeDtypeStruct((b, l, d_ssd), BF16),
        jax.ShapeDtypeStruct((b, l, d_conv), BF16),
        jax.ShapeDtypeStruct((b, l, d_att), BF16),
        jax.ShapeDtypeStruct((b, l, d_att), F32),
        jax.ShapeDtypeStruct((b, l, d_att), F32),
        jax.ShapeDtypeStruct((b, l, d_att), BF16),
        jax.ShapeDtypeStruct((b, l, d_att), BF16),
        jax.ShapeDtypeStruct((b, l, SMALL_W), F32),
        jax.ShapeDtypeStruct((b, SMALL_W, l), F32),
        jax.ShapeDtypeStruct((b, CONV_W - 1, d_conv), F32),
    ]
    out_specs = [tok(d_ssd), tok(d_conv), tok(d_att), tok(d_att), tok(d_att), tok(d_att), tok(d_att),
                 tok(SMALL_W), pl.BlockSpec((1, SMALL_W, tl), lambda i, j: (i, 0, j)),
                 pl.BlockSpec((1, CONV_W - 1, d_conv), lambda i, j: (i, 0, 0))]
    return pl.pallas_call(
        _inproj_kernel,
        grid=(b, nl),
        in_specs=[tok(d), row(0), row(1), full(norm_w), full(wz), full(wx), full(wq), full(wk), full(wv),
                  full(ws), full(bs)],
        out_specs=out_specs,
        out_shape=out_shape,
        compiler_params=_params("parallel", "arbitrary"),
        name="inproj",
    )(x, mod3, mod3, norm_w, wz, wx, wq, wk, wv, ws, bs)


LOG2E = 1.4426950408889634
AUG = 3
SPLIT_ROWS = 128


PIECE_COLS = 32


def _aug_select(n_heads):
    pair = 2 * HD_ATT
    sq = np.zeros((n_heads // 2, PIECE_COLS, pair), np.float32)
    sk = np.zeros((n_heads // 2, PIECE_COLS, pair), np.float32)
    one = AUG * n_heads
    for h in range(n_heads):
        slot = ((h ^ 1) % 2) * HD_ATT
        for c in range(AUG):
            sq[h // 2, c * n_heads + h, slot + c] = 1.0
            sq[h // 2, one, slot + AUG + c] = 1.0
            sk[h // 2, one, slot + c] = 1.0
            sk[h // 2, c * n_heads + h, slot + AUG + c] = -1.0
    return sq, sk


def _cumsum_kernel(x_ref, eye_ref, o_ref, carry):
    @pl.when(pl.program_id(1) == 0)
    def _():
        carry[...] = jnp.zeros_like(carry)

    x = x_ref[0]
    h, tc = x.shape
    r = lax.broadcasted_iota(jnp.int32, (tc, tc), 0)
    c = lax.broadcasted_iota(jnp.int32, (tc, tc), 1)
    upper = jnp.where(r <= c, 1.0, 0.0).astype(BF16)
    stack = lambda ps: jnp.concatenate([p.astype(F32) for p in ps], axis=0)
    parts = _dot(stack(_split3(x)).astype(BF16), upper)
    cs = parts[:h] + parts[h:2 * h] + parts[2 * h:] + carry[...]
    carry[...] = cs[:, tc - 1:]
    rows = lax.broadcasted_iota(jnp.int32, (PIECE_COLS - AUG * h, tc), 0)
    pieces = jnp.concatenate([stack(_split3(cs * LOG2E)), jnp.where(rows == 0, 1.0, 0.0)], axis=0)
    o_ref[0] = _dot_tn(pieces.astype(BF16), eye_ref[...]).astype(BF16)


def _forget_cumsum(logf_t):
    b, h, lk = logf_t.shape
    assert AUG * h < PIECE_COLS
    tc = _pick_tile(lk, (1024, 512, 384, 256, 128))
    eye = jnp.asarray(np.eye(PIECE_COLS), BF16)
    return pl.pallas_call(
        _cumsum_kernel,
        grid=(b, lk // tc),
        in_specs=[pl.BlockSpec((1, h, tc), lambda i, j: (i, 0, j)),
                  pl.BlockSpec((PIECE_COLS, PIECE_COLS), lambda i, j: (0, 0))],
        out_specs=pl.BlockSpec((1, tc, PIECE_COLS), lambda i, j: (i, j, 0)),
        out_shape=jax.ShapeDtypeStruct((b, lk, PIECE_COLS), BF16),
        scratch_shapes=[pltpu.VMEM((h, 1), F32)],
        compiler_params=_params("parallel", "arbitrary"),
        name="forget_cumsum",
    )(logf_t, eye)


def _ssd_spread(n_heads):
    e = np.zeros((2 * AUG * n_heads, 2 * n_heads * P_SSD), np.float32)
    for v in range(2):
        for c in range(AUG):
            for h in range(n_heads):
                col = v * n_heads * P_SSD + h * P_SSD
                e[(v * AUG + c) * n_heads + h, col:col + P_SSD] = 1.0
    return e


def _ssd_kernel(xbc_ref, z_ref, sm_ref, smt_ref, cprev_ref, sprev_ref, cw_ref, cb_ref, arow_ref, acol_ref,
                dskip_ref, nw_ref, spread_ref, y_ref, snew_ref, hist, state, *, q):
    l = pl.program_id(1)
    d_ssd = z_ref.shape[2]
    n_heads = d_ssd // P_SSD
    pair = 2 * P_SSD
    hist_rows = hist.shape[0]

    @pl.when(l == 0)
    def _():
        hist[...] = jnp.zeros(hist.shape, F32)
        hist[hist_rows - (CONV_W - 1):, :] = cprev_ref[0]
        state[...] = sprev_ref[0]

    def chunk(rows):
        xb = xbc_ref[0, rows, :]
        r3 = lax.broadcasted_iota(jnp.int32, ((CONV_W - 1) * q, q), 0)
        c3 = lax.broadcasted_iota(jnp.int32, ((CONV_W - 1) * q, q), 1)
        shift = jnp.where(r3 % q - c3 == r3 // q + 1, 1.0, 0.0).astype(BF16)
        shifted = _dot(shift, xb)
        cw = cw_ref[...]
        conv = cb_ref[...] + xb.astype(F32) * cw[CONV_W - 1:CONV_W, :]
        for k in range(CONV_W - 1):
            conv = conv + shifted[k * q:(k + 1) * q, :] * cw[CONV_W - 2 - k:CONV_W - 1 - k, :]
        hrow = lambda j: hist[hist_rows - j:hist_rows - j + 1, :]
        frow = lax.broadcasted_iota(jnp.int32, (hist_rows, 1), 0)
        fix = jnp.zeros((hist_rows, conv.shape[1]), F32)
        for t in range(CONV_W - 1):
            acc = 0.0
            for j in range(1, CONV_W - t):
                acc = acc + hrow(j) * cw[CONV_W - 1 - t - j:CONV_W - t - j, :]
            fix = jnp.where(frow == t, acc, fix)
        conv = jnp.concatenate([conv[:hist_rows] + fix, conv[hist_rows:]], axis=0)
        hist[...] = xb[q - hist_rows:, :].astype(F32)
        act = _silu(conv)

        dt = sm_ref[0, rows, :][:, :n_heads]
        dt_t = smt_ref[0, :, rows][:n_heads, :]
        a_row = -jnp.exp(arow_ref[...]) * LOG2E
        a_col = -jnp.exp(acol_ref[...]) * LOG2E
        r = lax.broadcasted_iota(jnp.int32, (q, q), 0)
        c = lax.broadcasted_iota(jnp.int32, (q, q), 1)
        causal = r >= c
        lower = jnp.where(causal, 1.0, 0.0).astype(BF16)
        upper = jnp.where(r <= c, 1.0, 0.0).astype(BF16)
        h0, h1, h2 = _split3(dt * a_row)
        acum = _dot(lower, h0) + _dot(lower, h1) + _dot(lower, h2)
        t0, t1, t2 = _split3(dt_t * a_col)
        acum_t = _dot(t0, upper) + _dot(t1, upper) + _dot(t2, upper)
        pieces = [p.astype(F32) for p in _split3(acum) + _split3(dt)]
        wide = _dot(jnp.concatenate(pieces, axis=1).astype(BF16), spread_ref[...])
        acum_x = wide[:, :d_ssd]
        dt_x = wide[:, d_ssd:]
        a_last = acum[q - 1:q, :]
        e_all = jnp.exp2(a_last)
        e_cum_x = jnp.exp2(acum_x)
        e_end_x = jnp.exp2(acum_x[q - 1:q, :] - acum_x)

        xs = act[:, :d_ssd]
        bm = act[:, d_ssd:d_ssd + G_SSD * N_STATE]
        cm = act[:, d_ssd + G_SSD * N_STATE:]
        bm_b = bm.astype(BF16)
        cm_b = cm.astype(BF16)
        xdt = xs * dt_x
        xdt_b = xdt.astype(BF16)
        xe_b = (xdt * e_end_x).astype(BF16)
        lane = lax.broadcasted_iota(jnp.int32, (1, pair), 1)
        first = lane < P_SSD
        srow = lax.broadcasted_iota(jnp.int32, (pair, 1), 0)
        ys = []
        for p in range(n_heads // 2):
            g = (2 * p * G_SSD) // n_heads
            in_group = first if g == 0 else jnp.logical_not(first)
            sl = slice(p * pair, (p + 1) * pair)
            if (2 * p) % (n_heads // G_SSD) == 0:
                cb = _dot_nt(jnp.where(in_group, cm_b, jnp.zeros_like(cm_b)), bm_b)
            ms = []
            for hh in range(2):
                h = 2 * p + hh
                seg = acum[:, h:h + 1] - acum_t[h:h + 1, :]
                ms.append((cb * jnp.exp2(jnp.where(causal, seg, NEG_BIG))).astype(BF16))
            xp = xdt_b[:, sl]
            zero = jnp.zeros_like(xp)
            y = _dot(jnp.concatenate(ms, axis=1),
                     jnp.concatenate([jnp.where(first, xp, zero), jnp.where(first, zero, xp)], axis=0))
            s_in = state[sl, :]
            y = y + _dot_nt(cm_b, s_in.astype(BF16)) * e_cum_x[:, sl]
            upd = _dot_tn(xe_b[:, sl], bm_b)
            keep = jnp.where(srow < P_SSD, e_all[:, 2 * p:2 * p + 1], e_all[:, 2 * p + 1:2 * p + 2])
            state[sl, :] = s_in * keep + jnp.where(in_group, upd, 0.0)
            ys.append(y)
        y_all = jnp.concatenate(ys, axis=1) + dskip_ref[...] * xs
        yg = y_all * _silu(z_ref[0, rows, :].astype(F32))
        y_ref[0, rows, :] = _rms(yg, nw_ref[...]).astype(BF16)

    for c in range(xbc_ref.shape[1] // q):
        chunk(slice(c * q, (c + 1) * q))
    snew_ref[0] = state[...]


def _ssd(xbc, z, sm, smt, conv_prev, ssm_prev, conv_w, conv_b, a_log, d_skip, norm_w):
    b, l, d_conv = xbc.shape
    d_ssd = z.shape[2]
    n_heads = d_ssd // P_SSD
    hg = n_heads // G_SSD
    q = _pick_tile(l, (256, 128, 64))
    tl = 2 * q if l % (2 * q) == 0 else q
    tok = lambda w: pl.BlockSpec((1, tl, w), lambda i, j: (i, j, 0))
    full = lambda a: pl.BlockSpec(a.shape, lambda i, j: (0,) * a.ndim)
    a_row = a_log.reshape(1, n_heads)
    a_col = a_log.reshape(n_heads, 1)
    conv_b = conv_b.reshape(1, d_conv)
    d_skip = jnp.repeat(d_skip, P_SSD).reshape(1, d_ssd)
    norm_w = norm_w.reshape(1, d_ssd)
    spread = jnp.asarray(_ssd_spread(n_heads), BF16)
    s4 = ssm_prev.reshape(b, G_SSD, hg * P_SSD, N_STATE)
    s_in = jnp.concatenate([jnp.pad(s4[:, g], ((0, 0), (0, 0), (g * N_STATE, (G_SSD - 1 - g) * N_STATE)))
                            for g in range(G_SSD)], axis=1)
    state_spec = pl.BlockSpec((1, n_heads * P_SSD, G_SSD * N_STATE), lambda i, j: (i, 0, 0))
    y, s_out = pl.pallas_call(
        functools.partial(_ssd_kernel, q=q),
        grid=(b, l // tl),
        in_specs=[tok(d_conv), tok(d_ssd), tok(SMALL_W), pl.BlockSpec((1, SMALL_W, tl), lambda i, j: (i, 0, j)),
                  pl.BlockSpec((1, CONV_W - 1, d_conv), lambda i, j: (i, 0, 0)), state_spec,
                  full(conv_w), full(conv_b), full(a_row), full(a_col), full(d_skip), full(norm_w), full(spread)],
        out_specs=[tok(d_ssd), state_spec],
        out_shape=[jax.ShapeDtypeStruct((b, l, d_ssd), BF16),
                   jax.ShapeDtypeStruct((b, n_heads * P_SSD, G_SSD * N_STATE), F32)],
        scratch_shapes=[pltpu.VMEM((SUBLANES, d_conv), F32),
                        pltpu.VMEM((n_heads * P_SSD, G_SSD * N_STATE), F32)],
        compiler_params=_params("parallel", "arbitrary"),
        name="ssd",
    )(xbc, z, sm, smt, conv_prev, s_in, conv_w, conv_b, a_row, a_col, d_skip, norm_w, spread)
    s_out = s_out.reshape(b, G_SSD, hg * P_SSD, G_SSD * N_STATE)
    s_new = jnp.concatenate([s_out[:, g, :, g * N_STATE:(g + 1) * N_STATE] for g in range(G_SSD)], axis=1)
    return y, s_new.reshape(b, n_heads, P_SSD, N_STATE)


def _attn_kernel(q_ref, pq_ref, k_ref, pk_ref, v_ref, sq_ref, sk_ref, o_ref, kk_s, m_s, acc_s, *, p0, tq, tk):
    i = pl.program_id(2)
    lane = lax.broadcasted_iota(jnp.int32, (1, 2 * HD_ATT), 1)
    first = lane < HD_ATT
    own = (first, jnp.logical_not(first))
    sum_lane = (HD_ATT, 0)

    @pl.when(i == 0)
    def _():
        k = k_ref[0]
        ka = _dot(pk_ref[0], sk_ref[0]).astype(BF16)
        kk_s[0] = jnp.where(first, k, ka)
        kk_s[1] = jnp.where(first, ka, k)

    q = q_ref[0]
    qa = _dot(pq_ref[0], sq_ref[0]).astype(BF16)
    qq = (jnp.where(first, q, qa), jnp.where(first, qa, q))
    m_s[...] = jnp.full(m_s.shape, NEG_BIG, F32)
    acc_s[...] = jnp.zeros(acc_s.shape, F32)
    nc = tk // LANES
    n_split = 2 if tq % (2 * SPLIT_ROWS) == 0 else 1
    rows = tq // n_split
    aligned = tq == tk and p0 % tk == 0

    def step(j, masked):
        off = pl.multiple_of(j * tk, tk)
        v = v_ref[0, pl.ds(off, tk), :]
        vv = [jnp.where(own[hh], v, jnp.where(lane == sum_lane[hh], 1.0, 0.0).astype(BF16)) for hh in range(2)]
        width = [(r + 1) * rows if masked and aligned else tk for r in range(n_split)]
        logit = [[_dot_nt(qq[hh][r * rows:(r + 1) * rows], kk_s[hh, pl.ds(off, width[r]), :]) for hh in range(2)]
                 for r in range(n_split)]
        for r in range(n_split):
            rs = slice(r * rows, (r + 1) * rows)
            tiles = range(width[r] // LANES)
            if masked and aligned:
                row = lax.broadcasted_iota(jnp.int32, (rows, LANES), 0)
                col = lax.broadcasted_iota(jnp.int32, (rows, LANES), 1)
                seen = [None if (c + 1) * LANES <= r * rows + 1 else col + (c * LANES - r * rows) <= row for c in tiles]
            elif masked:
                q_pos = p0 + i * tq + r * rows + lax.broadcasted_iota(jnp.int32, (rows, LANES), 0)
                k_pos = j * tk + lax.broadcasted_iota(jnp.int32, (rows, LANES), 1)
                seen = [k_pos + c * LANES <= q_pos for c in tiles]
            else:
                seen = [None for _ in tiles]
            for hh in range(2):
                s = logit[r][hh]
                cols = [s[:, c * LANES:(c + 1) * LANES] for c in tiles]
                cols = [x if m is None else jnp.where(m, x, NEG_BIG) for x, m in zip(cols, seen)]
                m_cur = functools.reduce(jnp.maximum, cols)
                m_prev = m_s[hh, rs, :]
                m_new = jnp.maximum(m_prev, jnp.max(m_cur, axis=1, keepdims=True))
                alpha = jnp.exp2(m_prev - m_new)
                p = jnp.concatenate([jnp.exp2((col - m_new).astype(BF16)) for col in cols], axis=1)
                m_s[hh, rs, :] = m_new
                acc_s[hh, rs, :] = alpha * acc_s[hh, rs, :] + _dot(p, vv[hh][:width[r]])

    n_full = (p0 + i * tq + 1) // tk
    n_vis = (p0 + i * tq + tq - 1) // tk + 1

    def full_pair(jj, carry):
        step(2 * jj, False)
        step(2 * jj + 1, False)
        return carry

    def masked_body(j, carry):
        step(j, True)
        return carry

    lax.fori_loop(0, n_full // 2, full_pair, 0)
    odd = n_full % 2 == 1
    both = jnp.logical_and(odd, n_vis > n_full)

    @pl.when(both)
    def _():
        step(n_full - 1, False)
        step(n_full, True)

    pl.when(jnp.logical_and(odd, n_vis <= n_full))(lambda: step(n_full - 1, False))
    lax.fori_loop(n_full + both.astype(jnp.int32), n_vis, masked_body, 0)
    a0 = acc_s[0]
    a1 = acc_s[1]
    inv0 = 1.0 / a0[:, sum_lane[0]:sum_lane[0] + 1]
    inv1 = 1.0 / a1[:, sum_lane[1]:sum_lane[1] + 1]
    o = jnp.where(first, a0 * inv0, a1 * inv1)
    o_ref[0] = o.astype(o_ref.dtype)


def _attention(q, k, v, pieces, p0):
    b, lq, d_att = q.shape
    lk = k.shape[1]
    n_heads = d_att // HD_ATT
    tq = _pick_tile(lq, (512, 256, 128, 64))
    tk = lk if lk <= 1536 else _pick_tile(lk, (512, 256, 128))
    assert p0 % tq == 0 and lk >= p0 + lq
    pair = 2 * HD_ATT
    sq, sk = (jnp.asarray(a, BF16) for a in _aug_select(n_heads))
    q_spec = pl.BlockSpec((1, tq, pair), lambda bi, hp, i: (bi, i, hp))
    kv_spec = pl.BlockSpec((1, lk, pair), lambda bi, hp, i: (bi, 0, hp))
    sel_spec = pl.BlockSpec((1, PIECE_COLS, pair), lambda bi, hp, i: (hp, 0, 0))
    return pl.pallas_call(
        functools.partial(_attn_kernel, p0=p0, tq=tq, tk=tk),
        grid=(b, n_heads // 2, lq // tq),
        in_specs=[q_spec, pl.BlockSpec((1, tq, PIECE_COLS), lambda bi, hp, i: (bi, i + p0 // tq, 0)),
                  kv_spec, pl.BlockSpec((1, lk, PIECE_COLS), lambda bi, hp, i: (bi, 0, 0)), kv_spec,
                  sel_spec, sel_spec],
        out_specs=q_spec,
        out_shape=jax.ShapeDtypeStruct((b, lq, d_att), BF16),
        scratch_shapes=[pltpu.VMEM((2, lk, pair), BF16), pltpu.VMEM((2, tq, LANES), F32),
                        pltpu.VMEM((2, tq, pair), F32)],
        compiler_params=_params("parallel", "parallel", "arbitrary"),
        name="fox_attention",
    )(q, pieces, k, pieces, v, sq, sk)


ROUTER_ROWS = 32
EXPERT_ROW0 = 8


def _outproj_kernel(ys_ref, oa_ref, x_ref, g1_ref, sh_ref, sc_ref, anw_ref, wos_ref, woa_ref, n2w_ref, wr_ref,
                    br_ref, x1_ref, h2_ref, ld_ref, cnt_ref, wt_s, g_s):
    ya = _rms(oa_ref[0].astype(F32), anw_ref[...]).astype(BF16)
    m = _dot(ys_ref[0], wos_ref[...]) + _dot(ya, woa_ref[...])
    x1 = x_ref[0] + g1_ref[0] * m
    x1_ref[0] = x1
    h2 = _rms(x1, n2w_ref[...]) * (1.0 + sc_ref[0]) + sh_ref[0]
    h_hi, h_lo = _split2(h2)
    wr = wr_ref[...]
    p1 = _dot_nt(wr, h_hi)
    p2 = _dot_nt(wr[:ROUTER_ROWS], h_lo)
    logit = p1[:ROUTER_ROWS] + p1[ROUTER_ROWS:] + p2 + br_ref[...]

    lg = [logit[g:g + 1, :] for g in range(N_EGROUPS)]
    gmax = jnp.maximum(jnp.maximum(lg[0], lg[1]), jnp.maximum(lg[2], lg[3]))
    denom = sum(jnp.exp(x - gmax) for x in lg)
    p_sel = 1.0 / denom
    is_g = []
    taken = jnp.zeros_like(gmax) > 1.0
    for g in range(N_EGROUPS):
        hit = (lg[g] == gmax) & jnp.logical_not(taken)
        is_g.append(hit)
        taken = taken | hit
    le = []
    for e in range(EXPERTS_PER_GROUP):
        v = logit[EXPERT_ROW0 + 3 * EXPERTS_PER_GROUP + e:EXPERT_ROW0 + 3 * EXPERTS_PER_GROUP + e + 1, :]
        for g in range(N_EGROUPS - 2, -1, -1):
            r0 = EXPERT_ROW0 + g * EXPERTS_PER_GROUP + e
            v = jnp.where(is_g[g], logit[r0:r0 + 1, :], v)
        le.append(v)
    m1 = jnp.maximum(jnp.maximum(le[0], le[1]), jnp.maximum(le[2], le[3]))
    first = []
    taken = jnp.zeros_like(m1) > 1.0
    for e in range(EXPERTS_PER_GROUP):
        hit = (le[e] == m1) & jnp.logical_not(taken)
        first.append(hit)
        taken = taken | hit
    rest = [jnp.where(first[e], -jnp.inf, le[e]) for e in range(EXPERTS_PER_GROUP)]
    m2 = jnp.maximum(jnp.maximum(rest[0], rest[1]), jnp.maximum(rest[2], rest[3]))
    second = []
    taken = jnp.zeros_like(m1) > 1.0
    for e in range(EXPERTS_PER_GROUP):
        hit = (rest[e] == m2) & jnp.logical_not(taken)
        second.append(hit)
        taken = taken | hit
    e2 = jnp.exp(m2 - m1)
    w_a = p_sel / (1.0 + e2)
    w_b = w_a * e2
    d = x1.shape[1]
    wt_s[...] = jnp.zeros(wt_s.shape, F32)
    for e in range(EXPERTS_PER_GROUP):
        w = jnp.where(first[e], w_a, jnp.where(second[e], w_b, 0.0))
        w_hi = w.astype(BF16).astype(F32)
        wt_s[e:e + 1, :] = w_hi
        wt_s[EXPERTS_PER_GROUP + e:EXPERTS_PER_GROUP + e + 1, :] = w - w_hi
    h2_ref[:, :d] = h_hi
    h2_ref[:, d:] = wt_s[...].T.astype(BF16)

    tl = logit.shape[1]
    ts = min(SORT_TILE, tl)
    g_s[...] = jnp.zeros(g_s.shape, F32)
    for g in range(N_EGROUPS):
        g_s[g:g + 1, :] = jnp.where(is_g[g], 1.0, 0.0)
    r = lax.broadcasted_iota(jnp.int32, (ts, ts), 0)
    c = lax.broadcasted_iota(jnp.int32, (ts, ts), 1)
    upper = jnp.where(r <= c, 1.0, 0.0).astype(BF16)
    onehot = g_s[...].astype(BF16)
    cum = jnp.concatenate([_dot(onehot[:, sub * ts:(sub + 1) * ts], upper) for sub in range(tl // ts)],
                          axis=1)
    lane = lax.broadcasted_iota(jnp.int32, (1, tl), 1)
    crow = lax.broadcasted_iota(jnp.int32, (SUBLANES, LANES), 0)
    clane = lax.broadcasted_iota(jnp.int32, (SUBLANES, LANES), 1)
    ldest = -1.0
    cnt = jnp.zeros((SUBLANES, LANES), F32)
    for g in range(N_EGROUPS):
        ldest = ldest + jnp.where(is_g[g], cum[g:g + 1, :], 0.0)
    lo = [0.0] * (tl // ts)
    for g in range(N_EGROUPS):
        lo_row = jnp.zeros((1, tl), F32)
        for sub in range(tl // ts):
            n = cum[g:g + 1, (sub + 1) * ts - 1:(sub + 1) * ts]
            n_pad = jnp.ceil(n / SEG_ALIGN) * SEG_ALIGN
            lo_row = jnp.where(lane // ts == sub, lo[sub], lo_row)
            cnt = jnp.where((crow == sub) & (clane == g), n_pad, cnt)
            lo[sub] = lo[sub] + n_pad
        ldest = ldest + jnp.where(is_g[g], lo_row, 0.0)
    ld_ref[0] = ldest.astype(jnp.int32)
    cnt_ref[0] = cnt.astype(jnp.int32)


def _outproj(y_ssd, o_att, x, mod3, boff, attn_norm_w, wo_ssd, wo_att, norm2_w, wr, br):
    b, l, d = x.shape
    tl = _pick_tile(l, (1024, 512, 256, 128, 64))
    nl = l // tl
    d_ssd, d_att = y_ssd.shape[2], o_att.shape[2]
    row = lambda col: pl.BlockSpec((1, 1, d), lambda i, j, col=col: (i + boff, 0, col))
    full = lambda a: pl.BlockSpec(a.shape, lambda i, j: (0,) * a.ndim)
    tok = lambda w: pl.BlockSpec((1, tl, w), lambda i, j: (i, j, 0))
    return pl.pallas_call(
        _outproj_kernel,
        grid=(b, nl),
        in_specs=[tok(d_ssd), tok(d_att), tok(d), row(2), row(3), row(4), full(attn_norm_w), full(wo_ssd),
                  full(wo_att), full(norm2_w), full(wr), full(br)],
        out_specs=[tok(d), pl.BlockSpec((tl, d + LANES), lambda i, j: (i * nl + j, 0)),
                   pl.BlockSpec((1, 1, tl), lambda i, j: (i * nl + j, 0, 0)),
                   pl.BlockSpec((1, SUBLANES, LANES), lambda i, j: (i * nl + j, 0, 0))],
        out_shape=[jax.ShapeDtypeStruct((b, l, d), F32),
                   jax.ShapeDtypeStruct((b * l, d + LANES), BF16),
                   jax.ShapeDtypeStruct((b * nl, 1, tl), jnp.int32),
                   jax.ShapeDtypeStruct((b * nl, SUBLANES, LANES), jnp.int32)],
        scratch_shapes=[pltpu.VMEM((LANES, tl), F32), pltpu.VMEM((SUBLANES, tl), F32)],
        compiler_params=_params("parallel", "parallel"),
        name="outproj_router",
    )(y_ssd, o_att, x, mod3, mod3, mod3, attn_norm_w, wo_ssd, wo_att, norm2_w, wr, br)


def _sort_plan(cnt, n_tiles, block, n_blocks):
    ns = n_tiles // cnt.shape[0]
    n_pad = cnt[:, :ns, :N_EGROUPS].reshape(n_tiles, N_EGROUPS)
    lo = jnp.cumsum(n_pad, axis=1) - n_pad
    region = (jnp.sum(n_pad, axis=0) + block - 1) // block * block
    end = jnp.cumsum(region)
    off = (end - region)[None, :] + jnp.cumsum(n_pad, axis=0) - n_pad
    blk = jnp.arange(n_blocks, dtype=jnp.int32) * block
    blk_group = jnp.minimum(jnp.sum(blk[:, None] >= end[None, :], axis=1), N_EGROUPS - 1).astype(jnp.int32)
    n_used = (end[-1] // block).astype(jnp.int32).reshape(1)
    flat = lambda a: a.astype(jnp.int32).reshape(-1)
    return flat(lo), flat(off), flat(n_pad), blk_group, n_used


def _piece_copies(n_pad, sizes, make, act):
    for k, size in enumerate(sizes):
        done = n_pad & ~(2 * size - 1)

        @pl.when((n_pad & size) != 0)
        def _(k=k, size=size, done=done):
            act(make(k, size, done))


def _sort_kernel(lo_ref, off_ref, np_ref, x_ref, ld_ref, init_ref, out_ref, cbuf, sems, *, ts, tps):
    del init_ref
    i = pl.program_id(0)
    slot = i % 2
    rows = ts + N_EGROUPS * SEG_ALIGN
    sizes = [ts >> k for k in range((ts // SEG_ALIGN).bit_length())]

    r = lax.broadcasted_iota(jnp.int32, (rows, ts), 0)
    for u in range(tps):
        perm = jnp.where(r == ld_ref[u], 1.0, 0.0).astype(BF16)
        cbuf[slot, u] = _dot(perm, x_ref[u * ts:(u + 1) * ts, :]).astype(BF16)

    def for_each(step, sl, act):
        for u in range(tps):
            for g in range(N_EGROUPS):
                at = (step * tps + u) * N_EGROUPS + g

                def make(k, size, done, u=u, g=g, at=at):
                    src = pl.multiple_of(lo_ref[at] + done, SEG_ALIGN)
                    dst = pl.multiple_of(off_ref[at] + done, SEG_ALIGN)
                    return pltpu.make_async_copy(cbuf.at[sl, u, pl.ds(src, size)], out_ref.at[pl.ds(dst, size)],
                                                 sems.at[sl, u, g, k])

                _piece_copies(np_ref[at], sizes, make, act)

    for_each(i, slot, lambda cp: cp.start())

    @pl.when(i > 0)
    def _():
        for_each(i - 1, 1 - slot, lambda cp: cp.wait())

    @pl.when(i == pl.num_programs(0) - 1)
    def _():
        for_each(i, slot, lambda cp: cp.wait())


def _experts_kernel(grp_ref, nb_ref, x_ref, wg_ref, wu_ref, wd_ref, o_ref):
    del grp_ref
    b = pl.program_id(0)
    d = o_ref.shape[1]

    @pl.when(b < nb_ref[0])
    def _():
        blk = x_ref[...]
        x = blk[:, :d]
        wp = blk[:, d:].astype(F32)
        acc = jnp.zeros(o_ref.shape, F32)
        for e in range(EXPERTS_PER_GROUP):
            w_e = wp[:, e:e + 1] + wp[:, EXPERTS_PER_GROUP + e:EXPERTS_PER_GROUP + e + 1]
            hid = _silu(_dot(x, wg_ref[e])) * _dot(x, wu_ref[e]) * w_e
            acc = acc + _dot(hid.astype(BF16), wd_ref[e])
        o_ref[...] = acc.astype(BF16)

    @pl.when(b >= nb_ref[0])
    def _():
        o_ref[...] = jnp.zeros(o_ref.shape, BF16)


def _combine_kernel(lo_ref, off_ref, np_ref, ld_ref, x1_ref, g2_ref, shf_ref, scf_ref, fnw_ref, src_ref, y_ref,
                    seg, sems, *, ts, tps, final_norm):
    i = pl.program_id(0)
    slot = i % 2
    sizes = [ts >> k for k in range((ts // SEG_ALIGN).bit_length())]

    def fetch(step, sl, act):
        for u in range(tps):
            for g in range(N_EGROUPS):
                at = (step * tps + u) * N_EGROUPS + g

                def make(k, size, done, u=u, g=g, at=at):
                    src = pl.multiple_of(off_ref[at] + done, SEG_ALIGN)
                    dst = pl.multiple_of(lo_ref[at] + done, SEG_ALIGN)
                    return pltpu.make_async_copy(src_ref.at[pl.ds(src, size)], seg.at[sl, u, pl.ds(dst, size)],
                                                 sems.at[sl, u, g, k])

                _piece_copies(np_ref[at], sizes, make, act)

    @pl.when(i == 0)
    def _():
        seg[...] = jnp.zeros(seg.shape, BF16)
        fetch(0, 0, lambda cp: cp.start())

    @pl.when(i + 1 < pl.num_programs(0))
    def _():
        fetch(i + 1, 1 - slot, lambda cp: cp.start())

    fetch(i, slot, lambda cp: cp.wait())

    r = lax.broadcasted_iota(jnp.int32, (seg.shape[2], ts), 0)
    for u in range(tps):
        rs = slice(u * ts, (u + 1) * ts)
        perm = jnp.where(r == ld_ref[u], 1.0, 0.0).astype(BF16)
        xo = x1_ref[rs, :] + g2_ref[0] * _dot_tn(perm, seg[slot, u])
        if final_norm:
            xo = _rms(xo, fnw_ref[...]) * (1.0 + scf_ref[0]) + shf_ref[0]
        y_ref[rs, :] = xo


def _moe_final(h2ext, ldest, cnt, wg, wu, wd, x1, mod3, modf3, boff, rows_per_batch, final_norm_w, final_norm):
    t, d = x1.shape
    payload = h2ext.shape[1]
    ts = min(SORT_TILE, rows_per_batch)
    tps = 2 if rows_per_batch % (2 * ts) == 0 else 1
    block = MOE_BLOCK if t >= SMALL_INPUT_BLOCKS * MOE_BLOCK else MOE_BLOCK // 2
    n_tiles = t // ts
    ldest = ldest.reshape(n_tiles, 1, ts)
    n_blocks = -(-(t + n_tiles * N_EGROUPS * (SEG_ALIGN - 1) + N_EGROUPS * (block - 1)) // block)
    lo, off, n_pad, blk_group, n_used = _sort_plan(cnt, n_tiles, block, n_blocks)
    ld_spec = pl.BlockSpec((tps, 1, ts), lambda i, *_: (i, 0, 0))

    sorted_rows = pl.pallas_call(
        functools.partial(_sort_kernel, ts=ts, tps=tps),
        grid_spec=pltpu.PrefetchScalarGridSpec(
            num_scalar_prefetch=3,
            grid=(n_tiles // tps,),
            in_specs=[pl.BlockSpec((tps * ts, payload), lambda i, *_: (i, 0)), ld_spec,
                      pl.BlockSpec(memory_space=pl.ANY)],
            out_specs=pl.BlockSpec(memory_space=pl.ANY),
            scratch_shapes=[pltpu.VMEM((2, tps, ts + N_EGROUPS * SEG_ALIGN, payload), BF16),
                            pltpu.SemaphoreType.DMA((2, tps, N_EGROUPS, (ts // SEG_ALIGN).bit_length()))]),
        out_shape=jax.ShapeDtypeStruct((n_blocks * block, payload), BF16),
        input_output_aliases={5: 0},
        compiler_params=_params("arbitrary"),
        name="moe_sort",
    )(lo, off, n_pad, h2ext, ldest, jnp.zeros((n_blocks * block, payload), BF16))

    d_e = wg.shape[2]
    live = lambda b, nb: jnp.minimum(b, nb[0] - 1)
    out_sorted = pl.pallas_call(
        _experts_kernel,
        grid_spec=pltpu.PrefetchScalarGridSpec(
            num_scalar_prefetch=2,
            grid=(n_blocks,),
            in_specs=[pl.BlockSpec((block, payload), lambda b, grp, nb: (live(b, nb), 0)),
                      pl.BlockSpec((EXPERTS_PER_GROUP, d, d_e), lambda b, grp, nb: (grp[b], 0, 0)),
                      pl.BlockSpec((EXPERTS_PER_GROUP, d, d_e), lambda b, grp, nb: (grp[b], 0, 0)),
                      pl.BlockSpec((EXPERTS_PER_GROUP, d_e, d), lambda b, grp, nb: (grp[b], 0, 0))],
            out_specs=pl.BlockSpec((block, d), lambda b, grp, nb: (b, 0))),
        out_shape=jax.ShapeDtypeStruct((n_blocks * block, d), BF16),
        compiler_params=_params("arbitrary"),
        name="moe_experts",
    )(blk_group, n_used, sorted_rows, wg, wu, wd)

    per = rows_per_batch // (tps * ts)
    row = lambda col: pl.BlockSpec((1, 1, d), lambda i, *_, col=col: (i // per + boff, 0, col))
    return pl.pallas_call(
        functools.partial(_combine_kernel, ts=ts, tps=tps, final_norm=final_norm),
        grid_spec=pltpu.PrefetchScalarGridSpec(
            num_scalar_prefetch=3,
            grid=(n_tiles // tps,),
            in_specs=[ld_spec,
                      pl.BlockSpec((tps * ts, d), lambda i, *_: (i, 0)),
                      row(5), row(0), row(1),
                      pl.BlockSpec((1, d), lambda i, *_: (0, 0)),
                      pl.BlockSpec(memory_space=pl.ANY)],
            out_specs=pl.BlockSpec((tps * ts, d), lambda i, *_: (i, 0)),
            scratch_shapes=[pltpu.VMEM((2, tps, ts + N_EGROUPS * SEG_ALIGN, d), BF16),
                            pltpu.SemaphoreType.DMA((2, tps, N_EGROUPS, (ts // SEG_ALIGN).bit_length()))]),
        out_shape=jax.ShapeDtypeStruct((t, d), F32),
        compiler_params=_params("arbitrary"),
        name="moe_combine",
    )(lo, off, n_pad, ldest, x1, mod3, modf3, modf3, final_norm_w, out_sorted)


def _layer(x, mod3, modf3, boff, k_prev, v_prev, logf_prev, conv_prev, ssm_prev, p, final_norm_w, final_norm):
    b, l, d = x.shape
    z, xbc, q, k, v, k_b, v_b, sm, smt, conv_new = _inproj(
        x, mod3, boff, p["norm1_w"], p["wz"], p["wx"], p["wq"], p["wk"], p["wv"], p["ws"], p["bs"])
    n_heads_ssd = p["wz"].shape[1] // P_SSD
    y_ssd, ssm_new = _ssd(xbc, z, sm, smt, conv_prev, ssm_prev, p["conv_w"], p["conv_b"], p["a_log"],
                          p["d_skip"], p["ssd_norm_w"])
    n_heads = p["wq"].shape[1] // HD_ATT
    logf_t = smt[:, n_heads_ssd:n_heads_ssd + n_heads, :]
    if k_prev is None:
        p0, k_all, v_all, lf_all = 0, k_b, v_b, logf_t
    else:
        p0 = k_prev.shape[1]
        pad = (-(p0 + l)) % LANES
        zeros = lambda w: jnp.zeros((b, pad, w), BF16)
        k_all = jnp.concatenate([k_prev.astype(BF16), k_b, zeros(k_b.shape[2])], axis=1)
        v_all = jnp.concatenate([v_prev.astype(BF16), v_b, zeros(v_b.shape[2])], axis=1)
        lf_all = jnp.concatenate([jnp.swapaxes(logf_prev, 1, 2), logf_t, jnp.zeros((b, n_heads, pad), F32)], axis=2)
    o_att = _attention(q, k_all, v_all, _forget_cumsum(lf_all), p0)
    x1, h2ext, ldest, cnt = _outproj(y_ssd, o_att, x, mod3, boff, p["attn_norm_w"], p["wo_ssd"], p["wo_att"],
                                     p["norm2_w"], p["wr"], p["br"])
    y = _moe_final(h2ext, ldest, cnt, p["wg"], p["wu"], p["wd"], x1.reshape(b * l, d), mod3, modf3, boff, l,
                   final_norm_w, final_norm)
    return y.reshape(b, l, d), (k, v, jnp.swapaxes(logf_t, 1, 2), conv_new, ssm_new)


def kernel(x_prompt, x_sample, c_prompt, c_sample, cache_k, cache_v, cache_logf, state_conv, state_ssm, norm1_w, w_ada, b_ada, w_in, conv_w, conv_b, dt_bias, a_log, d_skip, ssd_norm_w, f_bias, attn_norm_w, w_out, norm2_w, w_rg, b_rg, w_re, b_re, w_gate, w_up, w_down, final_norm_w, w_ada_f, b_ada_f):
    depth = w_in.shape[0]
    bp, lp, d = x_prompt.shape
    bs = x_sample.shape[0]
    d_conv = conv_w.shape[2]
    d_ssd = ssd_norm_w.shape[1]
    d_att = attn_norm_w.shape[1]
    h_ssd = dt_bias.shape[1]
    h_att = f_bias.shape[1]
    assert h_ssd + h_att == SMALL_W and d_att // HD_ATT == h_att and d_ssd // P_SSD == h_ssd

    c_all = jnp.concatenate([c_prompt, c_sample], axis=0)
    modf3 = _modulation(c_all, w_ada_f, b_ada_f).reshape(bp + bs, 1, 2 * d)
    final_w = final_norm_w.reshape(1, d)

    i0 = d_ssd
    i1 = i0 + d_conv
    i2 = i1 + h_ssd
    i3 = i2 + d_att
    i4 = i3 + d_att
    i5 = i4 + d_att
    yp, ys = x_prompt, x_sample
    outs_p, outs_s = [], []
    for layer in range(depth):
        mod3 = _modulation(c_all, w_ada[layer], b_ada[layer]).reshape(bp + bs, 1, 6 * d)
        wi = w_in[layer]
        w_small = jnp.concatenate([wi[:, i1:i2], wi[:, i5:], jnp.zeros((d, LANES - SMALL_W), F32)], axis=1)
        b_small = jnp.concatenate([dt_bias[layer], f_bias[layer], jnp.zeros((LANES - SMALL_W,), F32)])
        wr = jnp.zeros((ROUTER_ROWS, d), F32)
        wr = wr.at[:N_EGROUPS].set(w_rg[layer].T)
        wr = wr.at[EXPERT_ROW0:EXPERT_ROW0 + N_EXPERTS].set(
            jnp.transpose(w_re[layer], (0, 2, 1)).reshape(N_EXPERTS, d))
        wr_hi, wr_lo = _split2(wr)
        br = jnp.zeros((ROUTER_ROWS,), F32)
        br = br.at[:N_EGROUPS].set(b_rg[layer])
        br = br.at[EXPERT_ROW0:EXPERT_ROW0 + N_EXPERTS].set(b_re[layer].reshape(N_EXPERTS))
        p = dict(
            norm1_w=norm1_w[layer].reshape(1, d),
            wz=wi[:, :i0].astype(BF16), wx=wi[:, i0:i1].astype(BF16),
            wq=(wi[:, i2:i3] * (LOG2E * HD_ATT ** -0.5)).astype(BF16),
            wk=wi[:, i3:i4].astype(BF16), wv=wi[:, i4:i5].astype(BF16),
            ws=w_small.astype(BF16), bs=b_small.reshape(1, LANES),
            conv_w=conv_w[layer], conv_b=conv_b[layer], a_log=a_log[layer], d_skip=d_skip[layer],
            ssd_norm_w=ssd_norm_w[layer], attn_norm_w=attn_norm_w[layer].reshape(1, d_att),
            wo_ssd=w_out[layer][:d_ssd].astype(BF16), wo_att=w_out[layer][d_ssd:].astype(BF16),
            norm2_w=norm2_w[layer].reshape(1, d),
            wr=jnp.concatenate([wr_hi, wr_lo], axis=0), br=br.reshape(ROUTER_ROWS, 1),
            wg=w_gate[layer].astype(BF16), wu=w_up[layer].astype(BF16), wd=w_down[layer].astype(BF16),
        )
        conv0 = jnp.zeros((bp, CONV_W - 1, d_conv), F32)
        ssm0 = jnp.zeros((bp, h_ssd, P_SSD, N_STATE), F32)
        last = layer == depth - 1
        yp, st_p = _layer(yp, mod3, modf3, 0, None, None, None, conv0, ssm0, p, final_w, last)
        ck = cache_k[layer].reshape(bs, -1, d_att)
        cv = cache_v[layer].reshape(bs, -1, d_att)
        ys, st_s = _layer(ys, mod3, modf3, bp, ck, cv, cache_logf[layer], state_conv[layer], state_ssm[layer], p,
                          final_w, last)
        outs_p.append(st_p)
        outs_s.append(st_s)

    def stack(outs, b, l):
        k = jnp.stack([o[0].reshape(b, l, h_att, HD_ATT) for o in outs])
        v = jnp.stack([o[1].reshape(b, l, h_att, HD_ATT) for o in outs])
        return (k, v, jnp.stack([o[2] for o in outs]), jnp.stack([o[3] for o in outs]),
                jnp.stack([o[4] for o in outs]))

    return (yp, ys) + stack(outs_p, bp, lp) + stack(outs_s, bs, x_sample.shape[1])
```
